```python
import math
import jax, jax.numpy as jnp
from jax import lax
import numpy as np

D_MODEL = 2048
BATCH = 2
SEQ = 4096
DEPTH = 2

HEAD_DIM = D_MODEL // 16
POOL_WINDOWS = (2, 4, 8, 16)
POOL_GROUP = HEAD_DIM
POOL_WIDTH = len(POOL_WINDOWS) * POOL_GROUP
DIL_PAIRS = ((128, 1), (512, 4), (2048, 16))
DIL_HEADS_PER_GROUP = 4
DIL_HEADS = DIL_HEADS_PER_GROUP * len(DIL_PAIRS)
DIL_WIDTH = DIL_HEADS * HEAD_DIM
DIL_OUT = DIL_HEADS_PER_GROUP * HEAD_DIM
GRID_W = 64
NA_ROWS_MAX = 8
NA_COLS = 16
NA_HEADS = 8
NA_WIDTH = NA_HEADS * HEAD_DIM
SG_CHUNK = 128
SG_GROUPS = 8
SG_WIDTH = 1024
D_FF = 5632
N_EVEN = (DEPTH + 1) // 2
N_ODD = DEPTH // 2
EVEN_IN = POOL_WIDTH + 3 * DIL_WIDTH
EVEN_OUT = POOL_WIDTH + DIL_OUT
ODD_IN = 3 * NA_WIDTH + 2 * SG_WIDTH
ODD_OUT = NA_WIDTH + SG_WIDTH
RMS_EPS = 1e-6
NEG_INF = -1e30

kernel_name = "hybrid_pool_dilated_natten_gmlp_encoder"


def rms_norm(x, g):
    xf = x.astype(jnp.float32)
    y = xf * lax.rsqrt(jnp.mean(xf * xf, axis=-1, keepdims=True) + RMS_EPS)
    return (y * g.astype(jnp.float32)).astype(x.dtype)


def alibi_slopes(n):
    return jnp.asarray(2.0 ** (-8.0 * np.arange(1, n + 1) / n), dtype=jnp.float32)


def swiglu(h, w_gate, w_up, w_down):
    return (jax.nn.silu(h @ w_gate) * (h @ w_up)) @ w_down


def pool_mixer(a, w_pool, pool_scale):
    Bsz, S, _ = a.shape
    af = a.reshape(Bsz, S, len(POOL_WINDOWS), POOL_GROUP).astype(jnp.float32)
    cs = jnp.concatenate([jnp.zeros_like(af[:, :1]), jnp.cumsum(af, axis=1)], axis=1)
    t = jnp.arange(S)
    outs = []
    for g, w in enumerate(POOL_WINDOWS):
        lo = jnp.clip(t - w // 2, 0, S)
        hi = jnp.clip(t + w // 2, 0, S)
        cs_g = cs[:, :, g]
        cnt = (hi - lo).astype(jnp.float32)[None, :, None]
        outs.append((cs_g[:, hi] - cs_g[:, lo]) / cnt - af[:, :, g])
    pooled = jnp.stack(outs, axis=2).astype(a.dtype)
    y = jnp.einsum('bsgc,gce->bsge', pooled, w_pool)
    return y.reshape(Bsz, S, POOL_WIDTH) * pool_scale


def dilated_group_attn(q, k, v, window, dil, slopes):
    Bsz, H, S, Dh = q.shape
    half = window // (2 * dil)
    blk = half
    L = S // dil
    nb = -(-L // blk)
    Lp = nb * blk

    def to_res(x):
        return x.reshape(Bsz, H, L, dil, Dh).transpose(0, 1, 3, 2, 4)

    qb = jnp.pad(to_res(q), ((0, 0), (0, 0), (0, 0), (0, Lp - L), (0, 0))).reshape(Bsz, H, dil, nb, blk, Dh)

    def halo(x):
        xp = jnp.pad(x, ((0, 0), (0, 0), (0, 0), (blk, Lp - L + blk), (0, 0))).reshape(Bsz, H, dil, nb + 2, blk, Dh)
        return jnp.concatenate([xp[:, :, :, :nb], xp[:, :, :, 1:nb + 1], xp[:, :, :, 2:nb + 2]], axis=4)

    kb, vb = halo(to_res(k)), halo(to_res(v))
    s = jnp.einsum('bhrnid,bhrnjd->bhrnij', qb, kb, preferred_element_type=jnp.float32)
    i = jnp.arange(blk)
    j = jnp.arange(3 * blk)
    n = jnp.arange(nb)
    rel = j[None, :] - blk - i[:, None]
    key_pos = (n[:, None] - 1) * blk + j[None, :]
    valid = (jnp.abs(rel) <= half)[None] & ((key_pos >= 0) & (key_pos < L))[:, None, :]
    alibi = -slopes[:, None, None] * (jnp.abs(rel) * dil).astype(jnp.float32)[None]
    s = jnp.where(valid[None, None, None], s + alibi[None, :, None, None], NEG_INF)
    m = jnp.max(s, axis=-1)
    p = jnp.exp(s - m[..., None])
    l = jnp.sum(p, axis=-1)
    o = jnp.einsum('bhrnij,bhrnjd->bhrnid', p, vb.astype(jnp.float32)) / l[..., None]

    def from_res(x):
        x = x.reshape((Bsz, H, dil, Lp) + x.shape[5:])[:, :, :, :L]
        x = jnp.moveaxis(x, 2, 3)
        return x.reshape((Bsz, H, S) + x.shape[4:])

    return from_res(o), from_res(m), from_res(l)


def dilated_mixer(q, k, v, q_gain, k_gain):
    Bsz, S, _ = q.shape

    def heads(x):
        return x.reshape(Bsz, S, DIL_HEADS, HEAD_DIM).transpose(0, 2, 1, 3)

    qh = rms_norm(heads(q), q_gain) * (HEAD_DIM ** -0.5)
    kh = rms_norm(heads(k), k_gain)
    vh = heads(v)
    slopes = alibi_slopes(DIL_HEADS)
    outs, maxs, dens = [], [], []
    for g, (w, d) in enumerate(DIL_PAIRS):
        sl = slice(g * DIL_HEADS_PER_GROUP, (g + 1) * DIL_HEADS_PER_GROUP)
        o, m, l = dilated_group_attn(qh[:, sl], kh[:, sl], vh[:, sl], w, d, slopes[sl])
        outs.append(o)
        maxs.append(m)
        dens.append(l)
    O = jnp.stack(outs)
    M = jnp.stack(maxs)
    Lden = jnp.stack(dens)
    wts = Lden * jnp.exp(M - jnp.max(M, axis=0, keepdims=True))
    out = jnp.sum(wts[..., None] * O, axis=0) / jnp.sum(wts, axis=0)[..., None]
    return out.transpose(0, 2, 1, 3).reshape(Bsz, S, DIL_OUT).astype(q.dtype)


def neighbourhood_mixer(q, k, v, q_gain, k_gain, rpb):
    Bsz, S, _ = q.shape
    rows = S // GRID_W
    wr = min(NA_ROWS_MAX, rows)

    def grid(x):
        return x.reshape(Bsz, rows, GRID_W, NA_HEADS, HEAD_DIM).transpose(0, 3, 1, 2, 4)

    qg = rms_norm(grid(q), q_gain) * (HEAD_DIM ** -0.5)
    kg = rms_norm(grid(k), k_gain)
    vg = grid(v)
    r = jnp.arange(rows)
    row_start = jnp.clip(r - wr // 2, 0, rows - wr)
    row_idx = row_start[:, None] + jnp.arange(wr)[None, :]
    k_rows = kg[:, :, row_idx]
    v_rows = vg[:, :, row_idx]
    c = jnp.arange(GRID_W)
    col_start = jnp.clip(c - NA_COLS // 2, 0, GRID_W - NA_COLS)
    col_ok = (c[None, :] >= col_start[:, None]) & (c[None, :] < col_start[:, None] + NA_COLS)
    rel_r = row_idx - r[:, None] + (NA_ROWS_MAX - 1)
    rel_c = jnp.clip(c[None, :] - c[:, None], -(NA_COLS - 1), NA_COLS - 1) + (NA_COLS - 1)
    bias = rpb[:, rel_r[:, None, :, None], rel_c[None, :, None, :]].astype(jnp.float32)
    s = jnp.einsum('bhrqd,bhrwkd->bhrqwk', qg, k_rows, preferred_element_type=jnp.float32) + bias[None]
    s = jnp.where(col_ok[:, None, :], s, NEG_INF)
    p = jax.nn.softmax(s.reshape(Bsz, NA_HEADS, rows, GRID_W, wr * GRID_W), axis=-1).reshape(s.shape)
    o = jnp.einsum('bhrqwk,bhrwkd->bhrqd', p, v_rows.astype(jnp.float32))
    return o.transpose(0, 2, 3, 1, 4).reshape(Bsz, S, NA_WIDTH).astype(q.dtype)


def spatial_gating(u, v, v_gain, w_s, b_s):
    Bsz, S, _ = u.shape
    nc = S // SG_CHUNK
    vn = rms_norm(v, v_gain).reshape(Bsz, nc, SG_CHUNK, SG_GROUPS, SG_WIDTH // SG_GROUPS)
    sv = jnp.einsum('gpq,bnqgc->bnpgc', w_s, vn) + b_s.T[None, None, :, :, None]
    return u * sv.reshape(Bsz, S, SG_WIDTH)


def setup_inputs(seed: int = 0) -> dict:
    key = jax.random.key(seed)
    ks = jax.random.split(key, 24)
    f32 = jnp.float32

    def nrm(k, shape, scale):
        return jax.random.normal(k, shape, f32) * scale

    def gain(k, shape):
        return 1.0 + 0.02 * jax.random.normal(k, shape, f32)

    return {
        "x": jax.random.normal(ks[0], (BATCH, SEQ, D_MODEL), f32),
        "norm_ffn1": gain(ks[1], (DEPTH, D_MODEL)),
        "norm_mix": gain(ks[2], (DEPTH, D_MODEL)),
        "norm_ffn2": gain(ks[3], (DEPTH, D_MODEL)),
        "norm_out": gain(ks[4], (DEPTH, D_MODEL)),
        "ffn_w_gate": nrm(ks[5], (DEPTH, 2, D_MODEL, D_FF), D_MODEL ** -0.5),
        "ffn_w_up": nrm(ks[6], (DEPTH, 2, D_MODEL, D_FF), D_MODEL ** -0.5),
        "ffn_w_down": nrm(ks[7], (DEPTH, 2, D_FF, D_MODEL), D_FF ** -0.5),
        "even_w_in": nrm(ks[8], (N_EVEN, D_MODEL, EVEN_IN), D_MODEL ** -0.5),
        "pool_w": nrm(ks[9], (N_EVEN, len(POOL_WINDOWS), POOL_GROUP, POOL_GROUP), POOL_GROUP ** -0.5),
        "pool_scale": gain(ks[10], (N_EVEN, POOL_WIDTH)),
        "dil_q_gain": gain(ks[11], (N_EVEN, HEAD_DIM)),
        "dil_k_gain": gain(ks[12], (N_EVEN, HEAD_DIM)),
        "even_w_out": nrm(ks[13], (N_EVEN, EVEN_OUT, D_MODEL), EVEN_OUT ** -0.5),
        "odd_w_in": nrm(ks[14], (N_ODD, D_MODEL, ODD_IN), D_MODEL ** -0.5),
        "na_q_gain": gain(ks[15], (N_ODD, HEAD_DIM)),
        "na_k_gain": gain(ks[16], (N_ODD, HEAD_DIM)),
        "na_rpb": nrm(ks[17], (N_ODD, NA_HEADS, 2 * NA_ROWS_MAX - 1, 2 * NA_COLS - 1), 0.1),
        "sg_v_gain": gain(ks[18], (N_ODD, SG_WIDTH)),
        "sg_w": nrm(ks[19], (N_ODD, SG_GROUPS, SG_CHUNK, SG_CHUNK), SG_CHUNK ** -0.5),
        "sg_b": gain(ks[20], (N_ODD, SG_GROUPS, SG_CHUNK)),
        "odd_w_out": nrm(ks[21], (N_ODD, ODD_OUT, D_MODEL), ODD_OUT ** -0.5),
    }


def reference(x, norm_ffn1, norm_mix, norm_ffn2, norm_out, ffn_w_gate, ffn_w_up, ffn_w_down,
              even_w_in, pool_w, pool_scale, dil_q_gain, dil_k_gain, even_w_out,
              odd_w_in, na_q_gain, na_k_gain, na_rpb, sg_v_gain, sg_w, sg_b, odd_w_out):
    for layer in range(DEPTH):
        h = rms_norm(x, norm_ffn1[layer])
        x = x + 0.5 * swiglu(h, ffn_w_gate[layer, 0], ffn_w_up[layer, 0], ffn_w_down[layer, 0])
        h = rms_norm(x, norm_mix[layer])
        if layer % 2 == 0:
            e = layer // 2
            z = h @ even_w_in[e]
            a, qb, kb, vb = jnp.split(z, [POOL_WIDTH, POOL_WIDTH + DIL_WIDTH, POOL_WIDTH + 2 * DIL_WIDTH], axis=-1)
            ya = pool_mixer(a, pool_w[e], pool_scale[e])
            yb = dilated_mixer(qb, kb, vb, dil_q_gain[e], dil_k_gain[e])
            y = jnp.concatenate([ya, yb], axis=-1) @ even_w_out[e]
        else:
            o = layer // 2
            z = h @ odd_w_in[o]
            qc, kc, vc, uv = jnp.split(z, [NA_WIDTH, 2 * NA_WIDTH, 3 * NA_WIDTH], axis=-1)
            yc = neighbourhood_mixer(qc, kc, vc, na_q_gain[o], na_k_gain[o], na_rpb[o])
            u, v = jnp.split(jax.nn.gelu(uv), 2, axis=-1)
            yd = spatial_gating(u, v, sg_v_gain[o], sg_w[o], sg_b[o])
            y = jnp.concatenate([yc, yd], axis=-1) @ odd_w_out[o]
        x = x + y
        h = rms_norm(x, norm_ffn2[layer])
        x = x + 0.5 * swiglu(h, ffn_w_gate[layer, 1], ffn_w_up[layer, 1], ffn_w_down[layer, 1])
        x = rms_norm(x, norm_out[layer])
    return x
```

```python
import functools

import numpy as np
import jax
import jax.numpy as jnp
from jax import lax
from jax.experimental import pallas as pl
from jax.experimental.pallas import tpu as pltpu

F32 = jnp.float32
BF16 = jnp.bfloat16

D_MODEL = 2048
D_FF = 5632
HEAD_DIM = 128
POOL_WINDOWS = (2, 4, 8, 16)
POOL_WIDTH = 512
POOL_HALO = 8
DIL_PAIRS = ((128, 1), (512, 4), (2048, 16))
DIL_HEADS_PER_GROUP = 4
DIL_HEADS = 12
DIL_WIDTH = DIL_HEADS * HEAD_DIM
DIL_OUT = DIL_HEADS_PER_GROUP * HEAD_DIM
GRID_W = 64
NA_ROWS = 8
NA_COLS = 16
NA_HEADS = 8
NA_WIDTH = NA_HEADS * HEAD_DIM
NA_QROWS = 4
NA_KROWS = NA_QROWS + NA_ROWS
SG_CHUNK = 128
SG_GROUPS = 8
SG_WIDTH = 1024
RMS_EPS = 1e-6
NEG_INF = -1e30

VMEM_LIMIT_BYTES = 56 * 1024 * 1024

FFN_TM = 512
FFN_TF = 512
PROJ_TM = 512
PROJ_TN = 512
DIL_TQ = 256
OUT_TM = 512


def _params(*semantics):
    return pltpu.CompilerParams(dimension_semantics=semantics, vmem_limit_bytes=VMEM_LIMIT_BYTES)


def _resident(block_shape, index_map):
    return pl.BlockSpec(block_shape, index_map, pipeline_mode=pl.Buffered(1))


def _rms(x, gain):
    return x * lax.rsqrt(jnp.mean(x * x, axis=-1, keepdims=True) + RMS_EPS) * gain


def _ffn_kernel(x_ref, g_ref, wg_ref, wu_ref, wd_ref, gout_ref, o_ref, h_ref, *, final_norm):
    j = pl.program_id(1)

    @pl.when(j == 0)
    def _():
        x = x_ref[...]
        h_ref[...] = _rms(x, g_ref[...]).astype(BF16)
        o_ref[...] = x

    h = h_ref[...]
    gate = jnp.dot(h, wg_ref[...], preferred_element_type=F32)
    up = jnp.dot(h, wu_ref[...], preferred_element_type=F32)
    act = (gate * jax.nn.sigmoid(gate)) * up * 0.5
    o_ref[...] += jnp.dot(act.astype(BF16), wd_ref[...], preferred_element_type=F32)

    if final_norm:
        @pl.when(j == pl.num_programs(1) - 1)
        def _():
            o_ref[...] = _rms(o_ref[...], gout_ref[...])


def _ffn(x, gain, w_gate, w_up, w_down, layer, idx, gain_out=None):
    t = x.shape[0]
    final_norm = gain_out is not None
    if gain_out is None:
        gain_out = gain
    return pl.pallas_call(
        functools.partial(_ffn_kernel, final_norm=final_norm),
        grid=(t // FFN_TM, D_FF // FFN_TF),
        in_specs=[
            pl.BlockSpec((FFN_TM, D_MODEL), lambda i, j: (i, 0)),
            pl.BlockSpec((1, D_MODEL), lambda i, j: (0, 0)),
            pl.BlockSpec((None, None, D_MODEL, FFN_TF), lambda i, j: (layer, idx, 0, j)),
            pl.BlockSpec((None, None, D_MODEL, FFN_TF), lambda i, j: (layer, idx, 0, j)),
            pl.BlockSpec((None, None, FFN_TF, D_MODEL), lambda i, j: (layer, idx, j, 0)),
            pl.BlockSpec((1, D_MODEL), lambda i, j: (0, 0)),
        ],
        out_specs=pl.BlockSpec((FFN_TM, D_MODEL), lambda i, j: (i, 0)),
        out_shape=jax.ShapeDtypeStruct((t, D_MODEL), F32),
        scratch_shapes=[pltpu.VMEM((FFN_TM, D_MODEL), BF16)],
        compiler_params=_params("parallel", "arbitrary"),
        name="ffn_final" if final_norm else "ffn",
    )(x, gain.reshape(1, D_MODEL), w_gate, w_up, w_down, gain_out.reshape(1, D_MODEL))


def _head_norm_store(z, out_ref, col0, gain):
    for hd in range(PROJ_TN // HEAD_DIM):
        zh = z[:, hd * HEAD_DIM:(hd + 1) * HEAD_DIM]
        out_ref[:, col0 + hd * HEAD_DIM:col0 + (hd + 1) * HEAD_DIM] = _rms(zh, gain).astype(out_ref.dtype)


def _even_in_kernel(x_ref, g_ref, w_ref, qg_ref, kg_ref, a_ref, q_ref, k_ref, v_ref, h_ref):
    h_ref[...] = _rms(x_ref[...], g_ref[...]).astype(BF16)

    def chunk(c):
        return jnp.dot(h_ref[...], w_ref[:, c * PROJ_TN:(c + 1) * PROJ_TN], preferred_element_type=F32)

    a_ref[...] = chunk(0)
    per = DIL_WIDTH // PROJ_TN
    for c in range(per):
        _head_norm_store(chunk(1 + c), q_ref, c * PROJ_TN, qg_ref[...])
        _head_norm_store(chunk(1 + per + c), k_ref, c * PROJ_TN, kg_ref[...])
        v_ref[:, c * PROJ_TN:(c + 1) * PROJ_TN] = chunk(1 + 2 * per + c).astype(BF16)


def _even_in(x, gain, w_in, q_gain, k_gain):
    t = x.shape[0]
    n_in = w_in.shape[1]
    row = lambda i: (i, 0)
    const = lambda i: (0, 0)
    return pl.pallas_call(
        _even_in_kernel,
        grid=(t // PROJ_TM,),
        in_specs=[
            pl.BlockSpec((PROJ_TM, D_MODEL), row),
            pl.BlockSpec((1, D_MODEL), const),
            _resident((D_MODEL, n_in), const),
            pl.BlockSpec((1, HEAD_DIM), const),
            pl.BlockSpec((1, HEAD_DIM), const),
        ],
        out_specs=[
            pl.BlockSpec((PROJ_TM, POOL_WIDTH), row),
            pl.BlockSpec((PROJ_TM, DIL_WIDTH), row),
            pl.BlockSpec((PROJ_TM, DIL_WIDTH), row),
            pl.BlockSpec((PROJ_TM, DIL_WIDTH), row),
        ],
        out_shape=[
            jax.ShapeDtypeStruct((t, POOL_WIDTH), F32),
            jax.ShapeDtypeStruct((t, DIL_WIDTH), BF16),
            jax.ShapeDtypeStruct((t, DIL_WIDTH), BF16),
            jax.ShapeDtypeStruct((t, DIL_WIDTH), BF16),
        ],
        scratch_shapes=[pltpu.VMEM((PROJ_TM, D_MODEL), BF16)],
        compiler_params=_params("parallel"),
        name="even_in",
    )(x, gain.reshape(1, D_MODEL), w_in,
      (q_gain * HEAD_DIM ** -0.5).reshape(1, HEAD_DIM), k_gain.reshape(1, HEAD_DIM))


def _odd_in_kernel(x_ref, g_ref, w_ref, qg_ref, kg_ref, vg_ref, q_ref, k_ref, v_ref, u_ref, vn_ref,
                   h_ref, gv_ref):
    h_ref[...] = _rms(x_ref[...], g_ref[...]).astype(BF16)

    def chunk(c):
        return jnp.dot(h_ref[...], w_ref[:, c * PROJ_TN:(c + 1) * PROJ_TN], preferred_element_type=F32)

    per = NA_WIDTH // PROJ_TN
    for c in range(per):
        _head_norm_store(chunk(c), q_ref, c * PROJ_TN, qg_ref[...])
        _head_norm_store(chunk(per + c), k_ref, c * PROJ_TN, kg_ref[...])
        v_ref[:, c * PROJ_TN:(c + 1) * PROJ_TN] = chunk(2 * per + c).astype(BF16)
    per_sg = SG_WIDTH // PROJ_TN
    for c in range(per_sg):
        u_ref[:, c * PROJ_TN:(c + 1) * PROJ_TN] = jax.nn.gelu(chunk(3 * per + c))
        gv_ref[:, c * PROJ_TN:(c + 1) * PROJ_TN] = jax.nn.gelu(chunk(3 * per + per_sg + c))
    vn_ref[...] = _rms(gv_ref[...], vg_ref[...]).astype(BF16)


def _odd_in(x, gain, w_in, q_gain, k_gain, v_gain):
    t = x.shape[0]
    n_in = w_in.shape[1]
    row = lambda i: (i, 0)
    const = lambda i: (0, 0)
    return pl.pallas_call(
        _odd_in_kernel,
        grid=(t // PROJ_TM,),
        in_specs=[
            pl.BlockSpec((PROJ_TM, D_MODEL), row),
            pl.BlockSpec((1, D_MODEL), const),
            _resident((D_MODEL, n_in), const),
            pl.BlockSpec((1, HEAD_DIM), const),
            pl.BlockSpec((1, HEAD_DIM), const),
            pl.BlockSpec((1, SG_WIDTH), const),
        ],
        out_specs=[
            pl.BlockSpec((PROJ_TM, NA_WIDTH), row),
            pl.BlockSpec((PROJ_TM, NA_WIDTH), row),
            pl.BlockSpec((PROJ_TM, NA_WIDTH), row),
            pl.BlockSpec((PROJ_TM, SG_WIDTH), row),
            pl.BlockSpec((PROJ_TM, SG_WIDTH), row),
        ],
        out_shape=[
            jax.ShapeDtypeStruct((t, NA_WIDTH), BF16),
            jax.ShapeDtypeStruct((t, NA_WIDTH), BF16),
            jax.ShapeDtypeStruct((t, NA_WIDTH), BF16),
            jax.ShapeDtypeStruct((t, SG_WIDTH), F32),
            jax.ShapeDtypeStruct((t, SG_WIDTH), BF16),
        ],
        scratch_shapes=[pltpu.VMEM((PROJ_TM, D_MODEL), BF16), pltpu.VMEM((PROJ_TM, SG_WIDTH), F32)],
        compiler_params=_params("parallel"),
        name="odd_in",
    )(x, gain.reshape(1, D_MODEL), w_in,
      (q_gain * HEAD_DIM ** -0.5).reshape(1, HEAD_DIM), k_gain.reshape(1, HEAD_DIM),
      v_gain.reshape(1, SG_WIDTH))


def _dilated_kernel(slopes_ref, q0_ref, q1_ref, q2_ref, k0_ref, k1_ref, k2_ref, v0_ref, v1_ref, v2_ref,
                    o_ref, *, seq):
    hh = pl.program_id(1)
    q_start = pl.program_id(2) * DIL_TQ
    q_refs = (q0_ref, q1_ref, q2_ref)
    k_refs = (k0_ref, k1_ref, k2_ref)
    v_refs = (v0_ref, v1_ref, v2_ref)
    maxes, dens, outs = [], [], []
    for g, (window, dil) in enumerate(DIL_PAIRS):
        reach = window // 2
        n_keys = DIL_TQ + 2 * reach
        k_start = pl.multiple_of(jnp.clip(q_start - reach, 0, seq - n_keys), 64)
        k = k_refs[g][pl.ds(k_start, n_keys), :]
        v = v_refs[g][pl.ds(k_start, n_keys), :]
        s = lax.dot_general(q_refs[g][...], k, (((1,), (1,)), ((), ())), preferred_element_type=F32)
        row = lax.broadcasted_iota(jnp.int32, (DIL_TQ, n_keys), 0)
        col = lax.broadcasted_iota(jnp.int32, (DIL_TQ, n_keys), 1)
        rel = col - row + (k_start - q_start)
        dist = jnp.abs(rel)
        valid = (dist <= reach) & ((rel & (dil - 1)) == 0)
        slope = slopes_ref[g * DIL_HEADS_PER_GROUP + hh]
        s = jnp.where(valid, s - slope * dist.astype(F32), NEG_INF)
        m = jnp.max(s, axis=-1, keepdims=True)
        p = jnp.exp(s - m)
        maxes.append(m)
        dens.append(jnp.sum(p, axis=-1, keepdims=True))
        outs.append(jnp.dot(p.astype(BF16), v, preferred_element_type=F32))
    m_all = jnp.maximum(jnp.maximum(maxes[0], maxes[1]), maxes[2])
    num = jnp.zeros((DIL_TQ, HEAD_DIM), F32)
    den = jnp.zeros((DIL_TQ, 1), F32)
    for g in range(len(DIL_PAIRS)):
        w = jnp.exp(maxes[g] - m_all)
        num = num + w * outs[g]
        den = den + w * dens[g]
    o_ref[...] = (num / den).astype(o_ref.dtype)


def _dilated(q, k, v, slopes):
    b, s, _ = q.shape

    def q_spec(g):
        return pl.BlockSpec((None, DIL_TQ, HEAD_DIM), lambda bi, hh, i: (bi, i, g * DIL_HEADS_PER_GROUP + hh))

    def kv_spec(g):
        return pl.BlockSpec((None, s, HEAD_DIM), lambda bi, hh, i: (bi, 0, g * DIL_HEADS_PER_GROUP + hh))

    groups = range(len(DIL_PAIRS))
    return pl.pallas_call(
        functools.partial(_dilated_kernel, seq=s),
        grid=(b, DIL_HEADS_PER_GROUP, s // DIL_TQ),
        in_specs=[pl.BlockSpec(memory_space=pltpu.SMEM)]
        + [q_spec(g) for g in groups] + [kv_spec(g) for g in groups] + [kv_spec(g) for g in groups],
        out_specs=pl.BlockSpec((None, DIL_TQ, HEAD_DIM), lambda bi, hh, i: (bi, i, hh)),
        out_shape=jax.ShapeDtypeStruct((b, s, DIL_OUT), BF16),
        compiler_params=_params("parallel", "parallel", "arbitrary"),
        name="dilated_attn",
    )(slopes, q, q, q, k, k, k, v, v, v)


def _even_out_kernel(a_ref, prev_ref, next_ref, yb_ref, pw_ref, ps_ref, wo_ref, x_ref, o_ref,
                     ext_ref, y_ref, *, seq):
    i = pl.program_id(1)
    tm = OUT_TM
    ext_ref[0:POOL_HALO, :] = jnp.where(i == 0, 0.0, prev_ref[...])
    ext_ref[POOL_HALO:POOL_HALO + tm, :] = a_ref[...]
    ext_ref[POOL_HALO + tm:, :] = jnp.where(i == pl.num_programs(1) - 1, 0.0, next_ref[...])
    pos = i * tm + lax.broadcasted_iota(jnp.int32, (tm, 1), 0)
    for g, window in enumerate(POOL_WINDOWS):
        half = window // 2
        cols = slice(g * HEAD_DIM, (g + 1) * HEAD_DIM)
        total = ext_ref[POOL_HALO - half:POOL_HALO - half + tm, cols]
        for shift in range(-half + 1, half):
            total = total + ext_ref[POOL_HALO + shift:POOL_HALO + shift + tm, cols]
        count = (jnp.minimum(pos + half, seq) - jnp.maximum(pos - half, 0)).astype(F32)
        pooled = total / count - a_ref[:, cols]
        ya = jnp.dot(pooled.astype(BF16), pw_ref[g], preferred_element_type=F32) * ps_ref[:, cols]
        y_ref[:, cols] = ya.astype(BF16)
    y_ref[:, POOL_WIDTH:] = yb_ref[...]
    o_ref[...] = x_ref[...] + jnp.dot(y_ref[...], wo_ref[...], preferred_element_type=F32)


def _even_out(a, yb, pool_w, pool_scale, w_out, x):
    b, s, _ = x.shape
    tm = OUT_TM
    halo_blocks = tm // POOL_HALO
    tile = lambda bi, i: (bi, i, 0)
    return pl.pallas_call(
        functools.partial(_even_out_kernel, seq=s),
        grid=(b, s // tm),
        in_specs=[
            pl.BlockSpec((None, tm, POOL_WIDTH), tile),
            pl.BlockSpec((None, POOL_HALO, POOL_WIDTH),
                         lambda bi, i: (bi, jnp.maximum(i * halo_blocks - 1, 0), 0)),
            pl.BlockSpec((None, POOL_HALO, POOL_WIDTH),
                         lambda bi, i: (bi, jnp.minimum((i + 1) * halo_blocks, s // POOL_HALO - 1), 0)),
            pl.BlockSpec((None, tm, DIL_OUT), tile),
            _resident((len(POOL_WINDOWS), HEAD_DIM, HEAD_DIM), lambda bi, i: (0, 0, 0)),
            pl.BlockSpec((1, POOL_WIDTH), lambda bi, i: (0, 0)),
            _resident((POOL_WIDTH + DIL_OUT, D_MODEL), lambda bi, i: (0, 0)),
            pl.BlockSpec((None, tm, D_MODEL), tile),
        ],
        out_specs=pl.BlockSpec((None, tm, D_MODEL), tile),
        out_shape=jax.ShapeDtypeStruct((b, s, D_MODEL), F32),
        scratch_shapes=[pltpu.VMEM((tm + 2 * POOL_HALO, POOL_WIDTH), F32),
                        pltpu.VMEM((tm, POOL_WIDTH + DIL_OUT), BF16)],
        compiler_params=_params("parallel", "parallel"),
        name="even_out",
    )(a, a, a, yb, pool_w, pool_scale.reshape(1, POOL_WIDTH), w_out, x)


def _na_kernel(q_ref, k_ref, v_ref, bias_ref, o_ref, *, rows):
    rb = pl.program_id(2)
    n_keys = NA_KROWS * GRID_W
    k_row0 = jnp.clip(rb * NA_QROWS - NA_ROWS // 2, 0, rows - NA_KROWS)
    k_start = pl.multiple_of(k_row0 * GRID_W, GRID_W)
    k = k_ref[pl.ds(k_start, n_keys), :]
    v = v_ref[pl.ds(k_start, n_keys), :]
    s = lax.dot_general(q_ref[...], k, (((1,), (1,)), ((), ())), preferred_element_type=F32) + bias_ref[...]
    m = jnp.max(s, axis=-1, keepdims=True)
    p = jnp.exp(s - m)
    den = jnp.sum(p, axis=-1, keepdims=True)
    o_ref[...] = (jnp.dot(p.astype(BF16), v, preferred_element_type=F32) / den).astype(o_ref.dtype)


def _na_bias_tables(rpb, rows):
    n_blocks = rows // NA_QROWS
    c = np.arange(GRID_W)
    col_start = np.clip(c - NA_COLS // 2, 0, GRID_W - NA_COLS)
    col_ok = (c[None, :] >= col_start[:, None]) & (c[None, :] < col_start[:, None] + NA_COLS)
    rel_c = np.clip(c[None, :] - c[:, None], -(NA_COLS - 1), NA_COLS - 1) + (NA_COLS - 1)
    tables = []
    for rb in (0, n_blocks // 2, n_blocks - 1):
        r = rb * NA_QROWS + np.arange(NA_QROWS)
        k_row0 = np.clip(rb * NA_QROWS - NA_ROWS // 2, 0, rows - NA_KROWS)
        kr = k_row0 + np.arange(NA_KROWS)
        row_start = np.clip(r - NA_ROWS // 2, 0, rows - NA_ROWS)
        row_ok = (kr[None, :] >= row_start[:, None]) & (kr[None, :] < row_start[:, None] + NA_ROWS)
        rel_r = np.clip(kr[None, :] - r[:, None] + (NA_ROWS - 1), 0, 2 * NA_ROWS - 2)
        ok = row_ok[:, None, :, None] & col_ok[None, :, None, :]
        bias = rpb[:, rel_r[:, None, :, None], rel_c[None, :, None, :]].astype(F32)
        bias = jnp.where(ok[None], bias, NEG_INF)
        tables.append(bias.reshape(rpb.shape[0], NA_QROWS * GRID_W, NA_KROWS * GRID_W))
    return jnp.stack(tables)


def _neighbourhood(q, k, v, bias):
    b, s, _ = q.shape
    rows = s // GRID_W
    n_blocks = rows // NA_QROWS
    tq = NA_QROWS * GRID_W
    n_keys = NA_KROWS * GRID_W

    def bias_map(bi, h, rb):
        kind = jnp.where(rb == 0, 0, jnp.where(rb == n_blocks - 1, 2, 1))
        return (kind, h, 0, 0)

    return pl.pallas_call(
        functools.partial(_na_kernel, rows=rows),
        grid=(b, NA_HEADS, n_blocks),
        in_specs=[
            pl.BlockSpec((None, tq, HEAD_DIM), lambda bi, h, rb: (bi, rb, h)),
            pl.BlockSpec((None, s, HEAD_DIM), lambda bi, h, rb: (bi, 0, h)),
            pl.BlockSpec((None, s, HEAD_DIM), lambda bi, h, rb: (bi, 0, h)),
            pl.BlockSpec((None, None, tq, n_keys), bias_map),
        ],
        out_specs=pl.BlockSpec((None, tq, HEAD_DIM), lambda bi, h, rb: (bi, rb, h)),
        out_shape=jax.ShapeDtypeStruct((b, s, NA_WIDTH), BF16),
        compiler_params=_params("parallel", "parallel", "arbitrary"),
        name="neighbourhood_attn",
    )(q, k, v, bias)


def _odd_out_kernel(yc_ref, u_ref, vn_ref, ws_ref, bs_ref, wo_ref, x_ref, o_ref, y_ref):
    width = SG_WIDTH // SG_GROUPS
    for c in range(OUT_TM // SG_CHUNK):
        rows = slice(c * SG_CHUNK, (c + 1) * SG_CHUNK)
        for g in range(SG_GROUPS):
            cols = slice(g * width, (g + 1) * width)
            sv = jnp.dot(ws_ref[g], vn_ref[rows, cols], preferred_element_type=F32) + bs_ref[g]
            y_ref[rows, NA_WIDTH + g * width:NA_WIDTH + (g + 1) * width] = (u_ref[rows, cols] * sv).astype(BF16)
    y_ref[:, :NA_WIDTH] = yc_ref[...]
    o_ref[...] = x_ref[...] + jnp.dot(y_ref[...], wo_ref[...], preferred_element_type=F32)


def _odd_out(yc, u, vn, w_s, b_s, w_out, x):
    b, s, _ = x.shape
    tm = OUT_TM
    width = SG_WIDTH // SG_GROUPS
    tile = lambda bi, i: (bi, i, 0)
    return pl.pallas_call(
        _odd_out_kernel,
        grid=(b, s // tm),
        in_specs=[
            pl.BlockSpec((None, tm, NA_WIDTH), tile),
            pl.BlockSpec((None, tm, SG_WIDTH), tile),
            pl.BlockSpec((None, tm, SG_WIDTH), tile),
            _resident((SG_GROUPS, SG_CHUNK, SG_CHUNK), lambda bi, i: (0, 0, 0)),
            _resident((SG_GROUPS, SG_CHUNK, width), lambda bi, i: (0, 0, 0)),
            _resident((NA_WIDTH + SG_WIDTH, D_MODEL), lambda bi, i: (0, 0)),
            pl.BlockSpec((None, tm, D_MODEL), tile),
        ],
        out_specs=pl.BlockSpec((None, tm, D_MODEL), tile),
        out_shape=jax.ShapeDtypeStruct((b, s, D_MODEL), F32),
        scratch_shapes=[pltpu.VMEM((tm, NA_WIDTH + SG_WIDTH), BF16)],
        compiler_params=_params("parallel", "parallel"),
        name="odd_out",
    )(yc, u, vn, w_s, b_s, w_out, x)


def kernel(x, norm_ffn1, norm_mix, norm_ffn2, norm_out, ffn_w_gate, ffn_w_up, ffn_w_down, even_w_in, pool_w, pool_scale, dil_q_gain, dil_k_gain, even_w_out, odd_w_in, na_q_gain, na_k_gain, na_rpb, sg_v_gain, sg_w, sg_b, odd_w_out):
    b, s, d = x.shape
    t = b * s
    depth = norm_ffn1.shape[0]
    w_gate, w_up, w_down = (w.astype(BF16) for w in (ffn_w_gate, ffn_w_up, ffn_w_down))
    slopes = jnp.asarray(2.0 ** (-8.0 * np.arange(1, DIL_HEADS + 1) / DIL_HEADS), dtype=F32)
    width = SG_WIDTH // SG_GROUPS

    x = x.reshape(t, d)
    for layer in range(depth):
        x = _ffn(x, norm_ffn1[layer], w_gate, w_up, w_down, layer, 0)
        if layer % 2 == 0:
            e = layer // 2
            a, q, k, v = _even_in(x, norm_mix[layer], even_w_in[e].astype(BF16), dil_q_gain[e], dil_k_gain[e])
            yb = _dilated(q.reshape(b, s, -1), k.reshape(b, s, -1), v.reshape(b, s, -1), slopes)
            x = _even_out(a.reshape(b, s, -1), yb, pool_w[e].astype(BF16), pool_scale[e],
                          even_w_out[e].astype(BF16), x.reshape(b, s, d))
        else:
            o = layer // 2
            q, k, v, u, vn = _odd_in(x, norm_mix[layer], odd_w_in[o].astype(BF16),
                                     na_q_gain[o], na_k_gain[o], sg_v_gain[o])
            bias = _na_bias_tables(na_rpb[o], s // GRID_W)
            yc = _neighbourhood(q.reshape(b, s, -1), k.reshape(b, s, -1), v.reshape(b, s, -1), bias)
            b_s = jnp.broadcast_to(sg_b[o][:, :, None], (SG_GROUPS, SG_CHUNK, width))
            x = _odd_out(yc, u.reshape(b, s, -1), vn.reshape(b, s, -1), sg_w[o].astype(BF16), b_s,
                         odd_w_out[o].astype(BF16), x.reshape(b, s, d))
        x = _ffn(x.reshape(t, d), norm_ffn2[layer], w_gate, w_up, w_down, layer, 1, gain_out=norm_out[layer])
    return x.reshape(b, s, d)
```

```python
import functools

import numpy as np
import jax
import jax.numpy as jnp
from jax import lax
from jax.experimental import pallas as pl
from jax.experimental.pallas import tpu as pltpu

F32 = jnp.float32
BF16 = jnp.bfloat16

D_MODEL = 2048
D_FF = 5632
HEAD_DIM = 128
POOL_WINDOWS = (2, 4, 8, 16)
POOL_WIDTH = 512
POOL_HALO = 8
DIL_PAIRS = ((128, 1), (512, 4), (2048, 16))
DIL_HEADS_PER_GROUP = 4
DIL_HEADS = 12
DIL_WIDTH = DIL_HEADS * HEAD_DIM
DIL_OUT = DIL_HEADS_PER_GROUP * HEAD_DIM
GRID_W = 64
NA_ROWS = 8
NA_COLS = 16
NA_HEADS = 8
NA_WIDTH = NA_HEADS * HEAD_DIM
NA_QROWS = 4
NA_KROWS = NA_QROWS + NA_ROWS
SG_CHUNK = 128
SG_GROUPS = 8
SG_WIDTH = 1024
RMS_EPS = 1e-6
NEG_INF = -1e30

VMEM_LIMIT_BYTES = 56 * 1024 * 1024

FFN_TM = 512
FFN_TF = 512
PROJ_TM = 512
PROJ_TN = 512
DIL_TQ = 256
OUT_TM = 512


def _params(*semantics):
    return pltpu.CompilerParams(dimension_semantics=semantics, vmem_limit_bytes=VMEM_LIMIT_BYTES)


def _resident(block_shape, index_map):
    return pl.BlockSpec(block_shape, index_map, pipeline_mode=pl.Buffered(1))


def _rms(x, gain):
    return x * lax.rsqrt(jnp.mean(x * x, axis=-1, keepdims=True) + RMS_EPS) * gain


def _ffn_kernel(x_ref, g_ref, wg_ref, wu_ref, wd_ref, gout_ref, o_ref, h_ref, *, final_norm):
    j = pl.program_id(1)

    @pl.when(j == 0)
    def _():
        x = x_ref[...]
        h_ref[...] = _rms(x, g_ref[...]).astype(BF16)
        o_ref[...] = x

    h = h_ref[...]
    gate = jnp.dot(h, wg_ref[...], preferred_element_type=F32)
    up = jnp.dot(h, wu_ref[...], preferred_element_type=F32)
    act = (gate * jax.nn.sigmoid(gate)) * up * 0.5
    o_ref[...] += jnp.dot(act.astype(BF16), wd_ref[...], preferred_element_type=F32)

    if final_norm:
        @pl.when(j == pl.num_programs(1) - 1)
        def _():
            o_ref[...] = _rms(o_ref[...], gout_ref[...])


def _ffn(x, gain, w_gate, w_up, w_down, layer, idx, gain_out=None):
    t = x.shape[0]
    final_norm = gain_out is not None
    if gain_out is None:
        gain_out = gain
    return pl.pallas_call(
        functools.partial(_ffn_kernel, final_norm=final_norm),
        grid=(t // FFN_TM, D_FF // FFN_TF),
        in_specs=[
            pl.BlockSpec((FFN_TM, D_MODEL), lambda i, j: (i, 0)),
            pl.BlockSpec((1, D_MODEL), lambda i, j: (0, 0)),
            pl.BlockSpec((None, None, D_MODEL, FFN_TF), lambda i, j: (layer, idx, 0, j)),
            pl.BlockSpec((None, None, D_MODEL, FFN_TF), lambda i, j: (layer, idx, 0, j)),
            pl.BlockSpec((None, None, FFN_TF, D_MODEL), lambda i, j: (layer, idx, j, 0)),
            pl.BlockSpec((1, D_MODEL), lambda i, j: (0, 0)),
        ],
        out_specs=pl.BlockSpec((FFN_TM, D_MODEL), lambda i, j: (i, 0)),
        out_shape=jax.ShapeDtypeStruct((t, D_MODEL), F32),
        scratch_shapes=[pltpu.VMEM((FFN_TM, D_MODEL), BF16)],
        compiler_params=_params("parallel", "arbitrary"),
        name="ffn_final" if final_norm else "ffn",
    )(x, gain.reshape(1, D_MODEL), w_gate, w_up, w_down, gain_out.reshape(1, D_MODEL))


def _head_norm_store(z, out_ref, col0, gain):
    for hd in range(PROJ_TN // HEAD_DIM):
        zh = z[:, hd * HEAD_DIM:(hd + 1) * HEAD_DIM]
        out_ref[:, col0 + hd * HEAD_DIM:col0 + (hd + 1) * HEAD_DIM] = _rms(zh, gain).astype(out_ref.dtype)


def _even_in_kernel(x_ref, g_ref, w_ref, qg_ref, kg_ref, a_ref, q_ref, k_ref, v_ref, h_ref):
    h_ref[...] = _rms(x_ref[...], g_ref[...]).astype(BF16)

    def chunk(c):
        return jnp.dot(h_ref[...], w_ref[:, c * PROJ_TN:(c + 1) * PROJ_TN], preferred_element_type=F32)

    a_ref[...] = chunk(0)
    per = DIL_WIDTH // PROJ_TN
    for c in range(per):
        _head_norm_store(chunk(1 + c), q_ref, c * PROJ_TN, qg_ref[...])
        _head_norm_store(chunk(1 + per + c), k_ref, c * PROJ_TN, kg_ref[...])
        v_ref[:, c * PROJ_TN:(c + 1) * PROJ_TN] = chunk(1 + 2 * per + c).astype(BF16)


def _even_in(x, gain, w_in, q_gain, k_gain):
    t = x.shape[0]
    n_in = w_in.shape[1]
    row = lambda i: (i, 0)
    const = lambda i: (0, 0)
    return pl.pallas_call(
        _even_in_kernel,
        grid=(t // PROJ_TM,),
        in_specs=[
            pl.BlockSpec((PROJ_TM, D_MODEL), row),
            pl.BlockSpec((1, D_MODEL), const),
            _resident((D_MODEL, n_in), const),
            pl.BlockSpec((1, HEAD_DIM), const),
            pl.BlockSpec((1, HEAD_DIM), const),
        ],
        out_specs=[
            pl.BlockSpec((PROJ_TM, POOL_WIDTH), row),
            pl.BlockSpec((PROJ_TM, DIL_WIDTH), row),
            pl.BlockSpec((PROJ_TM, DIL_WIDTH), row),
            pl.BlockSpec((PROJ_TM, DIL_WIDTH), row),
        ],
        out_shape=[
            jax.ShapeDtypeStruct((t, POOL_WIDTH), F32),
            jax.ShapeDtypeStruct((t, DIL_WIDTH), BF16),
            jax.ShapeDtypeStruct((t, DIL_WIDTH), BF16),
            jax.ShapeDtypeStruct((t, DIL_WIDTH), BF16),
        ],
        scratch_shapes=[pltpu.VMEM((PROJ_TM, D_MODEL), BF16)],
        compiler_params=_params("parallel"),
        name="even_in",
    )(x, gain.reshape(1, D_MODEL), w_in,
      (q_gain * HEAD_DIM ** -0.5).reshape(1, HEAD_DIM), k_gain.reshape(1, HEAD_DIM))


def _odd_in_kernel(x_ref, g_ref, w_ref, qg_ref, kg_ref, vg_ref, q_ref, k_ref, v_ref, u_ref, vn_ref,
                   h_ref, gv_ref):
    h_ref[...] = _rms(x_ref[...], g_ref[...]).astype(BF16)

    def chunk(c):
        return jnp.dot(h_ref[...], w_ref[:, c * PROJ_TN:(c + 1) * PROJ_TN], preferred_element_type=F32)

    per = NA_WIDTH // PROJ_TN
    for c in range(per):
        _head_norm_store(chunk(c), q_ref, c * PROJ_TN, qg_ref[...])
        _head_norm_store(chunk(per + c), k_ref, c * PROJ_TN, kg_ref[...])
        v_ref[:, c * PROJ_TN:(c + 1) * PROJ_TN] = chunk(2 * per + c).astype(BF16)
    per_sg = SG_WIDTH // PROJ_TN
    for c in range(per_sg):
        u_ref[:, c * PROJ_TN:(c + 1) * PROJ_TN] = jax.nn.gelu(chunk(3 * per + c))
        gv_ref[:, c * PROJ_TN:(c + 1) * PROJ_TN] = jax.nn.gelu(chunk(3 * per + per_sg + c))
    vn_ref[...] = _rms(gv_ref[...], vg_ref[...]).astype(BF16)


def _odd_in(x, gain, w_in, q_gain, k_gain, v_gain):
    t = x.shape[0]
    n_in = w_in.shape[1]
    row = lambda i: (i, 0)
    const = lambda i: (0, 0)
    return pl.pallas_call(
        _odd_in_kernel,
        grid=(t // PROJ_TM,),
        in_specs=[
            pl.BlockSpec((PROJ_TM, D_MODEL), row),
            pl.BlockSpec((1, D_MODEL), const),
            _resident((D_MODEL, n_in), const),
            pl.BlockSpec((1, HEAD_DIM), const),
            pl.BlockSpec((1, HEAD_DIM), const),
            pl.BlockSpec((1, SG_WIDTH), const),
        ],
        out_specs=[
            pl.BlockSpec((PROJ_TM, NA_WIDTH), row),
            pl.BlockSpec((PROJ_TM, NA_WIDTH), row),
            pl.BlockSpec((PROJ_TM, NA_WIDTH), row),
            pl.BlockSpec((PROJ_TM, SG_WIDTH), row),
            pl.BlockSpec((PROJ_TM, SG_WIDTH), row),
        ],
        out_shape=[
            jax.ShapeDtypeStruct((t, NA_WIDTH), BF16),
            jax.ShapeDtypeStruct((t, NA_WIDTH), BF16),
            jax.ShapeDtypeStruct((t, NA_WIDTH), BF16),
            jax.ShapeDtypeStruct((t, SG_WIDTH), F32),
            jax.ShapeDtypeStruct((t, SG_WIDTH), BF16),
        ],
        scratch_shapes=[pltpu.VMEM((PROJ_TM, D_MODEL), BF16), pltpu.VMEM((PROJ_TM, SG_WIDTH), F32)],
        compiler_params=_params("parallel"),
        name="odd_in",
    )(x, gain.reshape(1, D_MODEL), w_in,
      (q_gain * HEAD_DIM ** -0.5).reshape(1, HEAD_DIM), k_gain.reshape(1, HEAD_DIM),
      v_gain.reshape(1, SG_WIDTH))


def _dilated_kernel(slopes_ref, q0_ref, q1_ref, q2_ref, k0_ref, k1_ref, k2_ref, v0_ref, v1_ref, v2_ref,
                    o_ref, *, seq):
    hh = pl.program_id(1)
    q_start = pl.program_id(2) * DIL_TQ
    q_refs = (q0_ref, q1_ref, q2_ref)
    k_refs = (k0_ref, k1_ref, k2_ref)
    v_refs = (v0_ref, v1_ref, v2_ref)
    maxes, dens, outs = [], [], []
    for g, (window, dil) in enumerate(DIL_PAIRS):
        reach = window // 2
        n_keys = DIL_TQ + 2 * reach
        k_start = pl.multiple_of(jnp.clip(q_start - reach, 0, seq - n_keys), 64)
        k = k_refs[g][pl.ds(k_start, n_keys), :]
        v = v_refs[g][pl.ds(k_start, n_keys), :]
        s = lax.dot_general(q_refs[g][...], k, (((1,), (1,)), ((), ())), preferred_element_type=F32)
        row = lax.broadcasted_iota(jnp.int32, (DIL_TQ, n_keys), 0)
        col = lax.broadcasted_iota(jnp.int32, (DIL_TQ, n_keys), 1)
        rel = col - row + (k_start - q_start)
        dist = jnp.abs(rel)
        valid = (dist <= reach) & ((rel & (dil - 1)) == 0)
        slope = slopes_ref[g * DIL_HEADS_PER_GROUP + hh]
        s = jnp.where(valid, s - slope * dist.astype(F32), NEG_INF)
        m = jnp.max(s, axis=-1, keepdims=True)
        p = jnp.exp(s - m)
        maxes.append(m)
        dens.append(jnp.sum(p, axis=-1, keepdims=True))
        outs.append(jnp.dot(p.astype(BF16), v, preferred_element_type=F32))
    m_all = jnp.maximum(jnp.maximum(maxes[0], maxes[1]), maxes[2])
    num = jnp.zeros((DIL_TQ, HEAD_DIM), F32)
    den = jnp.zeros((DIL_TQ, 1), F32)
    for g in range(len(DIL_PAIRS)):
        w = jnp.exp(maxes[g] - m_all)
        num = num + w * outs[g]
        den = den + w * dens[g]
    o_ref[...] = (num / den).astype(o_ref.dtype)


def _dilated(q, k, v, slopes):
    b, s, _ = q.shape

    def q_spec(g):
        return pl.BlockSpec((None, DIL_TQ, HEAD_DIM), lambda bi, hh, i: (bi, i, g * DIL_HEADS_PER_GROUP + hh))

    def kv_spec(g):
        return pl.BlockSpec((None, s, HEAD_DIM), lambda bi, hh, i: (bi, 0, g * DIL_HEADS_PER_GROUP + hh))

    groups = range(len(DIL_PAIRS))
    return pl.pallas_call(
        functools.partial(_dilated_kernel, seq=s),
        grid=(b, DIL_HEADS_PER_GROUP, s // DIL_TQ),
        in_specs=[pl.BlockSpec(memory_space=pltpu.SMEM)]
        + [q_spec(g) for g in groups] + [kv_spec(g) for g in groups] + [kv_spec(g) for g in groups],
        out_specs=pl.BlockSpec((None, DIL_TQ, HEAD_DIM), lambda bi, hh, i: (bi, i, hh)),
        out_shape=jax.ShapeDtypeStruct((b, s, DIL_OUT), BF16),
        compiler_params=_params("parallel", "parallel", "arbitrary"),
        name="dilated_attn",
    )(slopes, q, q, q, k, k, k, v, v, v)


def _even_out_kernel(a_ref, prev_ref, next_ref, yb_ref, pw_ref, ps_ref, wo_ref, x_ref, o_ref,
                     ext_ref, y_ref, *, seq):
    i = pl.program_id(1)
    tm = OUT_TM
    ext_ref[0:POOL_HALO, :] = jnp.where(i == 0, 0.0, prev_ref[...])
    ext_ref[POOL_HALO:POOL_HALO + tm, :] = a_ref[...]
    ext_ref[POOL_HALO + tm:, :] = jnp.where(i == pl.num_programs(1) - 1, 0.0, next_ref[...])
    pos = i * tm + lax.broadcasted_iota(jnp.int32, (tm, 1), 0)
    for g, window in enumerate(POOL_WINDOWS):
        half = window // 2
        cols = slice(g * HEAD_DIM, (g + 1) * HEAD_DIM)
        total = ext_ref[POOL_HALO - half:POOL_HALO - half + tm, cols]
        for shift in range(-half + 1, half):
            total = total + ext_ref[POOL_HALO + shift:POOL_HALO + shift + tm, cols]
        count = (jnp.minimum(pos + half, seq) - jnp.maximum(pos - half, 0)).astype(F32)
        pooled = total / count - a_ref[:, cols]
        ya = jnp.dot(pooled.astype(BF16), pw_ref[g], preferred_element_type=F32) * ps_ref[:, cols]
        y_ref[:, cols] = ya.astype(BF16)
    y_ref[:, POOL_WIDTH:] = yb_ref[...]
    o_ref[...] = x_ref[...] + jnp.dot(y_ref[...], wo_ref[...], preferred_element_type=F32)


def _even_out(a, yb, pool_w, pool_scale, w_out, x):
    b, s, _ = x.shape
    tm = OUT_TM
    halo_blocks = tm // POOL_HALO
    tile = lambda bi, i: (bi, i, 0)
    return pl.pallas_call(
        functools.partial(_even_out_kernel, seq=s),
        grid=(b, s // tm),
        in_specs=[
            pl.BlockSpec((None, tm, POOL_WIDTH), tile),
            pl.BlockSpec((None, POOL_HALO, POOL_WIDTH),
                         lambda bi, i: (bi, jnp.maximum(i * halo_blocks - 1, 0), 0)),
            pl.BlockSpec((None, POOL_HALO, POOL_WIDTH),
                         lambda bi, i: (bi, jnp.minimum((i + 1) * halo_blocks, s // POOL_HALO - 1), 0)),
            pl.BlockSpec((None, tm, DIL_OUT), tile),
            _resident((len(POOL_WINDOWS), HEAD_DIM, HEAD_DIM), lambda bi, i: (0, 0, 0)),
            pl.BlockSpec((1, POOL_WIDTH), lambda bi, i: (0, 0)),
            _resident((POOL_WIDTH + DIL_OUT, D_MODEL), lambda bi, i: (0, 0)),
            pl.BlockSpec((None, tm, D_MODEL), tile),
        ],
        out_specs=pl.BlockSpec((None, tm, D_MODEL), tile),
        out_shape=jax.ShapeDtypeStruct((b, s, D_MODEL), F32),
        scratch_shapes=[pltpu.VMEM((tm + 2 * POOL_HALO, POOL_WIDTH), F32),
                        pltpu.VMEM((tm, POOL_WIDTH + DIL_OUT), BF16)],
        compiler_params=_params("parallel", "parallel"),
        name="even_out",
    )(a, a, a, yb, pool_w, pool_scale.reshape(1, POOL_WIDTH), w_out, x)


def _na_kernel(q_ref, k_ref, v_ref, bias_ref, o_ref, *, rows):
    rb = pl.program_id(2)
    n_keys = NA_KROWS * GRID_W
    k_row0 = jnp.clip(rb * NA_QROWS - NA_ROWS // 2, 0, rows - NA_KROWS)
    k_start = pl.multiple_of(k_row0 * GRID_W, GRID_W)
    k = k_ref[pl.ds(k_start, n_keys), :]
    v = v_ref[pl.ds(k_start, n_keys), :]
    s = lax.dot_general(q_ref[...], k, (((1,), (1,)), ((), ())), preferred_element_type=F32) + bias_ref[...]
    m = jnp.max(s, axis=-1, keepdims=True)
    p = jnp.exp(s - m)
    den = jnp.sum(p, axis=-1, keepdims=True)
    o_ref[...] = (jnp.dot(p.astype(BF16), v, preferred_element_type=F32) / den).astype(o_ref.dtype)


def _na_bias_tables(rpb, rows):
    n_heads = rpb.shape[0]
    n_blocks = rows // NA_QROWS
    c = np.arange(GRID_W)
    col_start = np.clip(c - NA_COLS // 2, 0, GRID_W - NA_COLS)
    col_ok = (c[None, :] >= col_start[:, None]) & (c[None, :] < col_start[:, None] + NA_COLS)
    edge = GRID_W - NA_COLS
    ext = jnp.concatenate([jnp.repeat(rpb[:, :, :1], edge, axis=2), rpb.astype(F32),
                           jnp.repeat(rpb[:, :, -1:], edge, axis=2)], axis=2)
    by_col = jnp.stack([ext[:, :, GRID_W - 1 - cq:2 * GRID_W - 1 - cq] for cq in range(GRID_W)], axis=2)
    by_col = jnp.where(col_ok[None, None], by_col, NEG_INF)
    masked = jnp.full((n_heads, GRID_W, GRID_W), NEG_INF, F32)
    tables = []
    for rb in (0, n_blocks // 2, n_blocks - 1):
        k_row0 = int(np.clip(rb * NA_QROWS - NA_ROWS // 2, 0, rows - NA_KROWS))
        per_qrow = []
        for qr in range(NA_QROWS):
            r = rb * NA_QROWS + qr
            row_start = int(np.clip(r - NA_ROWS // 2, 0, rows - NA_ROWS))
            per_krow = []
            for kl in range(NA_KROWS):
                kr = k_row0 + kl
                in_window = row_start <= kr < row_start + NA_ROWS
                per_krow.append(by_col[:, kr - r + NA_ROWS - 1] if in_window else masked)
            per_qrow.append(jnp.stack(per_krow, axis=2))
        bias = jnp.stack(per_qrow, axis=1)
        tables.append(bias.reshape(n_heads, NA_QROWS * GRID_W, NA_KROWS * GRID_W))
    return jnp.stack(tables)


def _neighbourhood(q, k, v, bias):
    b, s, _ = q.shape
    rows = s // GRID_W
    n_blocks = rows // NA_QROWS
    tq = NA_QROWS * GRID_W
    n_keys = NA_KROWS * GRID_W

    def bias_map(bi, h, rb):
        kind = jnp.where(rb == 0, 0, jnp.where(rb == n_blocks - 1, 2, 1))
        return (kind, h, 0, 0)

    return pl.pallas_call(
        functools.partial(_na_kernel, rows=rows),
        grid=(b, NA_HEADS, n_blocks),
        in_specs=[
            pl.BlockSpec((None, tq, HEAD_DIM), lambda bi, h, rb: (bi, rb, h)),
            pl.BlockSpec((None, s, HEAD_DIM), lambda bi, h, rb: (bi, 0, h)),
            pl.BlockSpec((None, s, HEAD_DIM), lambda bi, h, rb: (bi, 0, h)),
            pl.BlockSpec((None, None, tq, n_keys), bias_map),
        ],
        out_specs=pl.BlockSpec((None, tq, HEAD_DIM), lambda bi, h, rb: (bi, rb, h)),
        out_shape=jax.ShapeDtypeStruct((b, s, NA_WIDTH), BF16),
        compiler_params=_params("parallel", "parallel", "arbitrary"),
        name="neighbourhood_attn",
    )(q, k, v, bias)


def _odd_out_kernel(yc_ref, u_ref, vn_ref, ws_ref, bs_ref, wo_ref, x_ref, o_ref, y_ref):
    width = SG_WIDTH // SG_GROUPS
    for c in range(OUT_TM // SG_CHUNK):
        rows = slice(c * SG_CHUNK, (c + 1) * SG_CHUNK)
        for g in range(SG_GROUPS):
            cols = slice(g * width, (g + 1) * width)
            sv = jnp.dot(ws_ref[g], vn_ref[rows, cols], preferred_element_type=F32) + bs_ref[g]
            y_ref[rows, NA_WIDTH + g * width:NA_WIDTH + (g + 1) * width] = (u_ref[rows, cols] * sv).astype(BF16)
    y_ref[:, :NA_WIDTH] = yc_ref[...]
    o_ref[...] = x_ref[...] + jnp.dot(y_ref[...], wo_ref[...], preferred_element_type=F32)


def _odd_out(yc, u, vn, w_s, b_s, w_out, x):
    b, s, _ = x.shape
    tm = OUT_TM
    width = SG_WIDTH // SG_GROUPS
    tile = lambda bi, i: (bi, i, 0)
    return pl.pallas_call(
        _odd_out_kernel,
        grid=(b, s // tm),
        in_specs=[
            pl.BlockSpec((None, tm, NA_WIDTH), tile),
            pl.BlockSpec((None, tm, SG_WIDTH), tile),
            pl.BlockSpec((None, tm, SG_WIDTH), tile),
            _resident((SG_GROUPS, SG_CHUNK, SG_CHUNK), lambda bi, i: (0, 0, 0)),
            _resident((SG_GROUPS, SG_CHUNK, width), lambda bi, i: (0, 0, 0)),
            _resident((NA_WIDTH + SG_WIDTH, D_MODEL), lambda bi, i: (0, 0)),
            pl.BlockSpec((None, tm, D_MODEL), tile),
        ],
        out_specs=pl.BlockSpec((None, tm, D_MODEL), tile),
        out_shape=jax.ShapeDtypeStruct((b, s, D_MODEL), F32),
        scratch_shapes=[pltpu.VMEM((tm, NA_WIDTH + SG_WIDTH), BF16)],
        compiler_params=_params("parallel", "parallel"),
        name="odd_out",
    )(yc, u, vn, w_s, b_s, w_out, x)


def kernel(x, norm_ffn1, norm_mix, norm_ffn2, norm_out, ffn_w_gate, ffn_w_up, ffn_w_down, even_w_in, pool_w, pool_scale, dil_q_gain, dil_k_gain, even_w_out, odd_w_in, na_q_gain, na_k_gain, na_rpb, sg_v_gain, sg_w, sg_b, odd_w_out):
    b, s, d = x.shape
    t = b * s
    depth = norm_ffn1.shape[0]
    w_gate, w_up, w_down = (w.astype(BF16) for w in (ffn_w_gate, ffn_w_up, ffn_w_down))
    slopes = jnp.asarray(2.0 ** (-8.0 * np.arange(1, DIL_HEADS + 1) / DIL_HEADS), dtype=F32)
    width = SG_WIDTH // SG_GROUPS

    x = x.reshape(t, d)
    for layer in range(depth):
        x = _ffn(x, norm_ffn1[layer], w_gate, w_up, w_down, layer, 0)
        if layer % 2 == 0:
            e = layer // 2
            a, q, k, v = _even_in(x, norm_mix[layer], even_w_in[e].astype(BF16), dil_q_gain[e], dil_k_gain[e])
            yb = _dilated(q.reshape(b, s, -1), k.reshape(b, s, -1), v.reshape(b, s, -1), slopes)
            x = _even_out(a.reshape(b, s, -1), yb, pool_w[e].astype(BF16), pool_scale[e],
                          even_w_out[e].astype(BF16), x.reshape(b, s, d))
        else:
            o = layer // 2
            q, k, v, u, vn = _odd_in(x, norm_mix[layer], odd_w_in[o].astype(BF16),
                                     na_q_gain[o], na_k_gain[o], sg_v_gain[o])
            bias = _na_bias_tables(na_rpb[o], s // GRID_W)
            yc = _neighbourhood(q.reshape(b, s, -1), k.reshape(b, s, -1), v.reshape(b, s, -1), bias)
            b_s = jnp.broadcast_to(sg_b[o][:, :, None], (SG_GROUPS, SG_CHUNK, width))
            x = _odd_out(yc, u.reshape(b, s, -1), vn.reshape(b, s, -1), sg_w[o].astype(BF16), b_s,
                         odd_w_out[o].astype(BF16), x.reshape(b, s, d))
        x = _ffn(x.reshape(t, d), norm_ffn2[layer], w_gate, w_up, w_down, layer, 1, gain_out=norm_out[layer])
    return x.reshape(b, s, d)
```

```python
import functools

import numpy as np
import jax
import jax.numpy as jnp
from jax import lax
from jax.experimental import pallas as pl
from jax.experimental.pallas import tpu as pltpu

F32 = jnp.float32
BF16 = jnp.bfloat16

D_MODEL = 2048
D_FF = 5632
HEAD_DIM = 128
POOL_WINDOWS = (2, 4, 8, 16)
POOL_WIDTH = 512
POOL_HALO = 8
DIL_PAIRS = ((128, 1), (512, 4), (2048, 16))
DIL_HEADS_PER_GROUP = 4
DIL_HEADS = 12
DIL_WIDTH = DIL_HEADS * HEAD_DIM
DIL_OUT = DIL_HEADS_PER_GROUP * HEAD_DIM
GRID_W = 64
NA_ROWS = 8
NA_COLS = 16
NA_HEADS = 8
NA_WIDTH = NA_HEADS * HEAD_DIM
NA_QROWS = 4
NA_KROWS = NA_QROWS + NA_ROWS
SG_CHUNK = 128
SG_GROUPS = 8
SG_WIDTH = 1024
RMS_EPS = 1e-6
NEG_INF = -1e30

VMEM_LIMIT_BYTES = 56 * 1024 * 1024

FFN_TM = 1024
FFN_TF = 256
PROJ_TM = 512
PROJ_TN = 512
DIL_TQ = 256
OUT_TM = 512


def _params(*semantics):
    return pltpu.CompilerParams(dimension_semantics=semantics, vmem_limit_bytes=VMEM_LIMIT_BYTES)


def _resident(block_shape, index_map):
    return pl.BlockSpec(block_shape, index_map, pipeline_mode=pl.Buffered(1))


def _rms(x, gain):
    return x * lax.rsqrt(jnp.mean(x * x, axis=-1, keepdims=True) + RMS_EPS) * gain


def _ffn_kernel(x_ref, g_ref, wg_ref, wu_ref, wd_ref, gout_ref, o_ref, h_ref, *, final_norm):
    j = pl.program_id(2)

    @pl.when(j == 0)
    def _():
        x = x_ref[...]
        h_ref[...] = _rms(x, g_ref[...]).astype(BF16)
        o_ref[...] = x

    h = h_ref[...]
    gate = jnp.dot(h, wg_ref[...].astype(BF16), preferred_element_type=F32)
    up = jnp.dot(h, wu_ref[...].astype(BF16), preferred_element_type=F32)
    act = (gate * jax.nn.sigmoid(gate)) * up * 0.5
    o_ref[...] += jnp.dot(act.astype(BF16), wd_ref[...].astype(BF16), preferred_element_type=F32)

    if final_norm:
        @pl.when(j == pl.num_programs(2) - 1)
        def _():
            o_ref[...] = _rms(o_ref[...], gout_ref[...])


def _ffn(x, gain, w_gate, w_up, w_down, layer, idx, gain_out=None):
    b, s, _ = x.shape
    final_norm = gain_out is not None
    if gain_out is None:
        gain_out = gain
    tile = lambda bi, i, j: (bi, i, 0)
    const = lambda bi, i, j: (0, 0)
    return pl.pallas_call(
        functools.partial(_ffn_kernel, final_norm=final_norm),
        grid=(b, s // FFN_TM, D_FF // FFN_TF),
        in_specs=[
            pl.BlockSpec((None, FFN_TM, D_MODEL), tile, pipeline_mode=pl.Buffered(1)),
            pl.BlockSpec((1, D_MODEL), const),
            pl.BlockSpec((None, None, D_MODEL, FFN_TF), lambda bi, i, j: (layer, idx, 0, j)),
            pl.BlockSpec((None, None, D_MODEL, FFN_TF), lambda bi, i, j: (layer, idx, 0, j)),
            pl.BlockSpec((None, None, FFN_TF, D_MODEL), lambda bi, i, j: (layer, idx, j, 0)),
            pl.BlockSpec((1, D_MODEL), const),
        ],
        out_specs=pl.BlockSpec((None, FFN_TM, D_MODEL), tile),
        out_shape=jax.ShapeDtypeStruct((b, s, D_MODEL), F32),
        scratch_shapes=[pltpu.VMEM((FFN_TM, D_MODEL), BF16)],
        compiler_params=_params("parallel", "parallel", "arbitrary"),
        name="ffn_final" if final_norm else "ffn",
    )(x, gain.reshape(1, D_MODEL), w_gate, w_up, w_down, gain_out.reshape(1, D_MODEL))


def _head_norm_store(z, out_ref, col0, gain):
    for hd in range(PROJ_TN // HEAD_DIM):
        zh = z[:, hd * HEAD_DIM:(hd + 1) * HEAD_DIM]
        out_ref[:, col0 + hd * HEAD_DIM:col0 + (hd + 1) * HEAD_DIM] = _rms(zh, gain).astype(out_ref.dtype)


def _even_in_kernel(x_ref, g_ref, w_ref, qg_ref, kg_ref, a_ref, q_ref, k_ref, v_ref, h_ref):
    h_ref[...] = _rms(x_ref[...], g_ref[...]).astype(BF16)

    def chunk(c):
        return jnp.dot(h_ref[...], w_ref[:, c * PROJ_TN:(c + 1) * PROJ_TN], preferred_element_type=F32)

    a_ref[...] = chunk(0)
    per = DIL_WIDTH // PROJ_TN
    for c in range(per):
        _head_norm_store(chunk(1 + c), q_ref, c * PROJ_TN, qg_ref[...])
        _head_norm_store(chunk(1 + per + c), k_ref, c * PROJ_TN, kg_ref[...])
        v_ref[:, c * PROJ_TN:(c + 1) * PROJ_TN] = chunk(1 + 2 * per + c).astype(BF16)


def _even_in(x, gain, w_in, q_gain, k_gain):
    b, s, _ = x.shape
    n_in = w_in.shape[1]
    row = lambda bi, i: (bi, i, 0)
    const = lambda bi, i: (0, 0)
    return pl.pallas_call(
        _even_in_kernel,
        grid=(b, s // PROJ_TM),
        in_specs=[
            pl.BlockSpec((None, PROJ_TM, D_MODEL), row),
            pl.BlockSpec((1, D_MODEL), const),
            _resident((D_MODEL, n_in), const),
            pl.BlockSpec((1, HEAD_DIM), const),
            pl.BlockSpec((1, HEAD_DIM), const),
        ],
        out_specs=[
            pl.BlockSpec((None, PROJ_TM, POOL_WIDTH), row),
            pl.BlockSpec((None, PROJ_TM, DIL_WIDTH), row),
            pl.BlockSpec((None, PROJ_TM, DIL_WIDTH), row),
            pl.BlockSpec((None, PROJ_TM, DIL_WIDTH), row),
        ],
        out_shape=[
            jax.ShapeDtypeStruct((b, s, POOL_WIDTH), F32),
            jax.ShapeDtypeStruct((b, s, DIL_WIDTH), BF16),
            jax.ShapeDtypeStruct((b, s, DIL_WIDTH), BF16),
            jax.ShapeDtypeStruct((b, s, DIL_WIDTH), BF16),
        ],
        scratch_shapes=[pltpu.VMEM((PROJ_TM, D_MODEL), BF16)],
        compiler_params=_params("parallel", "parallel"),
        name="even_in",
    )(x, gain.reshape(1, D_MODEL), w_in,
      (q_gain * HEAD_DIM ** -0.5).reshape(1, HEAD_DIM), k_gain.reshape(1, HEAD_DIM))


def _odd_in_kernel(x_ref, g_ref, w_ref, qg_ref, kg_ref, vg_ref, q_ref, k_ref, v_ref, u_ref, vn_ref,
                   h_ref, gv_ref):
    h_ref[...] = _rms(x_ref[...], g_ref[...]).astype(BF16)

    def chunk(c):
        return jnp.dot(h_ref[...], w_ref[:, c * PROJ_TN:(c + 1) * PROJ_TN], preferred_element_type=F32)

    per = NA_WIDTH // PROJ_TN
    for c in range(per):
        _head_norm_store(chunk(c), q_ref, c * PROJ_TN, qg_ref[...])
        _head_norm_store(chunk(per + c), k_ref, c * PROJ_TN, kg_ref[...])
        v_ref[:, c * PROJ_TN:(c + 1) * PROJ_TN] = chunk(2 * per + c).astype(BF16)
    per_sg = SG_WIDTH // PROJ_TN
    for c in range(per_sg):
        u_ref[:, c * PROJ_TN:(c + 1) * PROJ_TN] = jax.nn.gelu(chunk(3 * per + c))
        gv_ref[:, c * PROJ_TN:(c + 1) * PROJ_TN] = jax.nn.gelu(chunk(3 * per + per_sg + c))
    vn_ref[...] = _rms(gv_ref[...], vg_ref[...]).astype(BF16)


def _odd_in(x, gain, w_in, q_gain, k_gain, v_gain):
    b, s, _ = x.shape
    n_in = w_in.shape[1]
    row = lambda bi, i: (bi, i, 0)
    const = lambda bi, i: (0, 0)
    return pl.pallas_call(
        _odd_in_kernel,
        grid=(b, s // PROJ_TM),
        in_specs=[
            pl.BlockSpec((None, PROJ_TM, D_MODEL), row),
            pl.BlockSpec((1, D_MODEL), const),
            _resident((D_MODEL, n_in), const),
            pl.BlockSpec((1, HEAD_DIM), const),
            pl.BlockSpec((1, HEAD_DIM), const),
            pl.BlockSpec((1, SG_WIDTH), const),
        ],
        out_specs=[
            pl.BlockSpec((None, PROJ_TM, NA_WIDTH), row),
            pl.BlockSpec((None, PROJ_TM, NA_WIDTH), row),
            pl.BlockSpec((None, PROJ_TM, NA_WIDTH), row),
            pl.BlockSpec((None, PROJ_TM, SG_WIDTH), row),
            pl.BlockSpec((None, PROJ_TM, SG_WIDTH), row),
        ],
        out_shape=[
            jax.ShapeDtypeStruct((b, s, NA_WIDTH), BF16),
            jax.ShapeDtypeStruct((b, s, NA_WIDTH), BF16),
            jax.ShapeDtypeStruct((b, s, NA_WIDTH), BF16),
            jax.ShapeDtypeStruct((b, s, SG_WIDTH), F32),
            jax.ShapeDtypeStruct((b, s, SG_WIDTH), BF16),
        ],
        scratch_shapes=[pltpu.VMEM((PROJ_TM, D_MODEL), BF16), pltpu.VMEM((PROJ_TM, SG_WIDTH), F32)],
        compiler_params=_params("parallel", "parallel"),
        name="odd_in",
    )(x, gain.reshape(1, D_MODEL), w_in,
      (q_gain * HEAD_DIM ** -0.5).reshape(1, HEAD_DIM), k_gain.reshape(1, HEAD_DIM),
      v_gain.reshape(1, SG_WIDTH))


def _dilated_kernel(slopes_ref, q0_ref, q1_ref, q2_ref, k0_ref, k1_ref, k2_ref, v0_ref, v1_ref, v2_ref,
                    o_ref, *, seq):
    hh = pl.program_id(1)
    q_start = pl.program_id(2) * DIL_TQ
    q_refs = (q0_ref, q1_ref, q2_ref)
    k_refs = (k0_ref, k1_ref, k2_ref)
    v_refs = (v0_ref, v1_ref, v2_ref)
    maxes, dens, outs = [], [], []
    for g, (window, dil) in enumerate(DIL_PAIRS):
        reach = window // 2
        n_keys = DIL_TQ + 2 * reach
        k_start = pl.multiple_of(jnp.clip(q_start - reach, 0, seq - n_keys), 64)
        k = k_refs[g][pl.ds(k_start, n_keys), :]
        v = v_refs[g][pl.ds(k_start, n_keys), :]
        s = lax.dot_general(q_refs[g][...], k, (((1,), (1,)), ((), ())), preferred_element_type=F32)
        row = lax.broadcasted_iota(jnp.int32, (DIL_TQ, n_keys), 0)
        col = lax.broadcasted_iota(jnp.int32, (DIL_TQ, n_keys), 1)
        rel = col - row + (k_start - q_start)
        dist = jnp.abs(rel)
        valid = (dist <= reach) & ((rel & (dil - 1)) == 0)
        slope = slopes_ref[g * DIL_HEADS_PER_GROUP + hh]
        s = jnp.where(valid, s - slope * dist.astype(F32), NEG_INF)
        m = jnp.max(s, axis=-1, keepdims=True)
        p = jnp.exp(s - m)
        maxes.append(m)
        dens.append(jnp.sum(p, axis=-1, keepdims=True))
        outs.append(jnp.dot(p.astype(BF16), v, preferred_element_type=F32))
    m_all = jnp.maximum(jnp.maximum(maxes[0], maxes[1]), maxes[2])
    num = jnp.zeros((DIL_TQ, HEAD_DIM), F32)
    den = jnp.zeros((DIL_TQ, 1), F32)
    for g in range(len(DIL_PAIRS)):
        w = jnp.exp(maxes[g] - m_all)
        num = num + w * outs[g]
        den = den + w * dens[g]
    o_ref[...] = (num / den).astype(o_ref.dtype)


def _dilated(q, k, v, slopes):
    b, s, _ = q.shape

    def q_spec(g):
        return pl.BlockSpec((None, DIL_TQ, HEAD_DIM), lambda bi, hh, i: (bi, i, g * DIL_HEADS_PER_GROUP + hh))

    def kv_spec(g):
        return pl.BlockSpec((None, s, HEAD_DIM), lambda bi, hh, i: (bi, 0, g * DIL_HEADS_PER_GROUP + hh))

    groups = range(len(DIL_PAIRS))
    return pl.pallas_call(
        functools.partial(_dilated_kernel, seq=s),
        grid=(b, DIL_HEADS_PER_GROUP, s // DIL_TQ),
        in_specs=[pl.BlockSpec(memory_space=pltpu.SMEM)]
        + [q_spec(g) for g in groups] + [kv_spec(g) for g in groups] + [kv_spec(g) for g in groups],
        out_specs=pl.BlockSpec((None, DIL_TQ, HEAD_DIM), lambda bi, hh, i: (bi, i, hh)),
        out_shape=jax.ShapeDtypeStruct((b, s, DIL_OUT), BF16),
        compiler_params=_params("parallel", "parallel", "arbitrary"),
        name="dilated_attn",
    )(slopes, q, q, q, k, k, k, v, v, v)


def _even_out_kernel(a_ref, prev_ref, next_ref, yb_ref, pw_ref, ps_ref, wo_ref, x_ref, o_ref,
                     ext_ref, y_ref, *, seq):
    i = pl.program_id(1)
    tm = OUT_TM
    ext_ref[0:POOL_HALO, :] = jnp.where(i == 0, 0.0, prev_ref[...])
    ext_ref[POOL_HALO:POOL_HALO + tm, :] = a_ref[...]
    ext_ref[POOL_HALO + tm:, :] = jnp.where(i == pl.num_programs(1) - 1, 0.0, next_ref[...])
    pos = i * tm + lax.broadcasted_iota(jnp.int32, (tm, 1), 0)
    for g, window in enumerate(POOL_WINDOWS):
        half = window // 2
        cols = slice(g * HEAD_DIM, (g + 1) * HEAD_DIM)
        total = ext_ref[POOL_HALO - half:POOL_HALO - half + tm, cols]
        for shift in range(-half + 1, half):
            total = total + ext_ref[POOL_HALO + shift:POOL_HALO + shift + tm, cols]
        count = (jnp.minimum(pos + half, seq) - jnp.maximum(pos - half, 0)).astype(F32)
        pooled = total / count - a_ref[:, cols]
        ya = jnp.dot(pooled.astype(BF16), pw_ref[g], preferred_element_type=F32) * ps_ref[:, cols]
        y_ref[:, cols] = ya.astype(BF16)
    y_ref[:, POOL_WIDTH:] = yb_ref[...]
    o_ref[...] = x_ref[...] + jnp.dot(y_ref[...], wo_ref[...], preferred_element_type=F32)


def _even_out(a, yb, pool_w, pool_scale, w_out, x):
    b, s, _ = x.shape
    tm = OUT_TM
    halo_blocks = tm // POOL_HALO
    tile = lambda bi, i: (bi, i, 0)
    return pl.pallas_call(
        functools.partial(_even_out_kernel, seq=s),
        grid=(b, s // tm),
        in_specs=[
            pl.BlockSpec((None, tm, POOL_WIDTH), tile),
            pl.BlockSpec((None, POOL_HALO, POOL_WIDTH),
                         lambda bi, i: (bi, jnp.maximum(i * halo_blocks - 1, 0), 0)),
            pl.BlockSpec((None, POOL_HALO, POOL_WIDTH),
                         lambda bi, i: (bi, jnp.minimum((i + 1) * halo_blocks, s // POOL_HALO - 1), 0)),
            pl.BlockSpec((None, tm, DIL_OUT), tile),
            _resident((len(POOL_WINDOWS), HEAD_DIM, HEAD_DIM), lambda bi, i: (0, 0, 0)),
            pl.BlockSpec((1, POOL_WIDTH), lambda bi, i: (0, 0)),
            _resident((POOL_WIDTH + DIL_OUT, D_MODEL), lambda bi, i: (0, 0)),
            pl.BlockSpec((None, tm, D_MODEL), tile),
        ],
        out_specs=pl.BlockSpec((None, tm, D_MODEL), tile),
        out_shape=jax.ShapeDtypeStruct((b, s, D_MODEL), F32),
        scratch_shapes=[pltpu.VMEM((tm + 2 * POOL_HALO, POOL_WIDTH), F32),
                        pltpu.VMEM((tm, POOL_WIDTH + DIL_OUT), BF16)],
        compiler_params=_params("parallel", "parallel"),
        name="even_out",
    )(a, a, a, yb, pool_w, pool_scale.reshape(1, POOL_WIDTH), w_out, x)


def _na_kernel(q_ref, k_ref, v_ref, bias_ref, o_ref, *, rows):
    rb = pl.program_id(2)
    n_keys = NA_KROWS * GRID_W
    k_row0 = jnp.clip(rb * NA_QROWS - NA_ROWS // 2, 0, rows - NA_KROWS)
    k_start = pl.multiple_of(k_row0 * GRID_W, GRID_W)
    k = k_ref[pl.ds(k_start, n_keys), :]
    v = v_ref[pl.ds(k_start, n_keys), :]
    s = lax.dot_general(q_ref[...], k, (((1,), (1,)), ((), ())), preferred_element_type=F32) + bias_ref[...]
    m = jnp.max(s, axis=-1, keepdims=True)
    p = jnp.exp(s - m)
    den = jnp.sum(p, axis=-1, keepdims=True)
    o_ref[...] = (jnp.dot(p.astype(BF16), v, preferred_element_type=F32) / den).astype(o_ref.dtype)


def _na_bias_tables(rpb, rows):
    n_heads = rpb.shape[0]
    n_blocks = rows // NA_QROWS
    c = np.arange(GRID_W)
    col_start = np.clip(c - NA_COLS // 2, 0, GRID_W - NA_COLS)
    col_ok = (c[None, :] >= col_start[:, None]) & (c[None, :] < col_start[:, None] + NA_COLS)
    edge = GRID_W - NA_COLS
    ext = jnp.concatenate([jnp.repeat(rpb[:, :, :1], edge, axis=2), rpb.astype(F32),
                           jnp.repeat(rpb[:, :, -1:], edge, axis=2)], axis=2)
    by_col = jnp.stack([ext[:, :, GRID_W - 1 - cq:2 * GRID_W - 1 - cq] for cq in range(GRID_W)], axis=2)
    by_col = jnp.where(col_ok[None, None], by_col, NEG_INF)
    masked = jnp.full((n_heads, GRID_W, GRID_W), NEG_INF, F32)
    tables = []
    for rb in (0, n_blocks // 2, n_blocks - 1):
        k_row0 = int(np.clip(rb * NA_QROWS - NA_ROWS // 2, 0, rows - NA_KROWS))
        per_qrow = []
        for qr in range(NA_QROWS):
            r = rb * NA_QROWS + qr
            row_start = int(np.clip(r - NA_ROWS // 2, 0, rows - NA_ROWS))
            per_krow = []
            for kl in range(NA_KROWS):
                kr = k_row0 + kl
                in_window = row_start <= kr < row_start + NA_ROWS
                per_krow.append(by_col[:, kr - r + NA_ROWS - 1] if in_window else masked)
            per_qrow.append(jnp.stack(per_krow, axis=2))
        bias = jnp.stack(per_qrow, axis=1)
        tables.append(bias.reshape(n_heads, NA_QROWS * GRID_W, NA_KROWS * GRID_W))
    return jnp.stack(tables)


def _neighbourhood(q, k, v, bias):
    b, s, _ = q.shape
    rows = s // GRID_W
    n_blocks = rows // NA_QROWS
    tq = NA_QROWS * GRID_W
    n_keys = NA_KROWS * GRID_W

    def bias_map(bi, h, rb):
        kind = jnp.where(rb == 0, 0, jnp.where(rb == n_blocks - 1, 2, 1))
        return (kind, h, 0, 0)

    return pl.pallas_call(
        functools.partial(_na_kernel, rows=rows),
        grid=(b, NA_HEADS, n_blocks),
        in_specs=[
            pl.BlockSpec((None, tq, HEAD_DIM), lambda bi, h, rb: (bi, rb, h)),
            pl.BlockSpec((None, s, HEAD_DIM), lambda bi, h, rb: (bi, 0, h)),
            pl.BlockSpec((None, s, HEAD_DIM), lambda bi, h, rb: (bi, 0, h)),
            pl.BlockSpec((None, None, tq, n_keys), bias_map),
        ],
        out_specs=pl.BlockSpec((None, tq, HEAD_DIM), lambda bi, h, rb: (bi, rb, h)),
        out_shape=jax.ShapeDtypeStruct((b, s, NA_WIDTH), BF16),
        compiler_params=_params("parallel", "parallel", "arbitrary"),
        name="neighbourhood_attn",
    )(q, k, v, bias)


def _odd_out_kernel(yc_ref, u_ref, vn_ref, ws_ref, bs_ref, wo_ref, x_ref, o_ref, y_ref):
    width = SG_WIDTH // SG_GROUPS
    for c in range(OUT_TM // SG_CHUNK):
        rows = slice(c * SG_CHUNK, (c + 1) * SG_CHUNK)
        for g in range(SG_GROUPS):
            cols = slice(g * width, (g + 1) * width)
            sv = jnp.dot(ws_ref[g], vn_ref[rows, cols], preferred_element_type=F32) + bs_ref[g]
            y_ref[rows, NA_WIDTH + g * width:NA_WIDTH + (g + 1) * width] = (u_ref[rows, cols] * sv).astype(BF16)
    y_ref[:, :NA_WIDTH] = yc_ref[...]
    o_ref[...] = x_ref[...] + jnp.dot(y_ref[...], wo_ref[...], preferred_element_type=F32)


def _odd_out(yc, u, vn, w_s, b_s, w_out, x):
    b, s, _ = x.shape
    tm = OUT_TM
    width = SG_WIDTH // SG_GROUPS
    tile = lambda bi, i: (bi, i, 0)
    return pl.pallas_call(
        _odd_out_kernel,
        grid=(b, s // tm),
        in_specs=[
            pl.BlockSpec((None, tm, NA_WIDTH), tile),
            pl.BlockSpec((None, tm, SG_WIDTH), tile),
            pl.BlockSpec((None, tm, SG_WIDTH), tile),
            _resident((SG_GROUPS, SG_CHUNK, SG_CHUNK), lambda bi, i: (0, 0, 0)),
            _resident((SG_GROUPS, SG_CHUNK, width), lambda bi, i: (0, 0, 0)),
            _resident((NA_WIDTH + SG_WIDTH, D_MODEL), lambda bi, i: (0, 0)),
            pl.BlockSpec((None, tm, D_MODEL), tile),
        ],
        out_specs=pl.BlockSpec((None, tm, D_MODEL), tile),
        out_shape=jax.ShapeDtypeStruct((b, s, D_MODEL), F32),
        scratch_shapes=[pltpu.VMEM((tm, NA_WIDTH + SG_WIDTH), BF16)],
        compiler_params=_params("parallel", "parallel"),
        name="odd_out",
    )(yc, u, vn, w_s, b_s, w_out, x)


def kernel(x, norm_ffn1, norm_mix, norm_ffn2, norm_out, ffn_w_gate, ffn_w_up, ffn_w_down, even_w_in, pool_w, pool_scale, dil_q_gain, dil_k_gain, even_w_out, odd_w_in, na_q_gain, na_k_gain, na_rpb, sg_v_gain, sg_w, sg_b, odd_w_out):
    s = x.shape[1]
    depth = norm_ffn1.shape[0]
    slopes = jnp.asarray(2.0 ** (-8.0 * np.arange(1, DIL_HEADS + 1) / DIL_HEADS), dtype=F32)
    width = SG_WIDTH // SG_GROUPS

    for layer in range(depth):
        x = _ffn(x, norm_ffn1[layer], ffn_w_gate, ffn_w_up, ffn_w_down, layer, 0)
        if layer % 2 == 0:
            e = layer // 2
            a, q, k, v = _even_in(x, norm_mix[layer], even_w_in[e].astype(BF16), dil_q_gain[e], dil_k_gain[e])
            yb = _dilated(q, k, v, slopes)
            x = _even_out(a, yb, pool_w[e].astype(BF16), pool_scale[e], even_w_out[e].astype(BF16), x)
        else:
            o = layer // 2
            q, k, v, u, vn = _odd_in(x, norm_mix[layer], odd_w_in[o].astype(BF16),
                                     na_q_gain[o], na_k_gain[o], sg_v_gain[o])
            bias = _na_bias_tables(na_rpb[o], s // GRID_W)
            yc = _neighbourhood(q, k, v, bias)
            b_s = jnp.broadcast_to(sg_b[o][:, :, None], (SG_GROUPS, SG_CHUNK, width))
            x = _odd_out(yc, u, vn, sg_w[o].astype(BF16), b_s, odd_w_out[o].astype(BF16), x)
        x = _ffn(x, norm_ffn2[layer], ffn_w_gate, ffn_w_up, ffn_w_down, layer, 1, gain_out=norm_out[layer])
    return x
```

```python
import functools

import numpy as np
import jax
import jax.numpy as jnp
from jax import lax
from jax.experimental import pallas as pl
from jax.experimental.pallas import tpu as pltpu

F32 = jnp.float32
BF16 = jnp.bfloat16

D_MODEL = 2048
D_FF = 5632
HEAD_DIM = 128
POOL_WINDOWS = (2, 4, 8, 16)
POOL_WIDTH = 512
POOL_HALO = 8
DIL_PAIRS = ((128, 1), (512, 4), (2048, 16))
DIL_HEADS_PER_GROUP = 4
DIL_HEADS = 12
DIL_WIDTH = DIL_HEADS * HEAD_DIM
DIL_OUT = DIL_HEADS_PER_GROUP * HEAD_DIM
GRID_W = 64
NA_ROWS = 8
NA_COLS = 16
NA_HEADS = 8
NA_WIDTH = NA_HEADS * HEAD_DIM
NA_QROWS = 4
NA_KROWS = NA_QROWS + NA_ROWS
SG_CHUNK = 128
SG_GROUPS = 8
SG_WIDTH = 1024
RMS_EPS = 1e-6
NEG_INF = -1e30

VMEM_LIMIT_BYTES = 56 * 1024 * 1024

FFN_TM = 1024
FFN_TF = 256
PROJ_TM = 512
PROJ_TN = 512
DIL_TQ = 256
DIL_TA = 2048
DIL_RADIUS = 64
DIL_UNROLL = 4
OUT_TM = 512


def _params(*semantics):
    return pltpu.CompilerParams(dimension_semantics=semantics, vmem_limit_bytes=VMEM_LIMIT_BYTES)


def _resident(block_shape, index_map):
    return pl.BlockSpec(block_shape, index_map, pipeline_mode=pl.Buffered(1))


def _rms(x, gain):
    return x * lax.rsqrt(jnp.mean(x * x, axis=-1, keepdims=True) + RMS_EPS) * gain


def _ffn_kernel(x_ref, g_ref, wg_ref, wu_ref, wd_ref, gout_ref, o_ref, h_ref, *, final_norm):
    j = pl.program_id(2)

    @pl.when(j == 0)
    def _():
        x = x_ref[...]
        h_ref[...] = _rms(x, g_ref[...]).astype(BF16)
        o_ref[...] = x

    h = h_ref[...]
    gate = jnp.dot(h, wg_ref[...].astype(BF16), preferred_element_type=F32)
    up = jnp.dot(h, wu_ref[...].astype(BF16), preferred_element_type=F32)
    act = (gate * jax.nn.sigmoid(gate)) * up * 0.5
    o_ref[...] += jnp.dot(act.astype(BF16), wd_ref[...].astype(BF16), preferred_element_type=F32)

    if final_norm:
        @pl.when(j == pl.num_programs(2) - 1)
        def _():
            o_ref[...] = _rms(o_ref[...], gout_ref[...])


def _ffn(x, gain, w_gate, w_up, w_down, layer, idx, gain_out=None):
    b, s, _ = x.shape
    final_norm = gain_out is not None
    if gain_out is None:
        gain_out = gain
    tile = lambda bi, i, j: (bi, i, 0)
    const = lambda bi, i, j: (0, 0)
    return pl.pallas_call(
        functools.partial(_ffn_kernel, final_norm=final_norm),
        grid=(b, s // FFN_TM, D_FF // FFN_TF),
        in_specs=[
            pl.BlockSpec((None, FFN_TM, D_MODEL), tile, pipeline_mode=pl.Buffered(1)),
            pl.BlockSpec((1, D_MODEL), const),
            pl.BlockSpec((None, None, D_MODEL, FFN_TF), lambda bi, i, j: (layer, idx, 0, j)),
            pl.BlockSpec((None, None, D_MODEL, FFN_TF), lambda bi, i, j: (layer, idx, 0, j)),
            pl.BlockSpec((None, None, FFN_TF, D_MODEL), lambda bi, i, j: (layer, idx, j, 0)),
            pl.BlockSpec((1, D_MODEL), const),
        ],
        out_specs=pl.BlockSpec((None, FFN_TM, D_MODEL), tile),
        out_shape=jax.ShapeDtypeStruct((b, s, D_MODEL), F32),
        scratch_shapes=[pltpu.VMEM((FFN_TM, D_MODEL), BF16)],
        compiler_params=_params("parallel", "parallel", "arbitrary"),
        name="ffn_final" if final_norm else "ffn",
    )(x, gain.reshape(1, D_MODEL), w_gate, w_up, w_down, gain_out.reshape(1, D_MODEL))


def _head_norm_store(z, out_ref, col0, gain):
    for hd in range(PROJ_TN // HEAD_DIM):
        zh = z[:, hd * HEAD_DIM:(hd + 1) * HEAD_DIM]
        out_ref[:, col0 + hd * HEAD_DIM:col0 + (hd + 1) * HEAD_DIM] = _rms(zh, gain).astype(out_ref.dtype)


def _store_by_residue(z, out_ref, stage_ref, dil, gain=None):
    tm = z.shape[0]
    for hd in range(z.shape[1] // HEAD_DIM):
        cols = slice(hd * HEAD_DIM, (hd + 1) * HEAD_DIM)
        zh = z[:, cols]
        if gain is not None:
            zh = _rms(zh, gain)
        if dil == 1:
            out_ref[0, :, cols] = zh.astype(BF16)
        else:
            stage_ref[hd] = zh
            for r in range(dil):
                out_ref[r, :, cols] = stage_ref[hd, pl.ds(r, tm // dil, stride=dil), :].astype(BF16)


def _even_in_kernel(x_ref, g_ref, w_ref, qg_ref, kg_ref, a_ref, *rest):
    n_groups = len(DIL_PAIRS)
    q_refs, k_refs, v_refs = rest[:n_groups], rest[n_groups:2 * n_groups], rest[2 * n_groups:3 * n_groups]
    h_ref, stage_ref = rest[3 * n_groups:]
    h_ref[...] = _rms(x_ref[...], g_ref[...]).astype(BF16)

    def chunk(c):
        return jnp.dot(h_ref[...], w_ref[:, c * PROJ_TN:(c + 1) * PROJ_TN], preferred_element_type=F32)

    a_ref[...] = chunk(0)
    for g, (_, dil) in enumerate(DIL_PAIRS):
        _store_by_residue(chunk(1 + g), q_refs[g], stage_ref, dil, qg_ref[...])
        _store_by_residue(chunk(1 + n_groups + g), k_refs[g], stage_ref, dil, kg_ref[...])
        _store_by_residue(chunk(1 + 2 * n_groups + g), v_refs[g], stage_ref, dil)


def _even_in(x, gain, w_in, q_gain, k_gain):
    b, s, _ = x.shape
    n_in = w_in.shape[1]
    assert PROJ_TN == DIL_OUT
    row = lambda bi, i: (bi, i, 0)
    const = lambda bi, i: (0, 0)
    group_specs = [pl.BlockSpec((None, dil, PROJ_TM // dil, DIL_OUT), lambda bi, i: (bi, 0, i, 0))
                   for _, dil in DIL_PAIRS]
    group_shapes = [jax.ShapeDtypeStruct((b, dil, s // dil, DIL_OUT), BF16) for _, dil in DIL_PAIRS]
    outs = pl.pallas_call(
        _even_in_kernel,
        grid=(b, s // PROJ_TM),
        in_specs=[
            pl.BlockSpec((None, PROJ_TM, D_MODEL), row),
            pl.BlockSpec((1, D_MODEL), const),
            _resident((D_MODEL, n_in), const),
            pl.BlockSpec((1, HEAD_DIM), const),
            pl.BlockSpec((1, HEAD_DIM), const),
        ],
        out_specs=[pl.BlockSpec((None, PROJ_TM, POOL_WIDTH), row)] + group_specs * 3,
        out_shape=[jax.ShapeDtypeStruct((b, s, POOL_WIDTH), F32)] + group_shapes * 3,
        scratch_shapes=[pltpu.VMEM((PROJ_TM, D_MODEL), BF16),
                        pltpu.VMEM((DIL_HEADS_PER_GROUP, PROJ_TM, HEAD_DIM), F32)],
        compiler_params=_params("parallel", "parallel"),
        name="even_in",
    )(x, gain.reshape(1, D_MODEL), w_in,
      (q_gain * HEAD_DIM ** -0.5).reshape(1, HEAD_DIM), k_gain.reshape(1, HEAD_DIM))
    n_groups = len(DIL_PAIRS)
    return outs[0], outs[1:1 + n_groups], outs[1 + n_groups:1 + 2 * n_groups], outs[1 + 2 * n_groups:]


def _odd_in_kernel(x_ref, g_ref, w_ref, qg_ref, kg_ref, vg_ref, q_ref, k_ref, v_ref, u_ref, vn_ref,
                   h_ref, gv_ref):
    h_ref[...] = _rms(x_ref[...], g_ref[...]).astype(BF16)

    def chunk(c):
        return jnp.dot(h_ref[...], w_ref[:, c * PROJ_TN:(c + 1) * PROJ_TN], preferred_element_type=F32)

    per = NA_WIDTH // PROJ_TN
    for c in range(per):
        _head_norm_store(chunk(c), q_ref, c * PROJ_TN, qg_ref[...])
        _head_norm_store(chunk(per + c), k_ref, c * PROJ_TN, kg_ref[...])
        v_ref[:, c * PROJ_TN:(c + 1) * PROJ_TN] = chunk(2 * per + c).astype(BF16)
    per_sg = SG_WIDTH // PROJ_TN
    for c in range(per_sg):
        u_ref[:, c * PROJ_TN:(c + 1) * PROJ_TN] = jax.nn.gelu(chunk(3 * per + c))
        gv_ref[:, c * PROJ_TN:(c + 1) * PROJ_TN] = jax.nn.gelu(chunk(3 * per + per_sg + c))
    vn_ref[...] = _rms(gv_ref[...], vg_ref[...]).astype(BF16)


def _odd_in(x, gain, w_in, q_gain, k_gain, v_gain):
    b, s, _ = x.shape
    n_in = w_in.shape[1]
    row = lambda bi, i: (bi, i, 0)
    const = lambda bi, i: (0, 0)
    return pl.pallas_call(
        _odd_in_kernel,
        grid=(b, s // PROJ_TM),
        in_specs=[
            pl.BlockSpec((None, PROJ_TM, D_MODEL), row),
            pl.BlockSpec((1, D_MODEL), const),
            _resident((D_MODEL, n_in), const),
            pl.BlockSpec((1, HEAD_DIM), const),
            pl.BlockSpec((1, HEAD_DIM), const),
            pl.BlockSpec((1, SG_WIDTH), const),
        ],
        out_specs=[
            pl.BlockSpec((None, PROJ_TM, NA_WIDTH), row),
            pl.BlockSpec((None, PROJ_TM, NA_WIDTH), row),
            pl.BlockSpec((None, PROJ_TM, NA_WIDTH), row),
            pl.BlockSpec((None, PROJ_TM, SG_WIDTH), row),
            pl.BlockSpec((None, PROJ_TM, SG_WIDTH), row),
        ],
        out_shape=[
            jax.ShapeDtypeStruct((b, s, NA_WIDTH), BF16),
            jax.ShapeDtypeStruct((b, s, NA_WIDTH), BF16),
            jax.ShapeDtypeStruct((b, s, NA_WIDTH), BF16),
            jax.ShapeDtypeStruct((b, s, SG_WIDTH), F32),
            jax.ShapeDtypeStruct((b, s, SG_WIDTH), BF16),
        ],
        scratch_shapes=[pltpu.VMEM((PROJ_TM, D_MODEL), BF16), pltpu.VMEM((PROJ_TM, SG_WIDTH), F32)],
        compiler_params=_params("parallel", "parallel"),
        name="odd_in",
    )(x, gain.reshape(1, D_MODEL), w_in,
      (q_gain * HEAD_DIM ** -0.5).reshape(1, HEAD_DIM), k_gain.reshape(1, HEAD_DIM),
      v_gain.reshape(1, SG_WIDTH))


def _dilated_kernel(slopes_ref, q0_ref, q1_ref, q2_ref, k0_ref, k1_ref, k2_ref, v0_ref, v1_ref, v2_ref,
                    o_ref, on_ref, ls_ref, *, seq):
    hh = pl.program_id(1)
    step = pl.program_id(2)
    q_refs = (q0_ref, q1_ref, q2_ref)
    k_refs = (k0_ref, k1_ref, k2_ref)
    v_refs = (v0_ref, v1_ref, v2_ref)
    for g, (window, dil) in enumerate(DIL_PAIRS):
        assert window // (2 * dil) == DIL_RADIUS
        q_ref, k_ref, v_ref = q_refs[g], k_refs[g], v_refs[g]
        sub_len = seq // dil
        per_res = DIL_TA // dil
        nq = min(per_res, DIL_TQ)
        n_sub = per_res // nq
        n_keys = min(nq + 2 * DIL_RADIUS, sub_len)
        slope = slopes_ref[g * DIL_HEADS_PER_GROUP + hh] * dil
        base = (lax.broadcasted_iota(jnp.int32, (nq, n_keys), 1)
                - lax.broadcasted_iota(jnp.int32, (nq, n_keys), 0))

        def body(t, carry, q_ref=q_ref, k_ref=k_ref, v_ref=v_ref, g=g, dil=dil, sub_len=sub_len,
                 per_res=per_res, nq=nq, n_sub=n_sub, n_keys=n_keys, slope=slope, base=base):
            r = t // n_sub
            q_off = pl.multiple_of((t % n_sub) * nq, nq)
            q_pos = step * per_res + q_off
            k_pos = pl.multiple_of(jnp.clip(q_pos - DIL_RADIUS, 0, sub_len - n_keys), DIL_RADIUS)
            q = q_ref[r, pl.ds(q_off, nq), :]
            k = k_ref[r, pl.ds(k_pos, n_keys), :]
            v = v_ref[r, pl.ds(k_pos, n_keys), :]
            s = lax.dot_general(q, k, (((1,), (1,)), ((), ())), preferred_element_type=F32)
            dist = jnp.abs(base + (k_pos - q_pos))
            s = jnp.where(dist <= DIL_RADIUS, s - slope * dist.astype(F32), NEG_INF)
            m = jnp.max(s, axis=-1, keepdims=True)
            p = jnp.exp(s - m)
            den = jnp.sum(p, axis=-1, keepdims=True)
            out = jnp.dot(p.astype(BF16), v, preferred_element_type=F32) / den
            lse = jnp.broadcast_to(m + jnp.log(den), (nq, HEAD_DIM))
            if dil == 1:
                rows = pl.ds(q_off, nq)
            else:
                rows = pl.ds(r + dil * q_off, nq, stride=dil)
            on_ref[g, rows, :] = out
            ls_ref[g, rows, :] = lse
            return carry

        lax.fori_loop(0, dil * n_sub, body, 0, unroll=DIL_UNROLL)

    def merge(c, carry):
        rows = pl.ds(pl.multiple_of(c * DIL_TQ, DIL_TQ), DIL_TQ)
        lse = [ls_ref[g, rows, :] for g in range(len(DIL_PAIRS))]
        top = jnp.maximum(jnp.maximum(lse[0], lse[1]), lse[2])
        num = jnp.zeros((DIL_TQ, HEAD_DIM), F32)
        den = jnp.zeros((DIL_TQ, HEAD_DIM), F32)
        for g in range(len(DIL_PAIRS)):
            w = jnp.exp(lse[g] - top)
            num = num + w * on_ref[g, rows, :]
            den = den + w
        o_ref[rows, :] = (num / den).astype(o_ref.dtype)
        return carry

    lax.fori_loop(0, DIL_TA // DIL_TQ, merge, 0)


def _dilated(q, k, v, slopes):
    b, _, s, _ = q[0].shape
    n_groups = len(DIL_PAIRS)

    def q_spec(dil):
        return pl.BlockSpec((None, dil, DIL_TA // dil, HEAD_DIM), lambda bi, hh, i: (bi, 0, i, hh))

    def kv_spec(dil):
        return pl.BlockSpec((None, dil, s // dil, HEAD_DIM), lambda bi, hh, i: (bi, 0, 0, hh))

    dils = [dil for _, dil in DIL_PAIRS]
    return pl.pallas_call(
        functools.partial(_dilated_kernel, seq=s),
        grid=(b, DIL_HEADS_PER_GROUP, s // DIL_TA),
        in_specs=[pl.BlockSpec(memory_space=pltpu.SMEM)]
        + [q_spec(d) for d in dils] + [kv_spec(d) for d in dils] + [kv_spec(d) for d in dils],
        out_specs=pl.BlockSpec((None, DIL_TA, HEAD_DIM), lambda bi, hh, i: (bi, i, hh)),
        out_shape=jax.ShapeDtypeStruct((b, s, DIL_OUT), BF16),
        scratch_shapes=[pltpu.VMEM((n_groups, DIL_TA, HEAD_DIM), F32),
                        pltpu.VMEM((n_groups, DIL_TA, HEAD_DIM), F32)],
        compiler_params=_params("parallel", "parallel", "arbitrary"),
        name="dilated_attn",
    )(slopes, *q, *k, *v)


def _even_out_kernel(a_ref, prev_ref, next_ref, yb_ref, pw_ref, ps_ref, wo_ref, x_ref, o_ref,
                     ext_ref, y_ref, *, seq):
    i = pl.program_id(1)
    tm = OUT_TM
    ext_ref[0:POOL_HALO, :] = jnp.where(i == 0, 0.0, prev_ref[...])
    ext_ref[POOL_HALO:POOL_HALO + tm, :] = a_ref[...]
    ext_ref[POOL_HALO + tm:, :] = jnp.where(i == pl.num_programs(1) - 1, 0.0, next_ref[...])
    pos = i * tm + lax.broadcasted_iota(jnp.int32, (tm, 1), 0)
    for g, window in enumerate(POOL_WINDOWS):
        half = window // 2
        cols = slice(g * HEAD_DIM, (g + 1) * HEAD_DIM)
        total = ext_ref[POOL_HALO - half:POOL_HALO - half + tm, cols]
        for shift in range(-half + 1, half):
            total = total + ext_ref[POOL_HALO + shift:POOL_HALO + shift + tm, cols]
        count = (jnp.minimum(pos + half, seq) - jnp.maximum(pos - half, 0)).astype(F32)
        pooled = total / count - a_ref[:, cols]
        ya = jnp.dot(pooled.astype(BF16), pw_ref[g], preferred_element_type=F32) * ps_ref[:, cols]
        y_ref[:, cols] = ya.astype(BF16)
    y_ref[:, POOL_WIDTH:] = yb_ref[...]
    o_ref[...] = x_ref[...] + jnp.dot(y_ref[...], wo_ref[...], preferred_element_type=F32)


def _even_out(a, yb, pool_w, pool_scale, w_out, x):
    b, s, _ = x.shape
    tm = OUT_TM
    halo_blocks = tm // POOL_HALO
    tile = lambda bi, i: (bi, i, 0)
    return pl.pallas_call(
        functools.partial(_even_out_kernel, seq=s),
        grid=(b, s // tm),
        in_specs=[
            pl.BlockSpec((None, tm, POOL_WIDTH), tile),
            pl.BlockSpec((None, POOL_HALO, POOL_WIDTH),
                         lambda bi, i: (bi, jnp.maximum(i * halo_blocks - 1, 0), 0)),
            pl.BlockSpec((None, POOL_HALO, POOL_WIDTH),
                         lambda bi, i: (bi, jnp.minimum((i + 1) * halo_blocks, s // POOL_HALO - 1), 0)),
            pl.BlockSpec((None, tm, DIL_OUT), tile),
            _resident((len(POOL_WINDOWS), HEAD_DIM, HEAD_DIM), lambda bi, i: (0, 0, 0)),
            pl.BlockSpec((1, POOL_WIDTH), lambda bi, i: (0, 0)),
            _resident((POOL_WIDTH + DIL_OUT, D_MODEL), lambda bi, i: (0, 0)),
            pl.BlockSpec((None, tm, D_MODEL), tile),
        ],
        out_specs=pl.BlockSpec((None, tm, D_MODEL), tile),
        out_shape=jax.ShapeDtypeStruct((b, s, D_MODEL), F32),
        scratch_shapes=[pltpu.VMEM((tm + 2 * POOL_HALO, POOL_WIDTH), F32),
                        pltpu.VMEM((tm, POOL_WIDTH + DIL_OUT), BF16)],
        compiler_params=_params("parallel", "parallel"),
        name="even_out",
    )(a, a, a, yb, pool_w, pool_scale.reshape(1, POOL_WIDTH), w_out, x)


def _na_fill_bias(pair_ref, bias_ref, rb, rows):
    k_row0 = min(max(rb * NA_QROWS - NA_ROWS // 2, 0), rows - NA_KROWS)
    left_half = lax.broadcasted_iota(jnp.int32, (GRID_W, 2 * GRID_W), 1) < GRID_W
    for qr in range(NA_QROWS):
        r = rb * NA_QROWS + qr
        row_start = min(max(r - NA_ROWS // 2, 0), rows - NA_ROWS)
        for p in range(NA_KROWS // 2):
            kr = k_row0 + 2 * p
            ok_left = row_start <= kr < row_start + NA_ROWS
            ok_right = row_start <= kr + 1 < row_start + NA_ROWS
            if ok_left or ok_right:
                block = pair_ref[kr + 1 - r + NA_ROWS - 1]
                if not ok_right:
                    block = jnp.where(left_half, block, NEG_INF)
                if not ok_left:
                    block = jnp.where(left_half, NEG_INF, block)
            else:
                block = jnp.full((GRID_W, 2 * GRID_W), NEG_INF, F32)
            bias_ref[qr * GRID_W:(qr + 1) * GRID_W, p * 2 * GRID_W:(p + 1) * 2 * GRID_W] = block


def _na_kernel(q_ref, k_ref, v_ref, pair_ref, o_ref, bias_ref, *, rows):
    n_blocks = rows // NA_QROWS
    tq = NA_QROWS * GRID_W
    n_keys = NA_KROWS * GRID_W
    kinds = (0, 1, n_blocks - 1)
    for slot, kind in enumerate(kinds):
        _na_fill_bias(pair_ref, bias_ref.at[slot], kind, rows)
    for rb in range(n_blocks):
        slot = 0 if rb == 0 else (2 if rb == n_blocks - 1 else 1)
        k_start = min(max(rb * NA_QROWS - NA_ROWS // 2, 0), rows - NA_KROWS) * GRID_W
        q = q_ref[rb * tq:(rb + 1) * tq, :]
        k = k_ref[k_start:k_start + n_keys, :]
        v = v_ref[k_start:k_start + n_keys, :]
        s = lax.dot_general(q, k, (((1,), (1,)), ((), ())), preferred_element_type=F32) + bias_ref[slot]
        m = jnp.max(s, axis=-1, keepdims=True)
        p = jnp.exp(s - m)
        den = jnp.sum(p, axis=-1, keepdims=True)
        out = jnp.dot(p.astype(BF16), v, preferred_element_type=F32) / den
        o_ref[rb * tq:(rb + 1) * tq, :] = out.astype(o_ref.dtype)


def _na_pair_tables(rpb):
    n_heads = rpb.shape[0]
    c = np.arange(GRID_W)
    col_start = np.clip(c - NA_COLS // 2, 0, GRID_W - NA_COLS)
    col_ok = (c[None, :] >= col_start[:, None]) & (c[None, :] < col_start[:, None] + NA_COLS)
    edge = GRID_W - NA_COLS
    ext = jnp.concatenate([jnp.repeat(rpb[:, :, :1], edge, axis=2), rpb.astype(F32),
                           jnp.repeat(rpb[:, :, -1:], edge, axis=2)], axis=2)
    by_col = jnp.stack([ext[:, :, GRID_W - 1 - cq:2 * GRID_W - 1 - cq] for cq in range(GRID_W)], axis=2)
    by_col = jnp.where(col_ok[None, None], by_col, NEG_INF)
    masked = jnp.full((n_heads, 1, GRID_W, GRID_W), NEG_INF, F32)
    padded = jnp.concatenate([masked, by_col, masked], axis=1)
    return jnp.concatenate([padded[:, :-1], padded[:, 1:]], axis=-1)


def _neighbourhood(q, k, v, pair_tables):
    b, s, _ = q.shape
    rows = s // GRID_W
    n_blocks = rows // NA_QROWS
    assert n_blocks >= 3 and rows >= NA_KROWS
    tq = NA_QROWS * GRID_W
    n_keys = NA_KROWS * GRID_W
    head = lambda bi, h: (bi, 0, h)
    return pl.pallas_call(
        functools.partial(_na_kernel, rows=rows),
        grid=(b, NA_HEADS),
        in_specs=[
            pl.BlockSpec((None, s, HEAD_DIM), head),
            pl.BlockSpec((None, s, HEAD_DIM), head),
            pl.BlockSpec((None, s, HEAD_DIM), head),
            pl.BlockSpec((None, 2 * NA_ROWS, GRID_W, 2 * GRID_W), lambda bi, h: (h, 0, 0, 0)),
        ],
        out_specs=pl.BlockSpec((None, s, HEAD_DIM), head),
        out_shape=jax.ShapeDtypeStruct((b, s, NA_WIDTH), BF16),
        scratch_shapes=[pltpu.VMEM((3, tq, n_keys), F32)],
        compiler_params=_params("parallel", "parallel"),
        name="neighbourhood_attn",
    )(q, k, v, pair_tables)


def _odd_out_kernel(yc_ref, u_ref, vn_ref, ws_ref, bs_ref, wo_ref, x_ref, o_ref, y_ref):
    width = SG_WIDTH // SG_GROUPS
    for c in range(OUT_TM // SG_CHUNK):
        rows = slice(c * SG_CHUNK, (c + 1) * SG_CHUNK)
        for g in range(SG_GROUPS):
            cols = slice(g * width, (g + 1) * width)
            sv = jnp.dot(ws_ref[g], vn_ref[rows, cols], preferred_element_type=F32) + bs_ref[g]
            y_ref[rows, NA_WIDTH + g * width:NA_WIDTH + (g + 1) * width] = (u_ref[rows, cols] * sv).astype(BF16)
    y_ref[:, :NA_WIDTH] = yc_ref[...]
    o_ref[...] = x_ref[...] + jnp.dot(y_ref[...], wo_ref[...], preferred_element_type=F32)


def _odd_out(yc, u, vn, w_s, b_s, w_out, x):
    b, s, _ = x.shape
    tm = OUT_TM
    width = SG_WIDTH // SG_GROUPS
    tile = lambda bi, i: (bi, i, 0)
    return pl.pallas_call(
        _odd_out_kernel,
        grid=(b, s // tm),
        in_specs=[
            pl.BlockSpec((None, tm, NA_WIDTH), tile),
            pl.BlockSpec((None, tm, SG_WIDTH), tile),
            pl.BlockSpec((None, tm, SG_WIDTH), tile),
            _resident((SG_GROUPS, SG_CHUNK, SG_CHUNK), lambda bi, i: (0, 0, 0)),
            _resident((SG_GROUPS, SG_CHUNK, width), lambda bi, i: (0, 0, 0)),
            _resident((NA_WIDTH + SG_WIDTH, D_MODEL), lambda bi, i: (0, 0)),
            pl.BlockSpec((None, tm, D_MODEL), tile),
        ],
        out_specs=pl.BlockSpec((None, tm, D_MODEL), tile),
        out_shape=jax.ShapeDtypeStruct((b, s, D_MODEL), F32),
        scratch_shapes=[pltpu.VMEM((tm, NA_WIDTH + SG_WIDTH), BF16)],
        compiler_params=_params("parallel", "parallel"),
        name="odd_out",
    )(yc, u, vn, w_s, b_s, w_out, x)


def kernel(x, norm_ffn1, norm_mix, norm_ffn2, norm_out, ffn_w_gate, ffn_w_up, ffn_w_down, even_w_in, pool_w, pool_scale, dil_q_gain, dil_k_gain, even_w_out, odd_w_in, na_q_gain, na_k_gain, na_rpb, sg_v_gain, sg_w, sg_b, odd_w_out):
    depth = norm_ffn1.shape[0]
    slopes = jnp.asarray(2.0 ** (-8.0 * np.arange(1, DIL_HEADS + 1) / DIL_HEADS), dtype=F32)
    width = SG_WIDTH // SG_GROUPS

    for layer in range(depth):
        x = _ffn(x, norm_ffn1[layer], ffn_w_gate, ffn_w_up, ffn_w_down, layer, 0)
        if layer % 2 == 0:
            e = layer // 2
            a, q, k, v = _even_in(x, norm_mix[layer], even_w_in[e].astype(BF16), dil_q_gain[e], dil_k_gain[e])
            yb = _dilated(q, k, v, slopes)
            x = _even_out(a, yb, pool_w[e].astype(BF16), pool_scale[e], even_w_out[e].astype(BF16), x)
        else:
            o = layer // 2
            q, k, v, u, vn = _odd_in(x, norm_mix[layer], odd_w_in[o].astype(BF16),
                                     na_q_gain[o], na_k_gain[o], sg_v_gain[o])
            yc = _neighbourhood(q, k, v, _na_pair_tables(na_rpb[o]))
            b_s = jnp.broadcast_to(sg_b[o][:, :, None], (SG_GROUPS, SG_CHUNK, width))
            x = _odd_out(yc, u, vn, sg_w[o].astype(BF16), b_s, odd_w_out[o].astype(BF16), x)
        x = _ffn(x, norm_ffn2[layer], ffn_w_gate, ffn_w_up, ffn_w_down, layer, 1, gain_out=norm_out[layer])
    return x
```

```python
import functools

import numpy as np
import jax
import jax.numpy as jnp
from jax import lax
from jax.experimental import pallas as pl
from jax.experimental.pallas import tpu as pltpu

F32 = jnp.float32
BF16 = jnp.bfloat16

D_MODEL = 2048
D_FF = 5632
HEAD_DIM = 128
POOL_WINDOWS = (2, 4, 8, 16)
POOL_WIDTH = 512
POOL_HALO = 8
DIL_PAIRS = ((128, 1), (512, 4), (2048, 16))
DIL_HEADS_PER_GROUP = 4
DIL_HEADS = 12
DIL_WIDTH = DIL_HEADS * HEAD_DIM
DIL_OUT = DIL_HEADS_PER_GROUP * HEAD_DIM
GRID_W = 64
NA_ROWS = 8
NA_COLS = 16
NA_HEADS = 8
NA_WIDTH = NA_HEADS * HEAD_DIM
NA_QROWS = 4
NA_KROWS = NA_QROWS + NA_ROWS
SG_CHUNK = 128
SG_GROUPS = 8
SG_WIDTH = 1024
RMS_EPS = 1e-6
NEG_INF = -1e30

VMEM_LIMIT_BYTES = 60 * 1024 * 1024

FFN_TM = 1024
FFN_TF = 512
FFN_TC = 256
PROJ_TM = 512
PROJ_TN = 512
DIL_TQ = 256
DIL_TA = 2048
DIL_RADIUS = 64
DIL_UNROLL = 8
OUT_TM = 512


def _params(*semantics):
    return pltpu.CompilerParams(dimension_semantics=semantics, vmem_limit_bytes=VMEM_LIMIT_BYTES)


def _resident(block_shape, index_map):
    return pl.BlockSpec(block_shape, index_map, pipeline_mode=pl.Buffered(1))


def _rms(x, gain):
    return x * lax.rsqrt(jnp.mean(x * x, axis=-1, keepdims=True) + RMS_EPS) * gain


def _ffn_kernel(x_ref, g_ref, wg_ref, wu_ref, wd_ref, gout_ref, o_ref, h_ref, *, final_norm):
    j = pl.program_id(2)

    @pl.when(j == 0)
    def _():
        x = x_ref[...]
        h_ref[...] = _rms(x, g_ref[...]).astype(BF16)
        o_ref[...] = x

    h = h_ref[...]
    for c in range(FFN_TF // FFN_TC):
        cols = slice(c * FFN_TC, (c + 1) * FFN_TC)
        gate = jnp.dot(h, wg_ref[:, cols].astype(BF16), preferred_element_type=F32)
        up = jnp.dot(h, wu_ref[:, cols].astype(BF16), preferred_element_type=F32)
        act = (gate * jax.nn.sigmoid(gate)) * up * 0.5
        o_ref[...] += jnp.dot(act.astype(BF16), wd_ref[cols, :].astype(BF16), preferred_element_type=F32)

    if final_norm:
        @pl.when(j == pl.num_programs(2) - 1)
        def _():
            o_ref[...] = _rms(o_ref[...], gout_ref[...])


def _ffn(x, gain, w_gate, w_up, w_down, layer, idx, gain_out=None):
    b, s, _ = x.shape
    final_norm = gain_out is not None
    if gain_out is None:
        gain_out = gain
    tile = lambda bi, i, j: (bi, i, 0)
    const = lambda bi, i, j: (0, 0)
    return pl.pallas_call(
        functools.partial(_ffn_kernel, final_norm=final_norm),
        grid=(b, s // FFN_TM, D_FF // FFN_TF),
        in_specs=[
            pl.BlockSpec((None, FFN_TM, D_MODEL), tile, pipeline_mode=pl.Buffered(1)),
            pl.BlockSpec((1, D_MODEL), const),
            pl.BlockSpec((None, None, D_MODEL, FFN_TF), lambda bi, i, j: (layer, idx, 0, j)),
            pl.BlockSpec((None, None, D_MODEL, FFN_TF), lambda bi, i, j: (layer, idx, 0, j)),
            pl.BlockSpec((None, None, FFN_TF, D_MODEL), lambda bi, i, j: (layer, idx, j, 0)),
            pl.BlockSpec((1, D_MODEL), const),
        ],
        out_specs=pl.BlockSpec((None, FFN_TM, D_MODEL), tile),
        out_shape=jax.ShapeDtypeStruct((b, s, D_MODEL), F32),
        scratch_shapes=[pltpu.VMEM((FFN_TM, D_MODEL), BF16)],
        compiler_params=_params("parallel", "parallel", "arbitrary"),
        name="ffn_final" if final_norm else "ffn",
    )(x, gain.reshape(1, D_MODEL), w_gate, w_up, w_down, gain_out.reshape(1, D_MODEL))


def _head_norm_store(z, out_ref, col0, gain):
    for hd in range(PROJ_TN // HEAD_DIM):
        zh = z[:, hd * HEAD_DIM:(hd + 1) * HEAD_DIM]
        out_ref[:, col0 + hd * HEAD_DIM:col0 + (hd + 1) * HEAD_DIM] = _rms(zh, gain).astype(out_ref.dtype)


def _store_by_residue(z, out_ref, stage_ref, dil, gain=None):
    tm = z.shape[0]
    for hd in range(z.shape[1] // HEAD_DIM):
        cols = slice(hd * HEAD_DIM, (hd + 1) * HEAD_DIM)
        zh = z[:, cols]
        if gain is not None:
            zh = _rms(zh, gain)
        if dil == 1:
            out_ref[0, :, cols] = zh.astype(BF16)
        else:
            stage_ref[hd] = zh
            for r in range(dil):
                out_ref[r, :, cols] = stage_ref[hd, pl.ds(r, tm // dil, stride=dil), :].astype(BF16)


def _even_in_kernel(x_ref, g_ref, w_ref, qg_ref, kg_ref, a_ref, *rest):
    n_groups = len(DIL_PAIRS)
    q_refs, k_refs, v_refs = rest[:n_groups], rest[n_groups:2 * n_groups], rest[2 * n_groups:3 * n_groups]
    h_ref, stage_ref = rest[3 * n_groups:]
    h_ref[...] = _rms(x_ref[...], g_ref[...]).astype(BF16)

    def chunk(c):
        return jnp.dot(h_ref[...], w_ref[:, c * PROJ_TN:(c + 1) * PROJ_TN], preferred_element_type=F32)

    a_ref[...] = chunk(0)
    for g, (_, dil) in enumerate(DIL_PAIRS):
        _store_by_residue(chunk(1 + g), q_refs[g], stage_ref, dil, qg_ref[...])
        _store_by_residue(chunk(1 + n_groups + g), k_refs[g], stage_ref, dil, kg_ref[...])
        _store_by_residue(chunk(1 + 2 * n_groups + g), v_refs[g], stage_ref, dil)


def _even_in(x, gain, w_in, q_gain, k_gain):
    b, s, _ = x.shape
    n_in = w_in.shape[1]
    assert PROJ_TN == DIL_OUT
    row = lambda bi, i: (bi, i, 0)
    const = lambda bi, i: (0, 0)
    group_specs = [pl.BlockSpec((None, dil, PROJ_TM // dil, DIL_OUT), lambda bi, i: (bi, 0, i, 0))
                   for _, dil in DIL_PAIRS]
    group_shapes = [jax.ShapeDtypeStruct((b, dil, s // dil, DIL_OUT), BF16) for _, dil in DIL_PAIRS]
    outs = pl.pallas_call(
        _even_in_kernel,
        grid=(b, s // PROJ_TM),
        in_specs=[
            pl.BlockSpec((None, PROJ_TM, D_MODEL), row),
            pl.BlockSpec((1, D_MODEL), const),
            _resident((D_MODEL, n_in), const),
            pl.BlockSpec((1, HEAD_DIM), const),
            pl.BlockSpec((1, HEAD_DIM), const),
        ],
        out_specs=[pl.BlockSpec((None, PROJ_TM, POOL_WIDTH), row)] + group_specs * 3,
        out_shape=[jax.ShapeDtypeStruct((b, s, POOL_WIDTH), F32)] + group_shapes * 3,
        scratch_shapes=[pltpu.VMEM((PROJ_TM, D_MODEL), BF16),
                        pltpu.VMEM((DIL_HEADS_PER_GROUP, PROJ_TM, HEAD_DIM), F32)],
        compiler_params=_params("parallel", "parallel"),
        name="even_in",
    )(x, gain.reshape(1, D_MODEL), w_in,
      (q_gain * HEAD_DIM ** -0.5).reshape(1, HEAD_DIM), k_gain.reshape(1, HEAD_DIM))
    n_groups = len(DIL_PAIRS)
    return outs[0], outs[1:1 + n_groups], outs[1 + n_groups:1 + 2 * n_groups], outs[1 + 2 * n_groups:]


def _odd_in_kernel(x_ref, g_ref, w_ref, qg_ref, kg_ref, vg_ref, q_ref, k_ref, v_ref, u_ref, vn_ref,
                   h_ref, gv_ref):
    h_ref[...] = _rms(x_ref[...], g_ref[...]).astype(BF16)

    def chunk(c):
        return jnp.dot(h_ref[...], w_ref[:, c * PROJ_TN:(c + 1) * PROJ_TN], preferred_element_type=F32)

    per = NA_WIDTH // PROJ_TN
    for c in range(per):
        _head_norm_store(chunk(c), q_ref, c * PROJ_TN, qg_ref[...])
        _head_norm_store(chunk(per + c), k_ref, c * PROJ_TN, kg_ref[...])
        v_ref[:, c * PROJ_TN:(c + 1) * PROJ_TN] = chunk(2 * per + c).astype(BF16)
    per_sg = SG_WIDTH // PROJ_TN
    for c in range(per_sg):
        u_ref[:, c * PROJ_TN:(c + 1) * PROJ_TN] = jax.nn.gelu(chunk(3 * per + c))
        gv_ref[:, c * PROJ_TN:(c + 1) * PROJ_TN] = jax.nn.gelu(chunk(3 * per + per_sg + c))
    vn_ref[...] = _rms(gv_ref[...], vg_ref[...]).astype(BF16)


def _odd_in(x, gain, w_in, q_gain, k_gain, v_gain):
    b, s, _ = x.shape
    n_in = w_in.shape[1]
    row = lambda bi, i: (bi, i, 0)
    const = lambda bi, i: (0, 0)
    return pl.pallas_call(
        _odd_in_kernel,
        grid=(b, s // PROJ_TM),
        in_specs=[
            pl.BlockSpec((None, PROJ_TM, D_MODEL), row),
            pl.BlockSpec((1, D_MODEL), const),
            _resident((D_MODEL, n_in), const),
            pl.BlockSpec((1, HEAD_DIM), const),
            pl.BlockSpec((1, HEAD_DIM), const),
            pl.BlockSpec((1, SG_WIDTH), const),
        ],
        out_specs=[
            pl.BlockSpec((None, PROJ_TM, NA_WIDTH), row),
            pl.BlockSpec((None, PROJ_TM, NA_WIDTH), row),
            pl.BlockSpec((None, PROJ_TM, NA_WIDTH), row),
            pl.BlockSpec((None, PROJ_TM, SG_WIDTH), row),
            pl.BlockSpec((None, PROJ_TM, SG_WIDTH), row),
        ],
        out_shape=[
            jax.ShapeDtypeStruct((b, s, NA_WIDTH), BF16),
            jax.ShapeDtypeStruct((b, s, NA_WIDTH), BF16),
            jax.ShapeDtypeStruct((b, s, NA_WIDTH), BF16),
            jax.ShapeDtypeStruct((b, s, SG_WIDTH), F32),
            jax.ShapeDtypeStruct((b, s, SG_WIDTH), BF16),
        ],
        scratch_shapes=[pltpu.VMEM((PROJ_TM, D_MODEL), BF16), pltpu.VMEM((PROJ_TM, SG_WIDTH), F32)],
        compiler_params=_params("parallel", "parallel"),
        name="odd_in",
    )(x, gain.reshape(1, D_MODEL), w_in,
      (q_gain * HEAD_DIM ** -0.5).reshape(1, HEAD_DIM), k_gain.reshape(1, HEAD_DIM),
      v_gain.reshape(1, SG_WIDTH))


def _dilated_kernel(slopes_ref, q0_ref, q1_ref, q2_ref, k0_ref, k1_ref, k2_ref, v0_ref, v1_ref, v2_ref,
                    o_ref, on_ref, ls_ref, *, seq):
    hh = pl.program_id(1)
    step = pl.program_id(2)
    q_refs = (q0_ref, q1_ref, q2_ref)
    k_refs = (k0_ref, k1_ref, k2_ref)
    v_refs = (v0_ref, v1_ref, v2_ref)
    for g, (window, dil) in enumerate(DIL_PAIRS):
        assert window // (2 * dil) == DIL_RADIUS
        q_ref, k_ref, v_ref = q_refs[g], k_refs[g], v_refs[g]
        sub_len = seq // dil
        per_res = DIL_TA // dil
        nq = min(per_res, DIL_TQ)
        n_sub = per_res // nq
        n_keys = min(nq + 2 * DIL_RADIUS, sub_len)
        slope = slopes_ref[g * DIL_HEADS_PER_GROUP + hh] * dil
        base = (lax.broadcasted_iota(jnp.int32, (nq, n_keys), 1)
                - lax.broadcasted_iota(jnp.int32, (nq, n_keys), 0))

        def body(t, carry, q_ref=q_ref, k_ref=k_ref, v_ref=v_ref, g=g, dil=dil, sub_len=sub_len,
                 per_res=per_res, nq=nq, n_sub=n_sub, n_keys=n_keys, slope=slope, base=base):
            r = t // n_sub
            q_off = pl.multiple_of((t % n_sub) * nq, nq)
            q_pos = step * per_res + q_off
            k_pos = pl.multiple_of(jnp.clip(q_pos - DIL_RADIUS, 0, sub_len - n_keys), DIL_RADIUS)
            q = q_ref[r, pl.ds(q_off, nq), :]
            k = k_ref[r, pl.ds(k_pos, n_keys), :]
            v = v_ref[r, pl.ds(k_pos, n_keys), :]
            s = lax.dot_general(q, k, (((1,), (1,)), ((), ())), preferred_element_type=F32)
            dist = jnp.abs(base + (k_pos - q_pos))
            s = jnp.where(dist <= DIL_RADIUS, s - slope * dist.astype(F32), NEG_INF)
            m = jnp.max(s, axis=-1, keepdims=True)
            p = jnp.exp(s - m)
            den = jnp.sum(p, axis=-1, keepdims=True)
            out = jnp.dot(p.astype(BF16), v, preferred_element_type=F32) / den
            lse = jnp.broadcast_to(m + jnp.log(den), (nq, HEAD_DIM))
            if dil == 1:
                rows = pl.ds(q_off, nq)
            else:
                rows = pl.ds(r + dil * q_off, nq, stride=dil)
            on_ref[g, rows, :] = out
            ls_ref[g, rows, :] = lse
            return carry

        lax.fori_loop(0, dil * n_sub, body, 0, unroll=DIL_UNROLL)

    def merge(c, carry):
        rows = pl.ds(pl.multiple_of(c * DIL_TQ, DIL_TQ), DIL_TQ)
        lse = [ls_ref[g, rows, :] for g in range(len(DIL_PAIRS))]
        top = jnp.maximum(jnp.maximum(lse[0], lse[1]), lse[2])
        num = jnp.zeros((DIL_TQ, HEAD_DIM), F32)
        den = jnp.zeros((DIL_TQ, HEAD_DIM), F32)
        for g in range(len(DIL_PAIRS)):
            w = jnp.exp(lse[g] - top)
            num = num + w * on_ref[g, rows, :]
            den = den + w
        o_ref[rows, :] = (num / den).astype(o_ref.dtype)
        return carry

    lax.fori_loop(0, DIL_TA // DIL_TQ, merge, 0)


def _dilated(q, k, v, slopes):
    b, _, s, _ = q[0].shape
    n_groups = len(DIL_PAIRS)

    def q_spec(dil):
        return pl.BlockSpec((None, dil, DIL_TA // dil, HEAD_DIM), lambda bi, hh, i: (bi, 0, i, hh))

    def kv_spec(dil):
        return pl.BlockSpec((None, dil, s // dil, HEAD_DIM), lambda bi, hh, i: (bi, 0, 0, hh))

    dils = [dil for _, dil in DIL_PAIRS]
    return pl.pallas_call(
        functools.partial(_dilated_kernel, seq=s),
        grid=(b, DIL_HEADS_PER_GROUP, s // DIL_TA),
        in_specs=[pl.BlockSpec(memory_space=pltpu.SMEM)]
        + [q_spec(d) for d in dils] + [kv_spec(d) for d in dils] + [kv_spec(d) for d in dils],
        out_specs=pl.BlockSpec((None, DIL_TA, HEAD_DIM), lambda bi, hh, i: (bi, i, hh)),
        out_shape=jax.ShapeDtypeStruct((b, s, DIL_OUT), BF16),
        scratch_shapes=[pltpu.VMEM((n_groups, DIL_TA, HEAD_DIM), F32),
                        pltpu.VMEM((n_groups, DIL_TA, HEAD_DIM), F32)],
        compiler_params=_params("parallel", "parallel", "arbitrary"),
        name="dilated_attn",
    )(slopes, *q, *k, *v)


def _even_out_kernel(a_ref, prev_ref, next_ref, yb_ref, pw_ref, ps_ref, wo_ref, x_ref, o_ref,
                     ext_ref, y_ref, *, seq):
    i = pl.program_id(1)
    tm = OUT_TM
    ext_ref[0:POOL_HALO, :] = jnp.where(i == 0, 0.0, prev_ref[...])
    ext_ref[POOL_HALO:POOL_HALO + tm, :] = a_ref[...]
    ext_ref[POOL_HALO + tm:, :] = jnp.where(i == pl.num_programs(1) - 1, 0.0, next_ref[...])
    pos = i * tm + lax.broadcasted_iota(jnp.int32, (tm, 1), 0)
    for g, window in enumerate(POOL_WINDOWS):
        half = window // 2
        cols = slice(g * HEAD_DIM, (g + 1) * HEAD_DIM)
        total = ext_ref[POOL_HALO - half:POOL_HALO - half + tm, cols]
        for shift in range(-half + 1, half):
            total = total + ext_ref[POOL_HALO + shift:POOL_HALO + shift + tm, cols]
        count = (jnp.minimum(pos + half, seq) - jnp.maximum(pos - half, 0)).astype(F32)
        pooled = total / count - a_ref[:, cols]
        ya = jnp.dot(pooled.astype(BF16), pw_ref[g], preferred_element_type=F32) * ps_ref[:, cols]
        y_ref[:, cols] = ya.astype(BF16)
    y_ref[:, POOL_WIDTH:] = yb_ref[...]
    o_ref[...] = x_ref[...] + jnp.dot(y_ref[...], wo_ref[...], preferred_element_type=F32)


def _even_out(a, yb, pool_w, pool_scale, w_out, x):
    b, s, _ = x.shape
    tm = OUT_TM
    halo_blocks = tm // POOL_HALO
    tile = lambda bi, i: (bi, i, 0)
    return pl.pallas_call(
        functools.partial(_even_out_kernel, seq=s),
        grid=(b, s // tm),
        in_specs=[
            pl.BlockSpec((None, tm, POOL_WIDTH), tile),
            pl.BlockSpec((None, POOL_HALO, POOL_WIDTH),
                         lambda bi, i: (bi, jnp.maximum(i * halo_blocks - 1, 0), 0)),
            pl.BlockSpec((None, POOL_HALO, POOL_WIDTH),
                         lambda bi, i: (bi, jnp.minimum((i + 1) * halo_blocks, s // POOL_HALO - 1), 0)),
            pl.BlockSpec((None, tm, DIL_OUT), tile),
            _resident((len(POOL_WINDOWS), HEAD_DIM, HEAD_DIM), lambda bi, i: (0, 0, 0)),
            pl.BlockSpec((1, POOL_WIDTH), lambda bi, i: (0, 0)),
            _resident((POOL_WIDTH + DIL_OUT, D_MODEL), lambda bi, i: (0, 0)),
            pl.BlockSpec((None, tm, D_MODEL), tile),
        ],
        out_specs=pl.BlockSpec((None, tm, D_MODEL), tile),
        out_shape=jax.ShapeDtypeStruct((b, s, D_MODEL), F32),
        scratch_shapes=[pltpu.VMEM((tm + 2 * POOL_HALO, POOL_WIDTH), F32),
                        pltpu.VMEM((tm, POOL_WIDTH + DIL_OUT), BF16)],
        compiler_params=_params("parallel", "parallel"),
        name="even_out",
    )(a, a, a, yb, pool_w, pool_scale.reshape(1, POOL_WIDTH), w_out, x)


def _na_fill_bias(pair_ref, bias_ref, rb, rows):
    k_row0 = min(max(rb * NA_QROWS - NA_ROWS // 2, 0), rows - NA_KROWS)
    left_half = lax.broadcasted_iota(jnp.int32, (GRID_W, 2 * GRID_W), 1) < GRID_W
    for qr in range(NA_QROWS):
        r = rb * NA_QROWS + qr
        row_start = min(max(r - NA_ROWS // 2, 0), rows - NA_ROWS)
        for p in range(NA_KROWS // 2):
            kr = k_row0 + 2 * p
            ok_left = row_start <= kr < row_start + NA_ROWS
            ok_right = row_start <= kr + 1 < row_start + NA_ROWS
            if ok_left or ok_right:
                block = pair_ref[kr + 1 - r + NA_ROWS - 1]
                if not ok_right:
                    block = jnp.where(left_half, block, NEG_INF)
                if not ok_left:
                    block = jnp.where(left_half, NEG_INF, block)
            else:
                block = jnp.full((GRID_W, 2 * GRID_W), NEG_INF, F32)
            bias_ref[qr * GRID_W:(qr + 1) * GRID_W, p * 2 * GRID_W:(p + 1) * 2 * GRID_W] = block


def _na_kernel(q_ref, k_ref, v_ref, pair_ref, o_ref, bias_ref, *, rows):
    n_blocks = rows // NA_QROWS
    tq = NA_QROWS * GRID_W
    n_keys = NA_KROWS * GRID_W
    kinds = (0, 1, n_blocks - 1)
    for slot, kind in enumerate(kinds):
        _na_fill_bias(pair_ref, bias_ref.at[slot], kind, rows)
    for rb in range(n_blocks):
        slot = 0 if rb == 0 else (2 if rb == n_blocks - 1 else 1)
        k_start = min(max(rb * NA_QROWS - NA_ROWS // 2, 0), rows - NA_KROWS) * GRID_W
        q = q_ref[rb * tq:(rb + 1) * tq, :]
        k = k_ref[k_start:k_start + n_keys, :]
        v = v_ref[k_start:k_start + n_keys, :]
        s = lax.dot_general(q, k, (((1,), (1,)), ((), ())), preferred_element_type=F32) + bias_ref[slot]
        m = jnp.max(s, axis=-1, keepdims=True)
        p = jnp.exp(s - m)
        den = jnp.sum(p, axis=-1, keepdims=True)
        out = jnp.dot(p.astype(BF16), v, preferred_element_type=F32) / den
        o_ref[rb * tq:(rb + 1) * tq, :] = out.astype(o_ref.dtype)


def _na_pair_tables(rpb):
    n_heads = rpb.shape[0]
    c = np.arange(GRID_W)
    col_start = np.clip(c - NA_COLS // 2, 0, GRID_W - NA_COLS)
    col_ok = (c[None, :] >= col_start[:, None]) & (c[None, :] < col_start[:, None] + NA_COLS)
    edge = GRID_W - NA_COLS
    ext = jnp.concatenate([jnp.repeat(rpb[:, :, :1], edge, axis=2), rpb.astype(F32),
                           jnp.repeat(rpb[:, :, -1:], edge, axis=2)], axis=2)
    by_col = jnp.stack([ext[:, :, GRID_W - 1 - cq:2 * GRID_W - 1 - cq] for cq in range(GRID_W)], axis=2)
    by_col = jnp.where(col_ok[None, None], by_col, NEG_INF)
    masked = jnp.full((n_heads, 1, GRID_W, GRID_W), NEG_INF, F32)
    padded = jnp.concatenate([masked, by_col, masked], axis=1)
    return jnp.concatenate([padded[:, :-1], padded[:, 1:]], axis=-1)


def _neighbourhood(q, k, v, pair_tables):
    b, s, _ = q.shape
    rows = s // GRID_W
    n_blocks = rows // NA_QROWS
    assert n_blocks >= 3 and rows >= NA_KROWS
    tq = NA_QROWS * GRID_W
    n_keys = NA_KROWS * GRID_W
    head = lambda bi, h: (bi, 0, h)
    return pl.pallas_call(
        functools.partial(_na_kernel, rows=rows),
        grid=(b, NA_HEADS),
        in_specs=[
            pl.BlockSpec((None, s, HEAD_DIM), head),
            pl.BlockSpec((None, s, HEAD_DIM), head),
            pl.BlockSpec((None, s, HEAD_DIM), head),
            pl.BlockSpec((None, 2 * NA_ROWS, GRID_W, 2 * GRID_W), lambda bi, h: (h, 0, 0, 0)),
        ],
        out_specs=pl.BlockSpec((None, s, HEAD_DIM), head),
        out_shape=jax.ShapeDtypeStruct((b, s, NA_WIDTH), BF16),
        scratch_shapes=[pltpu.VMEM((3, tq, n_keys), F32)],
        compiler_params=_params("parallel", "parallel"),
        name="neighbourhood_attn",
    )(q, k, v, pair_tables)


def _odd_out_kernel(yc_ref, u_ref, vn_ref, ws_ref, bs_ref, wo_ref, x_ref, o_ref, y_ref):
    width = SG_WIDTH // SG_GROUPS
    for c in range(OUT_TM // SG_CHUNK):
        rows = slice(c * SG_CHUNK, (c + 1) * SG_CHUNK)
        for g in range(SG_GROUPS):
            cols = slice(g * width, (g + 1) * width)
            sv = jnp.dot(ws_ref[g], vn_ref[rows, cols], preferred_element_type=F32) + bs_ref[g]
            y_ref[rows, NA_WIDTH + g * width:NA_WIDTH + (g + 1) * width] = (u_ref[rows, cols] * sv).astype(BF16)
    y_ref[:, :NA_WIDTH] = yc_ref[...]
    o_ref[...] = x_ref[...] + jnp.dot(y_ref[...], wo_ref[...], preferred_element_type=F32)


def _odd_out(yc, u, vn, w_s, b_s, w_out, x):
    b, s, _ = x.shape
    tm = OUT_TM
    width = SG_WIDTH // SG_GROUPS
    tile = lambda bi, i: (bi, i, 0)
    return pl.pallas_call(
        _odd_out_kernel,
        grid=(b, s // tm),
        in_specs=[
            pl.BlockSpec((None, tm, NA_WIDTH), tile),
            pl.BlockSpec((None, tm, SG_WIDTH), tile),
            pl.BlockSpec((None, tm, SG_WIDTH), tile),
            _resident((SG_GROUPS, SG_CHUNK, SG_CHUNK), lambda bi, i: (0, 0, 0)),
            _resident((SG_GROUPS, SG_CHUNK, width), lambda bi, i: (0, 0, 0)),
            _resident((NA_WIDTH + SG_WIDTH, D_MODEL), lambda bi, i: (0, 0)),
            pl.BlockSpec((None, tm, D_MODEL), tile),
        ],
        out_specs=pl.BlockSpec((None, tm, D_MODEL), tile),
        out_shape=jax.ShapeDtypeStruct((b, s, D_MODEL), F32),
        scratch_shapes=[pltpu.VMEM((tm, NA_WIDTH + SG_WIDTH), BF16)],
        compiler_params=_params("parallel", "parallel"),
        name="odd_out",
    )(yc, u, vn, w_s, b_s, w_out, x)


def kernel(x, norm_ffn1, norm_mix, norm_ffn2, norm_out, ffn_w_gate, ffn_w_up, ffn_w_down, even_w_in, pool_w, pool_scale, dil_q_gain, dil_k_gain, even_w_out, odd_w_in, na_q_gain, na_k_gain, na_rpb, sg_v_gain, sg_w, sg_b, odd_w_out):
    depth = norm_ffn1.shape[0]
    slopes = jnp.asarray(2.0 ** (-8.0 * np.arange(1, DIL_HEADS + 1) / DIL_HEADS), dtype=F32)
    width = SG_WIDTH // SG_GROUPS

    for layer in range(depth):
        x = _ffn(x, norm_ffn1[layer], ffn_w_gate, ffn_w_up, ffn_w_down, layer, 0)
        if layer % 2 == 0:
            e = layer // 2
            a, q, k, v = _even_in(x, norm_mix[layer], even_w_in[e].astype(BF16), dil_q_gain[e], dil_k_gain[e])
            yb = _dilated(q, k, v, slopes)
            x = _even_out(a, yb, pool_w[e].astype(BF16), pool_scale[e], even_w_out[e].astype(BF16), x)
        else:
            o = layer // 2
            q, k, v, u, vn = _odd_in(x, norm_mix[layer], odd_w_in[o].astype(BF16),
                                     na_q_gain[o], na_k_gain[o], sg_v_gain[o])
            yc = _neighbourhood(q, k, v, _na_pair_tables(na_rpb[o]))
            b_s = jnp.broadcast_to(sg_b[o][:, :, None], (SG_GROUPS, SG_CHUNK, width))
            x = _odd_out(yc, u, vn, sg_w[o].astype(BF16), b_s, odd_w_out[o].astype(BF16), x)
        x = _ffn(x, norm_ffn2[layer], ffn_w_gate, ffn_w_up, ffn_w_down, layer, 1, gain_out=norm_out[layer])
    return x
```

```python
import functools

import numpy as np
import jax
import jax.numpy as jnp
from jax import lax
from jax.experimental import pallas as pl
from jax.experimental.pallas import tpu as pltpu

F32 = jnp.float32
BF16 = jnp.bfloat16

D_MODEL = 2048
D_FF = 5632
HEAD_DIM = 128
POOL_WINDOWS = (2, 4, 8, 16)
POOL_WIDTH = 512
POOL_HALO = 8
DIL_PAIRS = ((128, 1), (512, 4), (2048, 16))
DIL_HEADS_PER_GROUP = 4
DIL_HEADS = 12
DIL_WIDTH = DIL_HEADS * HEAD_DIM
DIL_OUT = DIL_HEADS_PER_GROUP * HEAD_DIM
GRID_W = 64
NA_ROWS = 8
NA_COLS = 16
NA_HEADS = 8
NA_WIDTH = NA_HEADS * HEAD_DIM
NA_QROWS = 4
NA_KROWS = NA_QROWS + NA_ROWS
SG_CHUNK = 128
SG_GROUPS = 8
SG_WIDTH = 1024
RMS_EPS = 1e-6
NEG_INF = -1e30

VMEM_LIMIT_BYTES = 60 * 1024 * 1024

FFN_TM = 1024
FFN_TF = 512
FFN_TC = 256
PROJ_TM = 512
PROJ_TN = 512
DIL_TQ = 256
DIL_TA = 2048
DIL_RADIUS = 64
DIL_UNROLL = 8
OUT_TM = 512


def _params(*semantics):
    return pltpu.CompilerParams(dimension_semantics=semantics, vmem_limit_bytes=VMEM_LIMIT_BYTES)


def _resident(block_shape, index_map):
    return pl.BlockSpec(block_shape, index_map, pipeline_mode=pl.Buffered(1))


def _rms(x, gain):
    return x * lax.rsqrt(jnp.mean(x * x, axis=-1, keepdims=True) + RMS_EPS) * gain


def _ffn_kernel(x_hbm, g_ref, wg_ref, wu_ref, wd_ref, gout_ref, o_ref, h_ref, xbuf_ref, x_sem, *, final_norm):
    j = pl.program_id(2)
    tiles_per_batch = pl.num_programs(1)
    n_tiles = pl.num_programs(0) * tiles_per_batch
    tile = pl.program_id(0) * tiles_per_batch + pl.program_id(1)

    def x_copy(t):
        rows = pl.ds(pl.multiple_of((t % tiles_per_batch) * FFN_TM, FFN_TM), FFN_TM)
        return pltpu.make_async_copy(x_hbm.at[t // tiles_per_batch, rows, :], xbuf_ref, x_sem)

    @pl.when(j == 0)
    def _():
        @pl.when(tile == 0)
        def _():
            x_copy(tile).start()

        x_copy(tile).wait()
        x = xbuf_ref[...]
        h_ref[...] = _rms(x, g_ref[...]).astype(BF16)
        o_ref[...] = x

    @pl.when((j == 1) & (tile + 1 < n_tiles))
    def _():
        x_copy(tile + 1).start()

    h = h_ref[...]
    for c in range(FFN_TF // FFN_TC):
        cols = slice(c * FFN_TC, (c + 1) * FFN_TC)
        gate = jnp.dot(h, wg_ref[:, cols].astype(BF16), preferred_element_type=F32)
        up = jnp.dot(h, wu_ref[:, cols].astype(BF16), preferred_element_type=F32)
        act = (gate * jax.nn.sigmoid(gate)) * up * 0.5
        o_ref[...] += jnp.dot(act.astype(BF16), wd_ref[cols, :].astype(BF16), preferred_element_type=F32)

    if final_norm:
        @pl.when(j == pl.num_programs(2) - 1)
        def _():
            o_ref[...] = _rms(o_ref[...], gout_ref[...])


def _ffn(x, gain, w_gate, w_up, w_down, layer, idx, gain_out=None):
    b, s, _ = x.shape
    assert D_FF // FFN_TF >= 2
    final_norm = gain_out is not None
    if gain_out is None:
        gain_out = gain
    tile = lambda bi, i, j: (bi, i, 0)
    const = lambda bi, i, j: (0, 0)
    return pl.pallas_call(
        functools.partial(_ffn_kernel, final_norm=final_norm),
        grid=(b, s // FFN_TM, D_FF // FFN_TF),
        in_specs=[
            pl.BlockSpec(memory_space=pl.ANY),
            pl.BlockSpec((1, D_MODEL), const),
            pl.BlockSpec((None, None, D_MODEL, FFN_TF), lambda bi, i, j: (layer, idx, 0, j)),
            pl.BlockSpec((None, None, D_MODEL, FFN_TF), lambda bi, i, j: (layer, idx, 0, j)),
            pl.BlockSpec((None, None, FFN_TF, D_MODEL), lambda bi, i, j: (layer, idx, j, 0)),
            pl.BlockSpec((1, D_MODEL), const),
        ],
        out_specs=pl.BlockSpec((None, FFN_TM, D_MODEL), tile),
        out_shape=jax.ShapeDtypeStruct((b, s, D_MODEL), F32),
        scratch_shapes=[pltpu.VMEM((FFN_TM, D_MODEL), BF16), pltpu.VMEM((FFN_TM, D_MODEL), F32),
                        pltpu.SemaphoreType.DMA(())],
        compiler_params=_params("arbitrary", "arbitrary", "arbitrary"),
        name="ffn_final" if final_norm else "ffn",
    )(x, gain.reshape(1, D_MODEL), w_gate, w_up, w_down, gain_out.reshape(1, D_MODEL))


def _head_norm_store(z, out_ref, col0, gain):
    for hd in range(PROJ_TN // HEAD_DIM):
        zh = z[:, hd * HEAD_DIM:(hd + 1) * HEAD_DIM]
        out_ref[:, col0 + hd * HEAD_DIM:col0 + (hd + 1) * HEAD_DIM] = _rms(zh, gain).astype(out_ref.dtype)


def _store_by_residue(z, out_ref, stage_ref, dil, gain=None):
    tm = z.shape[0]
    for hd in range(z.shape[1] // HEAD_DIM):
        cols = slice(hd * HEAD_DIM, (hd + 1) * HEAD_DIM)
        zh = z[:, cols]
        if gain is not None:
            zh = _rms(zh, gain)
        if dil == 1:
            out_ref[0, :, cols] = zh.astype(BF16)
        else:
            stage_ref[hd] = zh
            for r in range(dil):
                out_ref[r, :, cols] = stage_ref[hd, pl.ds(r, tm // dil, stride=dil), :].astype(BF16)


def _even_in_kernel(x_ref, g_ref, w_ref, qg_ref, kg_ref, a_ref, *rest):
    n_groups = len(DIL_PAIRS)
    q_refs, k_refs, v_refs = rest[:n_groups], rest[n_groups:2 * n_groups], rest[2 * n_groups:3 * n_groups]
    h_ref, stage_ref = rest[3 * n_groups:]
    h_ref[...] = _rms(x_ref[...], g_ref[...]).astype(BF16)

    def chunk(c):
        return jnp.dot(h_ref[...], w_ref[:, c * PROJ_TN:(c + 1) * PROJ_TN], preferred_element_type=F32)

    a_ref[...] = chunk(0)
    for g, (_, dil) in enumerate(DIL_PAIRS):
        _store_by_residue(chunk(1 + g), q_refs[g], stage_ref, dil, qg_ref[...])
        _store_by_residue(chunk(1 + n_groups + g), k_refs[g], stage_ref, dil, kg_ref[...])
        _store_by_residue(chunk(1 + 2 * n_groups + g), v_refs[g], stage_ref, dil)


def _even_in(x, gain, w_in, q_gain, k_gain):
    b, s, _ = x.shape
    n_in = w_in.shape[1]
    assert PROJ_TN == DIL_OUT
    row = lambda bi, i: (bi, i, 0)
    const = lambda bi, i: (0, 0)
    group_specs = [pl.BlockSpec((None, dil, PROJ_TM // dil, DIL_OUT), lambda bi, i: (bi, 0, i, 0))
                   for _, dil in DIL_PAIRS]
    group_shapes = [jax.ShapeDtypeStruct((b, dil, s // dil, DIL_OUT), BF16) for _, dil in DIL_PAIRS]
    outs = pl.pallas_call(
        _even_in_kernel,
        grid=(b, s // PROJ_TM),
        in_specs=[
            pl.BlockSpec((None, PROJ_TM, D_MODEL), row),
            pl.BlockSpec((1, D_MODEL), const),
            _resident((D_MODEL, n_in), const),
            pl.BlockSpec((1, HEAD_DIM), const),
            pl.BlockSpec((1, HEAD_DIM), const),
        ],
        out_specs=[pl.BlockSpec((None, PROJ_TM, POOL_WIDTH), row)] + group_specs * 3,
        out_shape=[jax.ShapeDtypeStruct((b, s, POOL_WIDTH), F32)] + group_shapes * 3,
        scratch_shapes=[pltpu.VMEM((PROJ_TM, D_MODEL), BF16),
                        pltpu.VMEM((DIL_HEADS_PER_GROUP, PROJ_TM, HEAD_DIM), F32)],
        compiler_params=_params("parallel", "parallel"),
        name="even_in",
    )(x, gain.reshape(1, D_MODEL), w_in,
      (q_gain * HEAD_DIM ** -0.5).reshape(1, HEAD_DIM), k_gain.reshape(1, HEAD_DIM))
    n_groups = len(DIL_PAIRS)
    return outs[0], outs[1:1 + n_groups], outs[1 + n_groups:1 + 2 * n_groups], outs[1 + 2 * n_groups:]


def _odd_in_kernel(x_ref, g_ref, w_ref, qg_ref, kg_ref, vg_ref, q_ref, k_ref, v_ref, u_ref, vn_ref,
                   h_ref, gv_ref):
    h_ref[...] = _rms(x_ref[...], g_ref[...]).astype(BF16)

    def chunk(c):
        return jnp.dot(h_ref[...], w_ref[:, c * PROJ_TN:(c + 1) * PROJ_TN], preferred_element_type=F32)

    per = NA_WIDTH // PROJ_TN
    for c in range(per):
        _head_norm_store(chunk(c), q_ref, c * PROJ_TN, qg_ref[...])
        _head_norm_store(chunk(per + c), k_ref, c * PROJ_TN, kg_ref[...])
        v_ref[:, c * PROJ_TN:(c + 1) * PROJ_TN] = chunk(2 * per + c).astype(BF16)
    per_sg = SG_WIDTH // PROJ_TN
    for c in range(per_sg):
        u_ref[:, c * PROJ_TN:(c + 1) * PROJ_TN] = jax.nn.gelu(chunk(3 * per + c))
        gv_ref[:, c * PROJ_TN:(c + 1) * PROJ_TN] = jax.nn.gelu(chunk(3 * per + per_sg + c))
    vn_ref[...] = _rms(gv_ref[...], vg_ref[...]).astype(BF16)


def _odd_in(x, gain, w_in, q_gain, k_gain, v_gain):
    b, s, _ = x.shape
    n_in = w_in.shape[1]
    row = lambda bi, i: (bi, i, 0)
    const = lambda bi, i: (0, 0)
    return pl.pallas_call(
        _odd_in_kernel,
        grid=(b, s // PROJ_TM),
        in_specs=[
            pl.BlockSpec((None, PROJ_TM, D_MODEL), row),
            pl.BlockSpec((1, D_MODEL), const),
            _resident((D_MODEL, n_in), const),
            pl.BlockSpec((1, HEAD_DIM), const),
            pl.BlockSpec((1, HEAD_DIM), const),
            pl.BlockSpec((1, SG_WIDTH), const),
        ],
        out_specs=[
            pl.BlockSpec((None, PROJ_TM, NA_WIDTH), row),
            pl.BlockSpec((None, PROJ_TM, NA_WIDTH), row),
            pl.BlockSpec((None, PROJ_TM, NA_WIDTH), row),
            pl.BlockSpec((None, PROJ_TM, SG_WIDTH), row),
            pl.BlockSpec((None, PROJ_TM, SG_WIDTH), row),
        ],
        out_shape=[
            jax.ShapeDtypeStruct((b, s, NA_WIDTH), BF16),
            jax.ShapeDtypeStruct((b, s, NA_WIDTH), BF16),
            jax.ShapeDtypeStruct((b, s, NA_WIDTH), BF16),
            jax.ShapeDtypeStruct((b, s, SG_WIDTH), F32),
            jax.ShapeDtypeStruct((b, s, SG_WIDTH), BF16),
        ],
        scratch_shapes=[pltpu.VMEM((PROJ_TM, D_MODEL), BF16), pltpu.VMEM((PROJ_TM, SG_WIDTH), F32)],
        compiler_params=_params("parallel", "parallel"),
        name="odd_in",
    )(x, gain.reshape(1, D_MODEL), w_in,
      (q_gain * HEAD_DIM ** -0.5).reshape(1, HEAD_DIM), k_gain.reshape(1, HEAD_DIM),
      v_gain.reshape(1, SG_WIDTH))


def _dilated_kernel(slopes_ref, q0_ref, q1_ref, q2_ref, k0_ref, k1_ref, k2_ref, v0_ref, v1_ref, v2_ref,
                    o_ref, on_ref, ls_ref, *, seq):
    hh = pl.program_id(1)
    step = pl.program_id(2)
    q_refs = (q0_ref, q1_ref, q2_ref)
    k_refs = (k0_ref, k1_ref, k2_ref)
    v_refs = (v0_ref, v1_ref, v2_ref)
    for g, (window, dil) in enumerate(DIL_PAIRS):
        assert window // (2 * dil) == DIL_RADIUS
        q_ref, k_ref, v_ref = q_refs[g], k_refs[g], v_refs[g]
        sub_len = seq // dil
        per_res = DIL_TA // dil
        nq = min(per_res, DIL_TQ)
        n_sub = per_res // nq
        n_keys = min(nq + 2 * DIL_RADIUS, sub_len)
        slope = slopes_ref[g * DIL_HEADS_PER_GROUP + hh] * dil
        base = (lax.broadcasted_iota(jnp.int32, (nq, n_keys), 1)
                - lax.broadcasted_iota(jnp.int32, (nq, n_keys), 0))

        def body(t, carry, q_ref=q_ref, k_ref=k_ref, v_ref=v_ref, g=g, dil=dil, sub_len=sub_len,
                 per_res=per_res, nq=nq, n_sub=n_sub, n_keys=n_keys, slope=slope, base=base):
            r = t // n_sub
            q_off = pl.multiple_of((t % n_sub) * nq, nq)
            q_pos = step * per_res + q_off
            k_pos = pl.multiple_of(jnp.clip(q_pos - DIL_RADIUS, 0, sub_len - n_keys), DIL_RADIUS)
            q = q_ref[r, pl.ds(q_off, nq), :]
            k = k_ref[r, pl.ds(k_pos, n_keys), :]
            v = v_ref[r, pl.ds(k_pos, n_keys), :]
            s = lax.dot_general(q, k, (((1,), (1,)), ((), ())), preferred_element_type=F32)
            dist = jnp.abs(base + (k_pos - q_pos))
            s = jnp.where(dist <= DIL_RADIUS, s - slope * dist.astype(F32), NEG_INF)
            m = jnp.max(s, axis=-1, keepdims=True)
            p = jnp.exp(s - m)
            den = jnp.sum(p, axis=-1, keepdims=True)
            out = jnp.dot(p.astype(BF16), v, preferred_element_type=F32) / den
            lse = jnp.broadcast_to(m + jnp.log(den), (nq, HEAD_DIM))
            if dil == 1:
                rows = pl.ds(q_off, nq)
            else:
                rows = pl.ds(r + dil * q_off, nq, stride=dil)
            on_ref[g, rows, :] = out
            ls_ref[g, rows, :] = lse
            return carry

        lax.fori_loop(0, dil * n_sub, body, 0, unroll=DIL_UNROLL)

    def merge(c, carry):
        rows = pl.ds(pl.multiple_of(c * DIL_TQ, DIL_TQ), DIL_TQ)
        lse = [ls_ref[g, rows, :] for g in range(len(DIL_PAIRS))]
        top = jnp.maximum(jnp.maximum(lse[0], lse[1]), lse[2])
        num = jnp.zeros((DIL_TQ, HEAD_DIM), F32)
        den = jnp.zeros((DIL_TQ, HEAD_DIM), F32)
        for g in range(len(DIL_PAIRS)):
            w = jnp.exp(lse[g] - top)
            num = num + w * on_ref[g, rows, :]
            den = den + w
        o_ref[rows, :] = (num / den).astype(o_ref.dtype)
        return carry

    lax.fori_loop(0, DIL_TA // DIL_TQ, merge, 0)


def _dilated(q, k, v, slopes):
    b, _, s, _ = q[0].shape
    n_groups = len(DIL_PAIRS)

    def q_spec(dil):
        return pl.BlockSpec((None, dil, DIL_TA // dil, HEAD_DIM), lambda bi, hh, i: (bi, 0, i, hh))

    def kv_spec(dil):
        return pl.BlockSpec((None, dil, s // dil, HEAD_DIM), lambda bi, hh, i: (bi, 0, 0, hh))

    dils = [dil for _, dil in DIL_PAIRS]
    return pl.pallas_call(
        functools.partial(_dilated_kernel, seq=s),
        grid=(b, DIL_HEADS_PER_GROUP, s // DIL_TA),
        in_specs=[pl.BlockSpec(memory_space=pltpu.SMEM)]
        + [q_spec(d) for d in dils] + [kv_spec(d) for d in dils] + [kv_spec(d) for d in dils],
        out_specs=pl.BlockSpec((None, DIL_TA, HEAD_DIM), lambda bi, hh, i: (bi, i, hh)),
        out_shape=jax.ShapeDtypeStruct((b, s, DIL_OUT), BF16),
        scratch_shapes=[pltpu.VMEM((n_groups, DIL_TA, HEAD_DIM), F32),
                        pltpu.VMEM((n_groups, DIL_TA, HEAD_DIM), F32)],
        compiler_params=_params("parallel", "parallel", "arbitrary"),
        name="dilated_attn",
    )(slopes, *q, *k, *v)


def _even_out_kernel(a_ref, prev_ref, next_ref, yb_ref, pw_ref, ps_ref, wo_ref, x_ref, o_ref,
                     ext_ref, y_ref, *, seq):
    i = pl.program_id(1)
    tm = OUT_TM
    ext_ref[0:POOL_HALO, :] = jnp.where(i == 0, 0.0, prev_ref[...])
    ext_ref[POOL_HALO:POOL_HALO + tm, :] = a_ref[...]
    ext_ref[POOL_HALO + tm:, :] = jnp.where(i == pl.num_programs(1) - 1, 0.0, next_ref[...])
    pos = i * tm + lax.broadcasted_iota(jnp.int32, (tm, 1), 0)
    for g, window in enumerate(POOL_WINDOWS):
        half = window // 2
        cols = slice(g * HEAD_DIM, (g + 1) * HEAD_DIM)
        total = ext_ref[POOL_HALO - half:POOL_HALO - half + tm, cols]
        for shift in range(-half + 1, half):
            total = total + ext_ref[POOL_HALO + shift:POOL_HALO + shift + tm, cols]
        count = (jnp.minimum(pos + half, seq) - jnp.maximum(pos - half, 0)).astype(F32)
        pooled = total / count - a_ref[:, cols]
        ya = jnp.dot(pooled.astype(BF16), pw_ref[g], preferred_element_type=F32) * ps_ref[:, cols]
        y_ref[:, cols] = ya.astype(BF16)
    y_ref[:, POOL_WIDTH:] = yb_ref[...]
    o_ref[...] = x_ref[...] + jnp.dot(y_ref[...], wo_ref[...], preferred_element_type=F32)


def _even_out(a, yb, pool_w, pool_scale, w_out, x):
    b, s, _ = x.shape
    tm = OUT_TM
    halo_blocks = tm // POOL_HALO
    tile = lambda bi, i: (bi, i, 0)
    return pl.pallas_call(
        functools.partial(_even_out_kernel, seq=s),
        grid=(b, s // tm),
        in_specs=[
            pl.BlockSpec((None, tm, POOL_WIDTH), tile),
            pl.BlockSpec((None, POOL_HALO, POOL_WIDTH),
                         lambda bi, i: (bi, jnp.maximum(i * halo_blocks - 1, 0), 0)),
            pl.BlockSpec((None, POOL_HALO, POOL_WIDTH),
                         lambda bi, i: (bi, jnp.minimum((i + 1) * halo_blocks, s // POOL_HALO - 1), 0)),
            pl.BlockSpec((None, tm, DIL_OUT), tile),
            _resident((len(POOL_WINDOWS), HEAD_DIM, HEAD_DIM), lambda bi, i: (0, 0, 0)),
            pl.BlockSpec((1, POOL_WIDTH), lambda bi, i: (0, 0)),
            _resident((POOL_WIDTH + DIL_OUT, D_MODEL), lambda bi, i: (0, 0)),
            pl.BlockSpec((None, tm, D_MODEL), tile),
        ],
        out_specs=pl.BlockSpec((None, tm, D_MODEL), tile),
        out_shape=jax.ShapeDtypeStruct((b, s, D_MODEL), F32),
        scratch_shapes=[pltpu.VMEM((tm + 2 * POOL_HALO, POOL_WIDTH), F32),
                        pltpu.VMEM((tm, POOL_WIDTH + DIL_OUT), BF16)],
        compiler_params=_params("parallel", "parallel"),
        name="even_out",
    )(a, a, a, yb, pool_w, pool_scale.reshape(1, POOL_WIDTH), w_out, x)


def _na_fill_bias(pair_ref, bias_ref, rb, rows):
    k_row0 = min(max(rb * NA_QROWS - NA_ROWS // 2, 0), rows - NA_KROWS)
    left_half = lax.broadcasted_iota(jnp.int32, (GRID_W, 2 * GRID_W), 1) < GRID_W
    for qr in range(NA_QROWS):
        r = rb * NA_QROWS + qr
        row_start = min(max(r - NA_ROWS // 2, 0), rows - NA_ROWS)
        for p in range(NA_KROWS // 2):
            kr = k_row0 + 2 * p
            ok_left = row_start <= kr < row_start + NA_ROWS
            ok_right = row_start <= kr + 1 < row_start + NA_ROWS
            if ok_left or ok_right:
                block = pair_ref[kr + 1 - r + NA_ROWS - 1]
                if not ok_right:
                    block = jnp.where(left_half, block, NEG_INF)
                if not ok_left:
                    block = jnp.where(left_half, NEG_INF, block)
            else:
                block = jnp.full((GRID_W, 2 * GRID_W), NEG_INF, F32)
            bias_ref[qr * GRID_W:(qr + 1) * GRID_W, p * 2 * GRID_W:(p + 1) * 2 * GRID_W] = block


def _na_kernel(q_ref, k_ref, v_ref, pair_ref, o_ref, bias_ref, *, rows):
    n_blocks = rows // NA_QROWS
    tq = NA_QROWS * GRID_W
    n_keys = NA_KROWS * GRID_W
    kinds = (0, 1, n_blocks - 1)
    for slot, kind in enumerate(kinds):
        _na_fill_bias(pair_ref, bias_ref.at[slot], kind, rows)
    for rb in range(n_blocks):
        slot = 0 if rb == 0 else (2 if rb == n_blocks - 1 else 1)
        k_start = min(max(rb * NA_QROWS - NA_ROWS // 2, 0), rows - NA_KROWS) * GRID_W
        q = q_ref[rb * tq:(rb + 1) * tq, :]
        k = k_ref[k_start:k_start + n_keys, :]
        v = v_ref[k_start:k_start + n_keys, :]
        s = lax.dot_general(q, k, (((1,), (1,)), ((), ())), preferred_element_type=F32) + bias_ref[slot]
        m = jnp.max(s, axis=-1, keepdims=True)
        p = jnp.exp(s - m)
        den = jnp.sum(p, axis=-1, keepdims=True)
        out = jnp.dot(p.astype(BF16), v, preferred_element_type=F32) / den
        o_ref[rb * tq:(rb + 1) * tq, :] = out.astype(o_ref.dtype)


def _na_pair_tables(rpb):
    n_heads = rpb.shape[0]
    c = np.arange(GRID_W)
    col_start = np.clip(c - NA_COLS // 2, 0, GRID_W - NA_COLS)
    col_ok = (c[None, :] >= col_start[:, None]) & (c[None, :] < col_start[:, None] + NA_COLS)
    edge = GRID_W - NA_COLS
    ext = jnp.concatenate([jnp.repeat(rpb[:, :, :1], edge, axis=2), rpb.astype(F32),
                           jnp.repeat(rpb[:, :, -1:], edge, axis=2)], axis=2)
    by_col = jnp.stack([ext[:, :, GRID_W - 1 - cq:2 * GRID_W - 1 - cq] for cq in range(GRID_W)], axis=2)
    by_col = jnp.where(col_ok[None, None], by_col, NEG_INF)
    masked = jnp.full((n_heads, 1, GRID_W, GRID_W), NEG_INF, F32)
    padded = jnp.concatenate([masked, by_col, masked], axis=1)
    return jnp.concatenate([padded[:, :-1], padded[:, 1:]], axis=-1)


def _neighbourhood(q, k, v, pair_tables):
    b, s, _ = q.shape
    rows = s // GRID_W
    n_blocks = rows // NA_QROWS
    assert n_blocks >= 3 and rows >= NA_KROWS
    tq = NA_QROWS * GRID_W
    n_keys = NA_KROWS * GRID_W
    head = lambda bi, h: (bi, 0, h)
    return pl.pallas_call(
        functools.partial(_na_kernel, rows=rows),
        grid=(b, NA_HEADS),
        in_specs=[
            pl.BlockSpec((None, s, HEAD_DIM), head),
            pl.BlockSpec((None, s, HEAD_DIM), head),
            pl.BlockSpec((None, s, HEAD_DIM), head),
            pl.BlockSpec((None, 2 * NA_ROWS, GRID_W, 2 * GRID_W), lambda bi, h: (h, 0, 0, 0)),
        ],
        out_specs=pl.BlockSpec((None, s, HEAD_DIM), head),
        out_shape=jax.ShapeDtypeStruct((b, s, NA_WIDTH), BF16),
        scratch_shapes=[pltpu.VMEM((3, tq, n_keys), F32)],
        compiler_params=_params("parallel", "parallel"),
        name="neighbourhood_attn",
    )(q, k, v, pair_tables)


def _odd_out_kernel(yc_ref, u_ref, vn_ref, ws_ref, bs_ref, wo_ref, x_ref, o_ref, y_ref):
    width = SG_WIDTH // SG_GROUPS
    for c in range(OUT_TM // SG_CHUNK):
        rows = slice(c * SG_CHUNK, (c + 1) * SG_CHUNK)
        for g in range(SG_GROUPS):
            cols = slice(g * width, (g + 1) * width)
            sv = jnp.dot(ws_ref[g], vn_ref[rows, cols], preferred_element_type=F32) + bs_ref[g]
            y_ref[rows, NA_WIDTH + g * width:NA_WIDTH + (g + 1) * width] = (u_ref[rows, cols] * sv).astype(BF16)
    y_ref[:, :NA_WIDTH] = yc_ref[...]
    o_ref[...] = x_ref[...] + jnp.dot(y_ref[...], wo_ref[...], preferred_element_type=F32)


def _odd_out(yc, u, vn, w_s, b_s, w_out, x):
    b, s, _ = x.shape
    tm = OUT_TM
    width = SG_WIDTH // SG_GROUPS
    tile = lambda bi, i: (bi, i, 0)
    return pl.pallas_call(
        _odd_out_kernel,
        grid=(b, s // tm),
        in_specs=[
            pl.BlockSpec((None, tm, NA_WIDTH), tile),
            pl.BlockSpec((None, tm, SG_WIDTH), tile),
            pl.BlockSpec((None, tm, SG_WIDTH), tile),
            _resident((SG_GROUPS, SG_CHUNK, SG_CHUNK), lambda bi, i: (0, 0, 0)),
            _resident((SG_GROUPS, SG_CHUNK, width), lambda bi, i: (0, 0, 0)),
            _resident((NA_WIDTH + SG_WIDTH, D_MODEL), lambda bi, i: (0, 0)),
            pl.BlockSpec((None, tm, D_MODEL), tile),
        ],
        out_specs=pl.BlockSpec((None, tm, D_MODEL), tile),
        out_shape=jax.ShapeDtypeStruct((b, s, D_MODEL), F32),
        scratch_shapes=[pltpu.VMEM((tm, NA_WIDTH + SG_WIDTH), BF16)],
        compiler_params=_params("parallel", "parallel"),
        name="odd_out",
    )(yc, u, vn, w_s, b_s, w_out, x)


def kernel(x, norm_ffn1, norm_mix, norm_ffn2, norm_out, ffn_w_gate, ffn_w_up, ffn_w_down, even_w_in, pool_w, pool_scale, dil_q_gain, dil_k_gain, even_w_out, odd_w_in, na_q_gain, na_k_gain, na_rpb, sg_v_gain, sg_w, sg_b, odd_w_out):
    depth = norm_ffn1.shape[0]
    slopes = jnp.asarray(2.0 ** (-8.0 * np.arange(1, DIL_HEADS + 1) / DIL_HEADS), dtype=F32)
    width = SG_WIDTH // SG_GROUPS

    for layer in range(depth):
        x = _ffn(x, norm_ffn1[layer], ffn_w_gate, ffn_w_up, ffn_w_down, layer, 0)
        if layer % 2 == 0:
            e = layer // 2
            a, q, k, v = _even_in(x, norm_mix[layer], even_w_in[e].astype(BF16), dil_q_gain[e], dil_k_gain[e])
            yb = _dilated(q, k, v, slopes)
            x = _even_out(a, yb, pool_w[e].astype(BF16), pool_scale[e], even_w_out[e].astype(BF16), x)
        else:
            o = layer // 2
            q, k, v, u, vn = _odd_in(x, norm_mix[layer], odd_w_in[o].astype(BF16),
                                     na_q_gain[o], na_k_gain[o], sg_v_gain[o])
            yc = _neighbourhood(q, k, v, _na_pair_tables(na_rpb[o]))
            b_s = jnp.broadcast_to(sg_b[o][:, :, None], (SG_GROUPS, SG_CHUNK, width))
            x = _odd_out(yc, u, vn, sg_w[o].astype(BF16), b_s, odd_w_out[o].astype(BF16), x)
        x = _ffn(x, norm_ffn2[layer], ffn_w_gate, ffn_w_up, ffn_w_down, layer, 1, gain_out=norm_out[layer])
    return x
```

```python
import functools

import numpy as np
import jax
import jax.numpy as jnp
from jax import lax
from jax.experimental import pallas as pl
from jax.experimental.pallas import tpu as pltpu

F32 = jnp.float32
BF16 = jnp.bfloat16

D_MODEL = 2048
D_FF = 5632
HEAD_DIM = 128
POOL_WINDOWS = (2, 4, 8, 16)
POOL_WIDTH = 512
POOL_HALO = 8
DIL_PAIRS = ((128, 1), (512, 4), (2048, 16))
DIL_HEADS_PER_GROUP = 4
DIL_HEADS = 12
DIL_WIDTH = DIL_HEADS * HEAD_DIM
DIL_OUT = DIL_HEADS_PER_GROUP * HEAD_DIM
GRID_W = 64
NA_ROWS = 8
NA_COLS = 16
NA_HEADS = 8
NA_WIDTH = NA_HEADS * HEAD_DIM
NA_QROWS = 4
NA_KROWS = NA_QROWS + NA_ROWS
SG_CHUNK = 128
SG_GROUPS = 8
SG_WIDTH = 1024
RMS_EPS = 1e-6
NEG_INF = -1e30

VMEM_LIMIT_BYTES = 60 * 1024 * 1024

FFN_TM = 1024
FFN_TF = 512
FFN_TC = 256
PROJ_TM = 512
PROJ_TN = 512
DIL_TQ = 128
DIL_TA = 2048
DIL_RADIUS = 64
DIL_UNROLL = 8
OUT_TM = 512


def _params(*semantics):
    return pltpu.CompilerParams(dimension_semantics=semantics, vmem_limit_bytes=VMEM_LIMIT_BYTES)


def _resident(block_shape, index_map):
    return pl.BlockSpec(block_shape, index_map, pipeline_mode=pl.Buffered(1))


def _rms(x, gain):
    return x * lax.rsqrt(jnp.mean(x * x, axis=-1, keepdims=True) + RMS_EPS) * gain


def _ffn_kernel(x_hbm, g_ref, wg_ref, wu_ref, wd_ref, gout_ref, o_ref, h_ref, xbuf_ref, x_sem, *, final_norm):
    j = pl.program_id(2)
    tiles_per_batch = pl.num_programs(1)
    n_tiles = pl.num_programs(0) * tiles_per_batch
    tile = pl.program_id(0) * tiles_per_batch + pl.program_id(1)

    def x_copy(t):
        rows = pl.ds(pl.multiple_of((t % tiles_per_batch) * FFN_TM, FFN_TM), FFN_TM)
        return pltpu.make_async_copy(x_hbm.at[t // tiles_per_batch, rows, :], xbuf_ref, x_sem)

    @pl.when(j == 0)
    def _():
        @pl.when(tile == 0)
        def _():
            x_copy(tile).start()

        x_copy(tile).wait()
        x = xbuf_ref[...]
        h_ref[...] = _rms(x, g_ref[...]).astype(BF16)
        o_ref[...] = x

    @pl.when((j == 1) & (tile + 1 < n_tiles))
    def _():
        x_copy(tile + 1).start()

    h = h_ref[...]
    for c in range(FFN_TF // FFN_TC):
        cols = slice(c * FFN_TC, (c + 1) * FFN_TC)
        gate = jnp.dot(h, wg_ref[:, cols].astype(BF16), preferred_element_type=F32)
        up = jnp.dot(h, wu_ref[:, cols].astype(BF16), preferred_element_type=F32)
        act = (gate * jax.nn.sigmoid(gate)) * up * 0.5
        o_ref[...] += jnp.dot(act.astype(BF16), wd_ref[cols, :].astype(BF16), preferred_element_type=F32)

    if final_norm:
        @pl.when(j == pl.num_programs(2) - 1)
        def _():
            o_ref[...] = _rms(o_ref[...], gout_ref[...])


def _ffn(x, gain, w_gate, w_up, w_down, layer, idx, gain_out=None):
    b, s, _ = x.shape
    assert D_FF // FFN_TF >= 2
    final_norm = gain_out is not None
    if gain_out is None:
        gain_out = gain
    tile = lambda bi, i, j: (bi, i, 0)
    const = lambda bi, i, j: (0, 0)
    return pl.pallas_call(
        functools.partial(_ffn_kernel, final_norm=final_norm),
        grid=(b, s // FFN_TM, D_FF // FFN_TF),
        in_specs=[
            pl.BlockSpec(memory_space=pl.ANY),
            pl.BlockSpec((1, D_MODEL), const),
            pl.BlockSpec((None, None, D_MODEL, FFN_TF), lambda bi, i, j: (layer, idx, 0, j)),
            pl.BlockSpec((None, None, D_MODEL, FFN_TF), lambda bi, i, j: (layer, idx, 0, j)),
            pl.BlockSpec((None, None, FFN_TF, D_MODEL), lambda bi, i, j: (layer, idx, j, 0)),
            pl.BlockSpec((1, D_MODEL), const),
        ],
        out_specs=pl.BlockSpec((None, FFN_TM, D_MODEL), tile),
        out_shape=jax.ShapeDtypeStruct((b, s, D_MODEL), F32),
        scratch_shapes=[pltpu.VMEM((FFN_TM, D_MODEL), BF16), pltpu.VMEM((FFN_TM, D_MODEL), F32),
                        pltpu.SemaphoreType.DMA(())],
        compiler_params=_params("arbitrary", "arbitrary", "arbitrary"),
        name="ffn_final" if final_norm else "ffn",
    )(x, gain.reshape(1, D_MODEL), w_gate, w_up, w_down, gain_out.reshape(1, D_MODEL))


def _head_norm_store(z, out_ref, col0, gain):
    for hd in range(PROJ_TN // HEAD_DIM):
        zh = z[:, hd * HEAD_DIM:(hd + 1) * HEAD_DIM]
        out_ref[:, col0 + hd * HEAD_DIM:col0 + (hd + 1) * HEAD_DIM] = _rms(zh, gain).astype(out_ref.dtype)


def _store_by_residue(z, out_ref, stage_ref, dil, gain=None):
    tm = z.shape[0]
    for hd in range(z.shape[1] // HEAD_DIM):
        cols = slice(hd * HEAD_DIM, (hd + 1) * HEAD_DIM)
        zh = z[:, cols]
        if gain is not None:
            zh = _rms(zh, gain)
        if dil == 1:
            out_ref[0, :, cols] = zh.astype(BF16)
        else:
            stage_ref[hd] = zh
            for r in range(dil):
                out_ref[r, :, cols] = stage_ref[hd, pl.ds(r, tm // dil, stride=dil), :].astype(BF16)


def _even_in_kernel(x_ref, g_ref, w_ref, qg_ref, kg_ref, a_ref, *rest):
    n_groups = len(DIL_PAIRS)
    q_refs, k_refs, v_refs = rest[:n_groups], rest[n_groups:2 * n_groups], rest[2 * n_groups:3 * n_groups]
    h_ref, stage_ref = rest[3 * n_groups:]
    h_ref[...] = _rms(x_ref[...], g_ref[...]).astype(BF16)

    def chunk(c):
        return jnp.dot(h_ref[...], w_ref[:, c * PROJ_TN:(c + 1) * PROJ_TN], preferred_element_type=F32)

    a_ref[...] = chunk(0)
    for g, (_, dil) in enumerate(DIL_PAIRS):
        _store_by_residue(chunk(1 + g), q_refs[g], stage_ref, dil, qg_ref[...])
        _store_by_residue(chunk(1 + n_groups + g), k_refs[g], stage_ref, dil, kg_ref[...])
        _store_by_residue(chunk(1 + 2 * n_groups + g), v_refs[g], stage_ref, dil)


def _even_in(x, gain, w_in, q_gain, k_gain):
    b, s, _ = x.shape
    n_in = w_in.shape[1]
    assert PROJ_TN == DIL_OUT
    row = lambda bi, i: (bi, i, 0)
    const = lambda bi, i: (0, 0)
    group_specs = [pl.BlockSpec((None, dil, PROJ_TM // dil, DIL_OUT), lambda bi, i: (bi, 0, i, 0))
                   for _, dil in DIL_PAIRS]
    group_shapes = [jax.ShapeDtypeStruct((b, dil, s // dil, DIL_OUT), BF16) for _, dil in DIL_PAIRS]
    outs = pl.pallas_call(
        _even_in_kernel,
        grid=(b, s // PROJ_TM),
        in_specs=[
            pl.BlockSpec((None, PROJ_TM, D_MODEL), row),
            pl.BlockSpec((1, D_MODEL), const),
            _resident((D_MODEL, n_in), const),
            pl.BlockSpec((1, HEAD_DIM), const),
            pl.BlockSpec((1, HEAD_DIM), const),
        ],
        out_specs=[pl.BlockSpec((None, PROJ_TM, POOL_WIDTH), row)] + group_specs * 3,
        out_shape=[jax.ShapeDtypeStruct((b, s, POOL_WIDTH), F32)] + group_shapes * 3,
        scratch_shapes=[pltpu.VMEM((PROJ_TM, D_MODEL), BF16),
                        pltpu.VMEM((DIL_HEADS_PER_GROUP, PROJ_TM, HEAD_DIM), F32)],
        compiler_params=_params("parallel", "parallel"),
        name="even_in",
    )(x, gain.reshape(1, D_MODEL), w_in,
      (q_gain * HEAD_DIM ** -0.5).reshape(1, HEAD_DIM), k_gain.reshape(1, HEAD_DIM))
    n_groups = len(DIL_PAIRS)
    return outs[0], outs[1:1 + n_groups], outs[1 + n_groups:1 + 2 * n_groups], outs[1 + 2 * n_groups:]


def _odd_in_kernel(x_ref, g_ref, w_ref, qg_ref, kg_ref, vg_ref, q_ref, k_ref, v_ref, u_ref, vn_ref,
                   h_ref, gv_ref):
    h_ref[...] = _rms(x_ref[...], g_ref[...]).astype(BF16)

    def chunk(c):
        return jnp.dot(h_ref[...], w_ref[:, c * PROJ_TN:(c + 1) * PROJ_TN], preferred_element_type=F32)

    per = NA_WIDTH // PROJ_TN
    for c in range(per):
        _head_norm_store(chunk(c), q_ref, c * PROJ_TN, qg_ref[...])
        _head_norm_store(chunk(per + c), k_ref, c * PROJ_TN, kg_ref[...])
        v_ref[:, c * PROJ_TN:(c + 1) * PROJ_TN] = chunk(2 * per + c).astype(BF16)
    per_sg = SG_WIDTH // PROJ_TN
    for c in range(per_sg):
        u_ref[:, c * PROJ_TN:(c + 1) * PROJ_TN] = jax.nn.gelu(chunk(3 * per + c))
        gv_ref[:, c * PROJ_TN:(c + 1) * PROJ_TN] = jax.nn.gelu(chunk(3 * per + per_sg + c))
    vn_ref[...] = _rms(gv_ref[...], vg_ref[...]).astype(BF16)


def _odd_in(x, gain, w_in, q_gain, k_gain, v_gain):
    b, s, _ = x.shape
    n_in = w_in.shape[1]
    row = lambda bi, i: (bi, i, 0)
    const = lambda bi, i: (0, 0)
    return pl.pallas_call(
        _odd_in_kernel,
        grid=(b, s // PROJ_TM),
        in_specs=[
            pl.BlockSpec((None, PROJ_TM, D_MODEL), row),
            pl.BlockSpec((1, D_MODEL), const),
            _resident((D_MODEL, n_in), const),
            pl.BlockSpec((1, HEAD_DIM), const),
            pl.BlockSpec((1, HEAD_DIM), const),
            pl.BlockSpec((1, SG_WIDTH), const),
        ],
        out_specs=[
            pl.BlockSpec((None, PROJ_TM, NA_WIDTH), row),
            pl.BlockSpec((None, PROJ_TM, NA_WIDTH), row),
            pl.BlockSpec((None, PROJ_TM, NA_WIDTH), row),
            pl.BlockSpec((None, PROJ_TM, SG_WIDTH), row),
            pl.BlockSpec((None, PROJ_TM, SG_WIDTH), row),
        ],
        out_shape=[
            jax.ShapeDtypeStruct((b, s, NA_WIDTH), BF16),
            jax.ShapeDtypeStruct((b, s, NA_WIDTH), BF16),
            jax.ShapeDtypeStruct((b, s, NA_WIDTH), BF16),
            jax.ShapeDtypeStruct((b, s, SG_WIDTH), F32),
            jax.ShapeDtypeStruct((b, s, SG_WIDTH), BF16),
        ],
        scratch_shapes=[pltpu.VMEM((PROJ_TM, D_MODEL), BF16), pltpu.VMEM((PROJ_TM, SG_WIDTH), F32)],
        compiler_params=_params("parallel", "parallel"),
        name="odd_in",
    )(x, gain.reshape(1, D_MODEL), w_in,
      (q_gain * HEAD_DIM ** -0.5).reshape(1, HEAD_DIM), k_gain.reshape(1, HEAD_DIM),
      v_gain.reshape(1, SG_WIDTH))


def _dilated_kernel(slopes_ref, q0_ref, q1_ref, q2_ref, k0_ref, k1_ref, k2_ref, v0_ref, v1_ref, v2_ref,
                    o_ref, on_ref, ls_ref, *, seq):
    hh = pl.program_id(1)
    step = pl.program_id(2)
    q_refs = (q0_ref, q1_ref, q2_ref)
    k_refs = (k0_ref, k1_ref, k2_ref)
    v_refs = (v0_ref, v1_ref, v2_ref)
    for g, (window, dil) in enumerate(DIL_PAIRS):
        assert window // (2 * dil) == DIL_RADIUS
        q_ref, k_ref, v_ref = q_refs[g], k_refs[g], v_refs[g]
        sub_len = seq // dil
        per_res = DIL_TA // dil
        nq = min(per_res, DIL_TQ)
        n_sub = per_res // nq
        n_keys = min(nq + 2 * DIL_RADIUS, sub_len)
        slope = slopes_ref[g * DIL_HEADS_PER_GROUP + hh] * dil
        base = (lax.broadcasted_iota(jnp.int32, (nq, n_keys), 1)
                - lax.broadcasted_iota(jnp.int32, (nq, n_keys), 0))

        def body(t, carry, q_ref=q_ref, k_ref=k_ref, v_ref=v_ref, g=g, dil=dil, sub_len=sub_len,
                 per_res=per_res, nq=nq, n_sub=n_sub, n_keys=n_keys, slope=slope, base=base):
            r = t // n_sub
            q_off = pl.multiple_of((t % n_sub) * nq, nq)
            q_pos = step * per_res + q_off
            k_pos = pl.multiple_of(jnp.clip(q_pos - DIL_RADIUS, 0, sub_len - n_keys), DIL_RADIUS)
            q = q_ref[r, pl.ds(q_off, nq), :]
            k = k_ref[r, pl.ds(k_pos, n_keys), :]
            v = v_ref[r, pl.ds(k_pos, n_keys), :]
            s = lax.dot_general(q, k, (((1,), (1,)), ((), ())), preferred_element_type=F32)
            dist = jnp.abs(base + (k_pos - q_pos))
            s = jnp.where(dist <= DIL_RADIUS, s - slope * dist.astype(F32), NEG_INF)
            m = jnp.max(s, axis=-1, keepdims=True)
            p = jnp.exp(s - m)
            den = jnp.sum(p, axis=-1, keepdims=True)
            out = jnp.dot(p.astype(BF16), v, preferred_element_type=F32) / den
            lse = jnp.broadcast_to(m + jnp.log(den), (nq, HEAD_DIM))
            if dil == 1:
                rows = pl.ds(q_off, nq)
            else:
                rows = pl.ds(r + dil * q_off, nq, stride=dil)
            on_ref[g, rows, :] = out
            ls_ref[g, rows, :] = lse
            return carry

        lax.fori_loop(0, dil * n_sub, body, 0, unroll=DIL_UNROLL)

    def merge(c, carry):
        rows = pl.ds(pl.multiple_of(c * DIL_TQ, DIL_TQ), DIL_TQ)
        lse = [ls_ref[g, rows, :] for g in range(len(DIL_PAIRS))]
        top = jnp.maximum(jnp.maximum(lse[0], lse[1]), lse[2])
        num = jnp.zeros((DIL_TQ, HEAD_DIM), F32)
        den = jnp.zeros((DIL_TQ, HEAD_DIM), F32)
        for g in range(len(DIL_PAIRS)):
            w = jnp.exp(lse[g] - top)
            num = num + w * on_ref[g, rows, :]
            den = den + w
        o_ref[rows, :] = (num / den).astype(o_ref.dtype)
        return carry

    lax.fori_loop(0, DIL_TA // DIL_TQ, merge, 0)


def _dilated(q, k, v, slopes):
    b, _, s, _ = q[0].shape
    n_groups = len(DIL_PAIRS)

    def q_spec(dil):
        return pl.BlockSpec((None, dil, DIL_TA // dil, HEAD_DIM), lambda bi, hh, i: (bi, 0, i, hh))

    def kv_spec(dil):
        return pl.BlockSpec((None, dil, s // dil, HEAD_DIM), lambda bi, hh, i: (bi, 0, 0, hh))

    dils = [dil for _, dil in DIL_PAIRS]
    return pl.pallas_call(
        functools.partial(_dilated_kernel, seq=s),
        grid=(b, DIL_HEADS_PER_GROUP, s // DIL_TA),
        in_specs=[pl.BlockSpec(memory_space=pltpu.SMEM)]
        + [q_spec(d) for d in dils] + [kv_spec(d) for d in dils] + [kv_spec(d) for d in dils],
        out_specs=pl.BlockSpec((None, DIL_TA, HEAD_DIM), lambda bi, hh, i: (bi, i, hh)),
        out_shape=jax.ShapeDtypeStruct((b, s, DIL_OUT), BF16),
        scratch_shapes=[pltpu.VMEM((n_groups, DIL_TA, HEAD_DIM), F32),
                        pltpu.VMEM((n_groups, DIL_TA, HEAD_DIM), F32)],
        compiler_params=_params("parallel", "parallel", "arbitrary"),
        name="dilated_attn",
    )(slopes, *q, *k, *v)


def _even_out_kernel(a_ref, prev_ref, next_ref, yb_ref, pw_ref, ps_ref, wo_ref, x_ref, o_ref,
                     ext_ref, y_ref, *, seq):
    i = pl.program_id(1)
    tm = OUT_TM
    ext_ref[0:POOL_HALO, :] = jnp.where(i == 0, 0.0, prev_ref[...])
    ext_ref[POOL_HALO:POOL_HALO + tm, :] = a_ref[...]
    ext_ref[POOL_HALO + tm:, :] = jnp.where(i == pl.num_programs(1) - 1, 0.0, next_ref[...])
    pos = i * tm + lax.broadcasted_iota(jnp.int32, (tm, 1), 0)
    for g, window in enumerate(POOL_WINDOWS):
        half = window // 2
        cols = slice(g * HEAD_DIM, (g + 1) * HEAD_DIM)
        total = ext_ref[POOL_HALO - half:POOL_HALO - half + tm, cols]
        for shift in range(-half + 1, half):
            total = total + ext_ref[POOL_HALO + shift:POOL_HALO + shift + tm, cols]
        count = (jnp.minimum(pos + half, seq) - jnp.maximum(pos - half, 0)).astype(F32)
        pooled = total / count - a_ref[:, cols]
        ya = jnp.dot(pooled.astype(BF16), pw_ref[g], preferred_element_type=F32) * ps_ref[:, cols]
        y_ref[:, cols] = ya.astype(BF16)
    y_ref[:, POOL_WIDTH:] = yb_ref[...]
    o_ref[...] = x_ref[...] + jnp.dot(y_ref[...], wo_ref[...], preferred_element_type=F32)


def _even_out(a, yb, pool_w, pool_scale, w_out, x):
    b, s, _ = x.shape
    tm = OUT_TM
    halo_blocks = tm // POOL_HALO
    tile = lambda bi, i: (bi, i, 0)
    return pl.pallas_call(
        functools.partial(_even_out_kernel, seq=s),
        grid=(b, s // tm),
        in_specs=[
            pl.BlockSpec((None, tm, POOL_WIDTH), tile),
            pl.BlockSpec((None, POOL_HALO, POOL_WIDTH),
                         lambda bi, i: (bi, jnp.maximum(i * halo_blocks - 1, 0), 0)),
            pl.BlockSpec((None, POOL_HALO, POOL_WIDTH),
                         lambda bi, i: (bi, jnp.minimum((i + 1) * halo_blocks, s // POOL_HALO - 1), 0)),
            pl.BlockSpec((None, tm, DIL_OUT), tile),
            _resident((len(POOL_WINDOWS), HEAD_DIM, HEAD_DIM), lambda bi, i: (0, 0, 0)),
            pl.BlockSpec((1, POOL_WIDTH), lambda bi, i: (0, 0)),
            _resident((POOL_WIDTH + DIL_OUT, D_MODEL), lambda bi, i: (0, 0)),
            pl.BlockSpec((None, tm, D_MODEL), tile),
        ],
        out_specs=pl.BlockSpec((None, tm, D_MODEL), tile),
        out_shape=jax.ShapeDtypeStruct((b, s, D_MODEL), F32),
        scratch_shapes=[pltpu.VMEM((tm + 2 * POOL_HALO, POOL_WIDTH), F32),
                        pltpu.VMEM((tm, POOL_WIDTH + DIL_OUT), BF16)],
        compiler_params=_params("parallel", "parallel"),
        name="even_out",
    )(a, a, a, yb, pool_w, pool_scale.reshape(1, POOL_WIDTH), w_out, x)


def _na_fill_bias(pair_ref, bias_ref, rb, rows):
    k_row0 = min(max(rb * NA_QROWS - NA_ROWS // 2, 0), rows - NA_KROWS)
    left_half = lax.broadcasted_iota(jnp.int32, (GRID_W, 2 * GRID_W), 1) < GRID_W
    for qr in range(NA_QROWS):
        r = rb * NA_QROWS + qr
        row_start = min(max(r - NA_ROWS // 2, 0), rows - NA_ROWS)
        for p in range(NA_KROWS // 2):
            kr = k_row0 + 2 * p
            ok_left = row_start <= kr < row_start + NA_ROWS
            ok_right = row_start <= kr + 1 < row_start + NA_ROWS
            if ok_left or ok_right:
                block = pair_ref[kr + 1 - r + NA_ROWS - 1]
                if not ok_right:
                    block = jnp.where(left_half, block, NEG_INF)
                if not ok_left:
                    block = jnp.where(left_half, NEG_INF, block)
            else:
                block = jnp.full((GRID_W, 2 * GRID_W), NEG_INF, F32)
            bias_ref[qr * GRID_W:(qr + 1) * GRID_W, p * 2 * GRID_W:(p + 1) * 2 * GRID_W] = block


def _na_kernel(q_ref, k_ref, v_ref, pair_ref, o_ref, bias_ref, *, rows):
    n_blocks = rows // NA_QROWS
    tq = NA_QROWS * GRID_W
    n_keys = NA_KROWS * GRID_W
    kinds = (0, 1, n_blocks - 1)
    for slot, kind in enumerate(kinds):
        _na_fill_bias(pair_ref, bias_ref.at[slot], kind, rows)
    for rb in range(n_blocks):
        slot = 0 if rb == 0 else (2 if rb == n_blocks - 1 else 1)
        k_start = min(max(rb * NA_QROWS - NA_ROWS // 2, 0), rows - NA_KROWS) * GRID_W
        q = q_ref[rb * tq:(rb + 1) * tq, :]
        k = k_ref[k_start:k_start + n_keys, :]
        v = v_ref[k_start:k_start + n_keys, :]
        s = lax.dot_general(q, k, (((1,), (1,)), ((), ())), preferred_element_type=F32) + bias_ref[slot]
        m = jnp.max(s, axis=-1, keepdims=True)
        p = jnp.exp(s - m)
        den = jnp.sum(p, axis=-1, keepdims=True)
        out = jnp.dot(p.astype(BF16), v, preferred_element_type=F32) / den
        o_ref[rb * tq:(rb + 1) * tq, :] = out.astype(o_ref.dtype)


def _na_pair_tables(rpb):
    n_heads, n_rel_rows, n_rel_cols = rpb.shape
    c = np.arange(GRID_W)
    col_start = np.clip(c - NA_COLS // 2, 0, GRID_W - NA_COLS)
    col_ok = (c[None, :] >= col_start[:, None]) & (c[None, :] < col_start[:, None] + NA_COLS)
    rel_c = np.clip(c[None, :] - c[:, None], -(NA_COLS - 1), NA_COLS - 1) + (NA_COLS - 1)
    pick = (rel_c.reshape(1, -1) == np.arange(n_rel_cols)[:, None]).astype(np.float32)
    by_col = jnp.dot(rpb.astype(F32).reshape(n_heads * n_rel_rows, n_rel_cols), pick,
                     precision=lax.Precision.HIGHEST).reshape(n_heads, n_rel_rows, GRID_W, GRID_W)
    by_col = jnp.where(col_ok[None, None], by_col, NEG_INF)
    masked = jnp.full((n_heads, 1, GRID_W, GRID_W), NEG_INF, F32)
    padded = jnp.concatenate([masked, by_col, masked], axis=1)
    return jnp.concatenate([padded[:, :-1], padded[:, 1:]], axis=-1)


def _neighbourhood(q, k, v, pair_tables):
    b, s, _ = q.shape
    rows = s // GRID_W
    n_blocks = rows // NA_QROWS
    assert n_blocks >= 3 and rows >= NA_KROWS
    tq = NA_QROWS * GRID_W
    n_keys = NA_KROWS * GRID_W
    head = lambda bi, h: (bi, 0, h)
    return pl.pallas_call(
        functools.partial(_na_kernel, rows=rows),
        grid=(b, NA_HEADS),
        in_specs=[
            pl.BlockSpec((None, s, HEAD_DIM), head),
            pl.BlockSpec((None, s, HEAD_DIM), head),
            pl.BlockSpec((None, s, HEAD_DIM), head),
            pl.BlockSpec((None, 2 * NA_ROWS, GRID_W, 2 * GRID_W), lambda bi, h: (h, 0, 0, 0)),
        ],
        out_specs=pl.BlockSpec((None, s, HEAD_DIM), head),
        out_shape=jax.ShapeDtypeStruct((b, s, NA_WIDTH), BF16),
        scratch_shapes=[pltpu.VMEM((3, tq, n_keys), F32)],
        compiler_params=_params("parallel", "parallel"),
        name="neighbourhood_attn",
    )(q, k, v, pair_tables)


def _odd_out_kernel(yc_ref, u_ref, vn_ref, ws_ref, bs_ref, wo_ref, x_ref, o_ref, y_ref):
    width = SG_WIDTH // SG_GROUPS
    for c in range(OUT_TM // SG_CHUNK):
        rows = slice(c * SG_CHUNK, (c + 1) * SG_CHUNK)
        for g in range(SG_GROUPS):
            cols = slice(g * width, (g + 1) * width)
            sv = jnp.dot(ws_ref[g], vn_ref[rows, cols], preferred_element_type=F32) + bs_ref[g]
            y_ref[rows, NA_WIDTH + g * width:NA_WIDTH + (g + 1) * width] = (u_ref[rows, cols] * sv).astype(BF16)
    y_ref[:, :NA_WIDTH] = yc_ref[...]
    o_ref[...] = x_ref[...] + jnp.dot(y_ref[...], wo_ref[...], preferred_element_type=F32)


def _odd_out(yc, u, vn, w_s, b_s, w_out, x):
    b, s, _ = x.shape
    tm = OUT_TM
    width = SG_WIDTH // SG_GROUPS
    tile = lambda bi, i: (bi, i, 0)
    return pl.pallas_call(
        _odd_out_kernel,
        grid=(b, s // tm),
        in_specs=[
            pl.BlockSpec((None, tm, NA_WIDTH), tile),
            pl.BlockSpec((None, tm, SG_WIDTH), tile),
            pl.BlockSpec((None, tm, SG_WIDTH), tile),
            _resident((SG_GROUPS, SG_CHUNK, SG_CHUNK), lambda bi, i: (0, 0, 0)),
            _resident((SG_GROUPS, SG_CHUNK, width), lambda bi, i: (0, 0, 0)),
            _resident((NA_WIDTH + SG_WIDTH, D_MODEL), lambda bi, i: (0, 0)),
            pl.BlockSpec((None, tm, D_MODEL), tile),
        ],
        out_specs=pl.BlockSpec((None, tm, D_MODEL), tile),
        out_shape=jax.ShapeDtypeStruct((b, s, D_MODEL), F32),
        scratch_shapes=[pltpu.VMEM((tm, NA_WIDTH + SG_WIDTH), BF16)],
        compiler_params=_params("parallel", "parallel"),
        name="odd_out",
    )(yc, u, vn, w_s, b_s, w_out, x)


def kernel(x, norm_ffn1, norm_mix, norm_ffn2, norm_out, ffn_w_gate, ffn_w_up, ffn_w_down, even_w_in, pool_w, pool_scale, dil_q_gain, dil_k_gain, even_w_out, odd_w_in, na_q_gain, na_k_gain, na_rpb, sg_v_gain, sg_w, sg_b, odd_w_out):
    depth = norm_ffn1.shape[0]
    slopes = jnp.asarray(2.0 ** (-8.0 * np.arange(1, DIL_HEADS + 1) / DIL_HEADS), dtype=F32)
    width = SG_WIDTH // SG_GROUPS

    for layer in range(depth):
        x = _ffn(x, norm_ffn1[layer], ffn_w_gate, ffn_w_up, ffn_w_down, layer, 0)
        if layer % 2 == 0:
            e = layer // 2
            a, q, k, v = _even_in(x, norm_mix[layer], even_w_in[e].astype(BF16), dil_q_gain[e], dil_k_gain[e])
            yb = _dilated(q, k, v, slopes)
            x = _even_out(a, yb, pool_w[e].astype(BF16), pool_scale[e], even_w_out[e].astype(BF16), x)
        else:
            o = layer // 2
            q, k, v, u, vn = _odd_in(x, norm_mix[layer], odd_w_in[o].astype(BF16),
                                     na_q_gain[o], na_k_gain[o], sg_v_gain[o])
            yc = _neighbourhood(q, k, v, _na_pair_tables(na_rpb[o]))
            b_s = jnp.broadcast_to(sg_b[o][:, :, None], (SG_GROUPS, SG_CHUNK, width))
            x = _odd_out(yc, u, vn, sg_w[o].astype(BF16), b_s, odd_w_out[o].astype(BF16), x)
        x = _ffn(x, norm_ffn2[layer], ffn_w_gate, ffn_w_up, ffn_w_down, layer, 1, gain_out=norm_out[layer])
    return x
```

```python
import functools

import numpy as np
import jax
import jax.numpy as jnp
from jax import lax
from jax.experimental import pallas as pl
from jax.experimental.pallas import tpu as pltpu

F32 = jnp.float32
BF16 = jnp.bfloat16

D_MODEL = 2048
D_FF = 5632
HEAD_DIM = 128
POOL_WINDOWS = (2, 4, 8, 16)
POOL_WIDTH = 512
POOL_HALO = 8
DIL_PAIRS = ((128, 1), (512, 4), (2048, 16))
DIL_HEADS_PER_GROUP = 4
DIL_HEADS = 12
DIL_WIDTH = DIL_HEADS * HEAD_DIM
DIL_OUT = DIL_HEADS_PER_GROUP * HEAD_DIM
GRID_W = 64
NA_ROWS = 8
NA_COLS = 16
NA_HEADS = 8
NA_WIDTH = NA_HEADS * HEAD_DIM
NA_QROWS = 4
NA_KROWS = NA_QROWS + NA_ROWS
SG_CHUNK = 128
SG_GROUPS = 8
SG_WIDTH = 1024
RMS_EPS = 1e-6
NEG_INF = -1e30

VMEM_LIMIT_BYTES = 60 * 1024 * 1024

FFN_TM = 1024
FFN_TF = 512
FFN_TC = 256
PROJ_TM = 512
PROJ_TN = 512
DIL_TQ = 128
DIL_TA = 2048
DIL_RADIUS = 64
DIL_UNROLL = 8
OUT_TM = 512
W_CHUNK = 512


def _params(*semantics):
    return pltpu.CompilerParams(dimension_semantics=semantics, vmem_limit_bytes=VMEM_LIMIT_BYTES)


def _resident(block_shape, index_map):
    return pl.BlockSpec(block_shape, index_map, pipeline_mode=pl.Buffered(1))


def _rms(x, gain):
    return x * lax.rsqrt(jnp.mean(x * x, axis=-1, keepdims=True) + RMS_EPS) * gain


def _first_grid_step():
    return (pl.program_id(0) == 0) & (pl.program_id(1) == 0)


def _load_weight_as_bf16(w_hbm, wbf_ref, stage_ref, sems):
    n_chunks = wbf_ref.shape[1] // W_CHUNK

    def copy(c):
        cols = pl.ds(c * W_CHUNK, W_CHUNK)
        return pltpu.make_async_copy(w_hbm.at[:, cols], stage_ref.at[c % 2], sems.at[c % 2])

    copy(0).start()
    for c in range(n_chunks):
        if c + 1 < n_chunks:
            copy(c + 1).start()
        copy(c).wait()
        wbf_ref[:, c * W_CHUNK:(c + 1) * W_CHUNK] = stage_ref[c % 2].astype(BF16)


def _weight_scratch(rows, cols):
    assert cols % W_CHUNK == 0
    return [pltpu.VMEM((rows, cols), BF16), pltpu.VMEM((2, rows, W_CHUNK), F32), pltpu.SemaphoreType.DMA((2,))]


def _ffn_kernel(x_hbm, g_ref, wg_ref, wu_ref, wd_ref, gout_ref, o_ref, h_ref, xbuf_ref, x_sem, *, final_norm):
    j = pl.program_id(2)
    tiles_per_batch = pl.num_programs(1)
    n_tiles = pl.num_programs(0) * tiles_per_batch
    tile = pl.program_id(0) * tiles_per_batch + pl.program_id(1)

    def x_copy(t):
        rows = pl.ds(pl.multiple_of((t % tiles_per_batch) * FFN_TM, FFN_TM), FFN_TM)
        return pltpu.make_async_copy(x_hbm.at[t // tiles_per_batch, rows, :], xbuf_ref, x_sem)

    @pl.when(j == 0)
    def _():
        @pl.when(tile == 0)
        def _():
            x_copy(tile).start()

        x_copy(tile).wait()
        x = xbuf_ref[...]
        h_ref[...] = _rms(x, g_ref[...]).astype(BF16)
        o_ref[...] = x

    @pl.when((j == 1) & (tile + 1 < n_tiles))
    def _():
        x_copy(tile + 1).start()

    h = h_ref[...]
    for c in range(FFN_TF // FFN_TC):
        cols = slice(c * FFN_TC, (c + 1) * FFN_TC)
        gate = jnp.dot(h, wg_ref[:, cols].astype(BF16), preferred_element_type=F32)
        up = jnp.dot(h, wu_ref[:, cols].astype(BF16), preferred_element_type=F32)
        act = (gate * jax.nn.sigmoid(gate)) * up * 0.5
        o_ref[...] += jnp.dot(act.astype(BF16), wd_ref[cols, :].astype(BF16), preferred_element_type=F32)

    if final_norm:
        @pl.when(j == pl.num_programs(2) - 1)
        def _():
            o_ref[...] = _rms(o_ref[...], gout_ref[...])


def _ffn(x, gain, w_gate, w_up, w_down, layer, idx, gain_out=None):
    b, s, _ = x.shape
    assert D_FF // FFN_TF >= 2
    final_norm = gain_out is not None
    if gain_out is None:
        gain_out = gain
    tile = lambda bi, i, j: (bi, i, 0)
    const = lambda bi, i, j: (0, 0)
    return pl.pallas_call(
        functools.partial(_ffn_kernel, final_norm=final_norm),
        grid=(b, s // FFN_TM, D_FF // FFN_TF),
        in_specs=[
            pl.BlockSpec(memory_space=pl.ANY),
            pl.BlockSpec((1, D_MODEL), const),
            pl.BlockSpec((None, None, D_MODEL, FFN_TF), lambda bi, i, j: (layer, idx, 0, j)),
            pl.BlockSpec((None, None, D_MODEL, FFN_TF), lambda bi, i, j: (layer, idx, 0, j)),
            pl.BlockSpec((None, None, FFN_TF, D_MODEL), lambda bi, i, j: (layer, idx, j, 0)),
            pl.BlockSpec((1, D_MODEL), const),
        ],
        out_specs=pl.BlockSpec((None, FFN_TM, D_MODEL), tile),
        out_shape=jax.ShapeDtypeStruct((b, s, D_MODEL), F32),
        scratch_shapes=[pltpu.VMEM((FFN_TM, D_MODEL), BF16), pltpu.VMEM((FFN_TM, D_MODEL), F32),
                        pltpu.SemaphoreType.DMA(())],
        compiler_params=_params("arbitrary", "arbitrary", "arbitrary"),
        name="ffn_final" if final_norm else "ffn",
    )(x, gain.reshape(1, D_MODEL), w_gate, w_up, w_down, gain_out.reshape(1, D_MODEL))


def _head_norm_store(z, out_ref, col0, gain):
    for hd in range(PROJ_TN // HEAD_DIM):
        zh = z[:, hd * HEAD_DIM:(hd + 1) * HEAD_DIM]
        out_ref[:, col0 + hd * HEAD_DIM:col0 + (hd + 1) * HEAD_DIM] = _rms(zh, gain).astype(out_ref.dtype)


def _store_by_residue(z, out_ref, stage_ref, dil, gain=None):
    tm = z.shape[0]
    for hd in range(z.shape[1] // HEAD_DIM):
        cols = slice(hd * HEAD_DIM, (hd + 1) * HEAD_DIM)
        zh = z[:, cols]
        if gain is not None:
            zh = _rms(zh, gain)
        if dil == 1:
            out_ref[0, :, cols] = zh.astype(BF16)
        else:
            stage_ref[hd] = zh
            for r in range(dil):
                out_ref[r, :, cols] = stage_ref[hd, pl.ds(r, tm // dil, stride=dil), :].astype(BF16)


def _even_in_kernel(x_ref, g_ref, w_hbm, qg_ref, kg_ref, a_ref, *rest):
    n_groups = len(DIL_PAIRS)
    q_refs, k_refs, v_refs = rest[:n_groups], rest[n_groups:2 * n_groups], rest[2 * n_groups:3 * n_groups]
    h_ref, stage_ref, w_ref, wstage_ref, w_sems = rest[3 * n_groups:]
    pl.when(_first_grid_step())(functools.partial(_load_weight_as_bf16, w_hbm, w_ref, wstage_ref, w_sems))
    h_ref[...] = _rms(x_ref[...], g_ref[...]).astype(BF16)

    def chunk(c):
        return jnp.dot(h_ref[...], w_ref[:, c * PROJ_TN:(c + 1) * PROJ_TN], preferred_element_type=F32)

    a_ref[...] = chunk(0)
    for g, (_, dil) in enumerate(DIL_PAIRS):
        _store_by_residue(chunk(1 + g), q_refs[g], stage_ref, dil, qg_ref[...])
        _store_by_residue(chunk(1 + n_groups + g), k_refs[g], stage_ref, dil, kg_ref[...])
        _store_by_residue(chunk(1 + 2 * n_groups + g), v_refs[g], stage_ref, dil)


def _even_in(x, gain, w_in, q_gain, k_gain):
    b, s, _ = x.shape
    n_in = w_in.shape[1]
    assert PROJ_TN == DIL_OUT
    row = lambda bi, i: (bi, i, 0)
    const = lambda bi, i: (0, 0)
    group_specs = [pl.BlockSpec((None, dil, PROJ_TM // dil, DIL_OUT), lambda bi, i: (bi, 0, i, 0))
                   for _, dil in DIL_PAIRS]
    group_shapes = [jax.ShapeDtypeStruct((b, dil, s // dil, DIL_OUT), BF16) for _, dil in DIL_PAIRS]
    outs = pl.pallas_call(
        _even_in_kernel,
        grid=(b, s // PROJ_TM),
        in_specs=[
            pl.BlockSpec((None, PROJ_TM, D_MODEL), row),
            pl.BlockSpec((1, D_MODEL), const),
            pl.BlockSpec(memory_space=pl.ANY),
            pl.BlockSpec((1, HEAD_DIM), const),
            pl.BlockSpec((1, HEAD_DIM), const),
        ],
        out_specs=[pl.BlockSpec((None, PROJ_TM, POOL_WIDTH), row)] + group_specs * 3,
        out_shape=[jax.ShapeDtypeStruct((b, s, POOL_WIDTH), F32)] + group_shapes * 3,
        scratch_shapes=[pltpu.VMEM((PROJ_TM, D_MODEL), BF16),
                        pltpu.VMEM((DIL_HEADS_PER_GROUP, PROJ_TM, HEAD_DIM), F32)]
        + _weight_scratch(D_MODEL, n_in),
        compiler_params=_params("arbitrary", "arbitrary"),
        name="even_in",
    )(x, gain.reshape(1, D_MODEL), w_in,
      (q_gain * HEAD_DIM ** -0.5).reshape(1, HEAD_DIM), k_gain.reshape(1, HEAD_DIM))
    n_groups = len(DIL_PAIRS)
    return outs[0], outs[1:1 + n_groups], outs[1 + n_groups:1 + 2 * n_groups], outs[1 + 2 * n_groups:]


def _odd_in_kernel(x_ref, g_ref, w_hbm, qg_ref, kg_ref, vg_ref, q_ref, k_ref, v_ref, u_ref, vn_ref,
                   h_ref, gv_ref, w_ref, wstage_ref, w_sems):
    pl.when(_first_grid_step())(functools.partial(_load_weight_as_bf16, w_hbm, w_ref, wstage_ref, w_sems))
    h_ref[...] = _rms(x_ref[...], g_ref[...]).astype(BF16)

    def chunk(c):
        return jnp.dot(h_ref[...], w_ref[:, c * PROJ_TN:(c + 1) * PROJ_TN], preferred_element_type=F32)

    per = NA_WIDTH // PROJ_TN
    for c in range(per):
        _head_norm_store(chunk(c), q_ref, c * PROJ_TN, qg_ref[...])
        _head_norm_store(chunk(per + c), k_ref, c * PROJ_TN, kg_ref[...])
        v_ref[:, c * PROJ_TN:(c + 1) * PROJ_TN] = chunk(2 * per + c).astype(BF16)
    per_sg = SG_WIDTH // PROJ_TN
    for c in range(per_sg):
        u_ref[:, c * PROJ_TN:(c + 1) * PROJ_TN] = jax.nn.gelu(chunk(3 * per + c))
        gv_ref[:, c * PROJ_TN:(c + 1) * PROJ_TN] = jax.nn.gelu(chunk(3 * per + per_sg + c))
    vn_ref[...] = _rms(gv_ref[...], vg_ref[...]).astype(BF16)


def _odd_in(x, gain, w_in, q_gain, k_gain, v_gain):
    b, s, _ = x.shape
    n_in = w_in.shape[1]
    row = lambda bi, i: (bi, i, 0)
    const = lambda bi, i: (0, 0)
    return pl.pallas_call(
        _odd_in_kernel,
        grid=(b, s // PROJ_TM),
        in_specs=[
            pl.BlockSpec((None, PROJ_TM, D_MODEL), row),
            pl.BlockSpec((1, D_MODEL), const),
            pl.BlockSpec(memory_space=pl.ANY),
            pl.BlockSpec((1, HEAD_DIM), const),
            pl.BlockSpec((1, HEAD_DIM), const),
            pl.BlockSpec((1, SG_WIDTH), const),
        ],
        out_specs=[
            pl.BlockSpec((None, PROJ_TM, NA_WIDTH), row),
            pl.BlockSpec((None, PROJ_TM, NA_WIDTH), row),
            pl.BlockSpec((None, PROJ_TM, NA_WIDTH), row),
            pl.BlockSpec((None, PROJ_TM, SG_WIDTH), row),
            pl.BlockSpec((None, PROJ_TM, SG_WIDTH), row),
        ],
        out_shape=[
            jax.ShapeDtypeStruct((b, s, NA_WIDTH), BF16),
            jax.ShapeDtypeStruct((b, s, NA_WIDTH), BF16),
            jax.ShapeDtypeStruct((b, s, NA_WIDTH), BF16),
            jax.ShapeDtypeStruct((b, s, SG_WIDTH), F32),
            jax.ShapeDtypeStruct((b, s, SG_WIDTH), BF16),
        ],
        scratch_shapes=[pltpu.VMEM((PROJ_TM, D_MODEL), BF16), pltpu.VMEM((PROJ_TM, SG_WIDTH), F32)]
        + _weight_scratch(D_MODEL, n_in),
        compiler_params=_params("arbitrary", "arbitrary"),
        name="odd_in",
    )(x, gain.reshape(1, D_MODEL), w_in,
      (q_gain * HEAD_DIM ** -0.5).reshape(1, HEAD_DIM), k_gain.reshape(1, HEAD_DIM),
      v_gain.reshape(1, SG_WIDTH))


def _dilated_kernel(slopes_ref, q0_ref, q1_ref, q2_ref, k0_ref, k1_ref, k2_ref, v0_ref, v1_ref, v2_ref,
                    o_ref, on_ref, ls_ref, *, seq):
    hh = pl.program_id(1)
    step = pl.program_id(2)
    q_refs = (q0_ref, q1_ref, q2_ref)
    k_refs = (k0_ref, k1_ref, k2_ref)
    v_refs = (v0_ref, v1_ref, v2_ref)
    for g, (window, dil) in enumerate(DIL_PAIRS):
        assert window // (2 * dil) == DIL_RADIUS
        q_ref, k_ref, v_ref = q_refs[g], k_refs[g], v_refs[g]
        sub_len = seq // dil
        per_res = DIL_TA // dil
        nq = min(per_res, DIL_TQ)
        n_sub = per_res // nq
        n_keys = min(nq + 2 * DIL_RADIUS, sub_len)
        slope = slopes_ref[g * DIL_HEADS_PER_GROUP + hh] * dil
        base = (lax.broadcasted_iota(jnp.int32, (nq, n_keys), 1)
                - lax.broadcasted_iota(jnp.int32, (nq, n_keys), 0))

        def body(t, carry, q_ref=q_ref, k_ref=k_ref, v_ref=v_ref, g=g, dil=dil, sub_len=sub_len,
                 per_res=per_res, nq=nq, n_sub=n_sub, n_keys=n_keys, slope=slope, base=base):
            r = t // n_sub
            q_off = pl.multiple_of((t % n_sub) * nq, nq)
            q_pos = step * per_res + q_off
            k_pos = pl.multiple_of(jnp.clip(q_pos - DIL_RADIUS, 0, sub_len - n_keys), DIL_RADIUS)
            q = q_ref[r, pl.ds(q_off, nq), :]
            k = k_ref[r, pl.ds(k_pos, n_keys), :]
            v = v_ref[r, pl.ds(k_pos, n_keys), :]
            s = lax.dot_general(q, k, (((1,), (1,)), ((), ())), preferred_element_type=F32)
            dist = jnp.abs(base + (k_pos - q_pos))
            s = jnp.where(dist <= DIL_RADIUS, s - slope * dist.astype(F32), NEG_INF)
            m = jnp.max(s, axis=-1, keepdims=True)
            p = jnp.exp(s - m)
            den = jnp.sum(p, axis=-1, keepdims=True)
            out = jnp.dot(p.astype(BF16), v, preferred_element_type=F32) / den
            lse = jnp.broadcast_to(m + jnp.log(den), (nq, HEAD_DIM))
            if dil == 1:
                rows = pl.ds(q_off, nq)
            else:
                rows = pl.ds(r + dil * q_off, nq, stride=dil)
            on_ref[g, rows, :] = out
            ls_ref[g, rows, :] = lse
            return carry

        lax.fori_loop(0, dil * n_sub, body, 0, unroll=DIL_UNROLL)

    def merge(c, carry):
        rows = pl.ds(pl.multiple_of(c * DIL_TQ, DIL_TQ), DIL_TQ)
        lse = [ls_ref[g, rows, :] for g in range(len(DIL_PAIRS))]
        top = jnp.maximum(jnp.maximum(lse[0], lse[1]), lse[2])
        num = jnp.zeros((DIL_TQ, HEAD_DIM), F32)
        den = jnp.zeros((DIL_TQ, HEAD_DIM), F32)
        for g in range(len(DIL_PAIRS)):
            w = jnp.exp(lse[g] - top)
            num = num + w * on_ref[g, rows, :]
            den = den + w
        o_ref[rows, :] = (num / den).astype(o_ref.dtype)
        return carry

    lax.fori_loop(0, DIL_TA // DIL_TQ, merge, 0)


def _dilated(q, k, v, slopes):
    b, _, s, _ = q[0].shape
    n_groups = len(DIL_PAIRS)

    def q_spec(dil):
        return pl.BlockSpec((None, dil, DIL_TA // dil, HEAD_DIM), lambda bi, hh, i: (bi, 0, i, hh))

    def kv_spec(dil):
        return pl.BlockSpec((None, dil, s // dil, HEAD_DIM), lambda bi, hh, i: (bi, 0, 0, hh))

    dils = [dil for _, dil in DIL_PAIRS]
    return pl.pallas_call(
        functools.partial(_dilated_kernel, seq=s),
        grid=(b, DIL_HEADS_PER_GROUP, s // DIL_TA),
        in_specs=[pl.BlockSpec(memory_space=pltpu.SMEM)]
        + [q_spec(d) for d in dils] + [kv_spec(d) for d in dils] + [kv_spec(d) for d in dils],
        out_specs=pl.BlockSpec((None, DIL_TA, HEAD_DIM), lambda bi, hh, i: (bi, i, hh)),
        out_shape=jax.ShapeDtypeStruct((b, s, DIL_OUT), BF16),
        scratch_shapes=[pltpu.VMEM((n_groups, DIL_TA, HEAD_DIM), F32),
                        pltpu.VMEM((n_groups, DIL_TA, HEAD_DIM), F32)],
        compiler_params=_params("parallel", "parallel", "arbitrary"),
        name="dilated_attn",
    )(slopes, *q, *k, *v)


def _even_out_kernel(a_ref, prev_ref, next_ref, yb_ref, pw_ref, ps_ref, wo_hbm, x_ref, o_ref,
                     ext_ref, y_ref, wo_ref, wstage_ref, w_sems, *, seq):
    pl.when(_first_grid_step())(functools.partial(_load_weight_as_bf16, wo_hbm, wo_ref, wstage_ref, w_sems))
    i = pl.program_id(1)
    tm = OUT_TM
    ext_ref[0:POOL_HALO, :] = jnp.where(i == 0, 0.0, prev_ref[...])
    ext_ref[POOL_HALO:POOL_HALO + tm, :] = a_ref[...]
    ext_ref[POOL_HALO + tm:, :] = jnp.where(i == pl.num_programs(1) - 1, 0.0, next_ref[...])
    pos = i * tm + lax.broadcasted_iota(jnp.int32, (tm, 1), 0)
    for g, window in enumerate(POOL_WINDOWS):
        half = window // 2
        cols = slice(g * HEAD_DIM, (g + 1) * HEAD_DIM)
        total = ext_ref[POOL_HALO - half:POOL_HALO - half + tm, cols]
        for shift in range(-half + 1, half):
            total = total + ext_ref[POOL_HALO + shift:POOL_HALO + shift + tm, cols]
        count = (jnp.minimum(pos + half, seq) - jnp.maximum(pos - half, 0)).astype(F32)
        pooled = total / count - a_ref[:, cols]
        ya = jnp.dot(pooled.astype(BF16), pw_ref[g], preferred_element_type=F32) * ps_ref[:, cols]
        y_ref[:, cols] = ya.astype(BF16)
    y_ref[:, POOL_WIDTH:] = yb_ref[...]
    o_ref[...] = x_ref[...] + jnp.dot(y_ref[...], wo_ref[...], preferred_element_type=F32)


def _even_out(a, yb, pool_w, pool_scale, w_out, x):
    b, s, _ = x.shape
    tm = OUT_TM
    halo_blocks = tm // POOL_HALO
    tile = lambda bi, i: (bi, i, 0)
    return pl.pallas_call(
        functools.partial(_even_out_kernel, seq=s),
        grid=(b, s // tm),
        in_specs=[
            pl.BlockSpec((None, tm, POOL_WIDTH), tile),
            pl.BlockSpec((None, POOL_HALO, POOL_WIDTH),
                         lambda bi, i: (bi, jnp.maximum(i * halo_blocks - 1, 0), 0)),
            pl.BlockSpec((None, POOL_HALO, POOL_WIDTH),
                         lambda bi, i: (bi, jnp.minimum((i + 1) * halo_blocks, s // POOL_HALO - 1), 0)),
            pl.BlockSpec((None, tm, DIL_OUT), tile),
            _resident((len(POOL_WINDOWS), HEAD_DIM, HEAD_DIM), lambda bi, i: (0, 0, 0)),
            pl.BlockSpec((1, POOL_WIDTH), lambda bi, i: (0, 0)),
            pl.BlockSpec(memory_space=pl.ANY),
            pl.BlockSpec((None, tm, D_MODEL), tile),
        ],
        out_specs=pl.BlockSpec((None, tm, D_MODEL), tile),
        out_shape=jax.ShapeDtypeStruct((b, s, D_MODEL), F32),
        scratch_shapes=[pltpu.VMEM((tm + 2 * POOL_HALO, POOL_WIDTH), F32),
                        pltpu.VMEM((tm, POOL_WIDTH + DIL_OUT), BF16)]
        + _weight_scratch(POOL_WIDTH + DIL_OUT, D_MODEL),
        compiler_params=_params("arbitrary", "arbitrary"),
        name="even_out",
    )(a, a, a, yb, pool_w, pool_scale.reshape(1, POOL_WIDTH), w_out, x)


def _na_fill_bias(pair_ref, bias_ref, rb, rows):
    k_row0 = min(max(rb * NA_QROWS - NA_ROWS // 2, 0), rows - NA_KROWS)
    left_half = lax.broadcasted_iota(jnp.int32, (GRID_W, 2 * GRID_W), 1) < GRID_W
    for qr in range(NA_QROWS):
        r = rb * NA_QROWS + qr
        row_start = min(max(r - NA_ROWS // 2, 0), rows - NA_ROWS)
        for p in range(NA_KROWS // 2):
            kr = k_row0 + 2 * p
            ok_left = row_start <= kr < row_start + NA_ROWS
            ok_right = row_start <= kr + 1 < row_start + NA_ROWS
            if ok_left or ok_right:
                block = pair_ref[kr + 1 - r + NA_ROWS - 1]
                if not ok_right:
                    block = jnp.where(left_half, block, NEG_INF)
                if not ok_left:
                    block = jnp.where(left_half, NEG_INF, block)
            else:
                block = jnp.full((GRID_W, 2 * GRID_W), NEG_INF, F32)
            bias_ref[qr * GRID_W:(qr + 1) * GRID_W, p * 2 * GRID_W:(p + 1) * 2 * GRID_W] = block


def _na_kernel(q_ref, k_ref, v_ref, pair_ref, o_ref, bias_ref, *, rows):
    n_blocks = rows // NA_QROWS
    tq = NA_QROWS * GRID_W
    n_keys = NA_KROWS * GRID_W
    kinds = (0, 1, n_blocks - 1)
    for slot, kind in enumerate(kinds):
        _na_fill_bias(pair_ref, bias_ref.at[slot], kind, rows)
    for rb in range(n_blocks):
        slot = 0 if rb == 0 else (2 if rb == n_blocks - 1 else 1)
        k_start = min(max(rb * NA_QROWS - NA_ROWS // 2, 0), rows - NA_KROWS) * GRID_W
        q = q_ref[rb * tq:(rb + 1) * tq, :]
        k = k_ref[k_start:k_start + n_keys, :]
        v = v_ref[k_start:k_start + n_keys, :]
        s = lax.dot_general(q, k, (((1,), (1,)), ((), ())), preferred_element_type=F32) + bias_ref[slot]
        m = jnp.max(s, axis=-1, keepdims=True)
        p = jnp.exp(s - m)
        den = jnp.sum(p, axis=-1, keepdims=True)
        out = jnp.dot(p.astype(BF16), v, preferred_element_type=F32) / den
        o_ref[rb * tq:(rb + 1) * tq, :] = out.astype(o_ref.dtype)


def _na_pair_tables(rpb):
    n_heads, n_rel_rows, n_rel_cols = rpb.shape
    c = np.arange(GRID_W)
    col_start = np.clip(c - NA_COLS // 2, 0, GRID_W - NA_COLS)
    col_ok = (c[None, :] >= col_start[:, None]) & (c[None, :] < col_start[:, None] + NA_COLS)
    rel_c = np.clip(c[None, :] - c[:, None], -(NA_COLS - 1), NA_COLS - 1) + (NA_COLS - 1)
    pick = (rel_c.reshape(1, -1) == np.arange(n_rel_cols)[:, None]).astype(np.float32)
    by_col = jnp.dot(rpb.astype(F32).reshape(n_heads * n_rel_rows, n_rel_cols), pick,
                     precision=lax.Precision.HIGHEST).reshape(n_heads, n_rel_rows, GRID_W, GRID_W)
    by_col = jnp.where(col_ok[None, None], by_col, NEG_INF)
    masked = jnp.full((n_heads, 1, GRID_W, GRID_W), NEG_INF, F32)
    padded = jnp.concatenate([masked, by_col, masked], axis=1)
    return jnp.concatenate([padded[:, :-1], padded[:, 1:]], axis=-1)


def _neighbourhood(q, k, v, pair_tables):
    b, s, _ = q.shape
    rows = s // GRID_W
    n_blocks = rows // NA_QROWS
    assert n_blocks >= 3 and rows >= NA_KROWS
    tq = NA_QROWS * GRID_W
    n_keys = NA_KROWS * GRID_W
    head = lambda bi, h: (bi, 0, h)
    return pl.pallas_call(
        functools.partial(_na_kernel, rows=rows),
        grid=(b, NA_HEADS),
        in_specs=[
            pl.BlockSpec((None, s, HEAD_DIM), head),
            pl.BlockSpec((None, s, HEAD_DIM), head),
            pl.BlockSpec((None, s, HEAD_DIM), head),
            pl.BlockSpec((None, 2 * NA_ROWS, GRID_W, 2 * GRID_W), lambda bi, h: (h, 0, 0, 0)),
        ],
        out_specs=pl.BlockSpec((None, s, HEAD_DIM), head),
        out_shape=jax.ShapeDtypeStruct((b, s, NA_WIDTH), BF16),
        scratch_shapes=[pltpu.VMEM((3, tq, n_keys), F32)],
        compiler_params=_params("parallel", "parallel"),
        name="neighbourhood_attn",
    )(q, k, v, pair_tables)


def _odd_out_kernel(yc_ref, u_ref, vn_ref, ws_ref, bs_ref, wo_hbm, x_ref, o_ref, y_ref,
                    wo_ref, wstage_ref, w_sems):
    pl.when(_first_grid_step())(functools.partial(_load_weight_as_bf16, wo_hbm, wo_ref, wstage_ref, w_sems))
    width = SG_WIDTH // SG_GROUPS
    for c in range(OUT_TM // SG_CHUNK):
        rows = slice(c * SG_CHUNK, (c + 1) * SG_CHUNK)
        for g in range(SG_GROUPS):
            cols = slice(g * width, (g + 1) * width)
            sv = jnp.dot(ws_ref[g], vn_ref[rows, cols], preferred_element_type=F32) + bs_ref[g]
            y_ref[rows, NA_WIDTH + g * width:NA_WIDTH + (g + 1) * width] = (u_ref[rows, cols] * sv).astype(BF16)
    y_ref[:, :NA_WIDTH] = yc_ref[...]
    o_ref[...] = x_ref[...] + jnp.dot(y_ref[...], wo_ref[...], preferred_element_type=F32)


def _odd_out(yc, u, vn, w_s, b_s, w_out, x):
    b, s, _ = x.shape
    tm = OUT_TM
    width = SG_WIDTH // SG_GROUPS
    tile = lambda bi, i: (bi, i, 0)
    return pl.pallas_call(
        _odd_out_kernel,
        grid=(b, s // tm),
        in_specs=[
            pl.BlockSpec((None, tm, NA_WIDTH), tile),
            pl.BlockSpec((None, tm, SG_WIDTH), tile),
            pl.BlockSpec((None, tm, SG_WIDTH), tile),
            _resident((SG_GROUPS, SG_CHUNK, SG_CHUNK), lambda bi, i: (0, 0, 0)),
            _resident((SG_GROUPS, SG_CHUNK, width), lambda bi, i: (0, 0, 0)),
            pl.BlockSpec(memory_space=pl.ANY),
            pl.BlockSpec((None, tm, D_MODEL), tile),
        ],
        out_specs=pl.BlockSpec((None, tm, D_MODEL), tile),
        out_shape=jax.ShapeDtypeStruct((b, s, D_MODEL), F32),
        scratch_shapes=[pltpu.VMEM((tm, NA_WIDTH + SG_WIDTH), BF16)]
        + _weight_scratch(NA_WIDTH + SG_WIDTH, D_MODEL),
        compiler_params=_params("arbitrary", "arbitrary"),
        name="odd_out",
    )(yc, u, vn, w_s, b_s, w_out, x)


def kernel(x, norm_ffn1, norm_mix, norm_ffn2, norm_out, ffn_w_gate, ffn_w_up, ffn_w_down, even_w_in, pool_w, pool_scale, dil_q_gain, dil_k_gain, even_w_out, odd_w_in, na_q_gain, na_k_gain, na_rpb, sg_v_gain, sg_w, sg_b, odd_w_out):
    depth = norm_ffn1.shape[0]
    slopes = jnp.asarray(2.0 ** (-8.0 * np.arange(1, DIL_HEADS + 1) / DIL_HEADS), dtype=F32)
    width = SG_WIDTH // SG_GROUPS

    for layer in range(depth):
        x = _ffn(x, norm_ffn1[layer], ffn_w_gate, ffn_w_up, ffn_w_down, layer, 0)
        if layer % 2 == 0:
            e = layer // 2
            a, q, k, v = _even_in(x, norm_mix[layer], even_w_in[e], dil_q_gain[e], dil_k_gain[e])
            yb = _dilated(q, k, v, slopes)
            x = _even_out(a, yb, pool_w[e].astype(BF16), pool_scale[e], even_w_out[e], x)
        else:
            o = layer // 2
            q, k, v, u, vn = _odd_in(x, norm_mix[layer], odd_w_in[o],
                                     na_q_gain[o], na_k_gain[o], sg_v_gain[o])
            yc = _neighbourhood(q, k, v, _na_pair_tables(na_rpb[o]))
            b_s = jnp.broadcast_to(sg_b[o][:, :, None], (SG_GROUPS, SG_CHUNK, width))
            x = _odd_out(yc, u, vn, sg_w[o].astype(BF16), b_s, odd_w_out[o], x)
        x = _ffn(x, norm_ffn2[layer], ffn_w_gate, ffn_w_up, ffn_w_down, layer, 1, gain_out=norm_out[layer])
    return x
```

```python
import functools

import numpy as np
import jax
import jax.numpy as jnp
from jax import lax
from jax.experimental import pallas as pl
from jax.experimental.pallas import tpu as pltpu

F32 = jnp.float32
BF16 = jnp.bfloat16

D_MODEL = 2048
D_FF = 5632
HEAD_DIM = 128
POOL_WINDOWS = (2, 4, 8, 16)
POOL_WIDTH = 512
POOL_HALO = 8
DIL_PAIRS = ((128, 1), (512, 4), (2048, 16))
DIL_HEADS_PER_GROUP = 4
DIL_HEADS = 12
DIL_WIDTH = DIL_HEADS * HEAD_DIM
DIL_OUT = DIL_HEADS_PER_GROUP * HEAD_DIM
GRID_W = 64
NA_ROWS = 8
NA_COLS = 16
NA_HEADS = 8
NA_WIDTH = NA_HEADS * HEAD_DIM
NA_QROWS = 4
NA_KROWS = NA_QROWS + NA_ROWS
SG_CHUNK = 128
SG_GROUPS = 8
SG_WIDTH = 1024
RMS_EPS = 1e-6
NEG_INF = -1e30

VMEM_LIMIT_BYTES = 60 * 1024 * 1024

FFN_TM = 1024
FFN_TF = 512
FFN_TC = 256
PROJ_TM = 512
PROJ_TN = 512
DIL_TQ = 128
DIL_TA = 2048
DIL_RADIUS = 64
DIL_UNROLL = 8
OUT_TM = 512
W_CHUNK = 512


def _params(*semantics):
    return pltpu.CompilerParams(dimension_semantics=semantics, vmem_limit_bytes=VMEM_LIMIT_BYTES)


def _resident(block_shape, index_map):
    return pl.BlockSpec(block_shape, index_map, pipeline_mode=pl.Buffered(1))


def _rms(x, gain):
    return x * lax.rsqrt(jnp.mean(x * x, axis=-1, keepdims=True) + RMS_EPS) * gain


def _first_grid_step():
    return (pl.program_id(0) == 0) & (pl.program_id(1) == 0)


def _load_weight_as_bf16(w_hbm, wbf_ref, stage_ref, sems):
    n_chunks = wbf_ref.shape[1] // W_CHUNK

    def copy(c):
        cols = pl.ds(c * W_CHUNK, W_CHUNK)
        return pltpu.make_async_copy(w_hbm.at[:, cols], stage_ref.at[c % 2], sems.at[c % 2])

    copy(0).start()
    for c in range(n_chunks):
        if c + 1 < n_chunks:
            copy(c + 1).start()
        copy(c).wait()
        wbf_ref[:, c * W_CHUNK:(c + 1) * W_CHUNK] = stage_ref[c % 2].astype(BF16)


def _weight_scratch(rows, cols):
    assert cols % W_CHUNK == 0
    return [pltpu.VMEM((rows, cols), BF16), pltpu.VMEM((2, rows, W_CHUNK), F32), pltpu.SemaphoreType.DMA((2,))]


def _ffn_kernel(x_hbm, g_ref, wg_ref, wu_ref, wd_ref, gout_ref, o_ref, h_ref, xbuf_ref, x_sem, *, final_norm):
    j = pl.program_id(2)
    tiles_per_batch = pl.num_programs(1)
    n_tiles = pl.num_programs(0) * tiles_per_batch
    tile = pl.program_id(0) * tiles_per_batch + pl.program_id(1)

    def x_copy(t):
        rows = pl.ds(pl.multiple_of((t % tiles_per_batch) * FFN_TM, FFN_TM), FFN_TM)
        return pltpu.make_async_copy(x_hbm.at[t // tiles_per_batch, rows, :], xbuf_ref, x_sem)

    @pl.when(j == 0)
    def _():
        @pl.when(tile == 0)
        def _():
            x_copy(tile).start()

        x_copy(tile).wait()
        x = xbuf_ref[...]
        h_ref[...] = _rms(x, g_ref[...]).astype(BF16)
        o_ref[...] = x

    @pl.when((j == pl.num_programs(2) // 2) & (tile + 1 < n_tiles))
    def _():
        x_copy(tile + 1).start()

    h = h_ref[...]
    for c in range(FFN_TF // FFN_TC):
        cols = slice(c * FFN_TC, (c + 1) * FFN_TC)
        gate = jnp.dot(h, wg_ref[:, cols].astype(BF16), preferred_element_type=F32)
        up = jnp.dot(h, wu_ref[:, cols].astype(BF16), preferred_element_type=F32)
        act = (gate * jax.nn.sigmoid(gate)) * up * 0.5
        o_ref[...] += jnp.dot(act.astype(BF16), wd_ref[cols, :].astype(BF16), preferred_element_type=F32)

    if final_norm:
        @pl.when(j == pl.num_programs(2) - 1)
        def _():
            o_ref[...] = _rms(o_ref[...], gout_ref[...])


def _ffn(x, gain, w_gate, w_up, w_down, layer, idx, gain_out=None):
    b, s, _ = x.shape
    assert D_FF // FFN_TF >= 2
    final_norm = gain_out is not None
    if gain_out is None:
        gain_out = gain
    tile = lambda bi, i, j: (bi, i, 0)
    const = lambda bi, i, j: (0, 0)
    return pl.pallas_call(
        functools.partial(_ffn_kernel, final_norm=final_norm),
        grid=(b, s // FFN_TM, D_FF // FFN_TF),
        in_specs=[
            pl.BlockSpec(memory_space=pl.ANY),
            pl.BlockSpec((1, D_MODEL), const),
            pl.BlockSpec((None, None, D_MODEL, FFN_TF), lambda bi, i, j: (layer, idx, 0, j)),
            pl.BlockSpec((None, None, D_MODEL, FFN_TF), lambda bi, i, j: (layer, idx, 0, j)),
            pl.BlockSpec((None, None, FFN_TF, D_MODEL), lambda bi, i, j: (layer, idx, j, 0)),
            pl.BlockSpec((1, D_MODEL), const),
        ],
        out_specs=pl.BlockSpec((None, FFN_TM, D_MODEL), tile),
        out_shape=jax.ShapeDtypeStruct((b, s, D_MODEL), F32),
        scratch_shapes=[pltpu.VMEM((FFN_TM, D_MODEL), BF16), pltpu.VMEM((FFN_TM, D_MODEL), F32),
                        pltpu.SemaphoreType.DMA(())],
        compiler_params=_params("arbitrary", "arbitrary", "arbitrary"),
        name="ffn_final" if final_norm else "ffn",
    )(x, gain.reshape(1, D_MODEL), w_gate, w_up, w_down, gain_out.reshape(1, D_MODEL))


def _head_norm_store(z, out_ref, col0, gain):
    for hd in range(PROJ_TN // HEAD_DIM):
        zh = z[:, hd * HEAD_DIM:(hd + 1) * HEAD_DIM]
        out_ref[:, col0 + hd * HEAD_DIM:col0 + (hd + 1) * HEAD_DIM] = _rms(zh, gain).astype(out_ref.dtype)


def _store_by_residue(z, out_ref, stage_ref, dil, gain=None):
    tm = z.shape[0]
    for hd in range(z.shape[1] // HEAD_DIM):
        cols = slice(hd * HEAD_DIM, (hd + 1) * HEAD_DIM)
        zh = z[:, cols]
        if gain is not None:
            zh = _rms(zh, gain)
        if dil == 1:
            out_ref[0, :, cols] = zh.astype(BF16)
        else:
            stage_ref[hd] = zh
            for r in range(dil):
                out_ref[r, :, cols] = stage_ref[hd, pl.ds(r, tm // dil, stride=dil), :].astype(BF16)


def _even_in_kernel(x_ref, g_ref, w_hbm, qg_ref, kg_ref, a_ref, *rest):
    n_groups = len(DIL_PAIRS)
    q_refs, k_refs, v_refs = rest[:n_groups], rest[n_groups:2 * n_groups], rest[2 * n_groups:3 * n_groups]
    h_ref, stage_ref, w_ref, wstage_ref, w_sems = rest[3 * n_groups:]
    pl.when(_first_grid_step())(functools.partial(_load_weight_as_bf16, w_hbm, w_ref, wstage_ref, w_sems))
    h_ref[...] = _rms(x_ref[...], g_ref[...]).astype(BF16)

    def chunk(c):
        return jnp.dot(h_ref[...], w_ref[:, c * PROJ_TN:(c + 1) * PROJ_TN], preferred_element_type=F32)

    a_ref[...] = chunk(0)
    for g, (_, dil) in enumerate(DIL_PAIRS):
        _store_by_residue(chunk(1 + g), q_refs[g], stage_ref, dil, qg_ref[...])
        _store_by_residue(chunk(1 + n_groups + g), k_refs[g], stage_ref, dil, kg_ref[...])
        _store_by_residue(chunk(1 + 2 * n_groups + g), v_refs[g], stage_ref, dil)


def _even_in(x, gain, w_in, q_gain, k_gain):
    b, s, _ = x.shape
    n_in = w_in.shape[1]
    assert PROJ_TN == DIL_OUT
    row = lambda bi, i: (bi, i, 0)
    const = lambda bi, i: (0, 0)
    group_specs = [pl.BlockSpec((None, dil, PROJ_TM // dil, DIL_OUT), lambda bi, i: (bi, 0, i, 0))
                   for _, dil in DIL_PAIRS]
    group_shapes = [jax.ShapeDtypeStruct((b, dil, s // dil, DIL_OUT), BF16) for _, dil in DIL_PAIRS]
    outs = pl.pallas_call(
        _even_in_kernel,
        grid=(b, s // PROJ_TM),
        in_specs=[
            pl.BlockSpec((None, PROJ_TM, D_MODEL), row),
            pl.BlockSpec((1, D_MODEL), const),
            pl.BlockSpec(memory_space=pl.ANY),
            pl.BlockSpec((1, HEAD_DIM), const),
            pl.BlockSpec((1, HEAD_DIM), const),
        ],
        out_specs=[pl.BlockSpec((None, PROJ_TM, POOL_WIDTH), row)] + group_specs * 3,
        out_shape=[jax.ShapeDtypeStruct((b, s, POOL_WIDTH), F32)] + group_shapes * 3,
        scratch_shapes=[pltpu.VMEM((PROJ_TM, D_MODEL), BF16),
                        pltpu.VMEM((DIL_HEADS_PER_GROUP, PROJ_TM, HEAD_DIM), F32)]
        + _weight_scratch(D_MODEL, n_in),
        compiler_params=_params("arbitrary", "arbitrary"),
        name="even_in",
    )(x, gain.reshape(1, D_MODEL), w_in,
      (q_gain * HEAD_DIM ** -0.5).reshape(1, HEAD_DIM), k_gain.reshape(1, HEAD_DIM))
    n_groups = len(DIL_PAIRS)
    return outs[0], outs[1:1 + n_groups], outs[1 + n_groups:1 + 2 * n_groups], outs[1 + 2 * n_groups:]


def _odd_in_kernel(x_ref, g_ref, w_hbm, qg_ref, kg_ref, vg_ref, q_ref, k_ref, v_ref, u_ref, vn_ref,
                   h_ref, gv_ref, w_ref, wstage_ref, w_sems):
    pl.when(_first_grid_step())(functools.partial(_load_weight_as_bf16, w_hbm, w_ref, wstage_ref, w_sems))
    h_ref[...] = _rms(x_ref[...], g_ref[...]).astype(BF16)

    def chunk(c):
        return jnp.dot(h_ref[...], w_ref[:, c * PROJ_TN:(c + 1) * PROJ_TN], preferred_element_type=F32)

    per = NA_WIDTH // PROJ_TN
    for c in range(per):
        _head_norm_store(chunk(c), q_ref, c * PROJ_TN, qg_ref[...])
        _head_norm_store(chunk(per + c), k_ref, c * PROJ_TN, kg_ref[...])
        v_ref[:, c * PROJ_TN:(c + 1) * PROJ_TN] = chunk(2 * per + c).astype(BF16)
    per_sg = SG_WIDTH // PROJ_TN
    for c in range(per_sg):
        u_ref[:, c * PROJ_TN:(c + 1) * PROJ_TN] = jax.nn.gelu(chunk(3 * per + c))
        gv_ref[:, c * PROJ_TN:(c + 1) * PROJ_TN] = jax.nn.gelu(chunk(3 * per + per_sg + c))
    vn_ref[...] = _rms(gv_ref[...], vg_ref[...]).astype(BF16)


def _odd_in(x, gain, w_in, q_gain, k_gain, v_gain):
    b, s, _ = x.shape
    n_in = w_in.shape[1]
    row = lambda bi, i: (bi, i, 0)
    const = lambda bi, i: (0, 0)
    return pl.pallas_call(
        _odd_in_kernel,
        grid=(b, s // PROJ_TM),
        in_specs=[
            pl.BlockSpec((None, PROJ_TM, D_MODEL), row),
            pl.BlockSpec((1, D_MODEL), const),
            pl.BlockSpec(memory_space=pl.ANY),
            pl.BlockSpec((1, HEAD_DIM), const),
            pl.BlockSpec((1, HEAD_DIM), const),
            pl.BlockSpec((1, SG_WIDTH), const),
        ],
        out_specs=[
            pl.BlockSpec((None, PROJ_TM, NA_WIDTH), row),
            pl.BlockSpec((None, PROJ_TM, NA_WIDTH), row),
            pl.BlockSpec((None, PROJ_TM, NA_WIDTH), row),
            pl.BlockSpec((None, PROJ_TM, SG_WIDTH), row),
            pl.BlockSpec((None, PROJ_TM, SG_WIDTH), row),
        ],
        out_shape=[
            jax.ShapeDtypeStruct((b, s, NA_WIDTH), BF16),
            jax.ShapeDtypeStruct((b, s, NA_WIDTH), BF16),
            jax.ShapeDtypeStruct((b, s, NA_WIDTH), BF16),
            jax.ShapeDtypeStruct((b, s, SG_WIDTH), F32),
            jax.ShapeDtypeStruct((b, s, SG_WIDTH), BF16),
        ],
        scratch_shapes=[pltpu.VMEM((PROJ_TM, D_MODEL), BF16), pltpu.VMEM((PROJ_TM, SG_WIDTH), F32)]
        + _weight_scratch(D_MODEL, n_in),
        compiler_params=_params("arbitrary", "arbitrary"),
        name="odd_in",
    )(x, gain.reshape(1, D_MODEL), w_in,
      (q_gain * HEAD_DIM ** -0.5).reshape(1, HEAD_DIM), k_gain.reshape(1, HEAD_DIM),
      v_gain.reshape(1, SG_WIDTH))


def _dilated_kernel(slopes_ref, q0_ref, q1_ref, q2_ref, k0_ref, k1_ref, k2_ref, v0_ref, v1_ref, v2_ref,
                    o_ref, on_ref, ls_ref, *, seq):
    hh = pl.program_id(1)
    step = pl.program_id(2)
    q_refs = (q0_ref, q1_ref, q2_ref)
    k_refs = (k0_ref, k1_ref, k2_ref)
    v_refs = (v0_ref, v1_ref, v2_ref)
    for g, (window, dil) in enumerate(DIL_PAIRS):
        assert window // (2 * dil) == DIL_RADIUS
        q_ref, k_ref, v_ref = q_refs[g], k_refs[g], v_refs[g]
        sub_len = seq // dil
        per_res = DIL_TA // dil
        nq = min(per_res, DIL_TQ)
        n_sub = per_res // nq
        n_keys = min(nq + 2 * DIL_RADIUS, sub_len)
        slope = slopes_ref[g * DIL_HEADS_PER_GROUP + hh] * dil
        base = (lax.broadcasted_iota(jnp.int32, (nq, n_keys), 1)
                - lax.broadcasted_iota(jnp.int32, (nq, n_keys), 0))

        def body(t, carry, q_ref=q_ref, k_ref=k_ref, v_ref=v_ref, g=g, dil=dil, sub_len=sub_len,
                 per_res=per_res, nq=nq, n_sub=n_sub, n_keys=n_keys, slope=slope, base=base):
            r = t // n_sub
            q_off = pl.multiple_of((t % n_sub) * nq, nq)
            q_pos = step * per_res + q_off
            k_pos = pl.multiple_of(jnp.clip(q_pos - DIL_RADIUS, 0, sub_len - n_keys), DIL_RADIUS)
            q = q_ref[r, pl.ds(q_off, nq), :]
            k = k_ref[r, pl.ds(k_pos, n_keys), :]
            v = v_ref[r, pl.ds(k_pos, n_keys), :]
            s = lax.dot_general(q, k, (((1,), (1,)), ((), ())), preferred_element_type=F32)
            dist = jnp.abs(base + (k_pos - q_pos))
            s = jnp.where(dist <= DIL_RADIUS, s - slope * dist.astype(F32), NEG_INF)
            m = jnp.max(s, axis=-1, keepdims=True)
            p = jnp.exp(s - m)
            den = jnp.sum(p, axis=-1, keepdims=True)
            out = jnp.dot(p.astype(BF16), v, preferred_element_type=F32) / den
            lse = jnp.broadcast_to(m + jnp.log(den), (nq, HEAD_DIM))
            if dil == 1:
                rows = pl.ds(q_off, nq)
            else:
                rows = pl.ds(r + dil * q_off, nq, stride=dil)
            on_ref[g, rows, :] = out
            ls_ref[g, rows, :] = lse
            return carry

        lax.fori_loop(0, dil * n_sub, body, 0, unroll=DIL_UNROLL)

    def merge(c, carry):
        rows = pl.ds(pl.multiple_of(c * DIL_TQ, DIL_TQ), DIL_TQ)
        lse = [ls_ref[g, rows, :] for g in range(len(DIL_PAIRS))]
        top = jnp.maximum(jnp.maximum(lse[0], lse[1]), lse[2])
        num = jnp.zeros((DIL_TQ, HEAD_DIM), F32)
        den = jnp.zeros((DIL_TQ, HEAD_DIM), F32)
        for g in range(len(DIL_PAIRS)):
            w = jnp.exp(lse[g] - top)
            num = num + w * on_ref[g, rows, :]
            den = den + w
        o_ref[rows, :] = (num / den).astype(o_ref.dtype)
        return carry

    lax.fori_loop(0, DIL_TA // DIL_TQ, merge, 0)


def _dilated(q, k, v, slopes):
    b, _, s, _ = q[0].shape
    n_groups = len(DIL_PAIRS)

    def q_spec(dil):
        return pl.BlockSpec((None, dil, DIL_TA // dil, HEAD_DIM), lambda bi, hh, i: (bi, 0, i, hh))

    def kv_spec(dil):
        return pl.BlockSpec((None, dil, s // dil, HEAD_DIM), lambda bi, hh, i: (bi, 0, 0, hh))

    dils = [dil for _, dil in DIL_PAIRS]
    return pl.pallas_call(
        functools.partial(_dilated_kernel, seq=s),
        grid=(b, DIL_HEADS_PER_GROUP, s // DIL_TA),
        in_specs=[pl.BlockSpec(memory_space=pltpu.SMEM)]
        + [q_spec(d) for d in dils] + [kv_spec(d) for d in dils] + [kv_spec(d) for d in dils],
        out_specs=pl.BlockSpec((None, DIL_TA, HEAD_DIM), lambda bi, hh, i: (bi, i, hh)),
        out_shape=jax.ShapeDtypeStruct((b, s, DIL_OUT), BF16),
        scratch_shapes=[pltpu.VMEM((n_groups, DIL_TA, HEAD_DIM), F32),
                        pltpu.VMEM((n_groups, DIL_TA, HEAD_DIM), F32)],
        compiler_params=_params("parallel", "parallel", "arbitrary"),
        name="dilated_attn",
    )(slopes, *q, *k, *v)


def _even_out_kernel(a_ref, prev_ref, next_ref, yb_ref, pw_ref, ps_ref, wo_hbm, x_ref, o_ref,
                     ext_ref, y_ref, wo_ref, wstage_ref, w_sems, *, seq):
    pl.when(_first_grid_step())(functools.partial(_load_weight_as_bf16, wo_hbm, wo_ref, wstage_ref, w_sems))
    i = pl.program_id(1)
    tm = OUT_TM
    ext_ref[0:POOL_HALO, :] = jnp.where(i == 0, 0.0, prev_ref[...])
    ext_ref[POOL_HALO:POOL_HALO + tm, :] = a_ref[...]
    ext_ref[POOL_HALO + tm:, :] = jnp.where(i == pl.num_programs(1) - 1, 0.0, next_ref[...])
    pos = i * tm + lax.broadcasted_iota(jnp.int32, (tm, 1), 0)
    for g, window in enumerate(POOL_WINDOWS):
        half = window // 2
        cols = slice(g * HEAD_DIM, (g + 1) * HEAD_DIM)
        total = ext_ref[POOL_HALO - half:POOL_HALO - half + tm, cols]
        for shift in range(-half + 1, half):
            total = total + ext_ref[POOL_HALO + shift:POOL_HALO + shift + tm, cols]
        count = (jnp.minimum(pos + half, seq) - jnp.maximum(pos - half, 0)).astype(F32)
        pooled = total / count - a_ref[:, cols]
        ya = jnp.dot(pooled.astype(BF16), pw_ref[g], preferred_element_type=F32) * ps_ref[:, cols]
        y_ref[:, cols] = ya.astype(BF16)
    y_ref[:, POOL_WIDTH:] = yb_ref[...]
    o_ref[...] = x_ref[...] + jnp.dot(y_ref[...], wo_ref[...], preferred_element_type=F32)


def _even_out(a, yb, pool_w, pool_scale, w_out, x):
    b, s, _ = x.shape
    tm = OUT_TM
    halo_blocks = tm // POOL_HALO
    tile = lambda bi, i: (bi, i, 0)
    return pl.pallas_call(
        functools.partial(_even_out_kernel, seq=s),
        grid=(b, s // tm),
        in_specs=[
            pl.BlockSpec((None, tm, POOL_WIDTH), tile),
            pl.BlockSpec((None, POOL_HALO, POOL_WIDTH),
                         lambda bi, i: (bi, jnp.maximum(i * halo_blocks - 1, 0), 0)),
            pl.BlockSpec((None, POOL_HALO, POOL_WIDTH),
                         lambda bi, i: (bi, jnp.minimum((i + 1) * halo_blocks, s // POOL_HALO - 1), 0)),
            pl.BlockSpec((None, tm, DIL_OUT), tile),
            _resident((len(POOL_WINDOWS), HEAD_DIM, HEAD_DIM), lambda bi, i: (0, 0, 0)),
            pl.BlockSpec((1, POOL_WIDTH), lambda bi, i: (0, 0)),
            pl.BlockSpec(memory_space=pl.ANY),
            pl.BlockSpec((None, tm, D_MODEL), tile),
        ],
        out_specs=pl.BlockSpec((None, tm, D_MODEL), tile),
        out_shape=jax.ShapeDtypeStruct((b, s, D_MODEL), F32),
        scratch_shapes=[pltpu.VMEM((tm + 2 * POOL_HALO, POOL_WIDTH), F32),
                        pltpu.VMEM((tm, POOL_WIDTH + DIL_OUT), BF16)]
        + _weight_scratch(POOL_WIDTH + DIL_OUT, D_MODEL),
        compiler_params=_params("arbitrary", "arbitrary"),
        name="even_out",
    )(a, a, a, yb, pool_w, pool_scale.reshape(1, POOL_WIDTH), w_out, x)


def _na_fill_bias(pair_ref, bias_ref, rb, rows):
    k_row0 = min(max(rb * NA_QROWS - NA_ROWS // 2, 0), rows - NA_KROWS)
    left_half = lax.broadcasted_iota(jnp.int32, (GRID_W, 2 * GRID_W), 1) < GRID_W
    for qr in range(NA_QROWS):
        r = rb * NA_QROWS + qr
        row_start = min(max(r - NA_ROWS // 2, 0), rows - NA_ROWS)
        for p in range(NA_KROWS // 2):
            kr = k_row0 + 2 * p
            ok_left = row_start <= kr < row_start + NA_ROWS
            ok_right = row_start <= kr + 1 < row_start + NA_ROWS
            if ok_left or ok_right:
                block = pair_ref[kr + 1 - r + NA_ROWS - 1]
                if not ok_right:
                    block = jnp.where(left_half, block, NEG_INF)
                if not ok_left:
                    block = jnp.where(left_half, NEG_INF, block)
            else:
                block = jnp.full((GRID_W, 2 * GRID_W), NEG_INF, F32)
            bias_ref[qr * GRID_W:(qr + 1) * GRID_W, p * 2 * GRID_W:(p + 1) * 2 * GRID_W] = block


def _na_kernel(q_ref, k_ref, v_ref, pair_ref, o_ref, bias_ref, *, rows):
    n_blocks = rows // NA_QROWS
    tq = NA_QROWS * GRID_W
    n_keys = NA_KROWS * GRID_W
    kinds = (0, 1, n_blocks - 1)
    for slot, kind in enumerate(kinds):
        _na_fill_bias(pair_ref, bias_ref.at[slot], kind, rows)
    for rb in range(n_blocks):
        slot = 0 if rb == 0 else (2 if rb == n_blocks - 1 else 1)
        k_start = min(max(rb * NA_QROWS - NA_ROWS // 2, 0), rows - NA_KROWS) * GRID_W
        q = q_ref[rb * tq:(rb + 1) * tq, :]
        k = k_ref[k_start:k_start + n_keys, :]
        v = v_ref[k_start:k_start + n_keys, :]
        s = lax.dot_general(q, k, (((1,), (1,)), ((), ())), preferred_element_type=F32) + bias_ref[slot]
        m = jnp.max(s, axis=-1, keepdims=True)
        p = jnp.exp(s - m)
        den = jnp.sum(p, axis=-1, keepdims=True)
        out = jnp.dot(p.astype(BF16), v, preferred_element_type=F32) / den
        o_ref[rb * tq:(rb + 1) * tq, :] = out.astype(o_ref.dtype)


def _na_pair_tables(rpb):
    n_heads, n_rel_rows, n_rel_cols = rpb.shape
    c = np.arange(GRID_W)
    col_start = np.clip(c - NA_COLS // 2, 0, GRID_W - NA_COLS)
    col_ok = (c[None, :] >= col_start[:, None]) & (c[None, :] < col_start[:, None] + NA_COLS)
    rel_c = np.clip(c[None, :] - c[:, None], -(NA_COLS - 1), NA_COLS - 1) + (NA_COLS - 1)
    pick = (rel_c.reshape(1, -1) == np.arange(n_rel_cols)[:, None]).astype(np.float32)
    by_col = jnp.dot(rpb.astype(F32).reshape(n_heads * n_rel_rows, n_rel_cols), pick,
                     precision=lax.Precision.HIGHEST).reshape(n_heads, n_rel_rows, GRID_W, GRID_W)
    by_col = jnp.where(col_ok[None, None], by_col, NEG_INF)
    masked = jnp.full((n_heads, 1, GRID_W, GRID_W), NEG_INF, F32)
    padded = jnp.concatenate([masked, by_col, masked], axis=1)
    return jnp.concatenate([padded[:, :-1], padded[:, 1:]], axis=-1)


def _neighbourhood(q, k, v, pair_tables):
    b, s, _ = q.shape
    rows = s // GRID_W
    n_blocks = rows // NA_QROWS
    assert n_blocks >= 3 and rows >= NA_KROWS
    tq = NA_QROWS * GRID_W
    n_keys = NA_KROWS * GRID_W
    head = lambda bi, h: (bi, 0, h)
    return pl.pallas_call(
        functools.partial(_na_kernel, rows=rows),
        grid=(b, NA_HEADS),
        in_specs=[
            pl.BlockSpec((None, s, HEAD_DIM), head),
            pl.BlockSpec((None, s, HEAD_DIM), head),
            pl.BlockSpec((None, s, HEAD_DIM), head),
            pl.BlockSpec((None, 2 * NA_ROWS, GRID_W, 2 * GRID_W), lambda bi, h: (h, 0, 0, 0)),
        ],
        out_specs=pl.BlockSpec((None, s, HEAD_DIM), head),
        out_shape=jax.ShapeDtypeStruct((b, s, NA_WIDTH), BF16),
        scratch_shapes=[pltpu.VMEM((3, tq, n_keys), F32)],
        compiler_params=_params("parallel", "parallel"),
        name="neighbourhood_attn",
    )(q, k, v, pair_tables)


def _odd_out_kernel(yc_ref, u_ref, vn_ref, ws_ref, bs_ref, wo_hbm, x_ref, o_ref, y_ref,
                    wo_ref, wstage_ref, w_sems):
    pl.when(_first_grid_step())(functools.partial(_load_weight_as_bf16, wo_hbm, wo_ref, wstage_ref, w_sems))
    width = SG_WIDTH // SG_GROUPS
    for c in range(OUT_TM // SG_CHUNK):
        rows = slice(c * SG_CHUNK, (c + 1) * SG_CHUNK)
        for g in range(SG_GROUPS):
            cols = slice(g * width, (g + 1) * width)
            sv = jnp.dot(ws_ref[g], vn_ref[rows, cols], preferred_element_type=F32) + bs_ref[g]
            y_ref[rows, NA_WIDTH + g * width:NA_WIDTH + (g + 1) * width] = (u_ref[rows, cols] * sv).astype(BF16)
    y_ref[:, :NA_WIDTH] = yc_ref[...]
    o_ref[...] = x_ref[...] + jnp.dot(y_ref[...], wo_ref[...], preferred_element_type=F32)


def _odd_out(yc, u, vn, w_s, b_s, w_out, x):
    b, s, _ = x.shape
    tm = OUT_TM
    width = SG_WIDTH // SG_GROUPS
    tile = lambda bi, i: (bi, i, 0)
    return pl.pallas_call(
        _odd_out_kernel,
        grid=(b, s // tm),
        in_specs=[
            pl.BlockSpec((None, tm, NA_WIDTH), tile),
            pl.BlockSpec((None, tm, SG_WIDTH), tile),
            pl.BlockSpec((None, tm, SG_WIDTH), tile),
            _resident((SG_GROUPS, SG_CHUNK, SG_CHUNK), lambda bi, i: (0, 0, 0)),
            _resident((SG_GROUPS, SG_CHUNK, width), lambda bi, i: (0, 0, 0)),
            pl.BlockSpec(memory_space=pl.ANY),
            pl.BlockSpec((None, tm, D_MODEL), tile),
        ],
        out_specs=pl.BlockSpec((None, tm, D_MODEL), tile),
        out_shape=jax.ShapeDtypeStruct((b, s, D_MODEL), F32),
        scratch_shapes=[pltpu.VMEM((tm, NA_WIDTH + SG_WIDTH), BF16)]
        + _weight_scratch(NA_WIDTH + SG_WIDTH, D_MODEL),
        compiler_params=_params("arbitrary", "arbitrary"),
        name="odd_out",
    )(yc, u, vn, w_s, b_s, w_out, x)


def kernel(x, norm_ffn1, norm_mix, norm_ffn2, norm_out, ffn_w_gate, ffn_w_up, ffn_w_down, even_w_in, pool_w, pool_scale, dil_q_gain, dil_k_gain, even_w_out, odd_w_in, na_q_gain, na_k_gain, na_rpb, sg_v_gain, sg_w, sg_b, odd_w_out):
    depth = norm_ffn1.shape[0]
    slopes = jnp.asarray(2.0 ** (-8.0 * np.arange(1, DIL_HEADS + 1) / DIL_HEADS), dtype=F32)
    width = SG_WIDTH // SG_GROUPS

    for layer in range(depth):
        x = _ffn(x, norm_ffn1[layer], ffn_w_gate, ffn_w_up, ffn_w_down, layer, 0)
        if layer % 2 == 0:
            e = layer // 2
            a, q, k, v = _even_in(x, norm_mix[layer], even_w_in[e], dil_q_gain[e], dil_k_gain[e])
            yb = _dilated(q, k, v, slopes)
            x = _even_out(a, yb, pool_w[e].astype(BF16), pool_scale[e], even_w_out[e], x)
        else:
            o = layer // 2
            q, k, v, u, vn = _odd_in(x, norm_mix[layer], odd_w_in[o],
                                     na_q_gain[o], na_k_gain[o], sg_v_gain[o])
            yc = _neighbourhood(q, k, v, _na_pair_tables(na_rpb[o]))
            b_s = jnp.broadcast_to(sg_b[o][:, :, None], (SG_GROUPS, SG_CHUNK, width))
            x = _odd_out(yc, u, vn, sg_w[o].astype(BF16), b_s, odd_w_out[o], x)
        x = _ffn(x, norm_ffn2[layer], ffn_w_gate, ffn_w_up, ffn_w_down, layer, 1, gain_out=norm_out[layer])
    return x
```

```python
import functools

import numpy as np
import jax
import jax.numpy as jnp
from jax import lax
from jax.experimental import pallas as pl
from jax.experimental.pallas import tpu as pltpu

F32 = jnp.float32
BF16 = jnp.bfloat16

D_MODEL = 2048
D_FF = 5632
HEAD_DIM = 128
POOL_WINDOWS = (2, 4, 8, 16)
POOL_WIDTH = 512
POOL_HALO = 8
DIL_PAIRS = ((128, 1), (512, 4), (2048, 16))
DIL_HEADS_PER_GROUP = 4
DIL_HEADS = 12
DIL_WIDTH = DIL_HEADS * HEAD_DIM
DIL_OUT = DIL_HEADS_PER_GROUP * HEAD_DIM
GRID_W = 64
NA_ROWS = 8
NA_COLS = 16
NA_HEADS = 8
NA_WIDTH = NA_HEADS * HEAD_DIM
NA_QROWS = 4
NA_KROWS = NA_QROWS + NA_ROWS
SG_CHUNK = 128
SG_GROUPS = 8
SG_WIDTH = 1024
RMS_EPS = 1e-6
NEG_INF = -1e30

VMEM_LIMIT_BYTES = 60 * 1024 * 1024

FFN_TM = 1024
FFN_TF = 512
FFN_TC = 256
PROJ_TM = 512
PROJ_TN = 512
DIL_TQ = 128
DIL_TA = 2048
DIL_RADIUS = 64
DIL_UNROLL = 8
OUT_TM = 512
W_CHUNK = 512


def _params(*semantics):
    return pltpu.CompilerParams(dimension_semantics=semantics, vmem_limit_bytes=VMEM_LIMIT_BYTES)


def _resident(block_shape, index_map):
    return pl.BlockSpec(block_shape, index_map, pipeline_mode=pl.Buffered(1))


def _rms(x, gain):
    return x * lax.rsqrt(jnp.mean(x * x, axis=-1, keepdims=True) + RMS_EPS) * gain


def _first_grid_step():
    return (pl.program_id(0) == 0) & (pl.program_id(1) == 0)


def _load_weight_as_bf16(w_hbm, wbf_ref, stage_ref, sems):
    n_chunks = wbf_ref.shape[1] // W_CHUNK

    def copy(c):
        cols = pl.ds(c * W_CHUNK, W_CHUNK)
        return pltpu.make_async_copy(w_hbm.at[:, cols], stage_ref.at[c % 2], sems.at[c % 2])

    copy(0).start()
    for c in range(n_chunks):
        if c + 1 < n_chunks:
            copy(c + 1).start()
        copy(c).wait()
        wbf_ref[:, c * W_CHUNK:(c + 1) * W_CHUNK] = stage_ref[c % 2].astype(BF16)


def _weight_scratch(rows, cols):
    assert cols % W_CHUNK == 0
    return [pltpu.VMEM((rows, cols), BF16), pltpu.VMEM((2, rows, W_CHUNK), F32), pltpu.SemaphoreType.DMA((2,))]


def _ffn_kernel(x_hbm, g_ref, wg_ref, wu_ref, wd_ref, gout_ref, o_ref, h_ref, xbuf_ref, x_sem, *, final_norm):
    j = pl.program_id(2)
    tiles_per_batch = pl.num_programs(1)
    n_tiles = pl.num_programs(0) * tiles_per_batch
    tile = pl.program_id(0) * tiles_per_batch + pl.program_id(1)

    def x_copy(t):
        rows = pl.ds(pl.multiple_of((t % tiles_per_batch) * FFN_TM, FFN_TM), FFN_TM)
        return pltpu.make_async_copy(x_hbm.at[t // tiles_per_batch, rows, :], xbuf_ref, x_sem)

    @pl.when(j == 0)
    def _():
        @pl.when(tile == 0)
        def _():
            x_copy(tile).start()

        x_copy(tile).wait()
        x = xbuf_ref[...]
        h_ref[...] = _rms(x, g_ref[...]).astype(BF16)
        o_ref[...] = x

    @pl.when((j == 1) & (tile + 1 < n_tiles))
    def _():
        x_copy(tile + 1).start()

    h = h_ref[...]
    for c in range(FFN_TF // FFN_TC):
        cols = slice(c * FFN_TC, (c + 1) * FFN_TC)
        gate = jnp.dot(h, wg_ref[:, cols].astype(BF16), preferred_element_type=F32)
        up = jnp.dot(h, wu_ref[:, cols].astype(BF16), preferred_element_type=F32)
        act = (gate * jax.nn.sigmoid(gate)) * up * 0.5
        o_ref[...] += jnp.dot(act.astype(BF16), wd_ref[cols, :].astype(BF16), preferred_element_type=F32)

    if final_norm:
        @pl.when(j == pl.num_programs(2) - 1)
        def _():
            o_ref[...] = _rms(o_ref[...], gout_ref[...])


def _ffn(x, gain, w_gate, w_up, w_down, layer, idx, gain_out=None):
    b, s, _ = x.shape
    assert D_FF // FFN_TF >= 2
    final_norm = gain_out is not None
    if gain_out is None:
        gain_out = gain
    tile = lambda bi, i, j: (bi, i, 0)
    const = lambda bi, i, j: (0, 0)
    return pl.pallas_call(
        functools.partial(_ffn_kernel, final_norm=final_norm),
        grid=(b, s // FFN_TM, D_FF // FFN_TF),
        in_specs=[
            pl.BlockSpec(memory_space=pl.ANY),
            pl.BlockSpec((1, D_MODEL), const),
            pl.BlockSpec((None, None, D_MODEL, FFN_TF), lambda bi, i, j: (layer, idx, 0, j)),
            pl.BlockSpec((None, None, D_MODEL, FFN_TF), lambda bi, i, j: (layer, idx, 0, j)),
            pl.BlockSpec((None, None, FFN_TF, D_MODEL), lambda bi, i, j: (layer, idx, j, 0)),
            pl.BlockSpec((1, D_MODEL), const),
        ],
        out_specs=pl.BlockSpec((None, FFN_TM, D_MODEL), tile),
        out_shape=jax.ShapeDtypeStruct((b, s, D_MODEL), F32),
        scratch_shapes=[pltpu.VMEM((FFN_TM, D_MODEL), BF16), pltpu.VMEM((FFN_TM, D_MODEL), F32),
                        pltpu.SemaphoreType.DMA(())],
        compiler_params=_params("arbitrary", "arbitrary", "arbitrary"),
        name="ffn_final" if final_norm else "ffn",
    )(x, gain.reshape(1, D_MODEL), w_gate, w_up, w_down, gain_out.reshape(1, D_MODEL))


def _head_norm_store(z, out_ref, col0, gain):
    for hd in range(PROJ_TN // HEAD_DIM):
        zh = z[:, hd * HEAD_DIM:(hd + 1) * HEAD_DIM]
        out_ref[:, col0 + hd * HEAD_DIM:col0 + (hd + 1) * HEAD_DIM] = _rms(zh, gain).astype(out_ref.dtype)


def _store_by_residue(z, out_ref, stage_ref, dil, gain=None):
    tm = z.shape[0]
    for hd in range(z.shape[1] // HEAD_DIM):
        cols = slice(hd * HEAD_DIM, (hd + 1) * HEAD_DIM)
        zh = z[:, cols]
        if gain is not None:
            zh = _rms(zh, gain)
        if dil == 1:
            out_ref[0, :, cols] = zh.astype(BF16)
        else:
            stage_ref[hd] = zh
            for r in range(dil):
                out_ref[r, :, cols] = stage_ref[hd, pl.ds(r, tm // dil, stride=dil), :].astype(BF16)


def _even_in_kernel(x_ref, g_ref, w_hbm, qg_ref, kg_ref, a_ref, *rest):
    n_groups = len(DIL_PAIRS)
    q_refs, k_refs, v_refs = rest[:n_groups], rest[n_groups:2 * n_groups], rest[2 * n_groups:3 * n_groups]
    h_ref, stage_ref, w_ref, wstage_ref, w_sems = rest[3 * n_groups:]
    pl.when(_first_grid_step())(functools.partial(_load_weight_as_bf16, w_hbm, w_ref, wstage_ref, w_sems))
    h_ref[...] = _rms(x_ref[...], g_ref[...]).astype(BF16)

    def chunk(c):
        return jnp.dot(h_ref[...], w_ref[:, c * PROJ_TN:(c + 1) * PROJ_TN], preferred_element_type=F32)

    a_ref[...] = chunk(0)
    for g, (_, dil) in enumerate(DIL_PAIRS):
        _store_by_residue(chunk(1 + g), q_refs[g], stage_ref, dil, qg_ref[...])
        _store_by_residue(chunk(1 + n_groups + g), k_refs[g], stage_ref, dil, kg_ref[...])
        _store_by_residue(chunk(1 + 2 * n_groups + g), v_refs[g], stage_ref, dil)


def _even_in(x, gain, w_in, q_gain, k_gain):
    b, s, _ = x.shape
    n_in = w_in.shape[1]
    assert PROJ_TN == DIL_OUT
    row = lambda bi, i: (bi, i, 0)
    const = lambda bi, i: (0, 0)
    group_specs = [pl.BlockSpec((None, dil, PROJ_TM // dil, DIL_OUT), lambda bi, i: (bi, 0, i, 0))
                   for _, dil in DIL_PAIRS]
    group_shapes = [jax.ShapeDtypeStruct((b, dil, s // dil, DIL_OUT), BF16) for _, dil in DIL_PAIRS]
    outs = pl.pallas_call(
        _even_in_kernel,
        grid=(b, s // PROJ_TM),
        in_specs=[
            pl.BlockSpec((None, PROJ_TM, D_MODEL), row),
            pl.BlockSpec((1, D_MODEL), const),
            pl.BlockSpec(memory_space=pl.ANY),
            pl.BlockSpec((1, HEAD_DIM), const),
            pl.BlockSpec((1, HEAD_DIM), const),
        ],
        out_specs=[pl.BlockSpec((None, PROJ_TM, POOL_WIDTH), row)] + group_specs * 3,
        out_shape=[jax.ShapeDtypeStruct((b, s, POOL_WIDTH), F32)] + group_shapes * 3,
        scratch_shapes=[pltpu.VMEM((PROJ_TM, D_MODEL), BF16),
                        pltpu.VMEM((DIL_HEADS_PER_GROUP, PROJ_TM, HEAD_DIM), F32)]
        + _weight_scratch(D_MODEL, n_in),
        compiler_params=_params("arbitrary", "arbitrary"),
        name="even_in",
    )(x, gain.reshape(1, D_MODEL), w_in,
      (q_gain * HEAD_DIM ** -0.5).reshape(1, HEAD_DIM), k_gain.reshape(1, HEAD_DIM))
    n_groups = len(DIL_PAIRS)
    return outs[0], outs[1:1 + n_groups], outs[1 + n_groups:1 + 2 * n_groups], outs[1 + 2 * n_groups:]


def _odd_in_kernel(x_ref, g_ref, w_hbm, qg_ref, kg_ref, vg_ref, q_ref, k_ref, v_ref, u_ref, vn_ref,
                   h_ref, gv_ref, w_ref, wstage_ref, w_sems):
    pl.when(_first_grid_step())(functools.partial(_load_weight_as_bf16, w_hbm, w_ref, wstage_ref, w_sems))
    h_ref[...] = _rms(x_ref[...], g_ref[...]).astype(BF16)

    def chunk(c):
        return jnp.dot(h_ref[...], w_ref[:, c * PROJ_TN:(c + 1) * PROJ_TN], preferred_element_type=F32)

    per = NA_WIDTH // PROJ_TN
    for c in range(per):
        _head_norm_store(chunk(c), q_ref, c * PROJ_TN, qg_ref[...])
        _head_norm_store(chunk(per + c), k_ref, c * PROJ_TN, kg_ref[...])
        v_ref[:, c * PROJ_TN:(c + 1) * PROJ_TN] = chunk(2 * per + c).astype(BF16)
    per_sg = SG_WIDTH // PROJ_TN
    for c in range(per_sg):
        u_ref[:, c * PROJ_TN:(c + 1) * PROJ_TN] = jax.nn.gelu(chunk(3 * per + c))
        gv_ref[:, c * PROJ_TN:(c + 1) * PROJ_TN] = jax.nn.gelu(chunk(3 * per + per_sg + c))
    vn_ref[...] = _rms(gv_ref[...], vg_ref[...]).astype(BF16)


def _odd_in(x, gain, w_in, q_gain, k_gain, v_gain):
    b, s, _ = x.shape
    n_in = w_in.shape[1]
    row = lambda bi, i: (bi, i, 0)
    const = lambda bi, i: (0, 0)
    return pl.pallas_call(
        _odd_in_kernel,
        grid=(b, s // PROJ_TM),
        in_specs=[
            pl.BlockSpec((None, PROJ_TM, D_MODEL), row),
            pl.BlockSpec((1, D_MODEL), const),
            pl.BlockSpec(memory_space=pl.ANY),
            pl.BlockSpec((1, HEAD_DIM), const),
            pl.BlockSpec((1, HEAD_DIM), const),
            pl.BlockSpec((1, SG_WIDTH), const),
        ],
        out_specs=[
            pl.BlockSpec((None, PROJ_TM, NA_WIDTH), row),
            pl.BlockSpec((None, PROJ_TM, NA_WIDTH), row),
            pl.BlockSpec((None, PROJ_TM, NA_WIDTH), row),
            pl.BlockSpec((None, PROJ_TM, SG_WIDTH), row),
            pl.BlockSpec((None, PROJ_TM, SG_WIDTH), row),
        ],
        out_shape=[
            jax.ShapeDtypeStruct((b, s, NA_WIDTH), BF16),
            jax.ShapeDtypeStruct((b, s, NA_WIDTH), BF16),
            jax.ShapeDtypeStruct((b, s, NA_WIDTH), BF16),
            jax.ShapeDtypeStruct((b, s, SG_WIDTH), F32),
            jax.ShapeDtypeStruct((b, s, SG_WIDTH), BF16),
        ],
        scratch_shapes=[pltpu.VMEM((PROJ_TM, D_MODEL), BF16), pltpu.VMEM((PROJ_TM, SG_WIDTH), F32)]
        + _weight_scratch(D_MODEL, n_in),
        compiler_params=_params("arbitrary", "arbitrary"),
        name="odd_in",
    )(x, gain.reshape(1, D_MODEL), w_in,
      (q_gain * HEAD_DIM ** -0.5).reshape(1, HEAD_DIM), k_gain.reshape(1, HEAD_DIM),
      v_gain.reshape(1, SG_WIDTH))


def _dilated_kernel(slopes_ref, q0_ref, q1_ref, q2_ref, k0_ref, k1_ref, k2_ref, v0_ref, v1_ref, v2_ref,
                    o_ref, on_ref, ls_ref, s_ref, *, seq):
    hh = pl.program_id(1)
    step = pl.program_id(2)
    q_refs = (q0_ref, q1_ref, q2_ref)
    k_refs = (k0_ref, k1_ref, k2_ref)
    v_refs = (v0_ref, v1_ref, v2_ref)
    for g, (window, dil) in enumerate(DIL_PAIRS):
        assert window // (2 * dil) == DIL_RADIUS
        q_ref, k_ref, v_ref = q_refs[g], k_refs[g], v_refs[g]
        sub_len = seq // dil
        per_res = DIL_TA // dil
        nq = min(per_res, DIL_TQ)
        n_sub = per_res // nq
        n_keys = min(nq + 2 * DIL_RADIUS, sub_len)
        slope = slopes_ref[g * DIL_HEADS_PER_GROUP + hh] * dil
        base = (lax.broadcasted_iota(jnp.int32, (nq, n_keys), 1)
                - lax.broadcasted_iota(jnp.int32, (nq, n_keys), 0))

        def place(t, nq=nq, n_sub=n_sub, per_res=per_res, sub_len=sub_len, n_keys=n_keys):
            r = t // n_sub
            q_off = pl.multiple_of((t % n_sub) * nq, nq)
            q_pos = step * per_res + q_off
            k_pos = pl.multiple_of(jnp.clip(q_pos - DIL_RADIUS, 0, sub_len - n_keys), DIL_RADIUS)
            return r, q_off, q_pos, k_pos

        def scores(t, slot, q_ref=q_ref, k_ref=k_ref, nq=nq, n_keys=n_keys, slope=slope, base=base,
                   place=place):
            r, q_off, q_pos, k_pos = place(t)
            q = q_ref[r, pl.ds(q_off, nq), :]
            k = k_ref[r, pl.ds(k_pos, n_keys), :]
            s = lax.dot_general(q, k, (((1,), (1,)), ((), ())), preferred_element_type=F32)
            dist = jnp.abs(base + (k_pos - q_pos))
            s_ref[slot, 0:nq, 0:n_keys] = jnp.where(dist <= DIL_RADIUS, s - slope * dist.astype(F32), NEG_INF)

        def finish(t, slot, v_ref=v_ref, g=g, dil=dil, nq=nq, n_keys=n_keys, place=place):
            r, q_off, _, k_pos = place(t)
            v = v_ref[r, pl.ds(k_pos, n_keys), :]
            s = s_ref[slot, 0:nq, 0:n_keys]
            m = jnp.max(s, axis=-1, keepdims=True)
            p = jnp.exp(s - m)
            den = jnp.sum(p, axis=-1, keepdims=True)
            out = jnp.dot(p.astype(BF16), v, preferred_element_type=F32) / den
            lse = jnp.broadcast_to(m + jnp.log(den), (nq, HEAD_DIM))
            if dil == 1:
                rows = pl.ds(q_off, nq)
            else:
                rows = pl.ds(r + dil * q_off, nq, stride=dil)
            on_ref[g, rows, :] = out
            ls_ref[g, rows, :] = lse

        n_tiles = dil * n_sub
        assert n_tiles % DIL_UNROLL == 0

        def body(i, carry, scores=scores, finish=finish):
            t0 = i * DIL_UNROLL
            scores(t0, 0)
            for u in range(DIL_UNROLL):
                if u + 1 < DIL_UNROLL:
                    scores(t0 + u + 1, (u + 1) % 2)
                finish(t0 + u, u % 2)
            return carry

        lax.fori_loop(0, n_tiles // DIL_UNROLL, body, 0)

    def merge(c, carry):
        rows = pl.ds(pl.multiple_of(c * DIL_TQ, DIL_TQ), DIL_TQ)
        lse = [ls_ref[g, rows, :] for g in range(len(DIL_PAIRS))]
        top = jnp.maximum(jnp.maximum(lse[0], lse[1]), lse[2])
        num = jnp.zeros((DIL_TQ, HEAD_DIM), F32)
        den = jnp.zeros((DIL_TQ, HEAD_DIM), F32)
        for g in range(len(DIL_PAIRS)):
            w = jnp.exp(lse[g] - top)
            num = num + w * on_ref[g, rows, :]
            den = den + w
        o_ref[rows, :] = (num / den).astype(o_ref.dtype)
        return carry

    lax.fori_loop(0, DIL_TA // DIL_TQ, merge, 0)


def _dilated(q, k, v, slopes):
    b, _, s, _ = q[0].shape
    n_groups = len(DIL_PAIRS)

    def q_spec(dil):
        return pl.BlockSpec((None, dil, DIL_TA // dil, HEAD_DIM), lambda bi, hh, i: (bi, 0, i, hh))

    def kv_spec(dil):
        return pl.BlockSpec((None, dil, s // dil, HEAD_DIM), lambda bi, hh, i: (bi, 0, 0, hh))

    dils = [dil for _, dil in DIL_PAIRS]
    return pl.pallas_call(
        functools.partial(_dilated_kernel, seq=s),
        grid=(b, DIL_HEADS_PER_GROUP, s // DIL_TA),
        in_specs=[pl.BlockSpec(memory_space=pltpu.SMEM)]
        + [q_spec(d) for d in dils] + [kv_spec(d) for d in dils] + [kv_spec(d) for d in dils],
        out_specs=pl.BlockSpec((None, DIL_TA, HEAD_DIM), lambda bi, hh, i: (bi, i, hh)),
        out_shape=jax.ShapeDtypeStruct((b, s, DIL_OUT), BF16),
        scratch_shapes=[pltpu.VMEM((n_groups, DIL_TA, HEAD_DIM), F32),
                        pltpu.VMEM((n_groups, DIL_TA, HEAD_DIM), F32),
                        pltpu.VMEM((2, DIL_TQ, DIL_TQ + 2 * DIL_RADIUS), F32)],
        compiler_params=_params("parallel", "parallel", "arbitrary"),
        name="dilated_attn",
    )(slopes, *q, *k, *v)


def _even_out_kernel(a_ref, prev_ref, next_ref, yb_ref, pw_ref, ps_ref, wo_hbm, x_ref, o_ref,
                     ext_ref, y_ref, wo_ref, wstage_ref, w_sems, *, seq):
    pl.when(_first_grid_step())(functools.partial(_load_weight_as_bf16, wo_hbm, wo_ref, wstage_ref, w_sems))
    i = pl.program_id(1)
    tm = OUT_TM
    ext_ref[0:POOL_HALO, :] = jnp.where(i == 0, 0.0, prev_ref[...])
    ext_ref[POOL_HALO:POOL_HALO + tm, :] = a_ref[...]
    ext_ref[POOL_HALO + tm:, :] = jnp.where(i == pl.num_programs(1) - 1, 0.0, next_ref[...])
    pos = i * tm + lax.broadcasted_iota(jnp.int32, (tm, 1), 0)
    for g, window in enumerate(POOL_WINDOWS):
        half = window // 2
        cols = slice(g * HEAD_DIM, (g + 1) * HEAD_DIM)
        total = ext_ref[POOL_HALO - half:POOL_HALO - half + tm, cols]
        for shift in range(-half + 1, half):
            total = total + ext_ref[POOL_HALO + shift:POOL_HALO + shift + tm, cols]
        count = (jnp.minimum(pos + half, seq) - jnp.maximum(pos - half, 0)).astype(F32)
        pooled = total / count - a_ref[:, cols]
        ya = jnp.dot(pooled.astype(BF16), pw_ref[g], preferred_element_type=F32) * ps_ref[:, cols]
        y_ref[:, cols] = ya.astype(BF16)
    y_ref[:, POOL_WIDTH:] = yb_ref[...]
    o_ref[...] = x_ref[...] + jnp.dot(y_ref[...], wo_ref[...], preferred_element_type=F32)


def _even_out(a, yb, pool_w, pool_scale, w_out, x):
    b, s, _ = x.shape
    tm = OUT_TM
    halo_blocks = tm // POOL_HALO
    tile = lambda bi, i: (bi, i, 0)
    return pl.pallas_call(
        functools.partial(_even_out_kernel, seq=s),
        grid=(b, s // tm),
        in_specs=[
            pl.BlockSpec((None, tm, POOL_WIDTH), tile),
            pl.BlockSpec((None, POOL_HALO, POOL_WIDTH),
                         lambda bi, i: (bi, jnp.maximum(i * halo_blocks - 1, 0), 0)),
            pl.BlockSpec((None, POOL_HALO, POOL_WIDTH),
                         lambda bi, i: (bi, jnp.minimum((i + 1) * halo_blocks, s // POOL_HALO - 1), 0)),
            pl.BlockSpec((None, tm, DIL_OUT), tile),
            _resident((len(POOL_WINDOWS), HEAD_DIM, HEAD_DIM), lambda bi, i: (0, 0, 0)),
            pl.BlockSpec((1, POOL_WIDTH), lambda bi, i: (0, 0)),
            pl.BlockSpec(memory_space=pl.ANY),
            pl.BlockSpec((None, tm, D_MODEL), tile),
        ],
        out_specs=pl.BlockSpec((None, tm, D_MODEL), tile),
        out_shape=jax.ShapeDtypeStruct((b, s, D_MODEL), F32),
        scratch_shapes=[pltpu.VMEM((tm + 2 * POOL_HALO, POOL_WIDTH), F32),
                        pltpu.VMEM((tm, POOL_WIDTH + DIL_OUT), BF16)]
        + _weight_scratch(POOL_WIDTH + DIL_OUT, D_MODEL),
        compiler_params=_params("arbitrary", "arbitrary"),
        name="even_out",
    )(a, a, a, yb, pool_w, pool_scale.reshape(1, POOL_WIDTH), w_out, x)


def _na_fill_bias(pair_ref, bias_ref, rb, rows):
    k_row0 = min(max(rb * NA_QROWS - NA_ROWS // 2, 0), rows - NA_KROWS)
    left_half = lax.broadcasted_iota(jnp.int32, (GRID_W, 2 * GRID_W), 1) < GRID_W
    for qr in range(NA_QROWS):
        r = rb * NA_QROWS + qr
        row_start = min(max(r - NA_ROWS // 2, 0), rows - NA_ROWS)
        for p in range(NA_KROWS // 2):
            kr = k_row0 + 2 * p
            ok_left = row_start <= kr < row_start + NA_ROWS
            ok_right = row_start <= kr + 1 < row_start + NA_ROWS
            if ok_left or ok_right:
                block = pair_ref[kr + 1 - r + NA_ROWS - 1]
                if not ok_right:
                    block = jnp.where(left_half, block, NEG_INF)
                if not ok_left:
                    block = jnp.where(left_half, NEG_INF, block)
            else:
                block = jnp.full((GRID_W, 2 * GRID_W), NEG_INF, F32)
            bias_ref[qr * GRID_W:(qr + 1) * GRID_W, p * 2 * GRID_W:(p + 1) * 2 * GRID_W] = block


def _na_kernel(q_ref, k_ref, v_ref, pair_ref, o_ref, bias_ref, s_ref, *, rows):
    n_blocks = rows // NA_QROWS
    tq = NA_QROWS * GRID_W
    n_keys = NA_KROWS * GRID_W
    kinds = (0, 1, n_blocks - 1)
    for slot, kind in enumerate(kinds):
        _na_fill_bias(pair_ref, bias_ref.at[slot], kind, rows)

    def key_start(rb):
        return min(max(rb * NA_QROWS - NA_ROWS // 2, 0), rows - NA_KROWS) * GRID_W

    def scores(rb):
        kind_slot = 0 if rb == 0 else (2 if rb == n_blocks - 1 else 1)
        q = q_ref[rb * tq:(rb + 1) * tq, :]
        k = k_ref[key_start(rb):key_start(rb) + n_keys, :]
        s_ref[rb % 2] = (lax.dot_general(q, k, (((1,), (1,)), ((), ())), preferred_element_type=F32)
                         + bias_ref[kind_slot])

    def finish(rb):
        v = v_ref[key_start(rb):key_start(rb) + n_keys, :]
        s = s_ref[rb % 2]
        m = jnp.max(s, axis=-1, keepdims=True)
        p = jnp.exp(s - m)
        den = jnp.sum(p, axis=-1, keepdims=True)
        out = jnp.dot(p.astype(BF16), v, preferred_element_type=F32) / den
        o_ref[rb * tq:(rb + 1) * tq, :] = out.astype(o_ref.dtype)

    scores(0)
    for rb in range(n_blocks):
        if rb + 1 < n_blocks:
            scores(rb + 1)
        finish(rb)


def _na_pair_tables(rpb):
    n_heads, n_rel_rows, n_rel_cols = rpb.shape
    c = np.arange(GRID_W)
    col_start = np.clip(c - NA_COLS // 2, 0, GRID_W - NA_COLS)
    col_ok = (c[None, :] >= col_start[:, None]) & (c[None, :] < col_start[:, None] + NA_COLS)
    rel_c = np.clip(c[None, :] - c[:, None], -(NA_COLS - 1), NA_COLS - 1) + (NA_COLS - 1)
    pick = (rel_c.reshape(1, -1) == np.arange(n_rel_cols)[:, None]).astype(np.float32)
    by_col = jnp.dot(rpb.astype(F32).reshape(n_heads * n_rel_rows, n_rel_cols), pick,
                     precision=lax.Precision.HIGHEST).reshape(n_heads, n_rel_rows, GRID_W, GRID_W)
    by_col = jnp.where(col_ok[None, None], by_col, NEG_INF)
    masked = jnp.full((n_heads, 1, GRID_W, GRID_W), NEG_INF, F32)
    padded = jnp.concatenate([masked, by_col, masked], axis=1)
    return jnp.concatenate([padded[:, :-1], padded[:, 1:]], axis=-1)


def _neighbourhood(q, k, v, pair_tables):
    b, s, _ = q.shape
    rows = s // GRID_W
    n_blocks = rows // NA_QROWS
    assert n_blocks >= 3 and rows >= NA_KROWS
    tq = NA_QROWS * GRID_W
    n_keys = NA_KROWS * GRID_W
    head = lambda bi, h: (bi, 0, h)
    return pl.pallas_call(
        functools.partial(_na_kernel, rows=rows),
        grid=(b, NA_HEADS),
        in_specs=[
            pl.BlockSpec((None, s, HEAD_DIM), head),
            pl.BlockSpec((None, s, HEAD_DIM), head),
            pl.BlockSpec((None, s, HEAD_DIM), head),
            pl.BlockSpec((None, 2 * NA_ROWS, GRID_W, 2 * GRID_W), lambda bi, h: (h, 0, 0, 0)),
        ],
        out_specs=pl.BlockSpec((None, s, HEAD_DIM), head),
        out_shape=jax.ShapeDtypeStruct((b, s, NA_WIDTH), BF16),
        scratch_shapes=[pltpu.VMEM((3, tq, n_keys), F32), pltpu.VMEM((2, tq, n_keys), F32)],
        compiler_params=_params("parallel", "parallel"),
        name="neighbourhood_attn",
    )(q, k, v, pair_tables)


def _odd_out_kernel(yc_ref, u_ref, vn_ref, ws_ref, bs_ref, wo_hbm, x_ref, o_ref, y_ref,
                    wo_ref, wstage_ref, w_sems):
    pl.when(_first_grid_step())(functools.partial(_load_weight_as_bf16, wo_hbm, wo_ref, wstage_ref, w_sems))
    width = SG_WIDTH // SG_GROUPS
    for c in range(OUT_TM // SG_CHUNK):
        rows = slice(c * SG_CHUNK, (c + 1) * SG_CHUNK)
        for g in range(SG_GROUPS):
            cols = slice(g * width, (g + 1) * width)
            sv = jnp.dot(ws_ref[g], vn_ref[rows, cols], preferred_element_type=F32) + bs_ref[g]
            y_ref[rows, NA_WIDTH + g * width:NA_WIDTH + (g + 1) * width] = (u_ref[rows, cols] * sv).astype(BF16)
    y_ref[:, :NA_WIDTH] = yc_ref[...]
    o_ref[...] = x_ref[...] + jnp.dot(y_ref[...], wo_ref[...], preferred_element_type=F32)


def _odd_out(yc, u, vn, w_s, b_s, w_out, x):
    b, s, _ = x.shape
    tm = OUT_TM
    width = SG_WIDTH // SG_GROUPS
    tile = lambda bi, i: (bi, i, 0)
    return pl.pallas_call(
        _odd_out_kernel,
        grid=(b, s // tm),
        in_specs=[
            pl.BlockSpec((None, tm, NA_WIDTH), tile),
            pl.BlockSpec((None, tm, SG_WIDTH), tile),
            pl.BlockSpec((None, tm, SG_WIDTH), tile),
            _resident((SG_GROUPS, SG_CHUNK, SG_CHUNK), lambda bi, i: (0, 0, 0)),
            _resident((SG_GROUPS, SG_CHUNK, width), lambda bi, i: (0, 0, 0)),
            pl.BlockSpec(memory_space=pl.ANY),
            pl.BlockSpec((None, tm, D_MODEL), tile),
        ],
        out_specs=pl.BlockSpec((None, tm, D_MODEL), tile),
        out_shape=jax.ShapeDtypeStruct((b, s, D_MODEL), F32),
        scratch_shapes=[pltpu.VMEM((tm, NA_WIDTH + SG_WIDTH), BF16)]
        + _weight_scratch(NA_WIDTH + SG_WIDTH, D_MODEL),
        compiler_params=_params("arbitrary", "arbitrary"),
        name="odd_out",
    )(yc, u, vn, w_s, b_s, w_out, x)


def kernel(x, norm_ffn1, norm_mix, norm_ffn2, norm_out, ffn_w_gate, ffn_w_up, ffn_w_down, even_w_in, pool_w, pool_scale, dil_q_gain, dil_k_gain, even_w_out, odd_w_in, na_q_gain, na_k_gain, na_rpb, sg_v_gain, sg_w, sg_b, odd_w_out):
    depth = norm_ffn1.shape[0]
    slopes = jnp.asarray(2.0 ** (-8.0 * np.arange(1, DIL_HEADS + 1) / DIL_HEADS), dtype=F32)
    width = SG_WIDTH // SG_GROUPS

    for layer in range(depth):
        x = _ffn(x, norm_ffn1[layer], ffn_w_gate, ffn_w_up, ffn_w_down, layer, 0)
        if layer % 2 == 0:
            e = layer // 2
            a, q, k, v = _even_in(x, norm_mix[layer], even_w_in[e], dil_q_gain[e], dil_k_gain[e])
            yb = _dilated(q, k, v, slopes)
            x = _even_out(a, yb, pool_w[e].astype(BF16), pool_scale[e], even_w_out[e], x)
        else:
            o = layer // 2
            q, k, v, u, vn = _odd_in(x, norm_mix[layer], odd_w_in[o],
                                     na_q_gain[o], na_k_gain[o], sg_v_gain[o])
            yc = _neighbourhood(q, k, v, _na_pair_tables(na_rpb[o]))
            b_s = jnp.broadcast_to(sg_b[o][:, :, None], (SG_GROUPS, SG_CHUNK, width))
            x = _odd_out(yc, u, vn, sg_w[o].astype(BF16), b_s, odd_w_out[o], x)
        x = _ffn(x, norm_ffn2[layer], ffn_w_gate, ffn_w_up, ffn_w_down, layer, 1, gain_out=norm_out[layer])
    return x
```

```python
import functools

import numpy as np
import jax
import jax.numpy as jnp
from jax import lax
from jax.experimental import pallas as pl
from jax.experimental.pallas import tpu as pltpu

F32 = jnp.float32
BF16 = jnp.bfloat16

D_MODEL = 2048
D_FF = 5632
HEAD_DIM = 128
POOL_WINDOWS = (2, 4, 8, 16)
POOL_WIDTH = 512
POOL_HALO = 8
DIL_PAIRS = ((128, 1), (512, 4), (2048, 16))
DIL_HEADS_PER_GROUP = 4
DIL_HEADS = 12
DIL_WIDTH = DIL_HEADS * HEAD_DIM
DIL_OUT = DIL_HEADS_PER_GROUP * HEAD_DIM
GRID_W = 64
NA_ROWS = 8
NA_COLS = 16
NA_HEADS = 8
NA_WIDTH = NA_HEADS * HEAD_DIM
NA_QROWS = 4
NA_KROWS = NA_QROWS + NA_ROWS
SG_CHUNK = 128
SG_GROUPS = 8
SG_WIDTH = 1024
RMS_EPS = 1e-6
NEG_INF = -1e30

VMEM_LIMIT_BYTES = 60 * 1024 * 1024

FFN_TM = 1024
FFN_TF = 512
FFN_TC = 256
PROJ_TM = 512
PROJ_TN = 512
DIL_TQ = 128
DIL_TA = 2048
DIL_RADIUS = 64
DIL_UNROLL = 8
OUT_TM = 512
W_CHUNK = 512


def _params(*semantics):
    return pltpu.CompilerParams(dimension_semantics=semantics, vmem_limit_bytes=VMEM_LIMIT_BYTES)


def _resident(block_shape, index_map):
    return pl.BlockSpec(block_shape, index_map, pipeline_mode=pl.Buffered(1))


def _rms(x, gain):
    return x * lax.rsqrt(jnp.mean(x * x, axis=-1, keepdims=True) + RMS_EPS) * gain


def _first_grid_step():
    return (pl.program_id(0) == 0) & (pl.program_id(1) == 0)


def _load_weight_as_bf16(w_hbm, wbf_ref, stage_ref, sems):
    n_chunks = wbf_ref.shape[1] // W_CHUNK

    def copy(c):
        cols = pl.ds(c * W_CHUNK, W_CHUNK)
        return pltpu.make_async_copy(w_hbm.at[:, cols], stage_ref.at[c % 2], sems.at[c % 2])

    copy(0).start()
    for c in range(n_chunks):
        if c + 1 < n_chunks:
            copy(c + 1).start()
        copy(c).wait()
        wbf_ref[:, c * W_CHUNK:(c + 1) * W_CHUNK] = stage_ref[c % 2].astype(BF16)


def _weight_scratch(rows, cols):
    assert cols % W_CHUNK == 0
    return [pltpu.VMEM((rows, cols), BF16), pltpu.VMEM((2, rows, W_CHUNK), F32), pltpu.SemaphoreType.DMA((2,))]


def _ffn_kernel(x_hbm, g_ref, wg_ref, wu_ref, wd_ref, gout_ref, o_ref, h_ref, xbuf_ref, x_sem, *, final_norm):
    j = pl.program_id(2)
    tiles_per_batch = pl.num_programs(1)
    n_tiles = pl.num_programs(0) * tiles_per_batch
    tile = pl.program_id(0) * tiles_per_batch + pl.program_id(1)

    def x_copy(t):
        rows = pl.ds(pl.multiple_of((t % tiles_per_batch) * FFN_TM, FFN_TM), FFN_TM)
        return pltpu.make_async_copy(x_hbm.at[t // tiles_per_batch, rows, :], xbuf_ref, x_sem)

    @pl.when(j == 0)
    def _():
        @pl.when(tile == 0)
        def _():
            x_copy(tile).start()

        x_copy(tile).wait()
        x = xbuf_ref[...]
        h_ref[...] = _rms(x, g_ref[...]).astype(BF16)
        o_ref[...] = x

    @pl.when((j == 1) & (tile + 1 < n_tiles))
    def _():
        x_copy(tile + 1).start()

    h = h_ref[...]
    for c in range(FFN_TF // FFN_TC):
        cols = slice(c * FFN_TC, (c + 1) * FFN_TC)
        gate = jnp.dot(h, wg_ref[:, cols].astype(BF16), preferred_element_type=F32)
        up = jnp.dot(h, wu_ref[:, cols].astype(BF16), preferred_element_type=F32)
        act = (gate * jax.nn.sigmoid(gate)) * up * 0.5
        o_ref[...] += jnp.dot(act.astype(BF16), wd_ref[cols, :].astype(BF16), preferred_element_type=F32)

    if final_norm:
        @pl.when(j == pl.num_programs(2) - 1)
        def _():
            o_ref[...] = _rms(o_ref[...], gout_ref[...])


def _ffn(x, gain, w_gate, w_up, w_down, layer, idx, gain_out=None):
    b, s, _ = x.shape
    assert D_FF // FFN_TF >= 2
    final_norm = gain_out is not None
    if gain_out is None:
        gain_out = gain
    tile = lambda bi, i, j: (bi, i, 0)
    const = lambda bi, i, j: (0, 0)
    return pl.pallas_call(
        functools.partial(_ffn_kernel, final_norm=final_norm),
        grid=(b, s // FFN_TM, D_FF // FFN_TF),
        in_specs=[
            pl.BlockSpec(memory_space=pl.ANY),
            pl.BlockSpec((1, D_MODEL), const),
            pl.BlockSpec((None, None, D_MODEL, FFN_TF), lambda bi, i, j: (layer, idx, 0, j)),
            pl.BlockSpec((None, None, D_MODEL, FFN_TF), lambda bi, i, j: (layer, idx, 0, j)),
            pl.BlockSpec((None, None, FFN_TF, D_MODEL), lambda bi, i, j: (layer, idx, j, 0)),
            pl.BlockSpec((1, D_MODEL), const),
        ],
        out_specs=pl.BlockSpec((None, FFN_TM, D_MODEL), tile),
        out_shape=jax.ShapeDtypeStruct((b, s, D_MODEL), F32),
        scratch_shapes=[pltpu.VMEM((FFN_TM, D_MODEL), BF16), pltpu.VMEM((FFN_TM, D_MODEL), F32),
                        pltpu.SemaphoreType.DMA(())],
        compiler_params=_params("arbitrary", "arbitrary", "arbitrary"),
        name="ffn_final" if final_norm else "ffn",
    )(x, gain.reshape(1, D_MODEL), w_gate, w_up, w_down, gain_out.reshape(1, D_MODEL))


def _head_store(z, out_ref, head0, gain=None):
    for hd in range(PROJ_TN // HEAD_DIM):
        zh = z[:, hd * HEAD_DIM:(hd + 1) * HEAD_DIM]
        if gain is not None:
            zh = _rms(zh, gain)
        out_ref[head0 + hd] = zh.astype(out_ref.dtype)


def _store_by_residue(z, out_ref, stage_ref, dil, gain=None):
    tm = z.shape[0]
    for hd in range(z.shape[1] // HEAD_DIM):
        cols = slice(hd * HEAD_DIM, (hd + 1) * HEAD_DIM)
        zh = z[:, cols]
        if gain is not None:
            zh = _rms(zh, gain)
        if dil == 1:
            out_ref[hd, 0] = zh.astype(BF16)
        else:
            stage_ref[hd] = zh
            for r in range(dil):
                out_ref[hd, r] = stage_ref[hd, pl.ds(r, tm // dil, stride=dil), :].astype(BF16)


def _even_in_kernel(x_ref, g_ref, w_hbm, qg_ref, kg_ref, a_ref, *rest):
    n_groups = len(DIL_PAIRS)
    q_refs, k_refs, v_refs = rest[:n_groups], rest[n_groups:2 * n_groups], rest[2 * n_groups:3 * n_groups]
    h_ref, stage_ref, w_ref, wstage_ref, w_sems = rest[3 * n_groups:]
    pl.when(_first_grid_step())(functools.partial(_load_weight_as_bf16, w_hbm, w_ref, wstage_ref, w_sems))
    h_ref[...] = _rms(x_ref[...], g_ref[...]).astype(BF16)

    def chunk(c):
        return jnp.dot(h_ref[...], w_ref[:, c * PROJ_TN:(c + 1) * PROJ_TN], preferred_element_type=F32)

    a_ref[...] = chunk(0)
    for g, (_, dil) in enumerate(DIL_PAIRS):
        _store_by_residue(chunk(1 + g), q_refs[g], stage_ref, dil, qg_ref[...])
        _store_by_residue(chunk(1 + n_groups + g), k_refs[g], stage_ref, dil, kg_ref[...])
        _store_by_residue(chunk(1 + 2 * n_groups + g), v_refs[g], stage_ref, dil)


def _even_in(x, gain, w_in, q_gain, k_gain):
    b, s, _ = x.shape
    n_in = w_in.shape[1]
    assert PROJ_TN == DIL_OUT
    row = lambda bi, i: (bi, i, 0)
    const = lambda bi, i: (0, 0)
    hpg = DIL_HEADS_PER_GROUP
    group_specs = [pl.BlockSpec((None, hpg, dil, PROJ_TM // dil, HEAD_DIM), lambda bi, i: (bi, 0, 0, i, 0))
                   for _, dil in DIL_PAIRS]
    group_shapes = [jax.ShapeDtypeStruct((b, hpg, dil, s // dil, HEAD_DIM), BF16) for _, dil in DIL_PAIRS]
    outs = pl.pallas_call(
        _even_in_kernel,
        grid=(b, s // PROJ_TM),
        in_specs=[
            pl.BlockSpec((None, PROJ_TM, D_MODEL), row),
            pl.BlockSpec((1, D_MODEL), const),
            pl.BlockSpec(memory_space=pl.ANY),
            pl.BlockSpec((1, HEAD_DIM), const),
            pl.BlockSpec((1, HEAD_DIM), const),
        ],
        out_specs=[pl.BlockSpec((None, PROJ_TM, POOL_WIDTH), row)] + group_specs * 3,
        out_shape=[jax.ShapeDtypeStruct((b, s, POOL_WIDTH), F32)] + group_shapes * 3,
        scratch_shapes=[pltpu.VMEM((PROJ_TM, D_MODEL), BF16),
                        pltpu.VMEM((DIL_HEADS_PER_GROUP, PROJ_TM, HEAD_DIM), F32)]
        + _weight_scratch(D_MODEL, n_in),
        compiler_params=_params("arbitrary", "arbitrary"),
        name="even_in",
    )(x, gain.reshape(1, D_MODEL), w_in,
      (q_gain * HEAD_DIM ** -0.5).reshape(1, HEAD_DIM), k_gain.reshape(1, HEAD_DIM))
    n_groups = len(DIL_PAIRS)
    return outs[0], outs[1:1 + n_groups], outs[1 + n_groups:1 + 2 * n_groups], outs[1 + 2 * n_groups:]


def _odd_in_kernel(x_ref, g_ref, w_hbm, qg_ref, kg_ref, vg_ref, q_ref, k_ref, v_ref, u_ref, vn_ref,
                   h_ref, gv_ref, w_ref, wstage_ref, w_sems):
    pl.when(_first_grid_step())(functools.partial(_load_weight_as_bf16, w_hbm, w_ref, wstage_ref, w_sems))
    h_ref[...] = _rms(x_ref[...], g_ref[...]).astype(BF16)

    def chunk(c):
        return jnp.dot(h_ref[...], w_ref[:, c * PROJ_TN:(c + 1) * PROJ_TN], preferred_element_type=F32)

    per = NA_WIDTH // PROJ_TN
    heads_per_chunk = PROJ_TN // HEAD_DIM
    for c in range(per):
        _head_store(chunk(c), q_ref, c * heads_per_chunk, qg_ref[...])
        _head_store(chunk(per + c), k_ref, c * heads_per_chunk, kg_ref[...])
        _head_store(chunk(2 * per + c), v_ref, c * heads_per_chunk)
    per_sg = SG_WIDTH // PROJ_TN
    for c in range(per_sg):
        u_ref[:, c * PROJ_TN:(c + 1) * PROJ_TN] = jax.nn.gelu(chunk(3 * per + c))
        gv_ref[:, c * PROJ_TN:(c + 1) * PROJ_TN] = jax.nn.gelu(chunk(3 * per + per_sg + c))
    vn_ref[...] = _rms(gv_ref[...], vg_ref[...]).astype(BF16)


def _odd_in(x, gain, w_in, q_gain, k_gain, v_gain):
    b, s, _ = x.shape
    n_in = w_in.shape[1]
    row = lambda bi, i: (bi, i, 0)
    by_head = lambda bi, i: (bi, 0, i, 0)
    const = lambda bi, i: (0, 0)
    return pl.pallas_call(
        _odd_in_kernel,
        grid=(b, s // PROJ_TM),
        in_specs=[
            pl.BlockSpec((None, PROJ_TM, D_MODEL), row),
            pl.BlockSpec((1, D_MODEL), const),
            pl.BlockSpec(memory_space=pl.ANY),
            pl.BlockSpec((1, HEAD_DIM), const),
            pl.BlockSpec((1, HEAD_DIM), const),
            pl.BlockSpec((1, SG_WIDTH), const),
        ],
        out_specs=[
            pl.BlockSpec((None, NA_HEADS, PROJ_TM, HEAD_DIM), by_head),
            pl.BlockSpec((None, NA_HEADS, PROJ_TM, HEAD_DIM), by_head),
            pl.BlockSpec((None, NA_HEADS, PROJ_TM, HEAD_DIM), by_head),
            pl.BlockSpec((None, PROJ_TM, SG_WIDTH), row),
            pl.BlockSpec((None, PROJ_TM, SG_WIDTH), row),
        ],
        out_shape=[
            jax.ShapeDtypeStruct((b, NA_HEADS, s, HEAD_DIM), BF16),
            jax.ShapeDtypeStruct((b, NA_HEADS, s, HEAD_DIM), BF16),
            jax.ShapeDtypeStruct((b, NA_HEADS, s, HEAD_DIM), BF16),
            jax.ShapeDtypeStruct((b, s, SG_WIDTH), F32),
            jax.ShapeDtypeStruct((b, s, SG_WIDTH), BF16),
        ],
        scratch_shapes=[pltpu.VMEM((PROJ_TM, D_MODEL), BF16), pltpu.VMEM((PROJ_TM, SG_WIDTH), F32)]
        + _weight_scratch(D_MODEL, n_in),
        compiler_params=_params("arbitrary", "arbitrary"),
        name="odd_in",
    )(x, gain.reshape(1, D_MODEL), w_in,
      (q_gain * HEAD_DIM ** -0.5).reshape(1, HEAD_DIM), k_gain.reshape(1, HEAD_DIM),
      v_gain.reshape(1, SG_WIDTH))


def _dilated_kernel(slopes_ref, q0_ref, q1_ref, q2_ref, k0_ref, k1_ref, k2_ref, v0_ref, v1_ref, v2_ref,
                    o_ref, on_ref, ls_ref, s_ref, *, seq):
    hh = pl.program_id(1)
    step = pl.program_id(2)
    q_refs = (q0_ref, q1_ref, q2_ref)
    k_refs = (k0_ref, k1_ref, k2_ref)
    v_refs = (v0_ref, v1_ref, v2_ref)
    for g, (window, dil) in enumerate(DIL_PAIRS):
        assert window // (2 * dil) == DIL_RADIUS
        q_ref, k_ref, v_ref = q_refs[g], k_refs[g], v_refs[g]
        sub_len = seq // dil
        per_res = DIL_TA // dil
        nq = min(per_res, DIL_TQ)
        n_sub = per_res // nq
        n_keys = min(nq + 2 * DIL_RADIUS, sub_len)
        slope = slopes_ref[g * DIL_HEADS_PER_GROUP + hh] * dil
        base = (lax.broadcasted_iota(jnp.int32, (nq, n_keys), 1)
                - lax.broadcasted_iota(jnp.int32, (nq, n_keys), 0))

        def place(t, nq=nq, n_sub=n_sub, per_res=per_res, sub_len=sub_len, n_keys=n_keys):
            r = t // n_sub
            q_off = pl.multiple_of((t % n_sub) * nq, nq)
            q_pos = step * per_res + q_off
            k_pos = pl.multiple_of(jnp.clip(q_pos - DIL_RADIUS, 0, sub_len - n_keys), DIL_RADIUS)
            return r, q_off, q_pos, k_pos

        def scores(t, slot, q_ref=q_ref, k_ref=k_ref, nq=nq, n_keys=n_keys, slope=slope, base=base,
                   place=place):
            r, q_off, q_pos, k_pos = place(t)
            q = q_ref[r, pl.ds(q_off, nq), :]
            k = k_ref[r, pl.ds(k_pos, n_keys), :]
            s = lax.dot_general(q, k, (((1,), (1,)), ((), ())), preferred_element_type=F32)
            dist = jnp.abs(base + (k_pos - q_pos))
            s_ref[slot, 0:nq, 0:n_keys] = jnp.where(dist <= DIL_RADIUS, s - slope * dist.astype(F32), NEG_INF)

        def finish(t, slot, v_ref=v_ref, g=g, dil=dil, nq=nq, n_keys=n_keys, place=place):
            r, q_off, _, k_pos = place(t)
            v = v_ref[r, pl.ds(k_pos, n_keys), :]
            s = s_ref[slot, 0:nq, 0:n_keys]
            m = jnp.max(s, axis=-1, keepdims=True)
            p = jnp.exp(s - m)
            den = jnp.sum(p, axis=-1, keepdims=True)
            out = jnp.dot(p.astype(BF16), v, preferred_element_type=F32) / den
            lse = jnp.broadcast_to(m + jnp.log(den), (nq, HEAD_DIM))
            if dil == 1:
                rows = pl.ds(q_off, nq)
            else:
                rows = pl.ds(r + dil * q_off, nq, stride=dil)
            on_ref[g, rows, :] = out
            ls_ref[g, rows, :] = lse

        n_tiles = dil * n_sub
        assert n_tiles % DIL_UNROLL == 0

        def body(i, carry, scores=scores, finish=finish):
            t0 = i * DIL_UNROLL
            scores(t0, 0)
            for u in range(DIL_UNROLL):
                if u + 1 < DIL_UNROLL:
                    scores(t0 + u + 1, (u + 1) % 2)
                finish(t0 + u, u % 2)
            return carry

        lax.fori_loop(0, n_tiles // DIL_UNROLL, body, 0)

    def merge(c, carry):
        rows = pl.ds(pl.multiple_of(c * DIL_TQ, DIL_TQ), DIL_TQ)
        lse = [ls_ref[g, rows, :] for g in range(len(DIL_PAIRS))]
        top = jnp.maximum(jnp.maximum(lse[0], lse[1]), lse[2])
        num = jnp.zeros((DIL_TQ, HEAD_DIM), F32)
        den = jnp.zeros((DIL_TQ, HEAD_DIM), F32)
        for g in range(len(DIL_PAIRS)):
            w = jnp.exp(lse[g] - top)
            num = num + w * on_ref[g, rows, :]
            den = den + w
        o_ref[rows, :] = (num / den).astype(o_ref.dtype)
        return carry

    lax.fori_loop(0, DIL_TA // DIL_TQ, merge, 0)


def _dilated(q, k, v, slopes):
    b, _, dil0, s, _ = q[0].shape
    assert dil0 == 1
    n_groups = len(DIL_PAIRS)

    def q_spec(dil):
        return pl.BlockSpec((None, None, dil, DIL_TA // dil, HEAD_DIM), lambda bi, hh, i: (bi, hh, 0, i, 0))

    def kv_spec(dil):
        return pl.BlockSpec((None, None, dil, s // dil, HEAD_DIM), lambda bi, hh, i: (bi, hh, 0, 0, 0))

    dils = [dil for _, dil in DIL_PAIRS]
    return pl.pallas_call(
        functools.partial(_dilated_kernel, seq=s),
        grid=(b, DIL_HEADS_PER_GROUP, s // DIL_TA),
        in_specs=[pl.BlockSpec(memory_space=pltpu.SMEM)]
        + [q_spec(d) for d in dils] + [kv_spec(d) for d in dils] + [kv_spec(d) for d in dils],
        out_specs=pl.BlockSpec((None, None, DIL_TA, HEAD_DIM), lambda bi, hh, i: (bi, hh, i, 0)),
        out_shape=jax.ShapeDtypeStruct((b, DIL_HEADS_PER_GROUP, s, HEAD_DIM), BF16),
        scratch_shapes=[pltpu.VMEM((n_groups, DIL_TA, HEAD_DIM), F32),
                        pltpu.VMEM((n_groups, DIL_TA, HEAD_DIM), F32),
                        pltpu.VMEM((2, DIL_TQ, DIL_TQ + 2 * DIL_RADIUS), F32)],
        compiler_params=_params("parallel", "parallel", "arbitrary"),
        name="dilated_attn",
    )(slopes, *q, *k, *v)


def _even_out_kernel(a_ref, prev_ref, next_ref, yb_ref, pw_ref, ps_ref, wo_hbm, x_ref, o_ref,
                     ext_ref, y_ref, wo_ref, wstage_ref, w_sems, *, seq):
    pl.when(_first_grid_step())(functools.partial(_load_weight_as_bf16, wo_hbm, wo_ref, wstage_ref, w_sems))
    i = pl.program_id(1)
    tm = OUT_TM
    ext_ref[0:POOL_HALO, :] = jnp.where(i == 0, 0.0, prev_ref[...])
    ext_ref[POOL_HALO:POOL_HALO + tm, :] = a_ref[...]
    ext_ref[POOL_HALO + tm:, :] = jnp.where(i == pl.num_programs(1) - 1, 0.0, next_ref[...])
    pos = i * tm + lax.broadcasted_iota(jnp.int32, (tm, 1), 0)
    for g, window in enumerate(POOL_WINDOWS):
        half = window // 2
        cols = slice(g * HEAD_DIM, (g + 1) * HEAD_DIM)
        total = ext_ref[POOL_HALO - half:POOL_HALO - half + tm, cols]
        for shift in range(-half + 1, half):
            total = total + ext_ref[POOL_HALO + shift:POOL_HALO + shift + tm, cols]
        count = (jnp.minimum(pos + half, seq) - jnp.maximum(pos - half, 0)).astype(F32)
        pooled = total / count - a_ref[:, cols]
        ya = jnp.dot(pooled.astype(BF16), pw_ref[g], preferred_element_type=F32) * ps_ref[:, cols]
        y_ref[:, cols] = ya.astype(BF16)
    for hd in range(DIL_HEADS_PER_GROUP):
        y_ref[:, POOL_WIDTH + hd * HEAD_DIM:POOL_WIDTH + (hd + 1) * HEAD_DIM] = yb_ref[hd]
    o_ref[...] = x_ref[...] + jnp.dot(y_ref[...], wo_ref[...], preferred_element_type=F32)


def _even_out(a, yb, pool_w, pool_scale, w_out, x):
    b, s, _ = x.shape
    tm = OUT_TM
    halo_blocks = tm // POOL_HALO
    tile = lambda bi, i: (bi, i, 0)
    return pl.pallas_call(
        functools.partial(_even_out_kernel, seq=s),
        grid=(b, s // tm),
        in_specs=[
            pl.BlockSpec((None, tm, POOL_WIDTH), tile),
            pl.BlockSpec((None, POOL_HALO, POOL_WIDTH),
                         lambda bi, i: (bi, jnp.maximum(i * halo_blocks - 1, 0), 0)),
            pl.BlockSpec((None, POOL_HALO, POOL_WIDTH),
                         lambda bi, i: (bi, jnp.minimum((i + 1) * halo_blocks, s // POOL_HALO - 1), 0)),
            pl.BlockSpec((None, DIL_HEADS_PER_GROUP, tm, HEAD_DIM), lambda bi, i: (bi, 0, i, 0)),
            _resident((len(POOL_WINDOWS), HEAD_DIM, HEAD_DIM), lambda bi, i: (0, 0, 0)),
            pl.BlockSpec((1, POOL_WIDTH), lambda bi, i: (0, 0)),
            pl.BlockSpec(memory_space=pl.ANY),
            pl.BlockSpec((None, tm, D_MODEL), tile),
        ],
        out_specs=pl.BlockSpec((None, tm, D_MODEL), tile),
        out_shape=jax.ShapeDtypeStruct((b, s, D_MODEL), F32),
        scratch_shapes=[pltpu.VMEM((tm + 2 * POOL_HALO, POOL_WIDTH), F32),
                        pltpu.VMEM((tm, POOL_WIDTH + DIL_OUT), BF16)]
        + _weight_scratch(POOL_WIDTH + DIL_OUT, D_MODEL),
        compiler_params=_params("arbitrary", "arbitrary"),
        name="even_out",
    )(a, a, a, yb, pool_w, pool_scale.reshape(1, POOL_WIDTH), w_out, x)


def _na_fill_bias(pair_ref, bias_ref, rb, rows):
    k_row0 = min(max(rb * NA_QROWS - NA_ROWS // 2, 0), rows - NA_KROWS)
    left_half = lax.broadcasted_iota(jnp.int32, (GRID_W, 2 * GRID_W), 1) < GRID_W
    for qr in range(NA_QROWS):
        r = rb * NA_QROWS + qr
        row_start = min(max(r - NA_ROWS // 2, 0), rows - NA_ROWS)
        for p in range(NA_KROWS // 2):
            kr = k_row0 + 2 * p
            ok_left = row_start <= kr < row_start + NA_ROWS
            ok_right = row_start <= kr + 1 < row_start + NA_ROWS
            if ok_left or ok_right:
                block = pair_ref[kr + 1 - r + NA_ROWS - 1]
                if not ok_right:
                    block = jnp.where(left_half, block, NEG_INF)
                if not ok_left:
                    block = jnp.where(left_half, NEG_INF, block)
            else:
                block = jnp.full((GRID_W, 2 * GRID_W), NEG_INF, F32)
            bias_ref[qr * GRID_W:(qr + 1) * GRID_W, p * 2 * GRID_W:(p + 1) * 2 * GRID_W] = block


def _na_kernel(q_ref, k_ref, v_ref, pair_ref, o_ref, bias_ref, s_ref, *, rows):
    n_blocks = rows // NA_QROWS
    tq = NA_QROWS * GRID_W
    n_keys = NA_KROWS * GRID_W
    kinds = (0, 1, n_blocks - 1)
    for slot, kind in enumerate(kinds):
        _na_fill_bias(pair_ref, bias_ref.at[slot], kind, rows)

    def key_start(rb):
        return min(max(rb * NA_QROWS - NA_ROWS // 2, 0), rows - NA_KROWS) * GRID_W

    def scores(rb):
        kind_slot = 0 if rb == 0 else (2 if rb == n_blocks - 1 else 1)
        q = q_ref[rb * tq:(rb + 1) * tq, :]
        k = k_ref[key_start(rb):key_start(rb) + n_keys, :]
        s_ref[rb % 2] = (lax.dot_general(q, k, (((1,), (1,)), ((), ())), preferred_element_type=F32)
                         + bias_ref[kind_slot])

    def finish(rb):
        v = v_ref[key_start(rb):key_start(rb) + n_keys, :]
        s = s_ref[rb % 2]
        m = jnp.max(s, axis=-1, keepdims=True)
        p = jnp.exp(s - m)
        den = jnp.sum(p, axis=-1, keepdims=True)
        out = jnp.dot(p.astype(BF16), v, preferred_element_type=F32) / den
        o_ref[rb * tq:(rb + 1) * tq, :] = out.astype(o_ref.dtype)

    scores(0)
    for rb in range(n_blocks):
        if rb + 1 < n_blocks:
            scores(rb + 1)
        finish(rb)


def _na_pair_tables(rpb):
    n_heads, n_rel_rows, n_rel_cols = rpb.shape
    c = np.arange(GRID_W)
    col_start = np.clip(c - NA_COLS // 2, 0, GRID_W - NA_COLS)
    col_ok = (c[None, :] >= col_start[:, None]) & (c[None, :] < col_start[:, None] + NA_COLS)
    rel_c = np.clip(c[None, :] - c[:, None], -(NA_COLS - 1), NA_COLS - 1) + (NA_COLS - 1)
    pick = (rel_c.reshape(1, -1) == np.arange(n_rel_cols)[:, None]).astype(np.float32)
    by_col = jnp.dot(rpb.astype(F32).reshape(n_heads * n_rel_rows, n_rel_cols), pick,
                     precision=lax.Precision.HIGHEST).reshape(n_heads, n_rel_rows, GRID_W, GRID_W)
    by_col = jnp.where(col_ok[None, None], by_col, NEG_INF)
    masked = jnp.full((n_heads, 1, GRID_W, GRID_W), NEG_INF, F32)
    padded = jnp.concatenate([masked, by_col, masked], axis=1)
    return jnp.concatenate([padded[:, :-1], padded[:, 1:]], axis=-1)


def _neighbourhood(q, k, v, pair_tables):
    b, _, s, _ = q.shape
    rows = s // GRID_W
    n_blocks = rows // NA_QROWS
    assert n_blocks >= 3 and rows >= NA_KROWS
    tq = NA_QROWS * GRID_W
    n_keys = NA_KROWS * GRID_W
    head = lambda bi, h: (bi, h, 0, 0)
    return pl.pallas_call(
        functools.partial(_na_kernel, rows=rows),
        grid=(b, NA_HEADS),
        in_specs=[
            pl.BlockSpec((None, None, s, HEAD_DIM), head),
            pl.BlockSpec((None, None, s, HEAD_DIM), head),
            pl.BlockSpec((None, None, s, HEAD_DIM), head),
            pl.BlockSpec((None, 2 * NA_ROWS, GRID_W, 2 * GRID_W), lambda bi, h: (h, 0, 0, 0)),
        ],
        out_specs=pl.BlockSpec((None, None, s, HEAD_DIM), head),
        out_shape=jax.ShapeDtypeStruct((b, NA_HEADS, s, HEAD_DIM), BF16),
        scratch_shapes=[pltpu.VMEM((3, tq, n_keys), F32), pltpu.VMEM((2, tq, n_keys), F32)],
        compiler_params=_params("parallel", "parallel"),
        name="neighbourhood_attn",
    )(q, k, v, pair_tables)


def _odd_out_kernel(yc_ref, u_ref, vn_ref, ws_ref, bs_ref, wo_hbm, x_ref, o_ref, y_ref,
                    wo_ref, wstage_ref, w_sems):
    pl.when(_first_grid_step())(functools.partial(_load_weight_as_bf16, wo_hbm, wo_ref, wstage_ref, w_sems))
    width = SG_WIDTH // SG_GROUPS
    for c in range(OUT_TM // SG_CHUNK):
        rows = slice(c * SG_CHUNK, (c + 1) * SG_CHUNK)
        for g in range(SG_GROUPS):
            cols = slice(g * width, (g + 1) * width)
            sv = jnp.dot(ws_ref[g], vn_ref[rows, cols], preferred_element_type=F32) + bs_ref[g]
            y_ref[rows, NA_WIDTH + g * width:NA_WIDTH + (g + 1) * width] = (u_ref[rows, cols] * sv).astype(BF16)
    for hd in range(NA_HEADS):
        y_ref[:, hd * HEAD_DIM:(hd + 1) * HEAD_DIM] = yc_ref[hd]
    o_ref[...] = x_ref[...] + jnp.dot(y_ref[...], wo_ref[...], preferred_element_type=F32)


def _odd_out(yc, u, vn, w_s, b_s, w_out, x):
    b, s, _ = x.shape
    tm = OUT_TM
    width = SG_WIDTH // SG_GROUPS
    tile = lambda bi, i: (bi, i, 0)
    return pl.pallas_call(
        _odd_out_kernel,
        grid=(b, s // tm),
        in_specs=[
            pl.BlockSpec((None, NA_HEADS, tm, HEAD_DIM), lambda bi, i: (bi, 0, i, 0)),
            pl.BlockSpec((None, tm, SG_WIDTH), tile),
            pl.BlockSpec((None, tm, SG_WIDTH), tile),
            _resident((SG_GROUPS, SG_CHUNK, SG_CHUNK), lambda bi, i: (0, 0, 0)),
            _resident((SG_GROUPS, SG_CHUNK, width), lambda bi, i: (0, 0, 0)),
            pl.BlockSpec(memory_space=pl.ANY),
            pl.BlockSpec((None, tm, D_MODEL), tile),
        ],
        out_specs=pl.BlockSpec((None, tm, D_MODEL), tile),
        out_shape=jax.ShapeDtypeStruct((b, s, D_MODEL), F32),
        scratch_shapes=[pltpu.VMEM((tm, NA_WIDTH + SG_WIDTH), BF16)]
        + _weight_scratch(NA_WIDTH + SG_WIDTH, D_MODEL),
        compiler_params=_params("arbitrary", "arbitrary"),
        name="odd_out",
    )(yc, u, vn, w_s, b_s, w_out, x)


def kernel(x, norm_ffn1, norm_mix, norm_ffn2, norm_out, ffn_w_gate, ffn_w_up, ffn_w_down, even_w_in, pool_w, pool_scale, dil_q_gain, dil_k_gain, even_w_out, odd_w_in, na_q_gain, na_k_gain, na_rpb, sg_v_gain, sg_w, sg_b, odd_w_out):
    depth = norm_ffn1.shape[0]
    slopes = jnp.asarray(2.0 ** (-8.0 * np.arange(1, DIL_HEADS + 1) / DIL_HEADS), dtype=F32)
    width = SG_WIDTH // SG_GROUPS

    for layer in range(depth):
        x = _ffn(x, norm_ffn1[layer], ffn_w_gate, ffn_w_up, ffn_w_down, layer, 0)
        if layer % 2 == 0:
            e = layer // 2
            a, q, k, v = _even_in(x, norm_mix[layer], even_w_in[e], dil_q_gain[e], dil_k_gain[e])
            yb = _dilated(q, k, v, slopes)
            x = _even_out(a, yb, pool_w[e].astype(BF16), pool_scale[e], even_w_out[e], x)
        else:
            o = layer // 2
            q, k, v, u, vn = _odd_in(x, norm_mix[layer], odd_w_in[o],
                                     na_q_gain[o], na_k_gain[o], sg_v_gain[o])
            yc = _neighbourhood(q, k, v, _na_pair_tables(na_rpb[o]))
            b_s = jnp.broadcast_to(sg_b[o][:, :, None], (SG_GROUPS, SG_CHUNK, width))
            x = _odd_out(yc, u, vn, sg_w[o].astype(BF16), b_s, odd_w_out[o], x)
        x = _ffn(x, norm_ffn2[layer], ffn_w_gate, ffn_w_up, ffn_w_down, layer, 1, gain_out=norm_out[layer])
    return x
```

```python
import functools

import numpy as np
import jax
import jax.numpy as jnp
from jax import lax
from jax.experimental import pallas as pl
from jax.experimental.pallas import tpu as pltpu

F32 = jnp.float32
BF16 = jnp.bfloat16

D_MODEL = 2048
D_FF = 5632
HEAD_DIM = 128
POOL_WINDOWS = (2, 4, 8, 16)
POOL_WIDTH = 512
POOL_HALO = 8
DIL_PAIRS = ((128, 1), (512, 4), (2048, 16))
DIL_HEADS_PER_GROUP = 4
DIL_HEADS = 12
DIL_WIDTH = DIL_HEADS * HEAD_DIM
DIL_OUT = DIL_HEADS_PER_GROUP * HEAD_DIM
GRID_W = 64
NA_ROWS = 8
NA_COLS = 16
NA_HEADS = 8
NA_WIDTH = NA_HEADS * HEAD_DIM
NA_QROWS = 4
NA_KROWS = NA_QROWS + NA_ROWS
SG_CHUNK = 128
SG_GROUPS = 8
SG_WIDTH = 1024
RMS_EPS = 1e-6
NEG_INF = -1e30

VMEM_LIMIT_BYTES = 60 * 1024 * 1024

FFN_TM = 1024
FFN_TF = 512
FFN_TC = 256
PROJ_TM = 512
PROJ_TN = 512
DIL_TQ = 128
DIL_TA = 4096
DIL_RADIUS = 64
DIL_UNROLL = 8
OUT_TM = 512
W_CHUNK = 512


def _params(*semantics):
    return pltpu.CompilerParams(dimension_semantics=semantics, vmem_limit_bytes=VMEM_LIMIT_BYTES)


def _resident(block_shape, index_map):
    return pl.BlockSpec(block_shape, index_map, pipeline_mode=pl.Buffered(1))


def _rms(x, gain):
    return x * lax.rsqrt(jnp.mean(x * x, axis=-1, keepdims=True) + RMS_EPS) * gain


def _first_grid_step():
    return (pl.program_id(0) == 0) & (pl.program_id(1) == 0)


def _load_weight_as_bf16(w_hbm, wbf_ref, stage_ref, sems):
    n_chunks = wbf_ref.shape[1] // W_CHUNK

    def copy(c):
        cols = pl.ds(c * W_CHUNK, W_CHUNK)
        return pltpu.make_async_copy(w_hbm.at[:, cols], stage_ref.at[c % 2], sems.at[c % 2])

    copy(0).start()
    for c in range(n_chunks):
        if c + 1 < n_chunks:
            copy(c + 1).start()
        copy(c).wait()
        wbf_ref[:, c * W_CHUNK:(c + 1) * W_CHUNK] = stage_ref[c % 2].astype(BF16)


def _weight_scratch(rows, cols):
    assert cols % W_CHUNK == 0
    return [pltpu.VMEM((rows, cols), BF16), pltpu.VMEM((2, rows, W_CHUNK), F32), pltpu.SemaphoreType.DMA((2,))]


def _ffn_kernel(x_hbm, g_ref, wg_ref, wu_ref, wd_ref, gout_ref, o_ref, h_ref, xbuf_ref, x_sem, *, final_norm):
    j = pl.program_id(2)
    tiles_per_batch = pl.num_programs(1)
    n_tiles = pl.num_programs(0) * tiles_per_batch
    tile = pl.program_id(0) * tiles_per_batch + pl.program_id(1)

    def x_copy(t):
        rows = pl.ds(pl.multiple_of((t % tiles_per_batch) * FFN_TM, FFN_TM), FFN_TM)
        return pltpu.make_async_copy(x_hbm.at[t // tiles_per_batch, rows, :], xbuf_ref, x_sem)

    @pl.when(j == 0)
    def _():
        @pl.when(tile == 0)
        def _():
            x_copy(tile).start()

        x_copy(tile).wait()
        x = xbuf_ref[...]
        h_ref[...] = _rms(x, g_ref[...]).astype(BF16)
        o_ref[...] = x

    @pl.when((j == 1) & (tile + 1 < n_tiles))
    def _():
        x_copy(tile + 1).start()

    h = h_ref[...]
    for c in range(FFN_TF // FFN_TC):
        cols = slice(c * FFN_TC, (c + 1) * FFN_TC)
        gate = jnp.dot(h, wg_ref[:, cols].astype(BF16), preferred_element_type=F32)
        up = jnp.dot(h, wu_ref[:, cols].astype(BF16), preferred_element_type=F32)
        act = (gate * jax.nn.sigmoid(gate)) * up * 0.5
        o_ref[...] += jnp.dot(act.astype(BF16), wd_ref[cols, :].astype(BF16), preferred_element_type=F32)

    if final_norm:
        @pl.when(j == pl.num_programs(2) - 1)
        def _():
            o_ref[...] = _rms(o_ref[...], gout_ref[...])


def _ffn(x, gain, w_gate, w_up, w_down, layer, idx, gain_out=None):
    b, s, _ = x.shape
    assert D_FF // FFN_TF >= 2
    final_norm = gain_out is not None
    if gain_out is None:
        gain_out = gain
    tile = lambda bi, i, j: (bi, i, 0)
    const = lambda bi, i, j: (0, 0)
    return pl.pallas_call(
        functools.partial(_ffn_kernel, final_norm=final_norm),
        grid=(b, s // FFN_TM, D_FF // FFN_TF),
        in_specs=[
            pl.BlockSpec(memory_space=pl.ANY),
            pl.BlockSpec((1, D_MODEL), const),
            pl.BlockSpec((None, None, D_MODEL, FFN_TF), lambda bi, i, j: (layer, idx, 0, j)),
            pl.BlockSpec((None, None, D_MODEL, FFN_TF), lambda bi, i, j: (layer, idx, 0, j)),
            pl.BlockSpec((None, None, FFN_TF, D_MODEL), lambda bi, i, j: (layer, idx, j, 0)),
            pl.BlockSpec((1, D_MODEL), const),
        ],
        out_specs=pl.BlockSpec((None, FFN_TM, D_MODEL), tile),
        out_shape=jax.ShapeDtypeStruct((b, s, D_MODEL), F32),
        scratch_shapes=[pltpu.VMEM((FFN_TM, D_MODEL), BF16), pltpu.VMEM((FFN_TM, D_MODEL), F32),
                        pltpu.SemaphoreType.DMA(())],
        compiler_params=_params("arbitrary", "arbitrary", "arbitrary"),
        name="ffn_final" if final_norm else "ffn",
    )(x, gain.reshape(1, D_MODEL), w_gate, w_up, w_down, gain_out.reshape(1, D_MODEL))


def _head_store(z, out_ref, head0, gain=None):
    for hd in range(PROJ_TN // HEAD_DIM):
        zh = z[:, hd * HEAD_DIM:(hd + 1) * HEAD_DIM]
        if gain is not None:
            zh = _rms(zh, gain)
        out_ref[head0 + hd] = zh.astype(out_ref.dtype)


def _store_by_residue(z, out_ref, stage_ref, dil, gain=None):
    tm = z.shape[0]
    for hd in range(z.shape[1] // HEAD_DIM):
        cols = slice(hd * HEAD_DIM, (hd + 1) * HEAD_DIM)
        zh = z[:, cols]
        if gain is not None:
            zh = _rms(zh, gain)
        if dil == 1:
            out_ref[hd, 0] = zh.astype(BF16)
        else:
            stage_ref[hd] = zh
            for r in range(dil):
                out_ref[hd, r] = stage_ref[hd, pl.ds(r, tm // dil, stride=dil), :].astype(BF16)


def _even_in_kernel(x_ref, g_ref, w_hbm, qg_ref, kg_ref, a_ref, *rest):
    n_groups = len(DIL_PAIRS)
    q_refs, k_refs, v_refs = rest[:n_groups], rest[n_groups:2 * n_groups], rest[2 * n_groups:3 * n_groups]
    h_ref, stage_ref, w_ref, wstage_ref, w_sems = rest[3 * n_groups:]
    pl.when(_first_grid_step())(functools.partial(_load_weight_as_bf16, w_hbm, w_ref, wstage_ref, w_sems))
    h_ref[...] = _rms(x_ref[...], g_ref[...]).astype(BF16)

    def chunk(c):
        return jnp.dot(h_ref[...], w_ref[:, c * PROJ_TN:(c + 1) * PROJ_TN], preferred_element_type=F32)

    a_ref[...] = chunk(0)
    for g, (_, dil) in enumerate(DIL_PAIRS):
        _store_by_residue(chunk(1 + g), q_refs[g], stage_ref, dil, qg_ref[...])
        _store_by_residue(chunk(1 + n_groups + g), k_refs[g], stage_ref, dil, kg_ref[...])
        _store_by_residue(chunk(1 + 2 * n_groups + g), v_refs[g], stage_ref, dil)


def _even_in(x, gain, w_in, q_gain, k_gain):
    b, s, _ = x.shape
    n_in = w_in.shape[1]
    assert PROJ_TN == DIL_OUT
    row = lambda bi, i: (bi, i, 0)
    const = lambda bi, i: (0, 0)
    hpg = DIL_HEADS_PER_GROUP
    group_specs = [pl.BlockSpec((None, hpg, dil, PROJ_TM // dil, HEAD_DIM), lambda bi, i: (bi, 0, 0, i, 0))
                   for _, dil in DIL_PAIRS]
    group_shapes = [jax.ShapeDtypeStruct((b, hpg, dil, s // dil, HEAD_DIM), BF16) for _, dil in DIL_PAIRS]
    outs = pl.pallas_call(
        _even_in_kernel,
        grid=(b, s // PROJ_TM),
        in_specs=[
            pl.BlockSpec((None, PROJ_TM, D_MODEL), row),
            pl.BlockSpec((1, D_MODEL), const),
            pl.BlockSpec(memory_space=pl.ANY),
            pl.BlockSpec((1, HEAD_DIM), const),
            pl.BlockSpec((1, HEAD_DIM), const),
        ],
        out_specs=[pl.BlockSpec((None, PROJ_TM, POOL_WIDTH), row)] + group_specs * 3,
        out_shape=[jax.ShapeDtypeStruct((b, s, POOL_WIDTH), F32)] + group_shapes * 3,
        scratch_shapes=[pltpu.VMEM((PROJ_TM, D_MODEL), BF16),
                        pltpu.VMEM((DIL_HEADS_PER_GROUP, PROJ_TM, HEAD_DIM), F32)]
        + _weight_scratch(D_MODEL, n_in),
        compiler_params=_params("arbitrary", "arbitrary"),
        name="even_in",
    )(x, gain.reshape(1, D_MODEL), w_in,
      (q_gain * HEAD_DIM ** -0.5).reshape(1, HEAD_DIM), k_gain.reshape(1, HEAD_DIM))
    n_groups = len(DIL_PAIRS)
    return outs[0], outs[1:1 + n_groups], outs[1 + n_groups:1 + 2 * n_groups], outs[1 + 2 * n_groups:]


def _odd_in_kernel(x_ref, g_ref, w_hbm, qg_ref, kg_ref, vg_ref, q_ref, k_ref, v_ref, u_ref, vn_ref,
                   h_ref, gv_ref, w_ref, wstage_ref, w_sems):
    pl.when(_first_grid_step())(functools.partial(_load_weight_as_bf16, w_hbm, w_ref, wstage_ref, w_sems))
    h_ref[...] = _rms(x_ref[...], g_ref[...]).astype(BF16)

    def chunk(c):
        return jnp.dot(h_ref[...], w_ref[:, c * PROJ_TN:(c + 1) * PROJ_TN], preferred_element_type=F32)

    per = NA_WIDTH // PROJ_TN
    heads_per_chunk = PROJ_TN // HEAD_DIM
    for c in range(per):
        _head_store(chunk(c), q_ref, c * heads_per_chunk, qg_ref[...])
        _head_store(chunk(per + c), k_ref, c * heads_per_chunk, kg_ref[...])
        _head_store(chunk(2 * per + c), v_ref, c * heads_per_chunk)
    per_sg = SG_WIDTH // PROJ_TN
    for c in range(per_sg):
        u_ref[:, c * PROJ_TN:(c + 1) * PROJ_TN] = jax.nn.gelu(chunk(3 * per + c))
        gv_ref[:, c * PROJ_TN:(c + 1) * PROJ_TN] = jax.nn.gelu(chunk(3 * per + per_sg + c))
    vn_ref[...] = _rms(gv_ref[...], vg_ref[...]).astype(BF16)


def _odd_in(x, gain, w_in, q_gain, k_gain, v_gain):
    b, s, _ = x.shape
    n_in = w_in.shape[1]
    row = lambda bi, i: (bi, i, 0)
    by_head = lambda bi, i: (bi, 0, i, 0)
    const = lambda bi, i: (0, 0)
    return pl.pallas_call(
        _odd_in_kernel,
        grid=(b, s // PROJ_TM),
        in_specs=[
            pl.BlockSpec((None, PROJ_TM, D_MODEL), row),
            pl.BlockSpec((1, D_MODEL), const),
            pl.BlockSpec(memory_space=pl.ANY),
            pl.BlockSpec((1, HEAD_DIM), const),
            pl.BlockSpec((1, HEAD_DIM), const),
            pl.BlockSpec((1, SG_WIDTH), const),
        ],
        out_specs=[
            pl.BlockSpec((None, NA_HEADS, PROJ_TM, HEAD_DIM), by_head),
            pl.BlockSpec((None, NA_HEADS, PROJ_TM, HEAD_DIM), by_head),
            pl.BlockSpec((None, NA_HEADS, PROJ_TM, HEAD_DIM), by_head),
            pl.BlockSpec((None, PROJ_TM, SG_WIDTH), row),
            pl.BlockSpec((None, PROJ_TM, SG_WIDTH), row),
        ],
        out_shape=[
            jax.ShapeDtypeStruct((b, NA_HEADS, s, HEAD_DIM), BF16),
            jax.ShapeDtypeStruct((b, NA_HEADS, s, HEAD_DIM), BF16),
            jax.ShapeDtypeStruct((b, NA_HEADS, s, HEAD_DIM), BF16),
            jax.ShapeDtypeStruct((b, s, SG_WIDTH), F32),
            jax.ShapeDtypeStruct((b, s, SG_WIDTH), BF16),
        ],
        scratch_shapes=[pltpu.VMEM((PROJ_TM, D_MODEL), BF16), pltpu.VMEM((PROJ_TM, SG_WIDTH), F32)]
        + _weight_scratch(D_MODEL, n_in),
        compiler_params=_params("arbitrary", "arbitrary"),
        name="odd_in",
    )(x, gain.reshape(1, D_MODEL), w_in,
      (q_gain * HEAD_DIM ** -0.5).reshape(1, HEAD_DIM), k_gain.reshape(1, HEAD_DIM),
      v_gain.reshape(1, SG_WIDTH))


def _dilated_kernel(slopes_ref, q0_ref, q1_ref, q2_ref, k0_ref, k1_ref, k2_ref, v0_ref, v1_ref, v2_ref,
                    o_ref, on_ref, ls_ref, s_ref, *, seq):
    hh = pl.program_id(1)
    step = pl.program_id(2)
    q_refs = (q0_ref, q1_ref, q2_ref)
    k_refs = (k0_ref, k1_ref, k2_ref)
    v_refs = (v0_ref, v1_ref, v2_ref)
    for g, (window, dil) in enumerate(DIL_PAIRS):
        assert window // (2 * dil) == DIL_RADIUS
        q_ref, k_ref, v_ref = q_refs[g], k_refs[g], v_refs[g]
        sub_len = seq // dil
        per_res = DIL_TA // dil
        nq = min(per_res, DIL_TQ)
        n_sub = per_res // nq
        n_keys = min(nq + 2 * DIL_RADIUS, sub_len)
        slope = slopes_ref[g * DIL_HEADS_PER_GROUP + hh] * dil
        base = (lax.broadcasted_iota(jnp.int32, (nq, n_keys), 1)
                - lax.broadcasted_iota(jnp.int32, (nq, n_keys), 0))

        def place(t, nq=nq, n_sub=n_sub, per_res=per_res, sub_len=sub_len, n_keys=n_keys):
            r = t // n_sub
            q_off = pl.multiple_of((t % n_sub) * nq, nq)
            q_pos = step * per_res + q_off
            k_pos = pl.multiple_of(jnp.clip(q_pos - DIL_RADIUS, 0, sub_len - n_keys), DIL_RADIUS)
            return r, q_off, q_pos, k_pos

        def scores(t, slot, q_ref=q_ref, k_ref=k_ref, nq=nq, n_keys=n_keys, slope=slope, base=base,
                   place=place):
            r, q_off, q_pos, k_pos = place(t)
            q = q_ref[r, pl.ds(q_off, nq), :]
            k = k_ref[r, pl.ds(k_pos, n_keys), :]
            s = lax.dot_general(q, k, (((1,), (1,)), ((), ())), preferred_element_type=F32)
            dist = jnp.abs(base + (k_pos - q_pos))
            s_ref[slot, 0:nq, 0:n_keys] = jnp.where(dist <= DIL_RADIUS, s - slope * dist.astype(F32), NEG_INF)

        def finish(t, slot, v_ref=v_ref, g=g, dil=dil, nq=nq, n_keys=n_keys, place=place):
            r, q_off, _, k_pos = place(t)
            v = v_ref[r, pl.ds(k_pos, n_keys), :]
            s = s_ref[slot, 0:nq, 0:n_keys]
            m = jnp.max(s, axis=-1, keepdims=True)
            p = jnp.exp(s - m)
            den = jnp.sum(p, axis=-1, keepdims=True)
            out = jnp.dot(p.astype(BF16), v, preferred_element_type=F32) / den
            lse = jnp.broadcast_to(m + jnp.log(den), (nq, HEAD_DIM))
            if dil == 1:
                rows = pl.ds(q_off, nq)
            else:
                rows = pl.ds(r + dil * q_off, nq, stride=dil)
            on_ref[g, rows, :] = out
            ls_ref[g, rows, :] = lse

        n_tiles = dil * n_sub
        assert n_tiles % DIL_UNROLL == 0

        def body(i, carry, scores=scores, finish=finish):
            t0 = i * DIL_UNROLL
            scores(t0, 0)
            for u in range(DIL_UNROLL):
                if u + 1 < DIL_UNROLL:
                    scores(t0 + u + 1, (u + 1) % 2)
                finish(t0 + u, u % 2)
            return carry

        lax.fori_loop(0, n_tiles // DIL_UNROLL, body, 0)

    def merge(c, carry):
        rows = pl.ds(pl.multiple_of(c * DIL_TQ, DIL_TQ), DIL_TQ)
        lse = [ls_ref[g, rows, :] for g in range(len(DIL_PAIRS))]
        top = jnp.maximum(jnp.maximum(lse[0], lse[1]), lse[2])
        num = jnp.zeros((DIL_TQ, HEAD_DIM), F32)
        den = jnp.zeros((DIL_TQ, HEAD_DIM), F32)
        for g in range(len(DIL_PAIRS)):
            w = jnp.exp(lse[g] - top)
            num = num + w * on_ref[g, rows, :]
            den = den + w
        o_ref[rows, :] = (num / den).astype(o_ref.dtype)
        return carry

    lax.fori_loop(0, DIL_TA // DIL_TQ, merge, 0)


def _dilated(q, k, v, slopes):
    b, _, dil0, s, _ = q[0].shape
    assert dil0 == 1
    n_groups = len(DIL_PAIRS)

    def q_spec(dil):
        return pl.BlockSpec((None, None, dil, DIL_TA // dil, HEAD_DIM), lambda bi, hh, i: (bi, hh, 0, i, 0))

    def kv_spec(dil):
        return pl.BlockSpec((None, None, dil, s // dil, HEAD_DIM), lambda bi, hh, i: (bi, hh, 0, 0, 0))

    dils = [dil for _, dil in DIL_PAIRS]
    return pl.pallas_call(
        functools.partial(_dilated_kernel, seq=s),
        grid=(b, DIL_HEADS_PER_GROUP, s // DIL_TA),
        in_specs=[pl.BlockSpec(memory_space=pltpu.SMEM)]
        + [q_spec(d) for d in dils] + [kv_spec(d) for d in dils] + [kv_spec(d) for d in dils],
        out_specs=pl.BlockSpec((None, None, DIL_TA, HEAD_DIM), lambda bi, hh, i: (bi, hh, i, 0)),
        out_shape=jax.ShapeDtypeStruct((b, DIL_HEADS_PER_GROUP, s, HEAD_DIM), BF16),
        scratch_shapes=[pltpu.VMEM((n_groups, DIL_TA, HEAD_DIM), F32),
                        pltpu.VMEM((n_groups, DIL_TA, HEAD_DIM), F32),
                        pltpu.VMEM((2, DIL_TQ, DIL_TQ + 2 * DIL_RADIUS), F32)],
        compiler_params=_params("parallel", "parallel", "arbitrary"),
        name="dilated_attn",
    )(slopes, *q, *k, *v)


def _even_out_kernel(a_ref, prev_ref, next_ref, yb_ref, pw_ref, ps_ref, wo_hbm, x_ref, o_ref,
                     ext_ref, y_ref, wo_ref, wstage_ref, w_sems, *, seq):
    pl.when(_first_grid_step())(functools.partial(_load_weight_as_bf16, wo_hbm, wo_ref, wstage_ref, w_sems))
    i = pl.program_id(1)
    tm = OUT_TM
    ext_ref[0:POOL_HALO, :] = jnp.where(i == 0, 0.0, prev_ref[...])
    ext_ref[POOL_HALO:POOL_HALO + tm, :] = a_ref[...]
    ext_ref[POOL_HALO + tm:, :] = jnp.where(i == pl.num_programs(1) - 1, 0.0, next_ref[...])
    pos = i * tm + lax.broadcasted_iota(jnp.int32, (tm, 1), 0)
    for g, window in enumerate(POOL_WINDOWS):
        half = window // 2
        cols = slice(g * HEAD_DIM, (g + 1) * HEAD_DIM)
        total = ext_ref[POOL_HALO - half:POOL_HALO - half + tm, cols]
        for shift in range(-half + 1, half):
            total = total + ext_ref[POOL_HALO + shift:POOL_HALO + shift + tm, cols]
        count = (jnp.minimum(pos + half, seq) - jnp.maximum(pos - half, 0)).astype(F32)
        pooled = total / count - a_ref[:, cols]
        ya = jnp.dot(pooled.astype(BF16), pw_ref[g], preferred_element_type=F32) * ps_ref[:, cols]
        y_ref[:, cols] = ya.astype(BF16)
    for hd in range(DIL_HEADS_PER_GROUP):
        y_ref[:, POOL_WIDTH + hd * HEAD_DIM:POOL_WIDTH + (hd + 1) * HEAD_DIM] = yb_ref[hd]
    o_ref[...] = x_ref[...] + jnp.dot(y_ref[...], wo_ref[...], preferred_element_type=F32)


def _even_out(a, yb, pool_w, pool_scale, w_out, x):
    b, s, _ = x.shape
    tm = OUT_TM
    halo_blocks = tm // POOL_HALO
    tile = lambda bi, i: (bi, i, 0)
    return pl.pallas_call(
        functools.partial(_even_out_kernel, seq=s),
        grid=(b, s // tm),
        in_specs=[
            pl.BlockSpec((None, tm, POOL_WIDTH), tile),
            pl.BlockSpec((None, POOL_HALO, POOL_WIDTH),
                         lambda bi, i: (bi, jnp.maximum(i * halo_blocks - 1, 0), 0)),
            pl.BlockSpec((None, POOL_HALO, POOL_WIDTH),
                         lambda bi, i: (bi, jnp.minimum((i + 1) * halo_blocks, s // POOL_HALO - 1), 0)),
            pl.BlockSpec((None, DIL_HEADS_PER_GROUP, tm, HEAD_DIM), lambda bi, i: (bi, 0, i, 0)),
            _resident((len(POOL_WINDOWS), HEAD_DIM, HEAD_DIM), lambda bi, i: (0, 0, 0)),
            pl.BlockSpec((1, POOL_WIDTH), lambda bi, i: (0, 0)),
            pl.BlockSpec(memory_space=pl.ANY),
            pl.BlockSpec((None, tm, D_MODEL), tile),
        ],
        out_specs=pl.BlockSpec((None, tm, D_MODEL), tile),
        out_shape=jax.ShapeDtypeStruct((b, s, D_MODEL), F32),
        scratch_shapes=[pltpu.VMEM((tm + 2 * POOL_HALO, POOL_WIDTH), F32),
                        pltpu.VMEM((tm, POOL_WIDTH + DIL_OUT), BF16)]
        + _weight_scratch(POOL_WIDTH + DIL_OUT, D_MODEL),
        compiler_params=_params("arbitrary", "arbitrary"),
        name="even_out",
    )(a, a, a, yb, pool_w, pool_scale.reshape(1, POOL_WIDTH), w_out, x)


def _na_fill_bias(pair_ref, bias_ref, rb, rows):
    k_row0 = min(max(rb * NA_QROWS - NA_ROWS // 2, 0), rows - NA_KROWS)
    left_half = lax.broadcasted_iota(jnp.int32, (GRID_W, 2 * GRID_W), 1) < GRID_W
    for qr in range(NA_QROWS):
        r = rb * NA_QROWS + qr
        row_start = min(max(r - NA_ROWS // 2, 0), rows - NA_ROWS)
        for p in range(NA_KROWS // 2):
            kr = k_row0 + 2 * p
            ok_left = row_start <= kr < row_start + NA_ROWS
            ok_right = row_start <= kr + 1 < row_start + NA_ROWS
            if ok_left or ok_right:
                block = pair_ref[kr + 1 - r + NA_ROWS - 1]
                if not ok_right:
                    block = jnp.where(left_half, block, NEG_INF)
                if not ok_left:
                    block = jnp.where(left_half, NEG_INF, block)
            else:
                block = jnp.full((GRID_W, 2 * GRID_W), NEG_INF, F32)
            bias_ref[qr * GRID_W:(qr + 1) * GRID_W, p * 2 * GRID_W:(p + 1) * 2 * GRID_W] = block


def _na_kernel(q_ref, k_ref, v_ref, pair_ref, o_ref, bias_ref, s_ref, *, rows):
    n_blocks = rows // NA_QROWS
    tq = NA_QROWS * GRID_W
    n_keys = NA_KROWS * GRID_W
    kinds = (0, 1, n_blocks - 1)
    for slot, kind in enumerate(kinds):
        _na_fill_bias(pair_ref, bias_ref.at[slot], kind, rows)

    def key_start(rb):
        return min(max(rb * NA_QROWS - NA_ROWS // 2, 0), rows - NA_KROWS) * GRID_W

    def scores(rb):
        kind_slot = 0 if rb == 0 else (2 if rb == n_blocks - 1 else 1)
        q = q_ref[rb * tq:(rb + 1) * tq, :]
        k = k_ref[key_start(rb):key_start(rb) + n_keys, :]
        s_ref[rb % 2] = (lax.dot_general(q, k, (((1,), (1,)), ((), ())), preferred_element_type=F32)
                         + bias_ref[kind_slot])

    def finish(rb):
        v = v_ref[key_start(rb):key_start(rb) + n_keys, :]
        s = s_ref[rb % 2]
        m = jnp.max(s, axis=-1, keepdims=True)
        p = jnp.exp(s - m)
        den = jnp.sum(p, axis=-1, keepdims=True)
        out = jnp.dot(p.astype(BF16), v, preferred_element_type=F32) / den
        o_ref[rb * tq:(rb + 1) * tq, :] = out.astype(o_ref.dtype)

    scores(0)
    for rb in range(n_blocks):
        if rb + 1 < n_blocks:
            scores(rb + 1)
        finish(rb)


def _na_pair_tables(rpb):
    n_heads, n_rel_rows, n_rel_cols = rpb.shape
    c = np.arange(GRID_W)
    col_start = np.clip(c - NA_COLS // 2, 0, GRID_W - NA_COLS)
    col_ok = (c[None, :] >= col_start[:, None]) & (c[None, :] < col_start[:, None] + NA_COLS)
    rel_c = np.clip(c[None, :] - c[:, None], -(NA_COLS - 1), NA_COLS - 1) + (NA_COLS - 1)
    pick = (rel_c.reshape(1, -1) == np.arange(n_rel_cols)[:, None]).astype(np.float32)
    by_col = jnp.dot(rpb.astype(F32).reshape(n_heads * n_rel_rows, n_rel_cols), pick,
                     precision=lax.Precision.HIGHEST).reshape(n_heads, n_rel_rows, GRID_W, GRID_W)
    by_col = jnp.where(col_ok[None, None], by_col, NEG_INF)
    masked = jnp.full((n_heads, 1, GRID_W, GRID_W), NEG_INF, F32)
    padded = jnp.concatenate([masked, by_col, masked], axis=1)
    return jnp.concatenate([padded[:, :-1], padded[:, 1:]], axis=-1)


def _neighbourhood(q, k, v, pair_tables):
    b, _, s, _ = q.shape
    rows = s // GRID_W
    n_blocks = rows // NA_QROWS
    assert n_blocks >= 3 and rows >= NA_KROWS
    tq = NA_QROWS * GRID_W
    n_keys = NA_KROWS * GRID_W
    head = lambda bi, h: (bi, h, 0, 0)
    return pl.pallas_call(
        functools.partial(_na_kernel, rows=rows),
        grid=(b, NA_HEADS),
        in_specs=[
            pl.BlockSpec((None, None, s, HEAD_DIM), head),
            pl.BlockSpec((None, None, s, HEAD_DIM), head),
            pl.BlockSpec((None, None, s, HEAD_DIM), head),
            pl.BlockSpec((None, 2 * NA_ROWS, GRID_W, 2 * GRID_W), lambda bi, h: (h, 0, 0, 0)),
        ],
        out_specs=pl.BlockSpec((None, None, s, HEAD_DIM), head),
        out_shape=jax.ShapeDtypeStruct((b, NA_HEADS, s, HEAD_DIM), BF16),
        scratch_shapes=[pltpu.VMEM((3, tq, n_keys), F32), pltpu.VMEM((2, tq, n_keys), F32)],
        compiler_params=_params("parallel", "parallel"),
        name="neighbourhood_attn",
    )(q, k, v, pair_tables)


def _odd_out_kernel(yc_ref, u_ref, vn_ref, ws_ref, bs_ref, wo_hbm, x_ref, o_ref, y_ref,
                    wo_ref, wstage_ref, w_sems):
    pl.when(_first_grid_step())(functools.partial(_load_weight_as_bf16, wo_hbm, wo_ref, wstage_ref, w_sems))
    width = SG_WIDTH // SG_GROUPS
    for c in range(OUT_TM // SG_CHUNK):
        rows = slice(c * SG_CHUNK, (c + 1) * SG_CHUNK)
        for g in range(SG_GROUPS):
            cols = slice(g * width, (g + 1) * width)
            sv = jnp.dot(ws_ref[g], vn_ref[rows, cols], preferred_element_type=F32) + bs_ref[g]
            y_ref[rows, NA_WIDTH + g * width:NA_WIDTH + (g + 1) * width] = (u_ref[rows, cols] * sv).astype(BF16)
    for hd in range(NA_HEADS):
        y_ref[:, hd * HEAD_DIM:(hd + 1) * HEAD_DIM] = yc_ref[hd]
    o_ref[...] = x_ref[...] + jnp.dot(y_ref[...], wo_ref[...], preferred_element_type=F32)


def _odd_out(yc, u, vn, w_s, b_s, w_out, x):
    b, s, _ = x.shape
    tm = OUT_TM
    width = SG_WIDTH // SG_GROUPS
    tile = lambda bi, i: (bi, i, 0)
    return pl.pallas_call(
        _odd_out_kernel,
        grid=(b, s // tm),
        in_specs=[
            pl.BlockSpec((None, NA_HEADS, tm, HEAD_DIM), lambda bi, i: (bi, 0, i, 0)),
            pl.BlockSpec((None, tm, SG_WIDTH), tile),
            pl.BlockSpec((None, tm, SG_WIDTH), tile),
            _resident((SG_GROUPS, SG_CHUNK, SG_CHUNK), lambda bi, i: (0, 0, 0)),
            _resident((SG_GROUPS, SG_CHUNK, width), lambda bi, i: (0, 0, 0)),
            pl.BlockSpec(memory_space=pl.ANY),
            pl.BlockSpec((None, tm, D_MODEL), tile),
        ],
        out_specs=pl.BlockSpec((None, tm, D_MODEL), tile),
        out_shape=jax.ShapeDtypeStruct((b, s, D_MODEL), F32),
        scratch_shapes=[pltpu.VMEM((tm, NA_WIDTH + SG_WIDTH), BF16)]
        + _weight_scratch(NA_WIDTH + SG_WIDTH, D_MODEL),
        compiler_params=_params("arbitrary", "arbitrary"),
        name="odd_out",
    )(yc, u, vn, w_s, b_s, w_out, x)


def kernel(x, norm_ffn1, norm_mix, norm_ffn2, norm_out, ffn_w_gate, ffn_w_up, ffn_w_down, even_w_in, pool_w, pool_scale, dil_q_gain, dil_k_gain, even_w_out, odd_w_in, na_q_gain, na_k_gain, na_rpb, sg_v_gain, sg_w, sg_b, odd_w_out):
    depth = norm_ffn1.shape[0]
    slopes = jnp.asarray(2.0 ** (-8.0 * np.arange(1, DIL_HEADS + 1) / DIL_HEADS), dtype=F32)
    width = SG_WIDTH // SG_GROUPS

    for layer in range(depth):
        x = _ffn(x, norm_ffn1[layer], ffn_w_gate, ffn_w_up, ffn_w_down, layer, 0)
        if layer % 2 == 0:
            e = layer // 2
            a, q, k, v = _even_in(x, norm_mix[layer], even_w_in[e], dil_q_gain[e], dil_k_gain[e])
            yb = _dilated(q, k, v, slopes)
            x = _even_out(a, yb, pool_w[e].astype(BF16), pool_scale[e], even_w_out[e], x)
        else:
            o = layer // 2
            q, k, v, u, vn = _odd_in(x, norm_mix[layer], odd_w_in[o],
                                     na_q_gain[o], na_k_gain[o], sg_v_gain[o])
            yc = _neighbourhood(q, k, v, _na_pair_tables(na_rpb[o]))
            b_s = jnp.broadcast_to(sg_b[o][:, :, None], (SG_GROUPS, SG_CHUNK, width))
            x = _odd_out(yc, u, vn, sg_w[o].astype(BF16), b_s, odd_w_out[o], x)
        x = _ffn(x, norm_ffn2[layer], ffn_w_gate, ffn_w_up, ffn_w_down, layer, 1, gain_out=norm_out[layer])
    return x
```

```python
import functools

import numpy as np
import jax
import jax.numpy as jnp
from jax import lax
from jax.experimental import pallas as pl
from jax.experimental.pallas import tpu as pltpu

F32 = jnp.float32
BF16 = jnp.bfloat16

D_MODEL = 2048
D_FF = 5632
HEAD_DIM = 128
POOL_WINDOWS = (2, 4, 8, 16)
POOL_WIDTH = 512
POOL_HALO = 8
DIL_PAIRS = ((128, 1), (512, 4), (2048, 16))
DIL_HEADS_PER_GROUP = 4
DIL_HEADS = 12
DIL_WIDTH = DIL_HEADS * HEAD_DIM
DIL_OUT = DIL_HEADS_PER_GROUP * HEAD_DIM
GRID_W = 64
NA_ROWS = 8
NA_COLS = 16
NA_HEADS = 8
NA_WIDTH = NA_HEADS * HEAD_DIM
NA_QROWS = 4
NA_KROWS = NA_QROWS + NA_ROWS
SG_CHUNK = 128
SG_GROUPS = 8
SG_WIDTH = 1024
RMS_EPS = 1e-6
NEG_INF = -1e30

VMEM_LIMIT_BYTES = 60 * 1024 * 1024

FFN_TM = 1024
FFN_TF = 512
FFN_TC = 256
PROJ_TM = 512
PROJ_TN = 512
DIL_TQ = 128
DIL_TA = 2048
DIL_RADIUS = 64
DIL_UNROLL = 16
OUT_TM = 512
W_CHUNK = 512


def _params(*semantics):
    return pltpu.CompilerParams(dimension_semantics=semantics, vmem_limit_bytes=VMEM_LIMIT_BYTES)


def _resident(block_shape, index_map):
    return pl.BlockSpec(block_shape, index_map, pipeline_mode=pl.Buffered(1))


def _rms(x, gain):
    return x * lax.rsqrt(jnp.mean(x * x, axis=-1, keepdims=True) + RMS_EPS) * gain


def _first_grid_step():
    return (pl.program_id(0) == 0) & (pl.program_id(1) == 0)


def _load_weight_as_bf16(w_hbm, wbf_ref, stage_ref, sems):
    n_chunks = wbf_ref.shape[1] // W_CHUNK

    def copy(c):
        cols = pl.ds(c * W_CHUNK, W_CHUNK)
        return pltpu.make_async_copy(w_hbm.at[:, cols], stage_ref.at[c % 2], sems.at[c % 2])

    copy(0).start()
    for c in range(n_chunks):
        if c + 1 < n_chunks:
            copy(c + 1).start()
        copy(c).wait()
        wbf_ref[:, c * W_CHUNK:(c + 1) * W_CHUNK] = stage_ref[c % 2].astype(BF16)


def _weight_scratch(rows, cols):
    assert cols % W_CHUNK == 0
    return [pltpu.VMEM((rows, cols), BF16), pltpu.VMEM((2, rows, W_CHUNK), F32), pltpu.SemaphoreType.DMA((2,))]


def _ffn_kernel(x_hbm, g_ref, wg_ref, wu_ref, wd_ref, gout_ref, o_ref, h_ref, xbuf_ref, x_sem, *, final_norm):
    j = pl.program_id(2)
    tiles_per_batch = pl.num_programs(1)
    n_tiles = pl.num_programs(0) * tiles_per_batch
    tile = pl.program_id(0) * tiles_per_batch + pl.program_id(1)

    def x_copy(t):
        rows = pl.ds(pl.multiple_of((t % tiles_per_batch) * FFN_TM, FFN_TM), FFN_TM)
        return pltpu.make_async_copy(x_hbm.at[t // tiles_per_batch, rows, :], xbuf_ref, x_sem)

    @pl.when(j == 0)
    def _():
        @pl.when(tile == 0)
        def _():
            x_copy(tile).start()

        x_copy(tile).wait()
        x = xbuf_ref[...]
        h_ref[...] = _rms(x, g_ref[...]).astype(BF16)
        o_ref[...] = x

    @pl.when((j == 1) & (tile + 1 < n_tiles))
    def _():
        x_copy(tile + 1).start()

    h = h_ref[...]
    for c in range(FFN_TF // FFN_TC):
        cols = slice(c * FFN_TC, (c + 1) * FFN_TC)
        gate = jnp.dot(h, wg_ref[:, cols].astype(BF16), preferred_element_type=F32)
        up = jnp.dot(h, wu_ref[:, cols].astype(BF16), preferred_element_type=F32)
        act = (gate * jax.nn.sigmoid(gate)) * up * 0.5
        o_ref[...] += jnp.dot(act.astype(BF16), wd_ref[cols, :].astype(BF16), preferred_element_type=F32)

    if final_norm:
        @pl.when(j == pl.num_programs(2) - 1)
        def _():
            o_ref[...] = _rms(o_ref[...], gout_ref[...])


def _ffn(x, gain, w_gate, w_up, w_down, layer, idx, gain_out=None):
    b, s, _ = x.shape
    assert D_FF // FFN_TF >= 2
    final_norm = gain_out is not None
    if gain_out is None:
        gain_out = gain
    tile = lambda bi, i, j: (bi, i, 0)
    const = lambda bi, i, j: (0, 0)
    return pl.pallas_call(
        functools.partial(_ffn_kernel, final_norm=final_norm),
        grid=(b, s // FFN_TM, D_FF // FFN_TF),
        in_specs=[
            pl.BlockSpec(memory_space=pl.ANY),
            pl.BlockSpec((1, D_MODEL), const),
            pl.BlockSpec((None, None, D_MODEL, FFN_TF), lambda bi, i, j: (layer, idx, 0, j)),
            pl.BlockSpec((None, None, D_MODEL, FFN_TF), lambda bi, i, j: (layer, idx, 0, j)),
            pl.BlockSpec((None, None, FFN_TF, D_MODEL), lambda bi, i, j: (layer, idx, j, 0)),
            pl.BlockSpec((1, D_MODEL), const),
        ],
        out_specs=pl.BlockSpec((None, FFN_TM, D_MODEL), tile),
        out_shape=jax.ShapeDtypeStruct((b, s, D_MODEL), F32),
        scratch_shapes=[pltpu.VMEM((FFN_TM, D_MODEL), BF16), pltpu.VMEM((FFN_TM, D_MODEL), F32),
                        pltpu.SemaphoreType.DMA(())],
        compiler_params=_params("arbitrary", "arbitrary", "arbitrary"),
        name="ffn_final" if final_norm else "ffn",
    )(x, gain.reshape(1, D_MODEL), w_gate, w_up, w_down, gain_out.reshape(1, D_MODEL))


def _head_store(z, out_ref, head0, gain=None):
    for hd in range(PROJ_TN // HEAD_DIM):
        zh = z[:, hd * HEAD_DIM:(hd + 1) * HEAD_DIM]
        if gain is not None:
            zh = _rms(zh, gain)
        out_ref[head0 + hd] = zh.astype(out_ref.dtype)


def _store_by_residue(z, out_ref, stage_ref, dil, gain=None):
    tm = z.shape[0]
    for hd in range(z.shape[1] // HEAD_DIM):
        cols = slice(hd * HEAD_DIM, (hd + 1) * HEAD_DIM)
        zh = z[:, cols]
        if gain is not None:
            zh = _rms(zh, gain)
        if dil == 1:
            out_ref[hd, 0] = zh.astype(BF16)
        else:
            stage_ref[hd] = zh
            for r in range(dil):
                out_ref[hd, r] = stage_ref[hd, pl.ds(r, tm // dil, stride=dil), :].astype(BF16)


def _even_in_kernel(x_ref, g_ref, w_hbm, qg_ref, kg_ref, a_ref, *rest):
    n_groups = len(DIL_PAIRS)
    q_refs, k_refs, v_refs = rest[:n_groups], rest[n_groups:2 * n_groups], rest[2 * n_groups:3 * n_groups]
    h_ref, stage_ref, w_ref, wstage_ref, w_sems = rest[3 * n_groups:]
    pl.when(_first_grid_step())(functools.partial(_load_weight_as_bf16, w_hbm, w_ref, wstage_ref, w_sems))
    h_ref[...] = _rms(x_ref[...], g_ref[...]).astype(BF16)

    def chunk(c):
        return jnp.dot(h_ref[...], w_ref[:, c * PROJ_TN:(c + 1) * PROJ_TN], preferred_element_type=F32)

    a_ref[...] = chunk(0)
    for g, (_, dil) in enumerate(DIL_PAIRS):
        _store_by_residue(chunk(1 + g), q_refs[g], stage_ref, dil, qg_ref[...])
        _store_by_residue(chunk(1 + n_groups + g), k_refs[g], stage_ref, dil, kg_ref[...])
        _store_by_residue(chunk(1 + 2 * n_groups + g), v_refs[g], stage_ref, dil)


def _even_in(x, gain, w_in, q_gain, k_gain):
    b, s, _ = x.shape
    n_in = w_in.shape[1]
    assert PROJ_TN == DIL_OUT
    row = lambda bi, i: (bi, i, 0)
    const = lambda bi, i: (0, 0)
    hpg = DIL_HEADS_PER_GROUP
    group_specs = [pl.BlockSpec((None, hpg, dil, PROJ_TM // dil, HEAD_DIM), lambda bi, i: (bi, 0, 0, i, 0))
                   for _, dil in DIL_PAIRS]
    group_shapes = [jax.ShapeDtypeStruct((b, hpg, dil, s // dil, HEAD_DIM), BF16) for _, dil in DIL_PAIRS]
    outs = pl.pallas_call(
        _even_in_kernel,
        grid=(b, s // PROJ_TM),
        in_specs=[
            pl.BlockSpec((None, PROJ_TM, D_MODEL), row),
            pl.BlockSpec((1, D_MODEL), const),
            pl.BlockSpec(memory_space=pl.ANY),
            pl.BlockSpec((1, HEAD_DIM), const),
            pl.BlockSpec((1, HEAD_DIM), const),
        ],
        out_specs=[pl.BlockSpec((None, PROJ_TM, POOL_WIDTH), row)] + group_specs * 3,
        out_shape=[jax.ShapeDtypeStruct((b, s, POOL_WIDTH), F32)] + group_shapes * 3,
        scratch_shapes=[pltpu.VMEM((PROJ_TM, D_MODEL), BF16),
                        pltpu.VMEM((DIL_HEADS_PER_GROUP, PROJ_TM, HEAD_DIM), F32)]
        + _weight_scratch(D_MODEL, n_in),
        compiler_params=_params("arbitrary", "arbitrary"),
        name="even_in",
    )(x, gain.reshape(1, D_MODEL), w_in,
      (q_gain * HEAD_DIM ** -0.5).reshape(1, HEAD_DIM), k_gain.reshape(1, HEAD_DIM))
    n_groups = len(DIL_PAIRS)
    return outs[0], outs[1:1 + n_groups], outs[1 + n_groups:1 + 2 * n_groups], outs[1 + 2 * n_groups:]


def _odd_in_kernel(x_ref, g_ref, w_hbm, qg_ref, kg_ref, vg_ref, q_ref, k_ref, v_ref, u_ref, vn_ref,
                   h_ref, gv_ref, w_ref, wstage_ref, w_sems):
    pl.when(_first_grid_step())(functools.partial(_load_weight_as_bf16, w_hbm, w_ref, wstage_ref, w_sems))
    h_ref[...] = _rms(x_ref[...], g_ref[...]).astype(BF16)

    def chunk(c):
        return jnp.dot(h_ref[...], w_ref[:, c * PROJ_TN:(c + 1) * PROJ_TN], preferred_element_type=F32)

    per = NA_WIDTH // PROJ_TN
    heads_per_chunk = PROJ_TN // HEAD_DIM
    for c in range(per):
        _head_store(chunk(c), q_ref, c * heads_per_chunk, qg_ref[...])
        _head_store(chunk(per + c), k_ref, c * heads_per_chunk, kg_ref[...])
        _head_store(chunk(2 * per + c), v_ref, c * heads_per_chunk)
    per_sg = SG_WIDTH // PROJ_TN
    for c in range(per_sg):
        u_ref[:, c * PROJ_TN:(c + 1) * PROJ_TN] = jax.nn.gelu(chunk(3 * per + c))
        gv_ref[:, c * PROJ_TN:(c + 1) * PROJ_TN] = jax.nn.gelu(chunk(3 * per + per_sg + c))
    vn_ref[...] = _rms(gv_ref[...], vg_ref[...]).astype(BF16)


def _odd_in(x, gain, w_in, q_gain, k_gain, v_gain):
    b, s, _ = x.shape
    n_in = w_in.shape[1]
    row = lambda bi, i: (bi, i, 0)
    by_head = lambda bi, i: (bi, 0, i, 0)
    const = lambda bi, i: (0, 0)
    return pl.pallas_call(
        _odd_in_kernel,
        grid=(b, s // PROJ_TM),
        in_specs=[
            pl.BlockSpec((None, PROJ_TM, D_MODEL), row),
            pl.BlockSpec((1, D_MODEL), const),
            pl.BlockSpec(memory_space=pl.ANY),
            pl.BlockSpec((1, HEAD_DIM), const),
            pl.BlockSpec((1, HEAD_DIM), const),
            pl.BlockSpec((1, SG_WIDTH), const),
        ],
        out_specs=[
            pl.BlockSpec((None, NA_HEADS, PROJ_TM, HEAD_DIM), by_head),
            pl.BlockSpec((None, NA_HEADS, PROJ_TM, HEAD_DIM), by_head),
            pl.BlockSpec((None, NA_HEADS, PROJ_TM, HEAD_DIM), by_head),
            pl.BlockSpec((None, PROJ_TM, SG_WIDTH), row),
            pl.BlockSpec((None, PROJ_TM, SG_WIDTH), row),
        ],
        out_shape=[
            jax.ShapeDtypeStruct((b, NA_HEADS, s, HEAD_DIM), BF16),
            jax.ShapeDtypeStruct((b, NA_HEADS, s, HEAD_DIM), BF16),
            jax.ShapeDtypeStruct((b, NA_HEADS, s, HEAD_DIM), BF16),
            jax.ShapeDtypeStruct((b, s, SG_WIDTH), F32),
            jax.ShapeDtypeStruct((b, s, SG_WIDTH), BF16),
        ],
        scratch_shapes=[pltpu.VMEM((PROJ_TM, D_MODEL), BF16), pltpu.VMEM((PROJ_TM, SG_WIDTH), F32)]
        + _weight_scratch(D_MODEL, n_in),
        compiler_params=_params("arbitrary", "arbitrary"),
        name="odd_in",
    )(x, gain.reshape(1, D_MODEL), w_in,
      (q_gain * HEAD_DIM ** -0.5).reshape(1, HEAD_DIM), k_gain.reshape(1, HEAD_DIM),
      v_gain.reshape(1, SG_WIDTH))


def _dilated_kernel(slopes_ref, q0_ref, q1_ref, q2_ref, k0_ref, k1_ref, k2_ref, v0_ref, v1_ref, v2_ref,
                    o_ref, on_ref, ls_ref, s_ref, *, seq):
    hh = pl.program_id(1)
    step = pl.program_id(2)
    q_refs = (q0_ref, q1_ref, q2_ref)
    k_refs = (k0_ref, k1_ref, k2_ref)
    v_refs = (v0_ref, v1_ref, v2_ref)
    for g, (window, dil) in enumerate(DIL_PAIRS):
        assert window // (2 * dil) == DIL_RADIUS
        q_ref, k_ref, v_ref = q_refs[g], k_refs[g], v_refs[g]
        sub_len = seq // dil
        per_res = DIL_TA // dil
        nq = min(per_res, DIL_TQ)
        n_sub = per_res // nq
        n_keys = min(nq + 2 * DIL_RADIUS, sub_len)
        slope = slopes_ref[g * DIL_HEADS_PER_GROUP + hh] * dil
        base = (lax.broadcasted_iota(jnp.int32, (nq, n_keys), 1)
                - lax.broadcasted_iota(jnp.int32, (nq, n_keys), 0))

        def place(t, nq=nq, n_sub=n_sub, per_res=per_res, sub_len=sub_len, n_keys=n_keys):
            r = t // n_sub
            q_off = pl.multiple_of((t % n_sub) * nq, nq)
            q_pos = step * per_res + q_off
            k_pos = pl.multiple_of(jnp.clip(q_pos - DIL_RADIUS, 0, sub_len - n_keys), DIL_RADIUS)
            return r, q_off, q_pos, k_pos

        def scores(t, slot, q_ref=q_ref, k_ref=k_ref, nq=nq, n_keys=n_keys, slope=slope, base=base,
                   place=place):
            r, q_off, q_pos, k_pos = place(t)
            q = q_ref[r, pl.ds(q_off, nq), :]
            k = k_ref[r, pl.ds(k_pos, n_keys), :]
            s = lax.dot_general(q, k, (((1,), (1,)), ((), ())), preferred_element_type=F32)
            dist = jnp.abs(base + (k_pos - q_pos))
            s_ref[slot, 0:nq, 0:n_keys] = jnp.where(dist <= DIL_RADIUS, s - slope * dist.astype(F32), NEG_INF)

        def finish(t, slot, v_ref=v_ref, g=g, dil=dil, nq=nq, n_keys=n_keys, place=place):
            r, q_off, _, k_pos = place(t)
            v = v_ref[r, pl.ds(k_pos, n_keys), :]
            s = s_ref[slot, 0:nq, 0:n_keys]
            m = jnp.max(s, axis=-1, keepdims=True)
            p = jnp.exp(s - m)
            den = jnp.sum(p, axis=-1, keepdims=True)
            out = jnp.dot(p.astype(BF16), v, preferred_element_type=F32) / den
            lse = jnp.broadcast_to(m + jnp.log(den), (nq, HEAD_DIM))
            if dil == 1:
                rows = pl.ds(q_off, nq)
            else:
                rows = pl.ds(r + dil * q_off, nq, stride=dil)
            on_ref[g, rows, :] = out
            ls_ref[g, rows, :] = lse

        n_tiles = dil * n_sub
        assert n_tiles % DIL_UNROLL == 0

        def body(i, carry, scores=scores, finish=finish):
            t0 = i * DIL_UNROLL
            scores(t0, 0)
            for u in range(DIL_UNROLL):
                if u + 1 < DIL_UNROLL:
                    scores(t0 + u + 1, (u + 1) % 2)
                finish(t0 + u, u % 2)
            return carry

        lax.fori_loop(0, n_tiles // DIL_UNROLL, body, 0)

    def merge(c, carry):
        rows = pl.ds(pl.multiple_of(c * DIL_TQ, DIL_TQ), DIL_TQ)
        lse = [ls_ref[g, rows, :] for g in range(len(DIL_PAIRS))]
        top = jnp.maximum(jnp.maximum(lse[0], lse[1]), lse[2])
        num = jnp.zeros((DIL_TQ, HEAD_DIM), F32)
        den = jnp.zeros((DIL_TQ, HEAD_DIM), F32)
        for g in range(len(DIL_PAIRS)):
            w = jnp.exp(lse[g] - top)
            num = num + w * on_ref[g, rows, :]
            den = den + w
        o_ref[rows, :] = (num / den).astype(o_ref.dtype)
        return carry

    lax.fori_loop(0, DIL_TA // DIL_TQ, merge, 0)


def _dilated(q, k, v, slopes):
    b, _, dil0, s, _ = q[0].shape
    assert dil0 == 1
    n_groups = len(DIL_PAIRS)

    def q_spec(dil):
        return pl.BlockSpec((None, None, dil, DIL_TA // dil, HEAD_DIM), lambda bi, hh, i: (bi, hh, 0, i, 0))

    def kv_spec(dil):
        return pl.BlockSpec((None, None, dil, s // dil, HEAD_DIM), lambda bi, hh, i: (bi, hh, 0, 0, 0))

    dils = [dil for _, dil in DIL_PAIRS]
    return pl.pallas_call(
        functools.partial(_dilated_kernel, seq=s),
        grid=(b, DIL_HEADS_PER_GROUP, s // DIL_TA),
        in_specs=[pl.BlockSpec(memory_space=pltpu.SMEM)]
        + [q_spec(d) for d in dils] + [kv_spec(d) for d in dils] + [kv_spec(d) for d in dils],
        out_specs=pl.BlockSpec((None, None, DIL_TA, HEAD_DIM), lambda bi, hh, i: (bi, hh, i, 0)),
        out_shape=jax.ShapeDtypeStruct((b, DIL_HEADS_PER_GROUP, s, HEAD_DIM), BF16),
        scratch_shapes=[pltpu.VMEM((n_groups, DIL_TA, HEAD_DIM), F32),
                        pltpu.VMEM((n_groups, DIL_TA, HEAD_DIM), F32),
                        pltpu.VMEM((2, DIL_TQ, DIL_TQ + 2 * DIL_RADIUS), F32)],
        compiler_params=_params("parallel", "parallel", "arbitrary"),
        name="dilated_attn",
    )(slopes, *q, *k, *v)


def _even_out_kernel(a_ref, prev_ref, next_ref, yb_ref, pw_ref, ps_ref, wo_hbm, x_ref, o_ref,
                     ext_ref, y_ref, wo_ref, wstage_ref, w_sems, *, seq):
    pl.when(_first_grid_step())(functools.partial(_load_weight_as_bf16, wo_hbm, wo_ref, wstage_ref, w_sems))
    i = pl.program_id(1)
    tm = OUT_TM
    ext_ref[0:POOL_HALO, :] = jnp.where(i == 0, 0.0, prev_ref[...])
    ext_ref[POOL_HALO:POOL_HALO + tm, :] = a_ref[...]
    ext_ref[POOL_HALO + tm:, :] = jnp.where(i == pl.num_programs(1) - 1, 0.0, next_ref[...])
    pos = i * tm + lax.broadcasted_iota(jnp.int32, (tm, 1), 0)
    for g, window in enumerate(POOL_WINDOWS):
        half = window // 2
        cols = slice(g * HEAD_DIM, (g + 1) * HEAD_DIM)
        total = ext_ref[POOL_HALO - half:POOL_HALO - half + tm, cols]
        for shift in range(-half + 1, half):
            total = total + ext_ref[POOL_HALO + shift:POOL_HALO + shift + tm, cols]
        count = (jnp.minimum(pos + half, seq) - jnp.maximum(pos - half, 0)).astype(F32)
        pooled = total / count - a_ref[:, cols]
        ya = jnp.dot(pooled.astype(BF16), pw_ref[g], preferred_element_type=F32) * ps_ref[:, cols]
        y_ref[:, cols] = ya.astype(BF16)
    for hd in range(DIL_HEADS_PER_GROUP):
        y_ref[:, POOL_WIDTH + hd * HEAD_DIM:POOL_WIDTH + (hd + 1) * HEAD_DIM] = yb_ref[hd]
    o_ref[...] = x_ref[...] + jnp.dot(y_ref[...], wo_ref[...], preferred_element_type=F32)


def _even_out(a, yb, pool_w, pool_scale, w_out, x):
    b, s, _ = x.shape
    tm = OUT_TM
    halo_blocks = tm // POOL_HALO
    tile = lambda bi, i: (bi, i, 0)
    return pl.pallas_call(
        functools.partial(_even_out_kernel, seq=s),
        grid=(b, s // tm),
        in_specs=[
            pl.BlockSpec((None, tm, POOL_WIDTH), tile),
            pl.BlockSpec((None, POOL_HALO, POOL_WIDTH),
                         lambda bi, i: (bi, jnp.maximum(i * halo_blocks - 1, 0), 0)),
            pl.BlockSpec((None, POOL_HALO, POOL_WIDTH),
                         lambda bi, i: (bi, jnp.minimum((i + 1) * halo_blocks, s // POOL_HALO - 1), 0)),
            pl.BlockSpec((None, DIL_HEADS_PER_GROUP, tm, HEAD_DIM), lambda bi, i: (bi, 0, i, 0)),
            _resident((len(POOL_WINDOWS), HEAD_DIM, HEAD_DIM), lambda bi, i: (0, 0, 0)),
            pl.BlockSpec((1, POOL_WIDTH), lambda bi, i: (0, 0)),
            pl.BlockSpec(memory_space=pl.ANY),
            pl.BlockSpec((None, tm, D_MODEL), tile),
        ],
        out_specs=pl.BlockSpec((None, tm, D_MODEL), tile),
        out_shape=jax.ShapeDtypeStruct((b, s, D_MODEL), F32),
        scratch_shapes=[pltpu.VMEM((tm + 2 * POOL_HALO, POOL_WIDTH), F32),
                        pltpu.VMEM((tm, POOL_WIDTH + DIL_OUT), BF16)]
        + _weight_scratch(POOL_WIDTH + DIL_OUT, D_MODEL),
        compiler_params=_params("arbitrary", "arbitrary"),
        name="even_out",
    )(a, a, a, yb, pool_w, pool_scale.reshape(1, POOL_WIDTH), w_out, x)


def _na_fill_bias(pair_ref, bias_ref, rb, rows):
    k_row0 = min(max(rb * NA_QROWS - NA_ROWS // 2, 0), rows - NA_KROWS)
    left_half = lax.broadcasted_iota(jnp.int32, (GRID_W, 2 * GRID_W), 1) < GRID_W
    for qr in range(NA_QROWS):
        r = rb * NA_QROWS + qr
        row_start = min(max(r - NA_ROWS // 2, 0), rows - NA_ROWS)
        for p in range(NA_KROWS // 2):
            kr = k_row0 + 2 * p
            ok_left = row_start <= kr < row_start + NA_ROWS
            ok_right = row_start <= kr + 1 < row_start + NA_ROWS
            if ok_left or ok_right:
                block = pair_ref[kr + 1 - r + NA_ROWS - 1]
                if not ok_right:
                    block = jnp.where(left_half, block, NEG_INF)
                if not ok_left:
                    block = jnp.where(left_half, NEG_INF, block)
            else:
                block = jnp.full((GRID_W, 2 * GRID_W), NEG_INF, F32)
            bias_ref[qr * GRID_W:(qr + 1) * GRID_W, p * 2 * GRID_W:(p + 1) * 2 * GRID_W] = block


def _na_kernel(q_ref, k_ref, v_ref, pair_ref, o_ref, bias_ref, s_ref, *, rows):
    n_blocks = rows // NA_QROWS
    tq = NA_QROWS * GRID_W
    n_keys = NA_KROWS * GRID_W
    kinds = (0, 1, n_blocks - 1)
    for slot, kind in enumerate(kinds):
        _na_fill_bias(pair_ref, bias_ref.at[slot], kind, rows)

    def key_start(rb):
        return min(max(rb * NA_QROWS - NA_ROWS // 2, 0), rows - NA_KROWS) * GRID_W

    def scores(rb):
        kind_slot = 0 if rb == 0 else (2 if rb == n_blocks - 1 else 1)
        q = q_ref[rb * tq:(rb + 1) * tq, :]
        k = k_ref[key_start(rb):key_start(rb) + n_keys, :]
        s_ref[rb % 2] = (lax.dot_general(q, k, (((1,), (1,)), ((), ())), preferred_element_type=F32)
                         + bias_ref[kind_slot])

    def finish(rb):
        v = v_ref[key_start(rb):key_start(rb) + n_keys, :]
        s = s_ref[rb % 2]
        m = jnp.max(s, axis=-1, keepdims=True)
        p = jnp.exp(s - m)
        den = jnp.sum(p, axis=-1, keepdims=True)
        out = jnp.dot(p.astype(BF16), v, preferred_element_type=F32) / den
        o_ref[rb * tq:(rb + 1) * tq, :] = out.astype(o_ref.dtype)

    scores(0)
    for rb in range(n_blocks):
        if rb + 1 < n_blocks:
            scores(rb + 1)
        finish(rb)


def _na_pair_tables(rpb):
    n_heads, n_rel_rows, n_rel_cols = rpb.shape
    c = np.arange(GRID_W)
    col_start = np.clip(c - NA_COLS // 2, 0, GRID_W - NA_COLS)
    col_ok = (c[None, :] >= col_start[:, None]) & (c[None, :] < col_start[:, None] + NA_COLS)
    rel_c = np.clip(c[None, :] - c[:, None], -(NA_COLS - 1), NA_COLS - 1) + (NA_COLS - 1)
    pick = (rel_c.reshape(1, -1) == np.arange(n_rel_cols)[:, None]).astype(np.float32)
    by_col = jnp.dot(rpb.astype(F32).reshape(n_heads * n_rel_rows, n_rel_cols), pick,
                     precision=lax.Precision.HIGHEST).reshape(n_heads, n_rel_rows, GRID_W, GRID_W)
    by_col = jnp.where(col_ok[None, None], by_col, NEG_INF)
    masked = jnp.full((n_heads, 1, GRID_W, GRID_W), NEG_INF, F32)
    padded = jnp.concatenate([masked, by_col, masked], axis=1)
    return jnp.concatenate([padded[:, :-1], padded[:, 1:]], axis=-1)


def _neighbourhood(q, k, v, pair_tables):
    b, _, s, _ = q.shape
    rows = s // GRID_W
    n_blocks = rows // NA_QROWS
    assert n_blocks >= 3 and rows >= NA_KROWS
    tq = NA_QROWS * GRID_W
    n_keys = NA_KROWS * GRID_W
    head = lambda bi, h: (bi, h, 0, 0)
    return pl.pallas_call(
        functools.partial(_na_kernel, rows=rows),
        grid=(b, NA_HEADS),
        in_specs=[
            pl.BlockSpec((None, None, s, HEAD_DIM), head),
            pl.BlockSpec((None, None, s, HEAD_DIM), head),
            pl.BlockSpec((None, None, s, HEAD_DIM), head),
            pl.BlockSpec((None, 2 * NA_ROWS, GRID_W, 2 * GRID_W), lambda bi, h: (h, 0, 0, 0)),
        ],
        out_specs=pl.BlockSpec((None, None, s, HEAD_DIM), head),
        out_shape=jax.ShapeDtypeStruct((b, NA_HEADS, s, HEAD_DIM), BF16),
        scratch_shapes=[pltpu.VMEM((3, tq, n_keys), F32), pltpu.VMEM((2, tq, n_keys), F32)],
        compiler_params=_params("parallel", "parallel"),
        name="neighbourhood_attn",
    )(q, k, v, pair_tables)


def _odd_out_kernel(yc_ref, u_ref, vn_ref, ws_ref, bs_ref, wo_hbm, x_ref, o_ref, y_ref,
                    wo_ref, wstage_ref, w_sems):
    pl.when(_first_grid_step())(functools.partial(_load_weight_as_bf16, wo_hbm, wo_ref, wstage_ref, w_sems))
    width = SG_WIDTH // SG_GROUPS
    for c in range(OUT_TM // SG_CHUNK):
        rows = slice(c * SG_CHUNK, (c + 1) * SG_CHUNK)
        for g in range(SG_GROUPS):
            cols = slice(g * width, (g + 1) * width)
            sv = jnp.dot(ws_ref[g], vn_ref[rows, cols], preferred_element_type=F32) + bs_ref[g]
            y_ref[rows, NA_WIDTH + g * width:NA_WIDTH + (g + 1) * width] = (u_ref[rows, cols] * sv).astype(BF16)
    for hd in range(NA_HEADS):
        y_ref[:, hd * HEAD_DIM:(hd + 1) * HEAD_DIM] = yc_ref[hd]
    o_ref[...] = x_ref[...] + jnp.dot(y_ref[...], wo_ref[...], preferred_element_type=F32)


def _odd_out(yc, u, vn, w_s, b_s, w_out, x):
    b, s, _ = x.shape
    tm = OUT_TM
    width = SG_WIDTH // SG_GROUPS
    tile = lambda bi, i: (bi, i, 0)
    return pl.pallas_call(
        _odd_out_kernel,
        grid=(b, s // tm),
        in_specs=[
            pl.BlockSpec((None, NA_HEADS, tm, HEAD_DIM), lambda bi, i: (bi, 0, i, 0)),
            pl.BlockSpec((None, tm, SG_WIDTH), tile),
            pl.BlockSpec((None, tm, SG_WIDTH), tile),
            _resident((SG_GROUPS, SG_CHUNK, SG_CHUNK), lambda bi, i: (0, 0, 0)),
            _resident((SG_GROUPS, SG_CHUNK, width), lambda bi, i: (0, 0, 0)),
            pl.BlockSpec(memory_space=pl.ANY),
            pl.BlockSpec((None, tm, D_MODEL), tile),
        ],
        out_specs=pl.BlockSpec((None, tm, D_MODEL), tile),
        out_shape=jax.ShapeDtypeStruct((b, s, D_MODEL), F32),
        scratch_shapes=[pltpu.VMEM((tm, NA_WIDTH + SG_WIDTH), BF16)]
        + _weight_scratch(NA_WIDTH + SG_WIDTH, D_MODEL),
        compiler_params=_params("arbitrary", "arbitrary"),
        name="odd_out",
    )(yc, u, vn, w_s, b_s, w_out, x)


def kernel(x, norm_ffn1, norm_mix, norm_ffn2, norm_out, ffn_w_gate, ffn_w_up, ffn_w_down, even_w_in, pool_w, pool_scale, dil_q_gain, dil_k_gain, even_w_out, odd_w_in, na_q_gain, na_k_gain, na_rpb, sg_v_gain, sg_w, sg_b, odd_w_out):
    depth = norm_ffn1.shape[0]
    slopes = jnp.asarray(2.0 ** (-8.0 * np.arange(1, DIL_HEADS + 1) / DIL_HEADS), dtype=F32)
    width = SG_WIDTH // SG_GROUPS

    for layer in range(depth):
        x = _ffn(x, norm_ffn1[layer], ffn_w_gate, ffn_w_up, ffn_w_down, layer, 0)
        if layer % 2 == 0:
            e = layer // 2
            a, q, k, v = _even_in(x, norm_mix[layer], even_w_in[e], dil_q_gain[e], dil_k_gain[e])
            yb = _dilated(q, k, v, slopes)
            x = _even_out(a, yb, pool_w[e].astype(BF16), pool_scale[e], even_w_out[e], x)
        else:
            o = layer // 2
            q, k, v, u, vn = _odd_in(x, norm_mix[layer], odd_w_in[o],
                                     na_q_gain[o], na_k_gain[o], sg_v_gain[o])
            yc = _neighbourhood(q, k, v, _na_pair_tables(na_rpb[o]))
            b_s = jnp.broadcast_to(sg_b[o][:, :, None], (SG_GROUPS, SG_CHUNK, width))
            x = _odd_out(yc, u, vn, sg_w[o].astype(BF16), b_s, odd_w_out[o], x)
        x = _ffn(x, norm_ffn2[layer], ffn_w_gate, ffn_w_up, ffn_w_down, layer, 1, gain_out=norm_out[layer])
    return x
```

```python
import functools

import numpy as np
import jax
import jax.numpy as jnp
from jax import lax
from jax.experimental import pallas as pl
from jax.experimental.pallas import tpu as pltpu

F32 = jnp.float32
BF16 = jnp.bfloat16

D_MODEL = 2048
D_FF = 5632
HEAD_DIM = 128
POOL_WINDOWS = (2, 4, 8, 16)
POOL_WIDTH = 512
POOL_HALO = 8
DIL_PAIRS = ((128, 1), (512, 4), (2048, 16))
DIL_HEADS_PER_GROUP = 4
DIL_HEADS = 12
DIL_WIDTH = DIL_HEADS * HEAD_DIM
DIL_OUT = DIL_HEADS_PER_GROUP * HEAD_DIM
GRID_W = 64
NA_ROWS = 8
NA_COLS = 16
NA_HEADS = 8
NA_WIDTH = NA_HEADS * HEAD_DIM
NA_QROWS = 4
NA_KROWS = NA_QROWS + NA_ROWS
SG_CHUNK = 128
SG_GROUPS = 8
SG_WIDTH = 1024
RMS_EPS = 1e-6
NEG_INF = -1e30

VMEM_LIMIT_BYTES = 60 * 1024 * 1024

FFN_TM = 1024
FFN_TF = 512
FFN_TC = 256
PROJ_TM = 512
PROJ_TN = 512
DIL_TQ = 128
DIL_TA = 2048
DIL_RADIUS = 64
DIL_UNROLL = 16
OUT_TM = 512
W_CHUNK = 512


def _params(*semantics):
    return pltpu.CompilerParams(dimension_semantics=semantics, vmem_limit_bytes=VMEM_LIMIT_BYTES)


def _resident(block_shape, index_map):
    return pl.BlockSpec(block_shape, index_map, pipeline_mode=pl.Buffered(1))


def _rms(x, gain):
    return x * lax.rsqrt(jnp.mean(x * x, axis=-1, keepdims=True) + RMS_EPS) * gain


def _first_grid_step():
    return (pl.program_id(0) == 0) & (pl.program_id(1) == 0)


def _load_weight_as_bf16(w_hbm, wbf_ref, stage_ref, sems):
    n_chunks = wbf_ref.shape[1] // W_CHUNK

    def copy(c):
        cols = pl.ds(c * W_CHUNK, W_CHUNK)
        return pltpu.make_async_copy(w_hbm.at[:, cols], stage_ref.at[c % 2], sems.at[c % 2])

    copy(0).start()
    for c in range(n_chunks):
        if c + 1 < n_chunks:
            copy(c + 1).start()
        copy(c).wait()
        wbf_ref[:, c * W_CHUNK:(c + 1) * W_CHUNK] = stage_ref[c % 2].astype(BF16)


def _project_in_chunks(prologue, h_ref, w_hbm, w_ref, stage_ref, sems, jobs):
    assert W_CHUNK == PROJ_TN

    def run(first):
        def copy(k):
            cols = pl.ds(jobs[k][0] * W_CHUNK, W_CHUNK)
            return pltpu.make_async_copy(w_hbm.at[:, cols], stage_ref.at[k % 2], sems.at[k % 2])

        if first:
            for k in range(min(2, len(jobs))):
                copy(k).start()
        prologue()
        for k, (c, epilogue) in enumerate(jobs):
            cols = slice(c * PROJ_TN, (c + 1) * PROJ_TN)
            if first:
                copy(k).wait()
                w_ref[:, cols] = stage_ref[k % 2].astype(BF16)
                if k + 2 < len(jobs):
                    copy(k + 2).start()
            epilogue(jnp.dot(h_ref[...], w_ref[:, cols], preferred_element_type=F32))

    first_step = _first_grid_step()
    pl.when(first_step)(functools.partial(run, True))
    pl.when(jnp.logical_not(first_step))(functools.partial(run, False))


def _weight_scratch(rows, cols):
    assert cols % W_CHUNK == 0
    return [pltpu.VMEM((rows, cols), BF16), pltpu.VMEM((2, rows, W_CHUNK), F32), pltpu.SemaphoreType.DMA((2,))]


def _ffn_kernel(x_hbm, g_ref, wg_ref, wu_ref, wd_ref, gout_ref, o_ref, h_ref, xbuf_ref, x_sem, *, final_norm):
    j = pl.program_id(2)
    tiles_per_batch = pl.num_programs(1)
    n_tiles = pl.num_programs(0) * tiles_per_batch
    tile = pl.program_id(0) * tiles_per_batch + pl.program_id(1)

    def x_copy(t):
        rows = pl.ds(pl.multiple_of((t % tiles_per_batch) * FFN_TM, FFN_TM), FFN_TM)
        return pltpu.make_async_copy(x_hbm.at[t // tiles_per_batch, rows, :], xbuf_ref, x_sem)

    @pl.when(j == 0)
    def _():
        @pl.when(tile == 0)
        def _():
            x_copy(tile).start()

        x_copy(tile).wait()
        x = xbuf_ref[...]
        h_ref[...] = _rms(x, g_ref[...]).astype(BF16)
        o_ref[...] = x

    @pl.when((j == 1) & (tile + 1 < n_tiles))
    def _():
        x_copy(tile + 1).start()

    h = h_ref[...]
    for c in range(FFN_TF // FFN_TC):
        cols = slice(c * FFN_TC, (c + 1) * FFN_TC)
        gate = jnp.dot(h, wg_ref[:, cols].astype(BF16), preferred_element_type=F32)
        up = jnp.dot(h, wu_ref[:, cols].astype(BF16), preferred_element_type=F32)
        act = (gate * jax.nn.sigmoid(gate)) * up * 0.5
        o_ref[...] += jnp.dot(act.astype(BF16), wd_ref[cols, :].astype(BF16), preferred_element_type=F32)

    if final_norm:
        @pl.when(j == pl.num_programs(2) - 1)
        def _():
            o_ref[...] = _rms(o_ref[...], gout_ref[...])


def _ffn(x, gain, w_gate, w_up, w_down, layer, idx, gain_out=None):
    b, s, _ = x.shape
    assert D_FF // FFN_TF >= 2
    final_norm = gain_out is not None
    if gain_out is None:
        gain_out = gain
    tile = lambda bi, i, j: (bi, i, 0)
    const = lambda bi, i, j: (0, 0)
    return pl.pallas_call(
        functools.partial(_ffn_kernel, final_norm=final_norm),
        grid=(b, s // FFN_TM, D_FF // FFN_TF),
        in_specs=[
            pl.BlockSpec(memory_space=pl.ANY),
            pl.BlockSpec((1, D_MODEL), const),
            pl.BlockSpec((None, None, D_MODEL, FFN_TF), lambda bi, i, j: (layer, idx, 0, j)),
            pl.BlockSpec((None, None, D_MODEL, FFN_TF), lambda bi, i, j: (layer, idx, 0, j)),
            pl.BlockSpec((None, None, FFN_TF, D_MODEL), lambda bi, i, j: (layer, idx, j, 0)),
            pl.BlockSpec((1, D_MODEL), const),
        ],
        out_specs=pl.BlockSpec((None, FFN_TM, D_MODEL), tile),
        out_shape=jax.ShapeDtypeStruct((b, s, D_MODEL), F32),
        scratch_shapes=[pltpu.VMEM((FFN_TM, D_MODEL), BF16), pltpu.VMEM((FFN_TM, D_MODEL), F32),
                        pltpu.SemaphoreType.DMA(())],
        compiler_params=_params("arbitrary", "arbitrary", "arbitrary"),
        name="ffn_final" if final_norm else "ffn",
    )(x, gain.reshape(1, D_MODEL), w_gate, w_up, w_down, gain_out.reshape(1, D_MODEL))


def _head_store(z, out_ref, head0, gain=None):
    for hd in range(PROJ_TN // HEAD_DIM):
        zh = z[:, hd * HEAD_DIM:(hd + 1) * HEAD_DIM]
        if gain is not None:
            zh = _rms(zh, gain)
        out_ref[head0 + hd] = zh.astype(out_ref.dtype)


def _store_by_residue(z, out_ref, stage_ref, dil, gain=None):
    tm = z.shape[0]
    for hd in range(z.shape[1] // HEAD_DIM):
        cols = slice(hd * HEAD_DIM, (hd + 1) * HEAD_DIM)
        zh = z[:, cols]
        if gain is not None:
            zh = _rms(zh, gain)
        if dil == 1:
            out_ref[hd, 0] = zh.astype(BF16)
        else:
            stage_ref[hd] = zh
            for r in range(dil):
                out_ref[hd, r] = stage_ref[hd, pl.ds(r, tm // dil, stride=dil), :].astype(BF16)


def _even_in_kernel(x_ref, g_ref, w_hbm, qg_ref, kg_ref, a_ref, *rest):
    n_groups = len(DIL_PAIRS)
    q_refs, k_refs, v_refs = rest[:n_groups], rest[n_groups:2 * n_groups], rest[2 * n_groups:3 * n_groups]
    h_ref, stage_ref, w_ref, wstage_ref, w_sems = rest[3 * n_groups:]

    def normalise():
        h_ref[...] = _rms(x_ref[...], g_ref[...]).astype(BF16)

    def store_a(z):
        a_ref[...] = z

    jobs = [(0, store_a)]
    for g, (_, dil) in enumerate(DIL_PAIRS):
        jobs.append((1 + g, functools.partial(
            _store_by_residue, out_ref=q_refs[g], stage_ref=stage_ref, dil=dil, gain=qg_ref[...])))
        jobs.append((1 + n_groups + g, functools.partial(
            _store_by_residue, out_ref=k_refs[g], stage_ref=stage_ref, dil=dil, gain=kg_ref[...])))
        jobs.append((1 + 2 * n_groups + g, functools.partial(
            _store_by_residue, out_ref=v_refs[g], stage_ref=stage_ref, dil=dil)))
    _project_in_chunks(normalise, h_ref, w_hbm, w_ref, wstage_ref, w_sems, jobs)


def _even_in(x, gain, w_in, q_gain, k_gain):
    b, s, _ = x.shape
    n_in = w_in.shape[1]
    assert PROJ_TN == DIL_OUT
    row = lambda bi, i: (bi, i, 0)
    const = lambda bi, i: (0, 0)
    hpg = DIL_HEADS_PER_GROUP
    group_specs = [pl.BlockSpec((None, hpg, dil, PROJ_TM // dil, HEAD_DIM), lambda bi, i: (bi, 0, 0, i, 0))
                   for _, dil in DIL_PAIRS]
    group_shapes = [jax.ShapeDtypeStruct((b, hpg, dil, s // dil, HEAD_DIM), BF16) for _, dil in DIL_PAIRS]
    outs = pl.pallas_call(
        _even_in_kernel,
        grid=(b, s // PROJ_TM),
        in_specs=[
            pl.BlockSpec((None, PROJ_TM, D_MODEL), row),
            pl.BlockSpec((1, D_MODEL), const),
            pl.BlockSpec(memory_space=pl.ANY),
            pl.BlockSpec((1, HEAD_DIM), const),
            pl.BlockSpec((1, HEAD_DIM), const),
        ],
        out_specs=[pl.BlockSpec((None, PROJ_TM, POOL_WIDTH), row)] + group_specs * 3,
        out_shape=[jax.ShapeDtypeStruct((b, s, POOL_WIDTH), F32)] + group_shapes * 3,
        scratch_shapes=[pltpu.VMEM((PROJ_TM, D_MODEL), BF16),
                        pltpu.VMEM((DIL_HEADS_PER_GROUP, PROJ_TM, HEAD_DIM), F32)]
        + _weight_scratch(D_MODEL, n_in),
        compiler_params=_params("arbitrary", "arbitrary"),
        name="even_in",
    )(x, gain.reshape(1, D_MODEL), w_in,
      (q_gain * HEAD_DIM ** -0.5).reshape(1, HEAD_DIM), k_gain.reshape(1, HEAD_DIM))
    n_groups = len(DIL_PAIRS)
    return outs[0], outs[1:1 + n_groups], outs[1 + n_groups:1 + 2 * n_groups], outs[1 + 2 * n_groups:]


def _odd_in_kernel(x_ref, g_ref, w_hbm, qg_ref, kg_ref, vg_ref, q_ref, k_ref, v_ref, u_ref, vn_ref,
                   h_ref, gv_ref, w_ref, wstage_ref, w_sems):
    per = NA_WIDTH // PROJ_TN
    per_sg = SG_WIDTH // PROJ_TN
    heads_per_chunk = PROJ_TN // HEAD_DIM

    def normalise():
        h_ref[...] = _rms(x_ref[...], g_ref[...]).astype(BF16)

    def store_u(z, c):
        u_ref[:, c * PROJ_TN:(c + 1) * PROJ_TN] = jax.nn.gelu(z)

    def store_gated(z, c):
        gv_ref[:, c * PROJ_TN:(c + 1) * PROJ_TN] = jax.nn.gelu(z)
        if c == per_sg - 1:
            vn_ref[...] = _rms(gv_ref[...], vg_ref[...]).astype(BF16)

    jobs = []
    for c in range(per):
        jobs.append((c, functools.partial(_head_store, out_ref=q_ref, head0=c * heads_per_chunk, gain=qg_ref[...])))
        jobs.append((per + c, functools.partial(_head_store, out_ref=k_ref, head0=c * heads_per_chunk,
                                                gain=kg_ref[...])))
        jobs.append((2 * per + c, functools.partial(_head_store, out_ref=v_ref, head0=c * heads_per_chunk)))
    for c in range(per_sg):
        jobs.append((3 * per + c, functools.partial(store_u, c=c)))
        jobs.append((3 * per + per_sg + c, functools.partial(store_gated, c=c)))
    _project_in_chunks(normalise, h_ref, w_hbm, w_ref, wstage_ref, w_sems, jobs)


def _odd_in(x, gain, w_in, q_gain, k_gain, v_gain):
    b, s, _ = x.shape
    n_in = w_in.shape[1]
    row = lambda bi, i: (bi, i, 0)
    by_head = lambda bi, i: (bi, 0, i, 0)
    const = lambda bi, i: (0, 0)
    return pl.pallas_call(
        _odd_in_kernel,
        grid=(b, s // PROJ_TM),
        in_specs=[
            pl.BlockSpec((None, PROJ_TM, D_MODEL), row),
            pl.BlockSpec((1, D_MODEL), const),
            pl.BlockSpec(memory_space=pl.ANY),
            pl.BlockSpec((1, HEAD_DIM), const),
            pl.BlockSpec((1, HEAD_DIM), const),
            pl.BlockSpec((1, SG_WIDTH), const),
        ],
        out_specs=[
            pl.BlockSpec((None, NA_HEADS, PROJ_TM, HEAD_DIM), by_head),
            pl.BlockSpec((None, NA_HEADS, PROJ_TM, HEAD_DIM), by_head),
            pl.BlockSpec((None, NA_HEADS, PROJ_TM, HEAD_DIM), by_head),
            pl.BlockSpec((None, PROJ_TM, SG_WIDTH), row),
            pl.BlockSpec((None, PROJ_TM, SG_WIDTH), row),
        ],
        out_shape=[
            jax.ShapeDtypeStruct((b, NA_HEADS, s, HEAD_DIM), BF16),
            jax.ShapeDtypeStruct((b, NA_HEADS, s, HEAD_DIM), BF16),
            jax.ShapeDtypeStruct((b, NA_HEADS, s, HEAD_DIM), BF16),
            jax.ShapeDtypeStruct((b, s, SG_WIDTH), F32),
            jax.ShapeDtypeStruct((b, s, SG_WIDTH), BF16),
        ],
        scratch_shapes=[pltpu.VMEM((PROJ_TM, D_MODEL), BF16), pltpu.VMEM((PROJ_TM, SG_WIDTH), F32)]
        + _weight_scratch(D_MODEL, n_in),
        compiler_params=_params("arbitrary", "arbitrary"),
        name="odd_in",
    )(x, gain.reshape(1, D_MODEL), w_in,
      (q_gain * HEAD_DIM ** -0.5).reshape(1, HEAD_DIM), k_gain.reshape(1, HEAD_DIM),
      v_gain.reshape(1, SG_WIDTH))


def _dilated_kernel(slopes_ref, q0_ref, q1_ref, q2_ref, k0_ref, k1_ref, k2_ref, v0_ref, v1_ref, v2_ref,
                    o_ref, on_ref, ls_ref, s_ref, *, seq):
    hh = pl.program_id(1)
    step = pl.program_id(2)
    q_refs = (q0_ref, q1_ref, q2_ref)
    k_refs = (k0_ref, k1_ref, k2_ref)
    v_refs = (v0_ref, v1_ref, v2_ref)
    for g, (window, dil) in enumerate(DIL_PAIRS):
        assert window // (2 * dil) == DIL_RADIUS
        q_ref, k_ref, v_ref = q_refs[g], k_refs[g], v_refs[g]
        sub_len = seq // dil
        per_res = DIL_TA // dil
        nq = min(per_res, DIL_TQ)
        n_sub = per_res // nq
        n_keys = min(nq + 2 * DIL_RADIUS, sub_len)
        slope = slopes_ref[g * DIL_HEADS_PER_GROUP + hh] * dil
        base = (lax.broadcasted_iota(jnp.int32, (nq, n_keys), 1)
                - lax.broadcasted_iota(jnp.int32, (nq, n_keys), 0))

        def place(t, nq=nq, n_sub=n_sub, per_res=per_res, sub_len=sub_len, n_keys=n_keys):
            r = t // n_sub
            q_off = pl.multiple_of((t % n_sub) * nq, nq)
            q_pos = step * per_res + q_off
            k_pos = pl.multiple_of(jnp.clip(q_pos - DIL_RADIUS, 0, sub_len - n_keys), DIL_RADIUS)
            return r, q_off, q_pos, k_pos

        def scores(t, slot, q_ref=q_ref, k_ref=k_ref, nq=nq, n_keys=n_keys, slope=slope, base=base,
                   place=place):
            r, q_off, q_pos, k_pos = place(t)
            q = q_ref[r, pl.ds(q_off, nq), :]
            k = k_ref[r, pl.ds(k_pos, n_keys), :]
            s = lax.dot_general(q, k, (((1,), (1,)), ((), ())), preferred_element_type=F32)
            dist = jnp.abs(base + (k_pos - q_pos))
            s_ref[slot, 0:nq, 0:n_keys] = jnp.where(dist <= DIL_RADIUS, s - slope * dist.astype(F32), NEG_INF)

        def finish(t, slot, v_ref=v_ref, g=g, dil=dil, nq=nq, n_keys=n_keys, place=place):
            r, q_off, _, k_pos = place(t)
            v = v_ref[r, pl.ds(k_pos, n_keys), :]
            s = s_ref[slot, 0:nq, 0:n_keys]
            m = jnp.max(s, axis=-1, keepdims=True)
            p = jnp.exp(s - m)
            den = jnp.sum(p, axis=-1, keepdims=True)
            out = jnp.dot(p.astype(BF16), v, preferred_element_type=F32) / den
            lse = jnp.broadcast_to(m + jnp.log(den), (nq, HEAD_DIM))
            if dil == 1:
                rows = pl.ds(q_off, nq)
            else:
                rows = pl.ds(r + dil * q_off, nq, stride=dil)
            on_ref[g, rows, :] = out
            ls_ref[g, rows, :] = lse

        n_tiles = dil * n_sub
        assert n_tiles % DIL_UNROLL == 0

        def body(i, carry, scores=scores, finish=finish):
            t0 = i * DIL_UNROLL
            scores(t0, 0)
            for u in range(DIL_UNROLL):
                if u + 1 < DIL_UNROLL:
                    scores(t0 + u + 1, (u + 1) % 2)
                finish(t0 + u, u % 2)
            return carry

        lax.fori_loop(0, n_tiles // DIL_UNROLL, body, 0)

    def merge(c, carry):
        rows = pl.ds(pl.multiple_of(c * DIL_TQ, DIL_TQ), DIL_TQ)
        lse = [ls_ref[g, rows, :] for g in range(len(DIL_PAIRS))]
        top = jnp.maximum(jnp.maximum(lse[0], lse[1]), lse[2])
        num = jnp.zeros((DIL_TQ, HEAD_DIM), F32)
        den = jnp.zeros((DIL_TQ, HEAD_DIM), F32)
        for g in range(len(DIL_PAIRS)):
            w = jnp.exp(lse[g] - top)
            num = num + w * on_ref[g, rows, :]
            den = den + w
        o_ref[rows, :] = (num / den).astype(o_ref.dtype)
        return carry

    lax.fori_loop(0, DIL_TA // DIL_TQ, merge, 0)


def _dilated(q, k, v, slopes):
    b, _, dil0, s, _ = q[0].shape
    assert dil0 == 1
    n_groups = len(DIL_PAIRS)

    def q_spec(dil):
        return pl.BlockSpec((None, None, dil, DIL_TA // dil, HEAD_DIM), lambda bi, hh, i: (bi, hh, 0, i, 0))

    def kv_spec(dil):
        return pl.BlockSpec((None, None, dil, s // dil, HEAD_DIM), lambda bi, hh, i: (bi, hh, 0, 0, 0))

    dils = [dil for _, dil in DIL_PAIRS]
    return pl.pallas_call(
        functools.partial(_dilated_kernel, seq=s),
        grid=(b, DIL_HEADS_PER_GROUP, s // DIL_TA),
        in_specs=[pl.BlockSpec(memory_space=pltpu.SMEM)]
        + [q_spec(d) for d in dils] + [kv_spec(d) for d in dils] + [kv_spec(d) for d in dils],
        out_specs=pl.BlockSpec((None, None, DIL_TA, HEAD_DIM), lambda bi, hh, i: (bi, hh, i, 0)),
        out_shape=jax.ShapeDtypeStruct((b, DIL_HEADS_PER_GROUP, s, HEAD_DIM), BF16),
        scratch_shapes=[pltpu.VMEM((n_groups, DIL_TA, HEAD_DIM), F32),
                        pltpu.VMEM((n_groups, DIL_TA, HEAD_DIM), F32),
                        pltpu.VMEM((2, DIL_TQ, DIL_TQ + 2 * DIL_RADIUS), F32)],
        compiler_params=_params("parallel", "parallel", "arbitrary"),
        name="dilated_attn",
    )(slopes, *q, *k, *v)


def _even_out_kernel(a_ref, prev_ref, next_ref, yb_ref, pw_ref, ps_ref, wo_hbm, x_ref, o_ref,
                     ext_ref, y_ref, wo_ref, wstage_ref, w_sems, *, seq):
    pl.when(_first_grid_step())(functools.partial(_load_weight_as_bf16, wo_hbm, wo_ref, wstage_ref, w_sems))
    i = pl.program_id(1)
    tm = OUT_TM
    ext_ref[0:POOL_HALO, :] = jnp.where(i == 0, 0.0, prev_ref[...])
    ext_ref[POOL_HALO:POOL_HALO + tm, :] = a_ref[...]
    ext_ref[POOL_HALO + tm:, :] = jnp.where(i == pl.num_programs(1) - 1, 0.0, next_ref[...])
    pos = i * tm + lax.broadcasted_iota(jnp.int32, (tm, 1), 0)
    for g, window in enumerate(POOL_WINDOWS):
        half = window // 2
        cols = slice(g * HEAD_DIM, (g + 1) * HEAD_DIM)
        total = ext_ref[POOL_HALO - half:POOL_HALO - half + tm, cols]
        for shift in range(-half + 1, half):
            total = total + ext_ref[POOL_HALO + shift:POOL_HALO + shift + tm, cols]
        count = (jnp.minimum(pos + half, seq) - jnp.maximum(pos - half, 0)).astype(F32)
        pooled = total / count - a_ref[:, cols]
        ya = jnp.dot(pooled.astype(BF16), pw_ref[g], preferred_element_type=F32) * ps_ref[:, cols]
        y_ref[:, cols] = ya.astype(BF16)
    for hd in range(DIL_HEADS_PER_GROUP):
        y_ref[:, POOL_WIDTH + hd * HEAD_DIM:POOL_WIDTH + (hd + 1) * HEAD_DIM] = yb_ref[hd]
    o_ref[...] = x_ref[...] + jnp.dot(y_ref[...], wo_ref[...], preferred_element_type=F32)


def _even_out(a, yb, pool_w, pool_scale, w_out, x):
    b, s, _ = x.shape
    tm = OUT_TM
    halo_blocks = tm // POOL_HALO
    tile = lambda bi, i: (bi, i, 0)
    return pl.pallas_call(
        functools.partial(_even_out_kernel, seq=s),
        grid=(b, s // tm),
        in_specs=[
            pl.BlockSpec((None, tm, POOL_WIDTH), tile),
            pl.BlockSpec((None, POOL_HALO, POOL_WIDTH),
                         lambda bi, i: (bi, jnp.maximum(i * halo_blocks - 1, 0), 0)),
            pl.BlockSpec((None, POOL_HALO, POOL_WIDTH),
                         lambda bi, i: (bi, jnp.minimum((i + 1) * halo_blocks, s // POOL_HALO - 1), 0)),
            pl.BlockSpec((None, DIL_HEADS_PER_GROUP, tm, HEAD_DIM), lambda bi, i: (bi, 0, i, 0)),
            _resident((len(POOL_WINDOWS), HEAD_DIM, HEAD_DIM), lambda bi, i: (0, 0, 0)),
            pl.BlockSpec((1, POOL_WIDTH), lambda bi, i: (0, 0)),
            pl.BlockSpec(memory_space=pl.ANY),
            pl.BlockSpec((None, tm, D_MODEL), tile),
        ],
        out_specs=pl.BlockSpec((None, tm, D_MODEL), tile),
        out_shape=jax.ShapeDtypeStruct((b, s, D_MODEL), F32),
        scratch_shapes=[pltpu.VMEM((tm + 2 * POOL_HALO, POOL_WIDTH), F32),
                        pltpu.VMEM((tm, POOL_WIDTH + DIL_OUT), BF16)]
        + _weight_scratch(POOL_WIDTH + DIL_OUT, D_MODEL),
        compiler_params=_params("arbitrary", "arbitrary"),
        name="even_out",
    )(a, a, a, yb, pool_w, pool_scale.reshape(1, POOL_WIDTH), w_out, x)


def _na_fill_bias(pair_ref, bias_ref, rb, rows):
    k_row0 = min(max(rb * NA_QROWS - NA_ROWS // 2, 0), rows - NA_KROWS)
    left_half = lax.broadcasted_iota(jnp.int32, (GRID_W, 2 * GRID_W), 1) < GRID_W
    for qr in range(NA_QROWS):
        r = rb * NA_QROWS + qr
        row_start = min(max(r - NA_ROWS // 2, 0), rows - NA_ROWS)
        for p in range(NA_KROWS // 2):
            kr = k_row0 + 2 * p
            ok_left = row_start <= kr < row_start + NA_ROWS
            ok_right = row_start <= kr + 1 < row_start + NA_ROWS
            if ok_left or ok_right:
                block = pair_ref[kr + 1 - r + NA_ROWS - 1]
                if not ok_right:
                    block = jnp.where(left_half, block, NEG_INF)
                if not ok_left:
                    block = jnp.where(left_half, NEG_INF, block)
            else:
                block = jnp.full((GRID_W, 2 * GRID_W), NEG_INF, F32)
            bias_ref[qr * GRID_W:(qr + 1) * GRID_W, p * 2 * GRID_W:(p + 1) * 2 * GRID_W] = block


def _na_kernel(q_ref, k_ref, v_ref, pair_ref, o_ref, bias_ref, s_ref, *, rows):
    n_blocks = rows // NA_QROWS
    tq = NA_QROWS * GRID_W
    n_keys = NA_KROWS * GRID_W
    kinds = (0, 1, n_blocks - 1)
    for slot, kind in enumerate(kinds):
        _na_fill_bias(pair_ref, bias_ref.at[slot], kind, rows)

    def key_start(rb):
        return min(max(rb * NA_QROWS - NA_ROWS // 2, 0), rows - NA_KROWS) * GRID_W

    def scores(rb):
        kind_slot = 0 if rb == 0 else (2 if rb == n_blocks - 1 else 1)
        q = q_ref[rb * tq:(rb + 1) * tq, :]
        k = k_ref[key_start(rb):key_start(rb) + n_keys, :]
        s_ref[rb % 2] = (lax.dot_general(q, k, (((1,), (1,)), ((), ())), preferred_element_type=F32)
                         + bias_ref[kind_slot])

    def finish(rb):
        v = v_ref[key_start(rb):key_start(rb) + n_keys, :]
        s = s_ref[rb % 2]
        m = jnp.max(s, axis=-1, keepdims=True)
        p = jnp.exp(s - m)
        den = jnp.sum(p, axis=-1, keepdims=True)
        out = jnp.dot(p.astype(BF16), v, preferred_element_type=F32) / den
        o_ref[rb * tq:(rb + 1) * tq, :] = out.astype(o_ref.dtype)

    scores(0)
    for rb in range(n_blocks):
        if rb + 1 < n_blocks:
            scores(rb + 1)
        finish(rb)


def _na_pair_tables(rpb):
    n_heads, n_rel_rows, n_rel_cols = rpb.shape
    c = np.arange(GRID_W)
    col_start = np.clip(c - NA_COLS // 2, 0, GRID_W - NA_COLS)
    col_ok = (c[None, :] >= col_start[:, None]) & (c[None, :] < col_start[:, None] + NA_COLS)
    rel_c = np.clip(c[None, :] - c[:, None], -(NA_COLS - 1), NA_COLS - 1) + (NA_COLS - 1)
    pick = (rel_c.reshape(1, -1) == np.arange(n_rel_cols)[:, None]).astype(np.float32)
    by_col = jnp.dot(rpb.astype(F32).reshape(n_heads * n_rel_rows, n_rel_cols), pick,
                     precision=lax.Precision.HIGHEST).reshape(n_heads, n_rel_rows, GRID_W, GRID_W)
    by_col = jnp.where(col_ok[None, None], by_col, NEG_INF)
    masked = jnp.full((n_heads, 1, GRID_W, GRID_W), NEG_INF, F32)
    padded = jnp.concatenate([masked, by_col, masked], axis=1)
    return jnp.concatenate([padded[:, :-1], padded[:, 1:]], axis=-1)


def _neighbourhood(q, k, v, pair_tables):
    b, _, s, _ = q.shape
    rows = s // GRID_W
    n_blocks = rows // NA_QROWS
    assert n_blocks >= 3 and rows >= NA_KROWS
    tq = NA_QROWS * GRID_W
    n_keys = NA_KROWS * GRID_W
    head = lambda bi, h: (bi, h, 0, 0)
    return pl.pallas_call(
        functools.partial(_na_kernel, rows=rows),
        grid=(b, NA_HEADS),
        in_specs=[
            pl.BlockSpec((None, None, s, HEAD_DIM), head),
            pl.BlockSpec((None, None, s, HEAD_DIM), head),
            pl.BlockSpec((None, None, s, HEAD_DIM), head),
            pl.BlockSpec((None, 2 * NA_ROWS, GRID_W, 2 * GRID_W), lambda bi, h: (h, 0, 0, 0)),
        ],
        out_specs=pl.BlockSpec((None, None, s, HEAD_DIM), head),
        out_shape=jax.ShapeDtypeStruct((b, NA_HEADS, s, HEAD_DIM), BF16),
        scratch_shapes=[pltpu.VMEM((3, tq, n_keys), F32), pltpu.VMEM((2, tq, n_keys), F32)],
        compiler_params=_params("parallel", "parallel"),
        name="neighbourhood_attn",
    )(q, k, v, pair_tables)


def _odd_out_kernel(yc_ref, u_ref, vn_ref, ws_ref, bs_ref, wo_hbm, x_ref, o_ref, y_ref,
                    wo_ref, wstage_ref, w_sems):
    pl.when(_first_grid_step())(functools.partial(_load_weight_as_bf16, wo_hbm, wo_ref, wstage_ref, w_sems))
    width = SG_WIDTH // SG_GROUPS
    for c in range(OUT_TM // SG_CHUNK):
        rows = slice(c * SG_CHUNK, (c + 1) * SG_CHUNK)
        for g in range(SG_GROUPS):
            cols = slice(g * width, (g + 1) * width)
            sv = jnp.dot(ws_ref[g], vn_ref[rows, cols], preferred_element_type=F32) + bs_ref[g]
            y_ref[rows, NA_WIDTH + g * width:NA_WIDTH + (g + 1) * width] = (u_ref[rows, cols] * sv).astype(BF16)
    for hd in range(NA_HEADS):
        y_ref[:, hd * HEAD_DIM:(hd + 1) * HEAD_DIM] = yc_ref[hd]
    o_ref[...] = x_ref[...] + jnp.dot(y_ref[...], wo_ref[...], preferred_element_type=F32)


def _odd_out(yc, u, vn, w_s, b_s, w_out, x):
    b, s, _ = x.shape
    tm = OUT_TM
    width = SG_WIDTH // SG_GROUPS
    tile = lambda bi, i: (bi, i, 0)
    return pl.pallas_call(
        _odd_out_kernel,
        grid=(b, s // tm),
        in_specs=[
            pl.BlockSpec((None, NA_HEADS, tm, HEAD_DIM), lambda bi, i: (bi, 0, i, 0)),
            pl.BlockSpec((None, tm, SG_WIDTH), tile),
            pl.BlockSpec((None, tm, SG_WIDTH), tile),
            _resident((SG_GROUPS, SG_CHUNK, SG_CHUNK), lambda bi, i: (0, 0, 0)),
            _resident((SG_GROUPS, SG_CHUNK, width), lambda bi, i: (0, 0, 0)),
            pl.BlockSpec(memory_space=pl.ANY),
            pl.BlockSpec((None, tm, D_MODEL), tile),
        ],
        out_specs=pl.BlockSpec((None, tm, D_MODEL), tile),
        out_shape=jax.ShapeDtypeStruct((b, s, D_MODEL), F32),
        scratch_shapes=[pltpu.VMEM((tm, NA_WIDTH + SG_WIDTH), BF16)]
        + _weight_scratch(NA_WIDTH + SG_WIDTH, D_MODEL),
        compiler_params=_params("arbitrary", "arbitrary"),
        name="odd_out",
    )(yc, u, vn, w_s, b_s, w_out, x)


def kernel(x, norm_ffn1, norm_mix, norm_ffn2, norm_out, ffn_w_gate, ffn_w_up, ffn_w_down, even_w_in, pool_w, pool_scale, dil_q_gain, dil_k_gain, even_w_out, odd_w_in, na_q_gain, na_k_gain, na_rpb, sg_v_gain, sg_w, sg_b, odd_w_out):
    depth = norm_ffn1.shape[0]
    slopes = jnp.asarray(2.0 ** (-8.0 * np.arange(1, DIL_HEADS + 1) / DIL_HEADS), dtype=F32)
    width = SG_WIDTH // SG_GROUPS

    for layer in range(depth):
        x = _ffn(x, norm_ffn1[layer], ffn_w_gate, ffn_w_up, ffn_w_down, layer, 0)
        if layer % 2 == 0:
            e = layer // 2
            a, q, k, v = _even_in(x, norm_mix[layer], even_w_in[e], dil_q_gain[e], dil_k_gain[e])
            yb = _dilated(q, k, v, slopes)
            x = _even_out(a, yb, pool_w[e].astype(BF16), pool_scale[e], even_w_out[e], x)
        else:
            o = layer // 2
            q, k, v, u, vn = _odd_in(x, norm_mix[layer], odd_w_in[o],
                                     na_q_gain[o], na_k_gain[o], sg_v_gain[o])
            yc = _neighbourhood(q, k, v, _na_pair_tables(na_rpb[o]))
            b_s = jnp.broadcast_to(sg_b[o][:, :, None], (SG_GROUPS, SG_CHUNK, width))
            x = _odd_out(yc, u, vn, sg_w[o].astype(BF16), b_s, odd_w_out[o], x)
        x = _ffn(x, norm_ffn2[layer], ffn_w_gate, ffn_w_up, ffn_w_down, layer, 1, gain_out=norm_out[layer])
    return x
```

```python
import functools

import numpy as np
import jax
import jax.numpy as jnp
from jax import lax
from jax.experimental import pallas as pl
from jax.experimental.pallas import tpu as pltpu

F32 = jnp.float32
BF16 = jnp.bfloat16

D_MODEL = 2048
D_FF = 5632
HEAD_DIM = 128
POOL_WINDOWS = (2, 4, 8, 16)
POOL_WIDTH = 512
POOL_HALO = 8
DIL_PAIRS = ((128, 1), (512, 4), (2048, 16))
DIL_HEADS_PER_GROUP = 4
DIL_HEADS = 12
DIL_WIDTH = DIL_HEADS * HEAD_DIM
DIL_OUT = DIL_HEADS_PER_GROUP * HEAD_DIM
GRID_W = 64
NA_ROWS = 8
NA_COLS = 16
NA_HEADS = 8
NA_WIDTH = NA_HEADS * HEAD_DIM
NA_QROWS = 4
NA_KROWS = NA_QROWS + NA_ROWS
SG_CHUNK = 128
SG_GROUPS = 8
SG_WIDTH = 1024
RMS_EPS = 1e-6
NEG_INF = -1e30

VMEM_LIMIT_BYTES = 60 * 1024 * 1024

FFN_TM = 1024
FFN_TF = 512
FFN_TC = 256
FFN_TR = 256
PROJ_TM = 512
PROJ_TN = 512
DIL_TQ = 128
DIL_TA = 2048
DIL_RADIUS = 64
DIL_UNROLL = 16
OUT_TM = 512
W_CHUNK = 512


def _params(*semantics):
    return pltpu.CompilerParams(dimension_semantics=semantics, vmem_limit_bytes=VMEM_LIMIT_BYTES)


def _resident(block_shape, index_map):
    return pl.BlockSpec(block_shape, index_map, pipeline_mode=pl.Buffered(1))


def _rms(x, gain):
    return x * lax.rsqrt(jnp.mean(x * x, axis=-1, keepdims=True) + RMS_EPS) * gain


def _first_grid_step():
    return (pl.program_id(0) == 0) & (pl.program_id(1) == 0)


def _load_weight_as_bf16(w_hbm, wbf_ref, stage_ref, sems):
    n_chunks = wbf_ref.shape[1] // W_CHUNK

    def copy(c):
        cols = pl.ds(c * W_CHUNK, W_CHUNK)
        return pltpu.make_async_copy(w_hbm.at[:, cols], stage_ref.at[c % 2], sems.at[c % 2])

    copy(0).start()
    for c in range(n_chunks):
        if c + 1 < n_chunks:
            copy(c + 1).start()
        copy(c).wait()
        wbf_ref[:, c * W_CHUNK:(c + 1) * W_CHUNK] = stage_ref[c % 2].astype(BF16)


def _weight_scratch(rows, cols):
    assert cols % W_CHUNK == 0
    return [pltpu.VMEM((rows, cols), BF16), pltpu.VMEM((2, rows, W_CHUNK), F32), pltpu.SemaphoreType.DMA((2,))]


def _ffn_kernel(x_hbm, g_ref, wg_ref, wu_ref, wd_ref, gout_ref, o_ref, h_ref, xbuf_ref, x_sem, *, final_norm):
    j = pl.program_id(2)
    tiles_per_batch = pl.num_programs(1)
    n_tiles = pl.num_programs(0) * tiles_per_batch
    tile = pl.program_id(0) * tiles_per_batch + pl.program_id(1)

    def x_copy(t):
        rows = pl.ds(pl.multiple_of((t % tiles_per_batch) * FFN_TM, FFN_TM), FFN_TM)
        return pltpu.make_async_copy(x_hbm.at[t // tiles_per_batch, rows, :], xbuf_ref, x_sem)

    @pl.when((j == 0) & (tile == 0))
    def _():
        x_copy(tile).start()

    @pl.when((j == 1) & (tile + 1 < n_tiles))
    def _():
        x_copy(tile + 1).start()

    def step(first):
        if first:
            x_copy(tile).wait()
        for c in range(FFN_TF // FFN_TC):
            cols = slice(c * FFN_TC, (c + 1) * FFN_TC)
            wg = wg_ref[:, cols].astype(BF16)
            wu = wu_ref[:, cols].astype(BF16)
            if first and c == 0:
                gates, ups = [], []
                for r in range(FFN_TM // FFN_TR):
                    rows = slice(r * FFN_TR, (r + 1) * FFN_TR)
                    x = xbuf_ref[rows, :]
                    h_ref[rows, :] = _rms(x, g_ref[...]).astype(BF16)
                    o_ref[rows, :] = x
                    gates.append(jnp.dot(h_ref[rows, :], wg, preferred_element_type=F32))
                    ups.append(jnp.dot(h_ref[rows, :], wu, preferred_element_type=F32))
                gate = jnp.concatenate(gates, axis=0)
                up = jnp.concatenate(ups, axis=0)
            else:
                h = h_ref[...]
                gate = jnp.dot(h, wg, preferred_element_type=F32)
                up = jnp.dot(h, wu, preferred_element_type=F32)
            act = (gate * jax.nn.sigmoid(gate)) * up * 0.5
            o_ref[...] += jnp.dot(act.astype(BF16), wd_ref[cols, :].astype(BF16), preferred_element_type=F32)

    pl.when(j == 0)(functools.partial(step, True))
    pl.when(j != 0)(functools.partial(step, False))

    if final_norm:
        @pl.when(j == pl.num_programs(2) - 1)
        def _():
            o_ref[...] = _rms(o_ref[...], gout_ref[...])


def _ffn(x, gain, w_gate, w_up, w_down, layer, idx, gain_out=None):
    b, s, _ = x.shape
    assert D_FF // FFN_TF >= 2
    final_norm = gain_out is not None
    if gain_out is None:
        gain_out = gain
    tile = lambda bi, i, j: (bi, i, 0)
    const = lambda bi, i, j: (0, 0)
    return pl.pallas_call(
        functools.partial(_ffn_kernel, final_norm=final_norm),
        grid=(b, s // FFN_TM, D_FF // FFN_TF),
        in_specs=[
            pl.BlockSpec(memory_space=pl.ANY),
            pl.BlockSpec((1, D_MODEL), const),
            pl.BlockSpec((None, None, D_MODEL, FFN_TF), lambda bi, i, j: (layer, idx, 0, j)),
            pl.BlockSpec((None, None, D_MODEL, FFN_TF), lambda bi, i, j: (layer, idx, 0, j)),
            pl.BlockSpec((None, None, FFN_TF, D_MODEL), lambda bi, i, j: (layer, idx, j, 0)),
            pl.BlockSpec((1, D_MODEL), const),
        ],
        out_specs=pl.BlockSpec((None, FFN_TM, D_MODEL), tile),
        out_shape=jax.ShapeDtypeStruct((b, s, D_MODEL), F32),
        scratch_shapes=[pltpu.VMEM((FFN_TM, D_MODEL), BF16), pltpu.VMEM((FFN_TM, D_MODEL), F32),
                        pltpu.SemaphoreType.DMA(())],
        compiler_params=_params("arbitrary", "arbitrary", "arbitrary"),
        name="ffn_final" if final_norm else "ffn",
    )(x, gain.reshape(1, D_MODEL), w_gate, w_up, w_down, gain_out.reshape(1, D_MODEL))


def _head_store(z, out_ref, head0, gain=None):
    for hd in range(PROJ_TN // HEAD_DIM):
        zh = z[:, hd * HEAD_DIM:(hd + 1) * HEAD_DIM]
        if gain is not None:
            zh = _rms(zh, gain)
        out_ref[head0 + hd] = zh.astype(out_ref.dtype)


def _store_by_residue(z, out_ref, stage_ref, dil, gain=None):
    tm = z.shape[0]
    for hd in range(z.shape[1] // HEAD_DIM):
        cols = slice(hd * HEAD_DIM, (hd + 1) * HEAD_DIM)
        zh = z[:, cols]
        if gain is not None:
            zh = _rms(zh, gain)
        if dil == 1:
            out_ref[hd, 0] = zh.astype(BF16)
        else:
            stage_ref[hd] = zh
            for r in range(dil):
                out_ref[hd, r] = stage_ref[hd, pl.ds(r, tm // dil, stride=dil), :].astype(BF16)


def _even_in_kernel(x_ref, g_ref, w_hbm, qg_ref, kg_ref, a_ref, *rest):
    n_groups = len(DIL_PAIRS)
    q_refs, k_refs, v_refs = rest[:n_groups], rest[n_groups:2 * n_groups], rest[2 * n_groups:3 * n_groups]
    h_ref, stage_ref, w_ref, wstage_ref, w_sems = rest[3 * n_groups:]
    pl.when(_first_grid_step())(functools.partial(_load_weight_as_bf16, w_hbm, w_ref, wstage_ref, w_sems))
    h_ref[...] = _rms(x_ref[...], g_ref[...]).astype(BF16)

    def chunk(c):
        return jnp.dot(h_ref[...], w_ref[:, c * PROJ_TN:(c + 1) * PROJ_TN], preferred_element_type=F32)

    a_ref[...] = chunk(0)
    for g, (_, dil) in enumerate(DIL_PAIRS):
        _store_by_residue(chunk(1 + g), q_refs[g], stage_ref, dil, qg_ref[...])
        _store_by_residue(chunk(1 + n_groups + g), k_refs[g], stage_ref, dil, kg_ref[...])
        _store_by_residue(chunk(1 + 2 * n_groups + g), v_refs[g], stage_ref, dil)


def _even_in(x, gain, w_in, q_gain, k_gain):
    b, s, _ = x.shape
    n_in = w_in.shape[1]
    assert PROJ_TN == DIL_OUT
    row = lambda bi, i: (bi, i, 0)
    const = lambda bi, i: (0, 0)
    hpg = DIL_HEADS_PER_GROUP
    group_specs = [pl.BlockSpec((None, hpg, dil, PROJ_TM // dil, HEAD_DIM), lambda bi, i: (bi, 0, 0, i, 0))
                   for _, dil in DIL_PAIRS]
    group_shapes = [jax.ShapeDtypeStruct((b, hpg, dil, s // dil, HEAD_DIM), BF16) for _, dil in DIL_PAIRS]
    outs = pl.pallas_call(
        _even_in_kernel,
        grid=(b, s // PROJ_TM),
        in_specs=[
            pl.BlockSpec((None, PROJ_TM, D_MODEL), row),
            pl.BlockSpec((1, D_MODEL), const),
            pl.BlockSpec(memory_space=pl.ANY),
            pl.BlockSpec((1, HEAD_DIM), const),
            pl.BlockSpec((1, HEAD_DIM), const),
        ],
        out_specs=[pl.BlockSpec((None, PROJ_TM, POOL_WIDTH), row)] + group_specs * 3,
        out_shape=[jax.ShapeDtypeStruct((b, s, POOL_WIDTH), F32)] + group_shapes * 3,
        scratch_shapes=[pltpu.VMEM((PROJ_TM, D_MODEL), BF16),
                        pltpu.VMEM((DIL_HEADS_PER_GROUP, PROJ_TM, HEAD_DIM), F32)]
        + _weight_scratch(D_MODEL, n_in),
        compiler_params=_params("arbitrary", "arbitrary"),
        name="even_in",
    )(x, gain.reshape(1, D_MODEL), w_in,
      (q_gain * HEAD_DIM ** -0.5).reshape(1, HEAD_DIM), k_gain.reshape(1, HEAD_DIM))
    n_groups = len(DIL_PAIRS)
    return outs[0], outs[1:1 + n_groups], outs[1 + n_groups:1 + 2 * n_groups], outs[1 + 2 * n_groups:]


def _odd_in_kernel(x_ref, g_ref, w_hbm, qg_ref, kg_ref, vg_ref, q_ref, k_ref, v_ref, u_ref, vn_ref,
                   h_ref, gv_ref, w_ref, wstage_ref, w_sems):
    pl.when(_first_grid_step())(functools.partial(_load_weight_as_bf16, w_hbm, w_ref, wstage_ref, w_sems))
    h_ref[...] = _rms(x_ref[...], g_ref[...]).astype(BF16)

    def chunk(c):
        return jnp.dot(h_ref[...], w_ref[:, c * PROJ_TN:(c + 1) * PROJ_TN], preferred_element_type=F32)

    per = NA_WIDTH // PROJ_TN
    heads_per_chunk = PROJ_TN // HEAD_DIM
    for c in range(per):
        _head_store(chunk(c), q_ref, c * heads_per_chunk, qg_ref[...])
        _head_store(chunk(per + c), k_ref, c * heads_per_chunk, kg_ref[...])
        _head_store(chunk(2 * per + c), v_ref, c * heads_per_chunk)
    per_sg = SG_WIDTH // PROJ_TN
    for c in range(per_sg):
        u_ref[:, c * PROJ_TN:(c + 1) * PROJ_TN] = jax.nn.gelu(chunk(3 * per + c))
        gv_ref[:, c * PROJ_TN:(c + 1) * PROJ_TN] = jax.nn.gelu(chunk(3 * per + per_sg + c))
    vn_ref[...] = _rms(gv_ref[...], vg_ref[...]).astype(BF16)


def _odd_in(x, gain, w_in, q_gain, k_gain, v_gain):
    b, s, _ = x.shape
    n_in = w_in.shape[1]
    row = lambda bi, i: (bi, i, 0)
    by_head = lambda bi, i: (bi, 0, i, 0)
    const = lambda bi, i: (0, 0)
    return pl.pallas_call(
        _odd_in_kernel,
        grid=(b, s // PROJ_TM),
        in_specs=[
            pl.BlockSpec((None, PROJ_TM, D_MODEL), row),
            pl.BlockSpec((1, D_MODEL), const),
            pl.BlockSpec(memory_space=pl.ANY),
            pl.BlockSpec((1, HEAD_DIM), const),
            pl.BlockSpec((1, HEAD_DIM), const),
            pl.BlockSpec((1, SG_WIDTH), const),
        ],
        out_specs=[
            pl.BlockSpec((None, NA_HEADS, PROJ_TM, HEAD_DIM), by_head),
            pl.BlockSpec((None, NA_HEADS, PROJ_TM, HEAD_DIM), by_head),
            pl.BlockSpec((None, NA_HEADS, PROJ_TM, HEAD_DIM), by_head),
            pl.BlockSpec((None, PROJ_TM, SG_WIDTH), row),
            pl.BlockSpec((None, PROJ_TM, SG_WIDTH), row),
        ],
        out_shape=[
            jax.ShapeDtypeStruct((b, NA_HEADS, s, HEAD_DIM), BF16),
            jax.ShapeDtypeStruct((b, NA_HEADS, s, HEAD_DIM), BF16),
            jax.ShapeDtypeStruct((b, NA_HEADS, s, HEAD_DIM), BF16),
            jax.ShapeDtypeStruct((b, s, SG_WIDTH), F32),
            jax.ShapeDtypeStruct((b, s, SG_WIDTH), BF16),
        ],
        scratch_shapes=[pltpu.VMEM((PROJ_TM, D_MODEL), BF16), pltpu.VMEM((PROJ_TM, SG_WIDTH), F32)]
        + _weight_scratch(D_MODEL, n_in),
        compiler_params=_params("arbitrary", "arbitrary"),
        name="odd_in",
    )(x, gain.reshape(1, D_MODEL), w_in,
      (q_gain * HEAD_DIM ** -0.5).reshape(1, HEAD_DIM), k_gain.reshape(1, HEAD_DIM),
      v_gain.reshape(1, SG_WIDTH))


def _dilated_kernel(slopes_ref, q0_ref, q1_ref, q2_ref, k0_ref, k1_ref, k2_ref, v0_ref, v1_ref, v2_ref,
                    o_ref, on_ref, ls_ref, s_ref, *, seq):
    hh = pl.program_id(1)
    step = pl.program_id(2)
    q_refs = (q0_ref, q1_ref, q2_ref)
    k_refs = (k0_ref, k1_ref, k2_ref)
    v_refs = (v0_ref, v1_ref, v2_ref)
    for g, (window, dil) in enumerate(DIL_PAIRS):
        assert window // (2 * dil) == DIL_RADIUS
        q_ref, k_ref, v_ref = q_refs[g], k_refs[g], v_refs[g]
        sub_len = seq // dil
        per_res = DIL_TA // dil
        nq = min(per_res, DIL_TQ)
        n_sub = per_res // nq
        n_keys = min(nq + 2 * DIL_RADIUS, sub_len)
        slope = slopes_ref[g * DIL_HEADS_PER_GROUP + hh] * dil
        base = (lax.broadcasted_iota(jnp.int32, (nq, n_keys), 1)
                - lax.broadcasted_iota(jnp.int32, (nq, n_keys), 0))

        def place(t, nq=nq, n_sub=n_sub, per_res=per_res, sub_len=sub_len, n_keys=n_keys):
            r = t // n_sub
            q_off = pl.multiple_of((t % n_sub) * nq, nq)
            q_pos = step * per_res + q_off
            k_pos = pl.multiple_of(jnp.clip(q_pos - DIL_RADIUS, 0, sub_len - n_keys), DIL_RADIUS)
            return r, q_off, q_pos, k_pos

        def scores(t, slot, q_ref=q_ref, k_ref=k_ref, nq=nq, n_keys=n_keys, slope=slope, base=base,
                   place=place):
            r, q_off, q_pos, k_pos = place(t)
            q = q_ref[r, pl.ds(q_off, nq), :]
            k = k_ref[r, pl.ds(k_pos, n_keys), :]
            s = lax.dot_general(q, k, (((1,), (1,)), ((), ())), preferred_element_type=F32)
            dist = jnp.abs(base + (k_pos - q_pos))
            s_ref[slot, 0:nq, 0:n_keys] = jnp.where(dist <= DIL_RADIUS, s - slope * dist.astype(F32), NEG_INF)

        def finish(t, slot, v_ref=v_ref, g=g, dil=dil, nq=nq, n_keys=n_keys, place=place):
            r, q_off, _, k_pos = place(t)
            v = v_ref[r, pl.ds(k_pos, n_keys), :]
            s = s_ref[slot, 0:nq, 0:n_keys]
            m = jnp.max(s, axis=-1, keepdims=True)
            p = jnp.exp(s - m)
            den = jnp.sum(p, axis=-1, keepdims=True)
            out = jnp.dot(p.astype(BF16), v, preferred_element_type=F32) / den
            lse = jnp.broadcast_to(m + jnp.log(den), (nq, HEAD_DIM))
            if dil == 1:
                rows = pl.ds(q_off, nq)
            else:
                rows = pl.ds(r + dil * q_off, nq, stride=dil)
            on_ref[g, rows, :] = out
            ls_ref[g, rows, :] = lse

        n_tiles = dil * n_sub
        assert n_tiles % DIL_UNROLL == 0

        def body(i, carry, scores=scores, finish=finish):
            t0 = i * DIL_UNROLL
            scores(t0, 0)
            for u in range(DIL_UNROLL):
                if u + 1 < DIL_UNROLL:
                    scores(t0 + u + 1, (u + 1) % 2)
                finish(t0 + u, u % 2)
            return carry

        lax.fori_loop(0, n_tiles // DIL_UNROLL, body, 0)

    def merge(c, carry):
        rows = pl.ds(pl.multiple_of(c * DIL_TQ, DIL_TQ), DIL_TQ)
        lse = [ls_ref[g, rows, :] for g in range(len(DIL_PAIRS))]
        top = jnp.maximum(jnp.maximum(lse[0], lse[1]), lse[2])
        num = jnp.zeros((DIL_TQ, HEAD_DIM), F32)
        den = jnp.zeros((DIL_TQ, HEAD_DIM), F32)
        for g in range(len(DIL_PAIRS)):
            w = jnp.exp(lse[g] - top)
            num = num + w * on_ref[g, rows, :]
            den = den + w
        o_ref[rows, :] = (num / den).astype(o_ref.dtype)
        return carry

    lax.fori_loop(0, DIL_TA // DIL_TQ, merge, 0)


def _dilated(q, k, v, slopes):
    b, _, dil0, s, _ = q[0].shape
    assert dil0 == 1
    n_groups = len(DIL_PAIRS)

    def q_spec(dil):
        return pl.BlockSpec((None, None, dil, DIL_TA // dil, HEAD_DIM), lambda bi, hh, i: (bi, hh, 0, i, 0))

    def kv_spec(dil):
        return pl.BlockSpec((None, None, dil, s // dil, HEAD_DIM), lambda bi, hh, i: (bi, hh, 0, 0, 0))

    dils = [dil for _, dil in DIL_PAIRS]
    return pl.pallas_call(
        functools.partial(_dilated_kernel, seq=s),
        grid=(b, DIL_HEADS_PER_GROUP, s // DIL_TA),
        in_specs=[pl.BlockSpec(memory_space=pltpu.SMEM)]
        + [q_spec(d) for d in dils] + [kv_spec(d) for d in dils] + [kv_spec(d) for d in dils],
        out_specs=pl.BlockSpec((None, None, DIL_TA, HEAD_DIM), lambda bi, hh, i: (bi, hh, i, 0)),
        out_shape=jax.ShapeDtypeStruct((b, DIL_HEADS_PER_GROUP, s, HEAD_DIM), BF16),
        scratch_shapes=[pltpu.VMEM((n_groups, DIL_TA, HEAD_DIM), F32),
                        pltpu.VMEM((n_groups, DIL_TA, HEAD_DIM), F32),
                        pltpu.VMEM((2, DIL_TQ, DIL_TQ + 2 * DIL_RADIUS), F32)],
        compiler_params=_params("parallel", "parallel", "arbitrary"),
        name="dilated_attn",
    )(slopes, *q, *k, *v)


def _even_out_kernel(a_ref, prev_ref, next_ref, yb_ref, pw_ref, ps_ref, wo_hbm, x_ref, o_ref,
                     ext_ref, y_ref, wo_ref, wstage_ref, w_sems, *, seq):
    pl.when(_first_grid_step())(functools.partial(_load_weight_as_bf16, wo_hbm, wo_ref, wstage_ref, w_sems))
    i = pl.program_id(1)
    tm = OUT_TM
    ext_ref[0:POOL_HALO, :] = jnp.where(i == 0, 0.0, prev_ref[...])
    ext_ref[POOL_HALO:POOL_HALO + tm, :] = a_ref[...]
    ext_ref[POOL_HALO + tm:, :] = jnp.where(i == pl.num_programs(1) - 1, 0.0, next_ref[...])
    pos = i * tm + lax.broadcasted_iota(jnp.int32, (tm, 1), 0)
    for g, window in enumerate(POOL_WINDOWS):
        half = window // 2
        cols = slice(g * HEAD_DIM, (g + 1) * HEAD_DIM)
        total = ext_ref[POOL_HALO - half:POOL_HALO - half + tm, cols]
        for shift in range(-half + 1, half):
            total = total + ext_ref[POOL_HALO + shift:POOL_HALO + shift + tm, cols]
        count = (jnp.minimum(pos + half, seq) - jnp.maximum(pos - half, 0)).astype(F32)
        pooled = total / count - a_ref[:, cols]
        ya = jnp.dot(pooled.astype(BF16), pw_ref[g], preferred_element_type=F32) * ps_ref[:, cols]
        y_ref[:, cols] = ya.astype(BF16)
    for hd in range(DIL_HEADS_PER_GROUP):
        y_ref[:, POOL_WIDTH + hd * HEAD_DIM:POOL_WIDTH + (hd + 1) * HEAD_DIM] = yb_ref[hd]
    o_ref[...] = x_ref[...] + jnp.dot(y_ref[...], wo_ref[...], preferred_element_type=F32)


def _even_out(a, yb, pool_w, pool_scale, w_out, x):
    b, s, _ = x.shape
    tm = OUT_TM
    halo_blocks = tm // POOL_HALO
    tile = lambda bi, i: (bi, i, 0)
    return pl.pallas_call(
        functools.partial(_even_out_kernel, seq=s),
        grid=(b, s // tm),
        in_specs=[
            pl.BlockSpec((None, tm, POOL_WIDTH), tile),
            pl.BlockSpec((None, POOL_HALO, POOL_WIDTH),
                         lambda bi, i: (bi, jnp.maximum(i * halo_blocks - 1, 0), 0)),
            pl.BlockSpec((None, POOL_HALO, POOL_WIDTH),
                         lambda bi, i: (bi, jnp.minimum((i + 1) * halo_blocks, s // POOL_HALO - 1), 0)),
            pl.BlockSpec((None, DIL_HEADS_PER_GROUP, tm, HEAD_DIM), lambda bi, i: (bi, 0, i, 0)),
            _resident((len(POOL_WINDOWS), HEAD_DIM, HEAD_DIM), lambda bi, i: (0, 0, 0)),
            pl.BlockSpec((1, POOL_WIDTH), lambda bi, i: (0, 0)),
            pl.BlockSpec(memory_space=pl.ANY),
            pl.BlockSpec((None, tm, D_MODEL), tile),
        ],
        out_specs=pl.BlockSpec((None, tm, D_MODEL), tile),
        out_shape=jax.ShapeDtypeStruct((b, s, D_MODEL), F32),
        scratch_shapes=[pltpu.VMEM((tm + 2 * POOL_HALO, POOL_WIDTH), F32),
                        pltpu.VMEM((tm, POOL_WIDTH + DIL_OUT), BF16)]
        + _weight_scratch(POOL_WIDTH + DIL_OUT, D_MODEL),
        compiler_params=_params("arbitrary", "arbitrary"),
        name="even_out",
    )(a, a, a, yb, pool_w, pool_scale.reshape(1, POOL_WIDTH), w_out, x)


def _na_fill_bias(pair_ref, bias_ref, rb, rows):
    k_row0 = min(max(rb * NA_QROWS - NA_ROWS // 2, 0), rows - NA_KROWS)
    left_half = lax.broadcasted_iota(jnp.int32, (GRID_W, 2 * GRID_W), 1) < GRID_W
    for qr in range(NA_QROWS):
        r = rb * NA_QROWS + qr
        row_start = min(max(r - NA_ROWS // 2, 0), rows - NA_ROWS)
        for p in range(NA_KROWS // 2):
            kr = k_row0 + 2 * p
            ok_left = row_start <= kr < row_start + NA_ROWS
            ok_right = row_start <= kr + 1 < row_start + NA_ROWS
            if ok_left or ok_right:
                block = pair_ref[kr + 1 - r + NA_ROWS - 1]
                if not ok_right:
                    block = jnp.where(left_half, block, NEG_INF)
                if not ok_left:
                    block = jnp.where(left_half, NEG_INF, block)
            else:
                block = jnp.full((GRID_W, 2 * GRID_W), NEG_INF, F32)
            bias_ref[qr * GRID_W:(qr + 1) * GRID_W, p * 2 * GRID_W:(p + 1) * 2 * GRID_W] = block


def _na_kernel(q_ref, k_ref, v_ref, pair_ref, o_ref, bias_ref, s_ref, *, rows):
    n_blocks = rows // NA_QROWS
    tq = NA_QROWS * GRID_W
    n_keys = NA_KROWS * GRID_W
    kinds = (0, 1, n_blocks - 1)
    for slot, kind in enumerate(kinds):
        _na_fill_bias(pair_ref, bias_ref.at[slot], kind, rows)

    def key_start(rb):
        return min(max(rb * NA_QROWS - NA_ROWS // 2, 0), rows - NA_KROWS) * GRID_W

    def scores(rb):
        kind_slot = 0 if rb == 0 else (2 if rb == n_blocks - 1 else 1)
        q = q_ref[rb * tq:(rb + 1) * tq, :]
        k = k_ref[key_start(rb):key_start(rb) + n_keys, :]
        s_ref[rb % 2] = (lax.dot_general(q, k, (((1,), (1,)), ((), ())), preferred_element_type=F32)
                         + bias_ref[kind_slot])

    def finish(rb):
        v = v_ref[key_start(rb):key_start(rb) + n_keys, :]
        s = s_ref[rb % 2]
        m = jnp.max(s, axis=-1, keepdims=True)
        p = jnp.exp(s - m)
        den = jnp.sum(p, axis=-1, keepdims=True)
        out = jnp.dot(p.astype(BF16), v, preferred_element_type=F32) / den
        o_ref[rb * tq:(rb + 1) * tq, :] = out.astype(o_ref.dtype)

    scores(0)
    for rb in range(n_blocks):
        if rb + 1 < n_blocks:
            scores(rb + 1)
        finish(rb)


def _na_pair_tables(rpb):
    n_heads, n_rel_rows, n_rel_cols = rpb.shape
    c = np.arange(GRID_W)
    col_start = np.clip(c - NA_COLS // 2, 0, GRID_W - NA_COLS)
    col_ok = (c[None, :] >= col_start[:, None]) & (c[None, :] < col_start[:, None] + NA_COLS)
    rel_c = np.clip(c[None, :] - c[:, None], -(NA_COLS - 1), NA_COLS - 1) + (NA_COLS - 1)
    pick = (rel_c.reshape(1, -1) == np.arange(n_rel_cols)[:, None]).astype(np.float32)
    by_col = jnp.dot(rpb.astype(F32).reshape(n_heads * n_rel_rows, n_rel_cols), pick,
                     precision=lax.Precision.HIGHEST).reshape(n_heads, n_rel_rows, GRID_W, GRID_W)
    by_col = jnp.where(col_ok[None, None], by_col, NEG_INF)
    masked = jnp.full((n_heads, 1, GRID_W, GRID_W), NEG_INF, F32)
    padded = jnp.concatenate([masked, by_col, masked], axis=1)
    return jnp.concatenate([padded[:, :-1], padded[:, 1:]], axis=-1)


def _neighbourhood(q, k, v, pair_tables):
    b, _, s, _ = q.shape
    rows = s // GRID_W
    n_blocks = rows // NA_QROWS
    assert n_blocks >= 3 and rows >= NA_KROWS
    tq = NA_QROWS * GRID_W
    n_keys = NA_KROWS * GRID_W
    head = lambda bi, h: (bi, h, 0, 0)
    return pl.pallas_call(
        functools.partial(_na_kernel, rows=rows),
        grid=(b, NA_HEADS),
        in_specs=[
            pl.BlockSpec((None, None, s, HEAD_DIM), head),
            pl.BlockSpec((None, None, s, HEAD_DIM), head),
            pl.BlockSpec((None, None, s, HEAD_DIM), head),
            pl.BlockSpec((None, 2 * NA_ROWS, GRID_W, 2 * GRID_W), lambda bi, h: (h, 0, 0, 0)),
        ],
        out_specs=pl.BlockSpec((None, None, s, HEAD_DIM), head),
        out_shape=jax.ShapeDtypeStruct((b, NA_HEADS, s, HEAD_DIM), BF16),
        scratch_shapes=[pltpu.VMEM((3, tq, n_keys), F32), pltpu.VMEM((2, tq, n_keys), F32)],
        compiler_params=_params("parallel", "parallel"),
        name="neighbourhood_attn",
    )(q, k, v, pair_tables)


def _odd_out_kernel(yc_ref, u_ref, vn_ref, ws_ref, bs_ref, wo_hbm, x_ref, o_ref, y_ref,
                    wo_ref, wstage_ref, w_sems):
    pl.when(_first_grid_step())(functools.partial(_load_weight_as_bf16, wo_hbm, wo_ref, wstage_ref, w_sems))
    width = SG_WIDTH // SG_GROUPS
    for c in range(OUT_TM // SG_CHUNK):
        rows = slice(c * SG_CHUNK, (c + 1) * SG_CHUNK)
        for g in range(SG_GROUPS):
            cols = slice(g * width, (g + 1) * width)
            sv = jnp.dot(ws_ref[g], vn_ref[rows, cols], preferred_element_type=F32) + bs_ref[g]
            y_ref[rows, NA_WIDTH + g * width:NA_WIDTH + (g + 1) * width] = (u_ref[rows, cols] * sv).astype(BF16)
    for hd in range(NA_HEADS):
        y_ref[:, hd * HEAD_DIM:(hd + 1) * HEAD_DIM] = yc_ref[hd]
    o_ref[...] = x_ref[...] + jnp.dot(y_ref[...], wo_ref[...], preferred_element_type=F32)


def _odd_out(yc, u, vn, w_s, b_s, w_out, x):
    b, s, _ = x.shape
    tm = OUT_TM
    width = SG_WIDTH // SG_GROUPS
    tile = lambda bi, i: (bi, i, 0)
    return pl.pallas_call(
        _odd_out_kernel,
        grid=(b, s // tm),
        in_specs=[
            pl.BlockSpec((None, NA_HEADS, tm, HEAD_DIM), lambda bi, i: (bi, 0, i, 0)),
            pl.BlockSpec((None, tm, SG_WIDTH), tile),
            pl.BlockSpec((None, tm, SG_WIDTH), tile),
            _resident((SG_GROUPS, SG_CHUNK, SG_CHUNK), lambda bi, i: (0, 0, 0)),
            _resident((SG_GROUPS, SG_CHUNK, width), lambda bi, i: (0, 0, 0)),
            pl.BlockSpec(memory_space=pl.ANY),
            pl.BlockSpec((None, tm, D_MODEL), tile),
        ],
        out_specs=pl.BlockSpec((None, tm, D_MODEL), tile),
        out_shape=jax.ShapeDtypeStruct((b, s, D_MODEL), F32),
        scratch_shapes=[pltpu.VMEM((tm, NA_WIDTH + SG_WIDTH), BF16)]
        + _weight_scratch(NA_WIDTH + SG_WIDTH, D_MODEL),
        compiler_params=_params("arbitrary", "arbitrary"),
        name="odd_out",
    )(yc, u, vn, w_s, b_s, w_out, x)


def kernel(x, norm_ffn1, norm_mix, norm_ffn2, norm_out, ffn_w_gate, ffn_w_up, ffn_w_down, even_w_in, pool_w, pool_scale, dil_q_gain, dil_k_gain, even_w_out, odd_w_in, na_q_gain, na_k_gain, na_rpb, sg_v_gain, sg_w, sg_b, odd_w_out):
    depth = norm_ffn1.shape[0]
    slopes = jnp.asarray(2.0 ** (-8.0 * np.arange(1, DIL_HEADS + 1) / DIL_HEADS), dtype=F32)
    width = SG_WIDTH // SG_GROUPS

    for layer in range(depth):
        x = _ffn(x, norm_ffn1[layer], ffn_w_gate, ffn_w_up, ffn_w_down, layer, 0)
        if layer % 2 == 0:
            e = layer // 2
            a, q, k, v = _even_in(x, norm_mix[layer], even_w_in[e], dil_q_gain[e], dil_k_gain[e])
            yb = _dilated(q, k, v, slopes)
            x = _even_out(a, yb, pool_w[e].astype(BF16), pool_scale[e], even_w_out[e], x)
        else:
            o = layer // 2
            q, k, v, u, vn = _odd_in(x, norm_mix[layer], odd_w_in[o],
                                     na_q_gain[o], na_k_gain[o], sg_v_gain[o])
            yc = _neighbourhood(q, k, v, _na_pair_tables(na_rpb[o]))
            b_s = jnp.broadcast_to(sg_b[o][:, :, None], (SG_GROUPS, SG_CHUNK, width))
            x = _odd_out(yc, u, vn, sg_w[o].astype(BF16), b_s, odd_w_out[o], x)
        x = _ffn(x, norm_ffn2[layer], ffn_w_gate, ffn_w_up, ffn_w_down, layer, 1, gain_out=norm_out[layer])
    return x
```

```python
import functools

import numpy as np
import jax
import jax.numpy as jnp
from jax import lax
from jax.experimental import pallas as pl
from jax.experimental.pallas import tpu as pltpu

F32 = jnp.float32
BF16 = jnp.bfloat16

D_MODEL = 2048
D_FF = 5632
HEAD_DIM = 128
POOL_WINDOWS = (2, 4, 8, 16)
POOL_WIDTH = 512
POOL_HALO = 8
DIL_PAIRS = ((128, 1), (512, 4), (2048, 16))
DIL_HEADS_PER_GROUP = 4
DIL_HEADS = 12
DIL_OUT = DIL_HEADS_PER_GROUP * HEAD_DIM
GRID_W = 64
NA_ROWS = 8
NA_COLS = 16
NA_HEADS = 8
NA_WIDTH = NA_HEADS * HEAD_DIM
NA_QROWS = 4
NA_KROWS = NA_QROWS + NA_ROWS
SG_CHUNK = 128
SG_GROUPS = 8
SG_WIDTH = 1024
RMS_EPS = 1e-6
NEG_INF = -1e30

VMEM_LIMIT_BYTES = 60 * 1024 * 1024

FFN_TM = 1024
FFN_TF = 512
FFN_TC = 256
PROJ_TM = 512
PROJ_TN = 512
DIL_TQ = 128
DIL_TA = 2048
DIL_RADIUS = 64
DIL_UNROLL = 16
OUT_TM = 512
W_CHUNK = 512


def _params(*semantics):
    return pltpu.CompilerParams(dimension_semantics=semantics, vmem_limit_bytes=VMEM_LIMIT_BYTES)


def _resident(block_shape, index_map):
    return pl.BlockSpec(block_shape, index_map, pipeline_mode=pl.Buffered(1))


def _rms(x, gain):
    return x * lax.rsqrt(jnp.mean(x * x, axis=-1, keepdims=True) + RMS_EPS) * gain


def _first_grid_step():
    return (pl.program_id(0) == 0) & (pl.program_id(1) == 0)


def _load_weight_as_bf16(w_hbm, wbf_ref, stage_ref, sems):
    n_chunks = wbf_ref.shape[1] // W_CHUNK

    def copy(c):
        cols = pl.ds(c * W_CHUNK, W_CHUNK)
        return pltpu.make_async_copy(w_hbm.at[:, cols], stage_ref.at[c % 2], sems.at[c % 2])

    copy(0).start()
    for c in range(n_chunks):
        if c + 1 < n_chunks:
            copy(c + 1).start()
        copy(c).wait()
        wbf_ref[:, c * W_CHUNK:(c + 1) * W_CHUNK] = stage_ref[c % 2].astype(BF16)


def _weight_scratch(rows, cols):
    assert cols % W_CHUNK == 0
    return [pltpu.VMEM((rows, cols), BF16), pltpu.VMEM((2, rows, W_CHUNK), F32), pltpu.SemaphoreType.DMA((2,))]


def _ffn_kernel(x_hbm, g_ref, wg_ref, wu_ref, wd_ref, gout_ref, o_ref, h_ref, xbuf_ref, x_sem, *, final_norm):
    j = pl.program_id(2)
    tiles_per_batch = pl.num_programs(1)
    n_tiles = pl.num_programs(0) * tiles_per_batch
    tile = pl.program_id(0) * tiles_per_batch + pl.program_id(1)

    def x_copy(t):
        rows = pl.ds(pl.multiple_of((t % tiles_per_batch) * FFN_TM, FFN_TM), FFN_TM)
        return pltpu.make_async_copy(x_hbm.at[t // tiles_per_batch, rows, :], xbuf_ref, x_sem)

    @pl.when(j == 0)
    def _():
        @pl.when(tile == 0)
        def _():
            x_copy(tile).start()

        x_copy(tile).wait()
        x = xbuf_ref[...]
        h_ref[...] = _rms(x, g_ref[...]).astype(BF16)
        o_ref[...] = x

    @pl.when((j == 1) & (tile + 1 < n_tiles))
    def _():
        x_copy(tile + 1).start()

    h = h_ref[...]
    for c in range(FFN_TF // FFN_TC):
        cols = slice(c * FFN_TC, (c + 1) * FFN_TC)
        gate = jnp.dot(h, wg_ref[:, cols].astype(BF16), preferred_element_type=F32)
        up = jnp.dot(h, wu_ref[:, cols].astype(BF16), preferred_element_type=F32)
        act = (gate * jax.nn.sigmoid(gate)) * up * 0.5
        o_ref[...] += jnp.dot(act.astype(BF16), wd_ref[cols, :].astype(BF16), preferred_element_type=F32)

    if final_norm:
        @pl.when(j == pl.num_programs(2) - 1)
        def _():
            o_ref[...] = _rms(o_ref[...], gout_ref[...])


def _ffn(x, gain, w_gate, w_up, w_down, layer, idx, gain_out=None):
    b, s, _ = x.shape
    assert D_FF // FFN_TF >= 2
    final_norm = gain_out is not None
    if gain_out is None:
        gain_out = gain
    tile = lambda bi, i, j: (bi, i, 0)
    const = lambda bi, i, j: (0, 0)
    return pl.pallas_call(
        functools.partial(_ffn_kernel, final_norm=final_norm),
        grid=(b, s // FFN_TM, D_FF // FFN_TF),
        in_specs=[
            pl.BlockSpec(memory_space=pl.ANY),
            pl.BlockSpec((1, D_MODEL), const),
            pl.BlockSpec((None, None, D_MODEL, FFN_TF), lambda bi, i, j: (layer, idx, 0, j)),
            pl.BlockSpec((None, None, D_MODEL, FFN_TF), lambda bi, i, j: (layer, idx, 0, j)),
            pl.BlockSpec((None, None, FFN_TF, D_MODEL), lambda bi, i, j: (layer, idx, j, 0)),
            pl.BlockSpec((1, D_MODEL), const),
        ],
        out_specs=pl.BlockSpec((None, FFN_TM, D_MODEL), tile),
        out_shape=jax.ShapeDtypeStruct((b, s, D_MODEL), F32),
        scratch_shapes=[pltpu.VMEM((FFN_TM, D_MODEL), BF16), pltpu.VMEM((FFN_TM, D_MODEL), F32),
                        pltpu.SemaphoreType.DMA(())],
        compiler_params=_params("arbitrary", "arbitrary", "arbitrary"),
        name="ffn_final" if final_norm else "ffn",
    )(x, gain.reshape(1, D_MODEL), w_gate, w_up, w_down, gain_out.reshape(1, D_MODEL))


def _head_store(z, out_ref, head0, gain=None):
    for hd in range(PROJ_TN // HEAD_DIM):
        zh = z[:, hd * HEAD_DIM:(hd + 1) * HEAD_DIM]
        if gain is not None:
            zh = _rms(zh, gain)
        out_ref[head0 + hd] = zh.astype(out_ref.dtype)


def _store_by_residue(z, out_ref, stage_ref, dil, gain=None):
    tm = z.shape[0]
    for hd in range(z.shape[1] // HEAD_DIM):
        cols = slice(hd * HEAD_DIM, (hd + 1) * HEAD_DIM)
        zh = z[:, cols]
        if gain is not None:
            zh = _rms(zh, gain)
        if dil == 1:
            out_ref[hd, 0] = zh.astype(BF16)
        else:
            stage_ref[hd] = zh
            for r in range(dil):
                out_ref[hd, r] = stage_ref[hd, pl.ds(r, tm // dil, stride=dil), :].astype(BF16)


def _even_in_kernel(x_ref, g_ref, w_hbm, qg_ref, kg_ref, a_ref, *rest):
    n_groups = len(DIL_PAIRS)
    q_refs, k_refs, v_refs = rest[:n_groups], rest[n_groups:2 * n_groups], rest[2 * n_groups:3 * n_groups]
    h_ref, stage_ref, w_ref, wstage_ref, w_sems = rest[3 * n_groups:]
    pl.when(_first_grid_step())(functools.partial(_load_weight_as_bf16, w_hbm, w_ref, wstage_ref, w_sems))
    h_ref[...] = _rms(x_ref[...], g_ref[...]).astype(BF16)

    def chunk(c):
        return jnp.dot(h_ref[...], w_ref[:, c * PROJ_TN:(c + 1) * PROJ_TN], preferred_element_type=F32)

    a_ref[...] = chunk(0)
    for g, (_, dil) in enumerate(DIL_PAIRS):
        _store_by_residue(chunk(1 + g), q_refs[g], stage_ref, dil, qg_ref[...])
        _store_by_residue(chunk(1 + n_groups + g), k_refs[g], stage_ref, dil, kg_ref[...])
        _store_by_residue(chunk(1 + 2 * n_groups + g), v_refs[g], stage_ref, dil)


def _even_in(x, gain, w_in, q_gain, k_gain):
    b, s, _ = x.shape
    n_in = w_in.shape[1]
    assert PROJ_TN == DIL_OUT
    row = lambda bi, i: (bi, i, 0)
    const = lambda bi, i: (0, 0)
    hpg = DIL_HEADS_PER_GROUP
    group_specs = [pl.BlockSpec((None, hpg, dil, PROJ_TM // dil, HEAD_DIM), lambda bi, i: (bi, 0, 0, i, 0))
                   for _, dil in DIL_PAIRS]
    group_shapes = [jax.ShapeDtypeStruct((b, hpg, dil, s // dil, HEAD_DIM), BF16) for _, dil in DIL_PAIRS]
    outs = pl.pallas_call(
        _even_in_kernel,
        grid=(b, s // PROJ_TM),
        in_specs=[
            pl.BlockSpec((None, PROJ_TM, D_MODEL), row),
            pl.BlockSpec((1, D_MODEL), const),
            pl.BlockSpec(memory_space=pl.ANY),
            pl.BlockSpec((1, HEAD_DIM), const),
            pl.BlockSpec((1, HEAD_DIM), const),
        ],
        out_specs=[pl.BlockSpec((None, PROJ_TM, POOL_WIDTH), row)] + group_specs * 3,
        out_shape=[jax.ShapeDtypeStruct((b, s, POOL_WIDTH), F32)] + group_shapes * 3,
        scratch_shapes=[pltpu.VMEM((PROJ_TM, D_MODEL), BF16),
                        pltpu.VMEM((DIL_HEADS_PER_GROUP, PROJ_TM, HEAD_DIM), F32)]
        + _weight_scratch(D_MODEL, n_in),
        compiler_params=_params("arbitrary", "arbitrary"),
        name="even_in",
    )(x, gain.reshape(1, D_MODEL), w_in,
      (q_gain * HEAD_DIM ** -0.5).reshape(1, HEAD_DIM), k_gain.reshape(1, HEAD_DIM))
    n_groups = len(DIL_PAIRS)
    return outs[0], outs[1:1 + n_groups], outs[1 + n_groups:1 + 2 * n_groups], outs[1 + 2 * n_groups:]


def _odd_in_kernel(x_ref, g_ref, w_hbm, qg_ref, kg_ref, vg_ref, q_ref, k_ref, v_ref, u_ref, vn_ref,
                   h_ref, gv_ref, w_ref, wstage_ref, w_sems):
    pl.when(_first_grid_step())(functools.partial(_load_weight_as_bf16, w_hbm, w_ref, wstage_ref, w_sems))
    h_ref[...] = _rms(x_ref[...], g_ref[...]).astype(BF16)

    def chunk(c):
        return jnp.dot(h_ref[...], w_ref[:, c * PROJ_TN:(c + 1) * PROJ_TN], preferred_element_type=F32)

    per = NA_WIDTH // PROJ_TN
    heads_per_chunk = PROJ_TN // HEAD_DIM
    for c in range(per):
        _head_store(chunk(c), q_ref, c * heads_per_chunk, qg_ref[...])
        _head_store(chunk(per + c), k_ref, c * heads_per_chunk, kg_ref[...])
        _head_store(chunk(2 * per + c), v_ref, c * heads_per_chunk)
    per_sg = SG_WIDTH // PROJ_TN
    for c in range(per_sg):
        u_ref[:, c * PROJ_TN:(c + 1) * PROJ_TN] = jax.nn.gelu(chunk(3 * per + c))
        gv_ref[:, c * PROJ_TN:(c + 1) * PROJ_TN] = jax.nn.gelu(chunk(3 * per + per_sg + c))
    vn_ref[...] = _rms(gv_ref[...], vg_ref[...]).astype(BF16)


def _odd_in(x, gain, w_in, q_gain, k_gain, v_gain):
    b, s, _ = x.shape
    n_in = w_in.shape[1]
    row = lambda bi, i: (bi, i, 0)
    by_head = lambda bi, i: (bi, 0, i, 0)
    const = lambda bi, i: (0, 0)
    return pl.pallas_call(
        _odd_in_kernel,
        grid=(b, s // PROJ_TM),
        in_specs=[
            pl.BlockSpec((None, PROJ_TM, D_MODEL), row),
            pl.BlockSpec((1, D_MODEL), const),
            pl.BlockSpec(memory_space=pl.ANY),
            pl.BlockSpec((1, HEAD_DIM), const),
            pl.BlockSpec((1, HEAD_DIM), const),
            pl.BlockSpec((1, SG_WIDTH), const),
        ],
        out_specs=[
            pl.BlockSpec((None, NA_HEADS, PROJ_TM, HEAD_DIM), by_head),
            pl.BlockSpec((None, NA_HEADS, PROJ_TM, HEAD_DIM), by_head),
            pl.BlockSpec((None, NA_HEADS, PROJ_TM, HEAD_DIM), by_head),
            pl.BlockSpec((None, PROJ_TM, SG_WIDTH), row),
            pl.BlockSpec((None, PROJ_TM, SG_WIDTH), row),
        ],
        out_shape=[
            jax.ShapeDtypeStruct((b, NA_HEADS, s, HEAD_DIM), BF16),
            jax.ShapeDtypeStruct((b, NA_HEADS, s, HEAD_DIM), BF16),
            jax.ShapeDtypeStruct((b, NA_HEADS, s, HEAD_DIM), BF16),
            jax.ShapeDtypeStruct((b, s, SG_WIDTH), F32),
            jax.ShapeDtypeStruct((b, s, SG_WIDTH), BF16),
        ],
        scratch_shapes=[pltpu.VMEM((PROJ_TM, D_MODEL), BF16), pltpu.VMEM((PROJ_TM, SG_WIDTH), F32)]
        + _weight_scratch(D_MODEL, n_in),
        compiler_params=_params("arbitrary", "arbitrary"),
        name="odd_in",
    )(x, gain.reshape(1, D_MODEL), w_in,
      (q_gain * HEAD_DIM ** -0.5).reshape(1, HEAD_DIM), k_gain.reshape(1, HEAD_DIM),
      v_gain.reshape(1, SG_WIDTH))


def _dilated_kernel(slopes_ref, q0_ref, q1_ref, q2_ref, k0_ref, k1_ref, k2_ref, v0_ref, v1_ref, v2_ref,
                    o_ref, on_ref, ls_ref, s_ref, *, seq):
    hh = pl.program_id(1)
    step = pl.program_id(2)
    q_refs = (q0_ref, q1_ref, q2_ref)
    k_refs = (k0_ref, k1_ref, k2_ref)
    v_refs = (v0_ref, v1_ref, v2_ref)
    for g, (window, dil) in enumerate(DIL_PAIRS):
        assert window // (2 * dil) == DIL_RADIUS
        q_ref, k_ref, v_ref = q_refs[g], k_refs[g], v_refs[g]
        sub_len = seq // dil
        per_res = DIL_TA // dil
        nq = min(per_res, DIL_TQ)
        n_sub = per_res // nq
        n_keys = min(nq + 2 * DIL_RADIUS, sub_len)
        slope = slopes_ref[g * DIL_HEADS_PER_GROUP + hh] * dil
        base = (lax.broadcasted_iota(jnp.int32, (nq, n_keys), 1)
                - lax.broadcasted_iota(jnp.int32, (nq, n_keys), 0))

        def place(t, nq=nq, n_sub=n_sub, per_res=per_res, sub_len=sub_len, n_keys=n_keys):
            r = t // n_sub
            q_off = pl.multiple_of((t % n_sub) * nq, nq)
            q_pos = step * per_res + q_off
            k_pos = pl.multiple_of(jnp.clip(q_pos - DIL_RADIUS, 0, sub_len - n_keys), DIL_RADIUS)
            return r, q_off, q_pos, k_pos

        def scores(t, slot, q_ref=q_ref, k_ref=k_ref, nq=nq, n_keys=n_keys, slope=slope, base=base,
                   place=place):
            r, q_off, q_pos, k_pos = place(t)
            q = q_ref[r, pl.ds(q_off, nq), :]
            k = k_ref[r, pl.ds(k_pos, n_keys), :]
            s = lax.dot_general(q, k, (((1,), (1,)), ((), ())), preferred_element_type=F32)
            dist = jnp.abs(base + (k_pos - q_pos))
            s_ref[slot, 0:nq, 0:n_keys] = jnp.where(dist <= DIL_RADIUS, s - slope * dist.astype(F32), NEG_INF)

        def finish(t, slot, v_ref=v_ref, g=g, dil=dil, nq=nq, n_keys=n_keys, place=place):
            r, q_off, _, k_pos = place(t)
            v = v_ref[r, pl.ds(k_pos, n_keys), :]
            s = s_ref[slot, 0:nq, 0:n_keys]
            m = jnp.max(s, axis=-1, keepdims=True)
            p = jnp.exp(s - m)
            den = jnp.sum(p, axis=-1, keepdims=True)
            out = jnp.dot(p.astype(BF16), v, preferred_element_type=F32) / den
            lse = jnp.broadcast_to(m + jnp.log(den), (nq, HEAD_DIM))
            if dil == 1:
                rows = pl.ds(q_off, nq)
            else:
                rows = pl.ds(r + dil * q_off, nq, stride=dil)
            on_ref[g, rows, :] = out
            ls_ref[g, rows, :] = lse

        n_tiles = dil * n_sub
        assert n_tiles % DIL_UNROLL == 0

        def body(i, carry, scores=scores, finish=finish):
            t0 = i * DIL_UNROLL
            scores(t0, 0)
            for u in range(DIL_UNROLL):
                if u + 1 < DIL_UNROLL:
                    scores(t0 + u + 1, (u + 1) % 2)
                finish(t0 + u, u % 2)
            return carry

        lax.fori_loop(0, n_tiles // DIL_UNROLL, body, 0)

    def merge(c, carry):
        rows = pl.ds(pl.multiple_of(c * DIL_TQ, DIL_TQ), DIL_TQ)
        lse = [ls_ref[g, rows, :] for g in range(len(DIL_PAIRS))]
        top = jnp.maximum(jnp.maximum(lse[0], lse[1]), lse[2])
        num = jnp.zeros((DIL_TQ, HEAD_DIM), F32)
        den = jnp.zeros((DIL_TQ, HEAD_DIM), F32)
        for g in range(len(DIL_PAIRS)):
            w = jnp.exp(lse[g] - top)
            num = num + w * on_ref[g, rows, :]
            den = den + w
        o_ref[rows, :] = (num / den).astype(o_ref.dtype)
        return carry

    lax.fori_loop(0, DIL_TA // DIL_TQ, merge, 0)


def _dilated(q, k, v, slopes):
    b, _, dil0, s, _ = q[0].shape
    assert dil0 == 1
    n_groups = len(DIL_PAIRS)

    def q_spec(dil):
        return pl.BlockSpec((None, None, dil, DIL_TA // dil, HEAD_DIM), lambda bi, hh, i: (bi, hh, 0, i, 0))

    def kv_spec(dil):
        return pl.BlockSpec((None, None, dil, s // dil, HEAD_DIM), lambda bi, hh, i: (bi, hh, 0, 0, 0))

    dils = [dil for _, dil in DIL_PAIRS]
    return pl.pallas_call(
        functools.partial(_dilated_kernel, seq=s),
        grid=(b, DIL_HEADS_PER_GROUP, s // DIL_TA),
        in_specs=[pl.BlockSpec(memory_space=pltpu.SMEM)]
        + [q_spec(d) for d in dils] + [kv_spec(d) for d in dils] + [kv_spec(d) for d in dils],
        out_specs=pl.BlockSpec((None, None, DIL_TA, HEAD_DIM), lambda bi, hh, i: (bi, hh, i, 0)),
        out_shape=jax.ShapeDtypeStruct((b, DIL_HEADS_PER_GROUP, s, HEAD_DIM), BF16),
        scratch_shapes=[pltpu.VMEM((n_groups, DIL_TA, HEAD_DIM), F32),
                        pltpu.VMEM((n_groups, DIL_TA, HEAD_DIM), F32),
                        pltpu.VMEM((2, DIL_TQ, DIL_TQ + 2 * DIL_RADIUS), F32)],
        compiler_params=_params("parallel", "parallel", "arbitrary"),
        name="dilated_attn",
    )(slopes, *q, *k, *v)


def _even_out_kernel(a_ref, prev_ref, next_ref, yb_ref, pw_ref, ps_ref, wo_hbm, x_ref, o_ref,
                     ext_ref, y_ref, wo_ref, wstage_ref, w_sems, *, seq):
    pl.when(_first_grid_step())(functools.partial(_load_weight_as_bf16, wo_hbm, wo_ref, wstage_ref, w_sems))
    i = pl.program_id(1)
    tm = OUT_TM
    ext_ref[0:POOL_HALO, :] = jnp.where(i == 0, 0.0, prev_ref[...])
    ext_ref[POOL_HALO:POOL_HALO + tm, :] = a_ref[...]
    ext_ref[POOL_HALO + tm:, :] = jnp.where(i == pl.num_programs(1) - 1, 0.0, next_ref[...])
    pos = i * tm + lax.broadcasted_iota(jnp.int32, (tm, 1), 0)
    for g, window in enumerate(POOL_WINDOWS):
        half = window // 2
        cols = slice(g * HEAD_DIM, (g + 1) * HEAD_DIM)
        total = ext_ref[POOL_HALO - half:POOL_HALO - half + tm, cols]
        for shift in range(-half + 1, half):
            total = total + ext_ref[POOL_HALO + shift:POOL_HALO + shift + tm, cols]
        count = (jnp.minimum(pos + half, seq) - jnp.maximum(pos - half, 0)).astype(F32)
        pooled = total / count - a_ref[:, cols]
        ya = jnp.dot(pooled.astype(BF16), pw_ref[g], preferred_element_type=F32) * ps_ref[:, cols]
        y_ref[:, cols] = ya.astype(BF16)
    for hd in range(DIL_HEADS_PER_GROUP):
        y_ref[:, POOL_WIDTH + hd * HEAD_DIM:POOL_WIDTH + (hd + 1) * HEAD_DIM] = yb_ref[hd]
    o_ref[...] = x_ref[...] + jnp.dot(y_ref[...], wo_ref[...], preferred_element_type=F32)


def _even_out(a, yb, pool_w, pool_scale, w_out, x):
    b, s, _ = x.shape
    tm = OUT_TM
    halo_blocks = tm // POOL_HALO
    tile = lambda bi, i: (bi, i, 0)
    return pl.pallas_call(
        functools.partial(_even_out_kernel, seq=s),
        grid=(b, s // tm),
        in_specs=[
            pl.BlockSpec((None, tm, POOL_WIDTH), tile),
            pl.BlockSpec((None, POOL_HALO, POOL_WIDTH),
                         lambda bi, i: (bi, jnp.maximum(i * halo_blocks - 1, 0), 0)),
            pl.BlockSpec((None, POOL_HALO, POOL_WIDTH),
                         lambda bi, i: (bi, jnp.minimum((i + 1) * halo_blocks, s // POOL_HALO - 1), 0)),
            pl.BlockSpec((None, DIL_HEADS_PER_GROUP, tm, HEAD_DIM), lambda bi, i: (bi, 0, i, 0)),
            _resident((len(POOL_WINDOWS), HEAD_DIM, HEAD_DIM), lambda bi, i: (0, 0, 0)),
            pl.BlockSpec((1, POOL_WIDTH), lambda bi, i: (0, 0)),
            pl.BlockSpec(memory_space=pl.ANY),
            pl.BlockSpec((None, tm, D_MODEL), tile),
        ],
        out_specs=pl.BlockSpec((None, tm, D_MODEL), tile),
        out_shape=jax.ShapeDtypeStruct((b, s, D_MODEL), F32),
        scratch_shapes=[pltpu.VMEM((tm + 2 * POOL_HALO, POOL_WIDTH), F32),
                        pltpu.VMEM((tm, POOL_WIDTH + DIL_OUT), BF16)]
        + _weight_scratch(POOL_WIDTH + DIL_OUT, D_MODEL),
        compiler_params=_params("arbitrary", "arbitrary"),
        name="even_out",
    )(a, a, a, yb, pool_w, pool_scale.reshape(1, POOL_WIDTH), w_out, x)


def _na_fill_bias(pair_ref, bias_ref, rb, rows):
    k_row0 = min(max(rb * NA_QROWS - NA_ROWS // 2, 0), rows - NA_KROWS)
    left_half = lax.broadcasted_iota(jnp.int32, (GRID_W, 2 * GRID_W), 1) < GRID_W
    for qr in range(NA_QROWS):
        r = rb * NA_QROWS + qr
        row_start = min(max(r - NA_ROWS // 2, 0), rows - NA_ROWS)
        for p in range(NA_KROWS // 2):
            kr = k_row0 + 2 * p
            ok_left = row_start <= kr < row_start + NA_ROWS
            ok_right = row_start <= kr + 1 < row_start + NA_ROWS
            if ok_left or ok_right:
                block = pair_ref[kr + 1 - r + NA_ROWS - 1]
                if not ok_right:
                    block = jnp.where(left_half, block, NEG_INF)
                if not ok_left:
                    block = jnp.where(left_half, NEG_INF, block)
            else:
                block = jnp.full((GRID_W, 2 * GRID_W), NEG_INF, F32)
            bias_ref[qr * GRID_W:(qr + 1) * GRID_W, p * 2 * GRID_W:(p + 1) * 2 * GRID_W] = block


def _na_kernel(q_ref, k_ref, v_ref, pair_ref, o_ref, bias_ref, s_ref, *, rows):
    n_blocks = rows // NA_QROWS
    tq = NA_QROWS * GRID_W
    n_keys = NA_KROWS * GRID_W
    kinds = (0, 1, n_blocks - 1)
    for slot, kind in enumerate(kinds):
        _na_fill_bias(pair_ref, bias_ref.at[slot], kind, rows)

    def key_start(rb):
        return min(max(rb * NA_QROWS - NA_ROWS // 2, 0), rows - NA_KROWS) * GRID_W

    def scores(rb):
        kind_slot = 0 if rb == 0 else (2 if rb == n_blocks - 1 else 1)
        q = q_ref[rb * tq:(rb + 1) * tq, :]
        k = k_ref[key_start(rb):key_start(rb) + n_keys, :]
        s_ref[rb % 2] = (lax.dot_general(q, k, (((1,), (1,)), ((), ())), preferred_element_type=F32)
                         + bias_ref[kind_slot])

    def finish(rb):
        v = v_ref[key_start(rb):key_start(rb) + n_keys, :]
        s = s_ref[rb % 2]
        m = jnp.max(s, axis=-1, keepdims=True)
        p = jnp.exp(s - m)
        den = jnp.sum(p, axis=-1, keepdims=True)
        out = jnp.dot(p.astype(BF16), v, preferred_element_type=F32) / den
        o_ref[rb * tq:(rb + 1) * tq, :] = out.astype(o_ref.dtype)

    scores(0)
    for rb in range(n_blocks):
        if rb + 1 < n_blocks:
            scores(rb + 1)
        finish(rb)


def _na_pair_tables(rpb):
    n_heads, n_rel_rows, n_rel_cols = rpb.shape
    c = np.arange(GRID_W)
    col_start = np.clip(c - NA_COLS // 2, 0, GRID_W - NA_COLS)
    col_ok = (c[None, :] >= col_start[:, None]) & (c[None, :] < col_start[:, None] + NA_COLS)
    rel_c = np.clip(c[None, :] - c[:, None], -(NA_COLS - 1), NA_COLS - 1) + (NA_COLS - 1)
    pick = (rel_c.reshape(1, -1) == np.arange(n_rel_cols)[:, None]).astype(np.float32)
    by_col = jnp.dot(rpb.astype(F32).reshape(n_heads * n_rel_rows, n_rel_cols), pick,
                     precision=lax.Precision.HIGHEST).reshape(n_heads, n_rel_rows, GRID_W, GRID_W)
    by_col = jnp.where(col_ok[None, None], by_col, NEG_INF)
    masked = jnp.full((n_heads, 1, GRID_W, GRID_W), NEG_INF, F32)
    padded = jnp.concatenate([masked, by_col, masked], axis=1)
    return jnp.concatenate([padded[:, :-1], padded[:, 1:]], axis=-1)


def _neighbourhood(q, k, v, pair_tables):
    b, _, s, _ = q.shape
    rows = s // GRID_W
    n_blocks = rows // NA_QROWS
    assert n_blocks >= 3 and rows >= NA_KROWS
    tq = NA_QROWS * GRID_W
    n_keys = NA_KROWS * GRID_W
    head = lambda bi, h: (bi, h, 0, 0)
    return pl.pallas_call(
        functools.partial(_na_kernel, rows=rows),
        grid=(b, NA_HEADS),
        in_specs=[
            pl.BlockSpec((None, None, s, HEAD_DIM), head),
            pl.BlockSpec((None, None, s, HEAD_DIM), head),
            pl.BlockSpec((None, None, s, HEAD_DIM), head),
            pl.BlockSpec((None, 2 * NA_ROWS, GRID_W, 2 * GRID_W), lambda bi, h: (h, 0, 0, 0)),
        ],
        out_specs=pl.BlockSpec((None, None, s, HEAD_DIM), head),
        out_shape=jax.ShapeDtypeStruct((b, NA_HEADS, s, HEAD_DIM), BF16),
        scratch_shapes=[pltpu.VMEM((3, tq, n_keys), F32), pltpu.VMEM((2, tq, n_keys), F32)],
        compiler_params=_params("parallel", "parallel"),
        name="neighbourhood_attn",
    )(q, k, v, pair_tables)


def _odd_out_kernel(yc_ref, u_ref, vn_ref, ws_ref, bs_ref, wo_hbm, x_ref, o_ref, y_ref,
                    wo_ref, wstage_ref, w_sems):
    pl.when(_first_grid_step())(functools.partial(_load_weight_as_bf16, wo_hbm, wo_ref, wstage_ref, w_sems))
    width = SG_WIDTH // SG_GROUPS
    for c in range(OUT_TM // SG_CHUNK):
        rows = slice(c * SG_CHUNK, (c + 1) * SG_CHUNK)
        for g in range(SG_GROUPS):
            cols = slice(g * width, (g + 1) * width)
            sv = jnp.dot(ws_ref[g], vn_ref[rows, cols], preferred_element_type=F32) + bs_ref[g]
            y_ref[rows, NA_WIDTH + g * width:NA_WIDTH + (g + 1) * width] = (u_ref[rows, cols] * sv).astype(BF16)
    for hd in range(NA_HEADS):
        y_ref[:, hd * HEAD_DIM:(hd + 1) * HEAD_DIM] = yc_ref[hd]
    o_ref[...] = x_ref[...] + jnp.dot(y_ref[...], wo_ref[...], preferred_element_type=F32)


def _odd_out(yc, u, vn, w_s, b_s, w_out, x):
    b, s, _ = x.shape
    tm = OUT_TM
    width = SG_WIDTH // SG_GROUPS
    tile = lambda bi, i: (bi, i, 0)
    return pl.pallas_call(
        _odd_out_kernel,
        grid=(b, s // tm),
        in_specs=[
            pl.BlockSpec((None, NA_HEADS, tm, HEAD_DIM), lambda bi, i: (bi, 0, i, 0)),
            pl.BlockSpec((None, tm, SG_WIDTH), tile),
            pl.BlockSpec((None, tm, SG_WIDTH), tile),
            _resident((SG_GROUPS, SG_CHUNK, SG_CHUNK), lambda bi, i: (0, 0, 0)),
            _resident((SG_GROUPS, SG_CHUNK, width), lambda bi, i: (0, 0, 0)),
            pl.BlockSpec(memory_space=pl.ANY),
            pl.BlockSpec((None, tm, D_MODEL), tile),
        ],
        out_specs=pl.BlockSpec((None, tm, D_MODEL), tile),
        out_shape=jax.ShapeDtypeStruct((b, s, D_MODEL), F32),
        scratch_shapes=[pltpu.VMEM((tm, NA_WIDTH + SG_WIDTH), BF16)]
        + _weight_scratch(NA_WIDTH + SG_WIDTH, D_MODEL),
        compiler_params=_params("arbitrary", "arbitrary"),
        name="odd_out",
    )(yc, u, vn, w_s, b_s, w_out, x)


def kernel(x, norm_ffn1, norm_mix, norm_ffn2, norm_out, ffn_w_gate, ffn_w_up, ffn_w_down, even_w_in, pool_w, pool_scale, dil_q_gain, dil_k_gain, even_w_out, odd_w_in, na_q_gain, na_k_gain, na_rpb, sg_v_gain, sg_w, sg_b, odd_w_out):
    depth = norm_ffn1.shape[0]
    slopes = jnp.asarray(2.0 ** (-8.0 * np.arange(1, DIL_HEADS + 1) / DIL_HEADS), dtype=F32)
    width = SG_WIDTH // SG_GROUPS

    for layer in range(depth):
        x = _ffn(x, norm_ffn1[layer], ffn_w_gate, ffn_w_up, ffn_w_down, layer, 0)
        if layer % 2 == 0:
            e = layer // 2
            a, q, k, v = _even_in(x, norm_mix[layer], even_w_in[e], dil_q_gain[e], dil_k_gain[e])
            yb = _dilated(q, k, v, slopes)
            x = _even_out(a, yb, pool_w[e].astype(BF16), pool_scale[e], even_w_out[e], x)
        else:
            o = layer // 2
            q, k, v, u, vn = _odd_in(x, norm_mix[layer], odd_w_in[o],
                                     na_q_gain[o], na_k_gain[o], sg_v_gain[o])
            yc = _neighbourhood(q, k, v, _na_pair_tables(na_rpb[o]))
            b_s = jnp.broadcast_to(sg_b[o][:, :, None], (SG_GROUPS, SG_CHUNK, width))
            x = _odd_out(yc, u, vn, sg_w[o].astype(BF16), b_s, odd_w_out[o], x)
        x = _ffn(x, norm_ffn2[layer], ffn_w_gate, ffn_w_up, ffn_w_down, layer, 1, gain_out=norm_out[layer])
    return x
```

```python
import functools

import numpy as np
import jax
import jax.numpy as jnp
from jax import lax
from jax.experimental import pallas as pl
from jax.experimental.pallas import tpu as pltpu

F32 = jnp.float32
BF16 = jnp.bfloat16

D_MODEL = 2048
D_FF = 5632
HEAD_DIM = 128
POOL_WINDOWS = (2, 4, 8, 16)
POOL_WIDTH = 512
POOL_HALO = 8
DIL_PAIRS = ((128, 1), (512, 4), (2048, 16))
DIL_HEADS_PER_GROUP = 4
DIL_HEADS = 12
DIL_OUT = DIL_HEADS_PER_GROUP * HEAD_DIM
GRID_W = 64
NA_ROWS = 8
NA_COLS = 16
NA_HEADS = 8
NA_WIDTH = NA_HEADS * HEAD_DIM
NA_QROWS = 4
NA_KROWS = NA_QROWS + NA_ROWS
SG_CHUNK = 128
SG_GROUPS = 8
SG_WIDTH = 1024
RMS_EPS = 1e-6
NEG_INF = -1e30

VMEM_LIMIT_BYTES = 60 * 1024 * 1024

FFN_TM = 1024
FFN_TF = 512
FFN_TC = 256
PROJ_TM = 512
PROJ_TN = 512
DIL_TQ = 128
DIL_TA = 2048
DIL_RADIUS = 64
DIL_UNROLL = 16
OUT_TM = 512
W_CHUNK = 512


def _params(*semantics):
    return pltpu.CompilerParams(dimension_semantics=semantics, vmem_limit_bytes=VMEM_LIMIT_BYTES)


def _resident(block_shape, index_map):
    return pl.BlockSpec(block_shape, index_map, pipeline_mode=pl.Buffered(1))


def _rms(x, gain):
    return x * lax.rsqrt(jnp.mean(x * x, axis=-1, keepdims=True) + RMS_EPS) * gain


def _first_grid_step():
    return (pl.program_id(0) == 0) & (pl.program_id(1) == 0)


def _load_weight_as_bf16(w_hbm, wbf_ref, stage_ref, sems):
    n_chunks = wbf_ref.shape[1] // W_CHUNK

    def copy(c):
        cols = pl.ds(c * W_CHUNK, W_CHUNK)
        return pltpu.make_async_copy(w_hbm.at[:, cols], stage_ref.at[c % 2], sems.at[c % 2])

    copy(0).start()
    for c in range(n_chunks):
        if c + 1 < n_chunks:
            copy(c + 1).start()
        copy(c).wait()
        wbf_ref[:, c * W_CHUNK:(c + 1) * W_CHUNK] = stage_ref[c % 2].astype(BF16)


def _weight_scratch(rows, cols):
    assert cols % W_CHUNK == 0
    return [pltpu.VMEM((rows, cols), BF16), pltpu.VMEM((2, rows, W_CHUNK), F32), pltpu.SemaphoreType.DMA((2,))]


def _ffn_kernel(x_hbm, g_ref, wg_ref, wu_ref, wd_ref, gout_ref, o_ref, h_ref, xbuf_ref, x_sem, *, final_norm):
    j = pl.program_id(2)
    tiles_per_batch = pl.num_programs(1)
    n_tiles = pl.num_programs(0) * tiles_per_batch
    tile = pl.program_id(0) * tiles_per_batch + pl.program_id(1)

    def x_copy(t):
        rows = pl.ds(pl.multiple_of((t % tiles_per_batch) * FFN_TM, FFN_TM), FFN_TM)
        return pltpu.make_async_copy(x_hbm.at[t // tiles_per_batch, rows, :], xbuf_ref, x_sem)

    @pl.when(j == 0)
    def _():
        @pl.when(tile == 0)
        def _():
            x_copy(tile).start()

        x_copy(tile).wait()
        x = xbuf_ref[...]
        h_ref[...] = _rms(x, g_ref[...]).astype(BF16)
        o_ref[...] = x

    @pl.when((j == 1) & (tile + 1 < n_tiles))
    def _():
        x_copy(tile + 1).start()

    h = h_ref[...]
    for c in range(FFN_TF // FFN_TC):
        cols = slice(c * FFN_TC, (c + 1) * FFN_TC)
        gate = jnp.dot(h, wg_ref[:, cols].astype(BF16), preferred_element_type=F32)
        up = jnp.dot(h, wu_ref[:, cols].astype(BF16), preferred_element_type=F32)
        act = (gate * jax.nn.sigmoid(gate)) * up * 0.5
        o_ref[...] += jnp.dot(act.astype(BF16), wd_ref[cols, :].astype(BF16), preferred_element_type=F32)

    if final_norm:
        @pl.when(j == pl.num_programs(2) - 1)
        def _():
            o_ref[...] = _rms(o_ref[...], gout_ref[...])


def _ffn(x, gain, w_gate, w_up, w_down, layer, idx, gain_out=None):
    b, s, _ = x.shape
    assert D_FF // FFN_TF >= 2
    final_norm = gain_out is not None
    if gain_out is None:
        gain_out = gain
    tile = lambda bi, i, j: (bi, i, 0)
    const = lambda bi, i, j: (0, 0)
    return pl.pallas_call(
        functools.partial(_ffn_kernel, final_norm=final_norm),
        grid=(b, s // FFN_TM, D_FF // FFN_TF),
        in_specs=[
            pl.BlockSpec(memory_space=pl.ANY),
            pl.BlockSpec((1, D_MODEL), const),
            pl.BlockSpec((None, None, D_MODEL, FFN_TF), lambda bi, i, j: (layer, idx, 0, j)),
            pl.BlockSpec((None, None, D_MODEL, FFN_TF), lambda bi, i, j: (layer, idx, 0, j)),
            pl.BlockSpec((None, None, FFN_TF, D_MODEL), lambda bi, i, j: (layer, idx, j, 0)),
            pl.BlockSpec((1, D_MODEL), const),
        ],
        out_specs=pl.BlockSpec((None, FFN_TM, D_MODEL), tile),
        out_shape=jax.ShapeDtypeStruct((b, s, D_MODEL), F32),
        scratch_shapes=[pltpu.VMEM((FFN_TM, D_MODEL), BF16), pltpu.VMEM((FFN_TM, D_MODEL), F32),
                        pltpu.SemaphoreType.DMA(())],
        compiler_params=_params("arbitrary", "arbitrary", "arbitrary"),
        name="ffn_final" if final_norm else "ffn",
    )(x, gain.reshape(1, D_MODEL), w_gate, w_up, w_down, gain_out.reshape(1, D_MODEL))


def _head_store(z, out_ref, head0, gain=None):
    for hd in range(PROJ_TN // HEAD_DIM):
        zh = z[:, hd * HEAD_DIM:(hd + 1) * HEAD_DIM]
        if gain is not None:
            zh = _rms(zh, gain)
        out_ref[head0 + hd] = zh.astype(out_ref.dtype)


def _store_by_residue(z, out_ref, stage_ref, dil, gain=None):
    tm = z.shape[0]
    for hd in range(z.shape[1] // HEAD_DIM):
        cols = slice(hd * HEAD_DIM, (hd + 1) * HEAD_DIM)
        zh = z[:, cols]
        if gain is not None:
            zh = _rms(zh, gain)
        if dil == 1:
            out_ref[hd, 0] = zh.astype(BF16)
        else:
            stage_ref[hd] = zh
            for r in range(dil):
                out_ref[hd, r] = stage_ref[hd, pl.ds(r, tm // dil, stride=dil), :].astype(BF16)


def _even_in_kernel(x_ref, g_ref, w_hbm, qg_ref, kg_ref, a_ref, *rest):
    n_groups = len(DIL_PAIRS)
    q_refs, k_refs, v_refs = rest[:n_groups], rest[n_groups:2 * n_groups], rest[2 * n_groups:3 * n_groups]
    h_ref, stage_ref, w_ref, wstage_ref, w_sems = rest[3 * n_groups:]
    pl.when(_first_grid_step())(functools.partial(_load_weight_as_bf16, w_hbm, w_ref, wstage_ref, w_sems))
    h_ref[...] = _rms(x_ref[...], g_ref[...]).astype(BF16)

    def chunk(c):
        return jnp.dot(h_ref[...], w_ref[:, c * PROJ_TN:(c + 1) * PROJ_TN], preferred_element_type=F32)

    a_ref[...] = chunk(0)
    for g, (_, dil) in enumerate(DIL_PAIRS):
        _store_by_residue(chunk(1 + g), q_refs[g], stage_ref, dil, qg_ref[...])
        _store_by_residue(chunk(1 + n_groups + g), k_refs[g], stage_ref, dil, kg_ref[...])
        _store_by_residue(chunk(1 + 2 * n_groups + g), v_refs[g], stage_ref, dil)


def _even_in(x, gain, w_in, q_gain, k_gain):
    b, s, _ = x.shape
    n_in = w_in.shape[1]
    assert PROJ_TN == DIL_OUT
    row = lambda bi, i: (bi, i, 0)
    const = lambda bi, i: (0, 0)
    hpg = DIL_HEADS_PER_GROUP
    group_specs = [pl.BlockSpec((None, hpg, dil, PROJ_TM // dil, HEAD_DIM), lambda bi, i: (bi, 0, 0, i, 0))
                   for _, dil in DIL_PAIRS]
    group_shapes = [jax.ShapeDtypeStruct((b, hpg, dil, s // dil, HEAD_DIM), BF16) for _, dil in DIL_PAIRS]
    outs = pl.pallas_call(
        _even_in_kernel,
        grid=(b, s // PROJ_TM),
        in_specs=[
            pl.BlockSpec((None, PROJ_TM, D_MODEL), row),
            pl.BlockSpec((1, D_MODEL), const),
            pl.BlockSpec(memory_space=pl.ANY),
            pl.BlockSpec((1, HEAD_DIM), const),
            pl.BlockSpec((1, HEAD_DIM), const),
        ],
        out_specs=[pl.BlockSpec((None, PROJ_TM, POOL_WIDTH), row)] + group_specs * 3,
        out_shape=[jax.ShapeDtypeStruct((b, s, POOL_WIDTH), F32)] + group_shapes * 3,
        scratch_shapes=[pltpu.VMEM((PROJ_TM, D_MODEL), BF16),
                        pltpu.VMEM((DIL_HEADS_PER_GROUP, PROJ_TM, HEAD_DIM), F32)]
        + _weight_scratch(D_MODEL, n_in),
        compiler_params=_params("arbitrary", "arbitrary"),
        name="even_in",
    )(x, gain.reshape(1, D_MODEL), w_in,
      (q_gain * HEAD_DIM ** -0.5).reshape(1, HEAD_DIM), k_gain.reshape(1, HEAD_DIM))
    n_groups = len(DIL_PAIRS)
    return outs[0], outs[1:1 + n_groups], outs[1 + n_groups:1 + 2 * n_groups], outs[1 + 2 * n_groups:]


def _odd_in_kernel(x_ref, g_ref, w_hbm, qg_ref, kg_ref, vg_ref, q_ref, k_ref, v_ref, u_ref, vn_ref,
                   h_ref, gv_ref, w_ref, wstage_ref, w_sems):
    pl.when(_first_grid_step())(functools.partial(_load_weight_as_bf16, w_hbm, w_ref, wstage_ref, w_sems))
    h_ref[...] = _rms(x_ref[...], g_ref[...]).astype(BF16)

    def chunk(c):
        return jnp.dot(h_ref[...], w_ref[:, c * PROJ_TN:(c + 1) * PROJ_TN], preferred_element_type=F32)

    per = NA_WIDTH // PROJ_TN
    heads_per_chunk = PROJ_TN // HEAD_DIM
    for c in range(per):
        _head_store(chunk(c), q_ref, c * heads_per_chunk, qg_ref[...])
        _head_store(chunk(per + c), k_ref, c * heads_per_chunk, kg_ref[...])
        _head_store(chunk(2 * per + c), v_ref, c * heads_per_chunk)
    per_sg = SG_WIDTH // PROJ_TN
    for c in range(per_sg):
        u_ref[:, c * PROJ_TN:(c + 1) * PROJ_TN] = jax.nn.gelu(chunk(3 * per + c))
        gv_ref[:, c * PROJ_TN:(c + 1) * PROJ_TN] = jax.nn.gelu(chunk(3 * per + per_sg + c))
    vn_ref[...] = _rms(gv_ref[...], vg_ref[...]).astype(BF16)


def _odd_in(x, gain, w_in, q_gain, k_gain, v_gain):
    b, s, _ = x.shape
    n_in = w_in.shape[1]
    row = lambda bi, i: (bi, i, 0)
    by_head = lambda bi, i: (bi, 0, i, 0)
    const = lambda bi, i: (0, 0)
    return pl.pallas_call(
        _odd_in_kernel,
        grid=(b, s // PROJ_TM),
        in_specs=[
            pl.BlockSpec((None, PROJ_TM, D_MODEL), row),
            pl.BlockSpec((1, D_MODEL), const),
            pl.BlockSpec(memory_space=pl.ANY),
            pl.BlockSpec((1, HEAD_DIM), const),
            pl.BlockSpec((1, HEAD_DIM), const),
            pl.BlockSpec((1, SG_WIDTH), const),
        ],
        out_specs=[
            pl.BlockSpec((None, NA_HEADS, PROJ_TM, HEAD_DIM), by_head),
            pl.BlockSpec((None, NA_HEADS, PROJ_TM, HEAD_DIM), by_head),
            pl.BlockSpec((None, NA_HEADS, PROJ_TM, HEAD_DIM), by_head),
            pl.BlockSpec((None, PROJ_TM, SG_WIDTH), row),
            pl.BlockSpec((None, PROJ_TM, SG_WIDTH), row),
        ],
        out_shape=[
            jax.ShapeDtypeStruct((b, NA_HEADS, s, HEAD_DIM), BF16),
            jax.ShapeDtypeStruct((b, NA_HEADS, s, HEAD_DIM), BF16),
            jax.ShapeDtypeStruct((b, NA_HEADS, s, HEAD_DIM), BF16),
            jax.ShapeDtypeStruct((b, s, SG_WIDTH), F32),
            jax.ShapeDtypeStruct((b, s, SG_WIDTH), BF16),
        ],
        scratch_shapes=[pltpu.VMEM((PROJ_TM, D_MODEL), BF16), pltpu.VMEM((PROJ_TM, SG_WIDTH), F32)]
        + _weight_scratch(D_MODEL, n_in),
        compiler_params=_params("arbitrary", "arbitrary"),
        name="odd_in",
    )(x, gain.reshape(1, D_MODEL), w_in,
      (q_gain * HEAD_DIM ** -0.5).reshape(1, HEAD_DIM), k_gain.reshape(1, HEAD_DIM),
      v_gain.reshape(1, SG_WIDTH))


def _dilated_kernel(slopes_ref, q0_ref, q1_ref, q2_ref, k0_ref, k1_ref, k2_ref, v0_ref, v1_ref, v2_ref,
                    o_ref, on_ref, ls_ref, s_ref, bias_ref, *, seq):
    hh = pl.program_id(1)
    step = pl.program_id(2)
    q_refs = (q0_ref, q1_ref, q2_ref)
    k_refs = (k0_ref, k1_ref, k2_ref)
    v_refs = (v0_ref, v1_ref, v2_ref)

    @pl.when(step == 0)
    def _():
        nk = DIL_TQ + 2 * DIL_RADIUS
        base0 = (lax.broadcasted_iota(jnp.int32, (DIL_TQ, nk), 1)
                 - lax.broadcasted_iota(jnp.int32, (DIL_TQ, nk), 0))
        for g, (_, dil) in enumerate(DIL_PAIRS):
            slope_g = slopes_ref[g * DIL_HEADS_PER_GROUP + hh] * dil
            for kind in range(3):
                dist = jnp.abs(base0 - kind * DIL_RADIUS)
                bias_ref[g, kind] = jnp.where(dist <= DIL_RADIUS, -slope_g * dist.astype(F32), NEG_INF)

    for g, (window, dil) in enumerate(DIL_PAIRS):
        assert window // (2 * dil) == DIL_RADIUS
        q_ref, k_ref, v_ref = q_refs[g], k_refs[g], v_refs[g]
        sub_len = seq // dil
        per_res = DIL_TA // dil
        nq = min(per_res, DIL_TQ)
        n_sub = per_res // nq
        n_keys = min(nq + 2 * DIL_RADIUS, sub_len)
        assert nq == DIL_TQ and n_keys == DIL_TQ + 2 * DIL_RADIUS

        def place(t, nq=nq, n_sub=n_sub, per_res=per_res, sub_len=sub_len, n_keys=n_keys):
            r = t // n_sub
            q_off = pl.multiple_of((t % n_sub) * nq, nq)
            q_pos = step * per_res + q_off
            k_pos = pl.multiple_of(jnp.clip(q_pos - DIL_RADIUS, 0, sub_len - n_keys), DIL_RADIUS)
            return r, q_off, q_pos, k_pos

        def scores(t, slot, q_ref=q_ref, k_ref=k_ref, nq=nq, n_keys=n_keys, g=g, place=place):
            r, q_off, q_pos, k_pos = place(t)
            q = q_ref[r, pl.ds(q_off, nq), :]
            k = k_ref[r, pl.ds(k_pos, n_keys), :]
            s = lax.dot_general(q, k, (((1,), (1,)), ((), ())), preferred_element_type=F32)
            s_ref[slot, 0:nq, 0:n_keys] = s + bias_ref[g, (q_pos - k_pos) // DIL_RADIUS]

        def finish(t, slot, v_ref=v_ref, g=g, dil=dil, nq=nq, n_keys=n_keys, place=place):
            r, q_off, _, k_pos = place(t)
            v = v_ref[r, pl.ds(k_pos, n_keys), :]
            s = s_ref[slot, 0:nq, 0:n_keys]
            m = jnp.max(s, axis=-1, keepdims=True)
            p = jnp.exp(s - m)
            den = jnp.sum(p, axis=-1, keepdims=True)
            out = jnp.dot(p.astype(BF16), v, preferred_element_type=F32) / den
            lse = jnp.broadcast_to(m + jnp.log(den), (nq, HEAD_DIM))
            if dil == 1:
                rows = pl.ds(q_off, nq)
            else:
                rows = pl.ds(r + dil * q_off, nq, stride=dil)
            on_ref[g, rows, :] = out
            ls_ref[g, rows, :] = lse

        n_tiles = dil * n_sub
        assert n_tiles % DIL_UNROLL == 0

        def body(i, carry, scores=scores, finish=finish):
            t0 = i * DIL_UNROLL
            scores(t0, 0)
            for u in range(DIL_UNROLL):
                if u + 1 < DIL_UNROLL:
                    scores(t0 + u + 1, (u + 1) % 2)
                finish(t0 + u, u % 2)
            return carry

        lax.fori_loop(0, n_tiles // DIL_UNROLL, body, 0)

    def merge(c, carry):
        rows = pl.ds(pl.multiple_of(c * DIL_TQ, DIL_TQ), DIL_TQ)
        lse = [ls_ref[g, rows, :] for g in range(len(DIL_PAIRS))]
        top = jnp.maximum(jnp.maximum(lse[0], lse[1]), lse[2])
        num = jnp.zeros((DIL_TQ, HEAD_DIM), F32)
        den = jnp.zeros((DIL_TQ, HEAD_DIM), F32)
        for g in range(len(DIL_PAIRS)):
            w = jnp.exp(lse[g] - top)
            num = num + w * on_ref[g, rows, :]
            den = den + w
        o_ref[rows, :] = (num / den).astype(o_ref.dtype)
        return carry

    lax.fori_loop(0, DIL_TA // DIL_TQ, merge, 0)


def _dilated(q, k, v, slopes):
    b, _, dil0, s, _ = q[0].shape
    assert dil0 == 1
    n_groups = len(DIL_PAIRS)

    def q_spec(dil):
        return pl.BlockSpec((None, None, dil, DIL_TA // dil, HEAD_DIM), lambda bi, hh, i: (bi, hh, 0, i, 0))

    def kv_spec(dil):
        return pl.BlockSpec((None, None, dil, s // dil, HEAD_DIM), lambda bi, hh, i: (bi, hh, 0, 0, 0))

    dils = [dil for _, dil in DIL_PAIRS]
    return pl.pallas_call(
        functools.partial(_dilated_kernel, seq=s),
        grid=(b, DIL_HEADS_PER_GROUP, s // DIL_TA),
        in_specs=[pl.BlockSpec(memory_space=pltpu.SMEM)]
        + [q_spec(d) for d in dils] + [kv_spec(d) for d in dils] + [kv_spec(d) for d in dils],
        out_specs=pl.BlockSpec((None, None, DIL_TA, HEAD_DIM), lambda bi, hh, i: (bi, hh, i, 0)),
        out_shape=jax.ShapeDtypeStruct((b, DIL_HEADS_PER_GROUP, s, HEAD_DIM), BF16),
        scratch_shapes=[pltpu.VMEM((n_groups, DIL_TA, HEAD_DIM), F32),
                        pltpu.VMEM((n_groups, DIL_TA, HEAD_DIM), F32),
                        pltpu.VMEM((2, DIL_TQ, DIL_TQ + 2 * DIL_RADIUS), F32),
                        pltpu.VMEM((n_groups, 3, DIL_TQ, DIL_TQ + 2 * DIL_RADIUS), F32)],
        compiler_params=_params("parallel", "parallel", "arbitrary"),
        name="dilated_attn",
    )(slopes, *q, *k, *v)


def _even_out_kernel(a_ref, prev_ref, next_ref, yb_ref, pw_ref, ps_ref, wo_hbm, x_ref, o_ref,
                     ext_ref, y_ref, wo_ref, wstage_ref, w_sems, *, seq):
    pl.when(_first_grid_step())(functools.partial(_load_weight_as_bf16, wo_hbm, wo_ref, wstage_ref, w_sems))
    i = pl.program_id(1)
    tm = OUT_TM
    ext_ref[0:POOL_HALO, :] = jnp.where(i == 0, 0.0, prev_ref[...])
    ext_ref[POOL_HALO:POOL_HALO + tm, :] = a_ref[...]
    ext_ref[POOL_HALO + tm:, :] = jnp.where(i == pl.num_programs(1) - 1, 0.0, next_ref[...])
    pos = i * tm + lax.broadcasted_iota(jnp.int32, (tm, 1), 0)
    for g, window in enumerate(POOL_WINDOWS):
        half = window // 2
        cols = slice(g * HEAD_DIM, (g + 1) * HEAD_DIM)
        total = ext_ref[POOL_HALO - half:POOL_HALO - half + tm, cols]
        for shift in range(-half + 1, half):
            total = total + ext_ref[POOL_HALO + shift:POOL_HALO + shift + tm, cols]
        count = (jnp.minimum(pos + half, seq) - jnp.maximum(pos - half, 0)).astype(F32)
        pooled = total / count - a_ref[:, cols]
        ya = jnp.dot(pooled.astype(BF16), pw_ref[g], preferred_element_type=F32) * ps_ref[:, cols]
        y_ref[:, cols] = ya.astype(BF16)
    for hd in range(DIL_HEADS_PER_GROUP):
        y_ref[:, POOL_WIDTH + hd * HEAD_DIM:POOL_WIDTH + (hd + 1) * HEAD_DIM] = yb_ref[hd]
    o_ref[...] = x_ref[...] + jnp.dot(y_ref[...], wo_ref[...], preferred_element_type=F32)


def _even_out(a, yb, pool_w, pool_scale, w_out, x):
    b, s, _ = x.shape
    tm = OUT_TM
    halo_blocks = tm // POOL_HALO
    tile = lambda bi, i: (bi, i, 0)
    return pl.pallas_call(
        functools.partial(_even_out_kernel, seq=s),
        grid=(b, s // tm),
        in_specs=[
            pl.BlockSpec((None, tm, POOL_WIDTH), tile),
            pl.BlockSpec((None, POOL_HALO, POOL_WIDTH),
                         lambda bi, i: (bi, jnp.maximum(i * halo_blocks - 1, 0), 0)),
            pl.BlockSpec((None, POOL_HALO, POOL_WIDTH),
                         lambda bi, i: (bi, jnp.minimum((i + 1) * halo_blocks, s // POOL_HALO - 1), 0)),
            pl.BlockSpec((None, DIL_HEADS_PER_GROUP, tm, HEAD_DIM), lambda bi, i: (bi, 0, i, 0)),
            _resident((len(POOL_WINDOWS), HEAD_DIM, HEAD_DIM), lambda bi, i: (0, 0, 0)),
            pl.BlockSpec((1, POOL_WIDTH), lambda bi, i: (0, 0)),
            pl.BlockSpec(memory_space=pl.ANY),
            pl.BlockSpec((None, tm, D_MODEL), tile),
        ],
        out_specs=pl.BlockSpec((None, tm, D_MODEL), tile),
        out_shape=jax.ShapeDtypeStruct((b, s, D_MODEL), F32),
        scratch_shapes=[pltpu.VMEM((tm + 2 * POOL_HALO, POOL_WIDTH), F32),
                        pltpu.VMEM((tm, POOL_WIDTH + DIL_OUT), BF16)]
        + _weight_scratch(POOL_WIDTH + DIL_OUT, D_MODEL),
        compiler_params=_params("arbitrary", "arbitrary"),
        name="even_out",
    )(a, a, a, yb, pool_w, pool_scale.reshape(1, POOL_WIDTH), w_out, x)


def _na_fill_bias(pair_ref, bias_ref, rb, rows):
    k_row0 = min(max(rb * NA_QROWS - NA_ROWS // 2, 0), rows - NA_KROWS)
    left_half = lax.broadcasted_iota(jnp.int32, (GRID_W, 2 * GRID_W), 1) < GRID_W
    for qr in range(NA_QROWS):
        r = rb * NA_QROWS + qr
        row_start = min(max(r - NA_ROWS // 2, 0), rows - NA_ROWS)
        for p in range(NA_KROWS // 2):
            kr = k_row0 + 2 * p
            ok_left = row_start <= kr < row_start + NA_ROWS
            ok_right = row_start <= kr + 1 < row_start + NA_ROWS
            if ok_left or ok_right:
                block = pair_ref[kr + 1 - r + NA_ROWS - 1]
                if not ok_right:
                    block = jnp.where(left_half, block, NEG_INF)
                if not ok_left:
                    block = jnp.where(left_half, NEG_INF, block)
            else:
                block = jnp.full((GRID_W, 2 * GRID_W), NEG_INF, F32)
            bias_ref[qr * GRID_W:(qr + 1) * GRID_W, p * 2 * GRID_W:(p + 1) * 2 * GRID_W] = block


def _na_kernel(q_ref, k_ref, v_ref, pair_ref, o_ref, bias_ref, s_ref, *, rows):
    n_blocks = rows // NA_QROWS
    tq = NA_QROWS * GRID_W
    n_keys = NA_KROWS * GRID_W
    kinds = (0, 1, n_blocks - 1)
    for slot, kind in enumerate(kinds):
        _na_fill_bias(pair_ref, bias_ref.at[slot], kind, rows)

    def key_start(rb):
        return min(max(rb * NA_QROWS - NA_ROWS // 2, 0), rows - NA_KROWS) * GRID_W

    def scores(rb):
        kind_slot = 0 if rb == 0 else (2 if rb == n_blocks - 1 else 1)
        q = q_ref[rb * tq:(rb + 1) * tq, :]
        k = k_ref[key_start(rb):key_start(rb) + n_keys, :]
        s_ref[rb % 2] = (lax.dot_general(q, k, (((1,), (1,)), ((), ())), preferred_element_type=F32)
                         + bias_ref[kind_slot])

    def finish(rb):
        v = v_ref[key_start(rb):key_start(rb) + n_keys, :]
        s = s_ref[rb % 2]
        m = jnp.max(s, axis=-1, keepdims=True)
        p = jnp.exp(s - m)
        den = jnp.sum(p, axis=-1, keepdims=True)
        out = jnp.dot(p.astype(BF16), v, preferred_element_type=F32) / den
        o_ref[rb * tq:(rb + 1) * tq, :] = out.astype(o_ref.dtype)

    scores(0)
    for rb in range(n_blocks):
        if rb + 1 < n_blocks:
            scores(rb + 1)
        finish(rb)


def _na_pair_tables(rpb):
    n_heads, n_rel_rows, n_rel_cols = rpb.shape
    c = np.arange(GRID_W)
    col_start = np.clip(c - NA_COLS // 2, 0, GRID_W - NA_COLS)
    col_ok = (c[None, :] >= col_start[:, None]) & (c[None, :] < col_start[:, None] + NA_COLS)
    rel_c = np.clip(c[None, :] - c[:, None], -(NA_COLS - 1), NA_COLS - 1) + (NA_COLS - 1)
    pick = (rel_c.reshape(1, -1) == np.arange(n_rel_cols)[:, None]).astype(np.float32)
    by_col = jnp.dot(rpb.astype(F32).reshape(n_heads * n_rel_rows, n_rel_cols), pick,
                     precision=lax.Precision.HIGHEST).reshape(n_heads, n_rel_rows, GRID_W, GRID_W)
    by_col = jnp.where(col_ok[None, None], by_col, NEG_INF)
    masked = jnp.full((n_heads, 1, GRID_W, GRID_W), NEG_INF, F32)
    padded = jnp.concatenate([masked, by_col, masked], axis=1)
    return jnp.concatenate([padded[:, :-1], padded[:, 1:]], axis=-1)


def _neighbourhood(q, k, v, pair_tables):
    b, _, s, _ = q.shape
    rows = s // GRID_W
    n_blocks = rows // NA_QROWS
    assert n_blocks >= 3 and rows >= NA_KROWS
    tq = NA_QROWS * GRID_W
    n_keys = NA_KROWS * GRID_W
    head = lambda bi, h: (bi, h, 0, 0)
    return pl.pallas_call(
        functools.partial(_na_kernel, rows=rows),
        grid=(b, NA_HEADS),
        in_specs=[
            pl.BlockSpec((None, None, s, HEAD_DIM), head),
            pl.BlockSpec((None, None, s, HEAD_DIM), head),
            pl.BlockSpec((None, None, s, HEAD_DIM), head),
            pl.BlockSpec((None, 2 * NA_ROWS, GRID_W, 2 * GRID_W), lambda bi, h: (h, 0, 0, 0)),
        ],
        out_specs=pl.BlockSpec((None, None, s, HEAD_DIM), head),
        out_shape=jax.ShapeDtypeStruct((b, NA_HEADS, s, HEAD_DIM), BF16),
        scratch_shapes=[pltpu.VMEM((3, tq, n_keys), F32), pltpu.VMEM((2, tq, n_keys), F32)],
        compiler_params=_params("parallel", "parallel"),
        name="neighbourhood_attn",
    )(q, k, v, pair_tables)


def _odd_out_kernel(yc_ref, u_ref, vn_ref, ws_ref, bs_ref, wo_hbm, x_ref, o_ref, y_ref,
                    wo_ref, wstage_ref, w_sems):
    pl.when(_first_grid_step())(functools.partial(_load_weight_as_bf16, wo_hbm, wo_ref, wstage_ref, w_sems))
    width = SG_WIDTH // SG_GROUPS
    for c in range(OUT_TM // SG_CHUNK):
        rows = slice(c * SG_CHUNK, (c + 1) * SG_CHUNK)
        for g in range(SG_GROUPS):
            cols = slice(g * width, (g + 1) * width)
            sv = jnp.dot(ws_ref[g], vn_ref[rows, cols], preferred_element_type=F32) + bs_ref[g]
            y_ref[rows, NA_WIDTH + g * width:NA_WIDTH + (g + 1) * width] = (u_ref[rows, cols] * sv).astype(BF16)
    for hd in range(NA_HEADS):
        y_ref[:, hd * HEAD_DIM:(hd + 1) * HEAD_DIM] = yc_ref[hd]
    o_ref[...] = x_ref[...] + jnp.dot(y_ref[...], wo_ref[...], preferred_element_type=F32)


def _odd_out(yc, u, vn, w_s, b_s, w_out, x):
    b, s, _ = x.shape
    tm = OUT_TM
    width = SG_WIDTH // SG_GROUPS
    tile = lambda bi, i: (bi, i, 0)
    return pl.pallas_call(
        _odd_out_kernel,
        grid=(b, s // tm),
        in_specs=[
            pl.BlockSpec((None, NA_HEADS, tm, HEAD_DIM), lambda bi, i: (bi, 0, i, 0)),
            pl.BlockSpec((None, tm, SG_WIDTH), tile),
            pl.BlockSpec((None, tm, SG_WIDTH), tile),
            _resident((SG_GROUPS, SG_CHUNK, SG_CHUNK), lambda bi, i: (0, 0, 0)),
            _resident((SG_GROUPS, SG_CHUNK, width), lambda bi, i: (0, 0, 0)),
            pl.BlockSpec(memory_space=pl.ANY),
            pl.BlockSpec((None, tm, D_MODEL), tile),
        ],
        out_specs=pl.BlockSpec((None, tm, D_MODEL), tile),
        out_shape=jax.ShapeDtypeStruct((b, s, D_MODEL), F32),
        scratch_shapes=[pltpu.VMEM((tm, NA_WIDTH + SG_WIDTH), BF16)]
        + _weight_scratch(NA_WIDTH + SG_WIDTH, D_MODEL),
        compiler_params=_params("arbitrary", "arbitrary"),
        name="odd_out",
    )(yc, u, vn, w_s, b_s, w_out, x)


def kernel(x, norm_ffn1, norm_mix, norm_ffn2, norm_out, ffn_w_gate, ffn_w_up, ffn_w_down, even_w_in, pool_w, pool_scale, dil_q_gain, dil_k_gain, even_w_out, odd_w_in, na_q_gain, na_k_gain, na_rpb, sg_v_gain, sg_w, sg_b, odd_w_out):
    depth = norm_ffn1.shape[0]
    slopes = jnp.asarray(2.0 ** (-8.0 * np.arange(1, DIL_HEADS + 1) / DIL_HEADS), dtype=F32)
    width = SG_WIDTH // SG_GROUPS

    for layer in range(depth):
        x = _ffn(x, norm_ffn1[layer], ffn_w_gate, ffn_w_up, ffn_w_down, layer, 0)
        if layer % 2 == 0:
            e = layer // 2
            a, q, k, v = _even_in(x, norm_mix[layer], even_w_in[e], dil_q_gain[e], dil_k_gain[e])
            yb = _dilated(q, k, v, slopes)
            x = _even_out(a, yb, pool_w[e].astype(BF16), pool_scale[e], even_w_out[e], x)
        else:
            o = layer // 2
            q, k, v, u, vn = _odd_in(x, norm_mix[layer], odd_w_in[o],
                                     na_q_gain[o], na_k_gain[o], sg_v_gain[o])
            yc = _neighbourhood(q, k, v, _na_pair_tables(na_rpb[o]))
            b_s = jnp.broadcast_to(sg_b[o][:, :, None], (SG_GROUPS, SG_CHUNK, width))
            x = _odd_out(yc, u, vn, sg_w[o].astype(BF16), b_s, odd_w_out[o], x)
        x = _ffn(x, norm_ffn2[layer], ffn_w_gate, ffn_w_up, ffn_w_down, layer, 1, gain_out=norm_out[layer])
    return x
```

```python
import functools

import numpy as np
import jax
import jax.numpy as jnp
from jax import lax
from jax.experimental import pallas as pl
from jax.experimental.pallas import tpu as pltpu

F32 = jnp.float32
BF16 = jnp.bfloat16

D_MODEL = 2048
D_FF = 5632
HEAD_DIM = 128
POOL_WINDOWS = (2, 4, 8, 16)
POOL_WIDTH = 512
POOL_HALO = 8
DIL_PAIRS = ((128, 1), (512, 4), (2048, 16))
DIL_HEADS_PER_GROUP = 4
DIL_HEADS = 12
DIL_OUT = DIL_HEADS_PER_GROUP * HEAD_DIM
GRID_W = 64
NA_ROWS = 8
NA_COLS = 16
NA_HEADS = 8
NA_WIDTH = NA_HEADS * HEAD_DIM
NA_QROWS = 4
NA_KROWS = NA_QROWS + NA_ROWS
SG_CHUNK = 128
SG_GROUPS = 8
SG_WIDTH = 1024
RMS_EPS = 1e-6
NEG_INF = -1e30

VMEM_LIMIT_BYTES = 60 * 1024 * 1024

FFN_TM = 1024
FFN_TF = 256
FFN_NBUF = 3
PROJ_TM = 512
PROJ_TN = 512
DIL_TQ = 128
DIL_TA = 2048
DIL_RADIUS = 64
DIL_UNROLL = 16
OUT_TM = 512
W_CHUNK = 512


def _params(*semantics):
    return pltpu.CompilerParams(dimension_semantics=semantics, vmem_limit_bytes=VMEM_LIMIT_BYTES)


def _resident(block_shape, index_map):
    return pl.BlockSpec(block_shape, index_map, pipeline_mode=pl.Buffered(1))


def _rms(x, gain):
    return x * lax.rsqrt(jnp.mean(x * x, axis=-1, keepdims=True) + RMS_EPS) * gain


def _first_grid_step():
    return (pl.program_id(0) == 0) & (pl.program_id(1) == 0)


def _load_weight_as_bf16(w_hbm, wbf_ref, stage_ref, sems):
    n_chunks = wbf_ref.shape[1] // W_CHUNK

    def copy(c):
        cols = pl.ds(c * W_CHUNK, W_CHUNK)
        return pltpu.make_async_copy(w_hbm.at[:, cols], stage_ref.at[c % 2], sems.at[c % 2])

    copy(0).start()
    for c in range(n_chunks):
        if c + 1 < n_chunks:
            copy(c + 1).start()
        copy(c).wait()
        wbf_ref[:, c * W_CHUNK:(c + 1) * W_CHUNK] = stage_ref[c % 2].astype(BF16)


def _weight_scratch(rows, cols):
    assert cols % W_CHUNK == 0
    return [pltpu.VMEM((rows, cols), BF16), pltpu.VMEM((2, rows, W_CHUNK), F32), pltpu.SemaphoreType.DMA((2,))]


def _ffn_kernel(x_hbm, g_ref, wg_hbm, wu_hbm, wd_hbm, gout_ref, o_ref, h_ref, xbuf_ref, x_sem,
                wg_buf, wu_buf, wd_buf, w_sems, *, final_norm, layer, idx):
    j = pl.program_id(2)
    n_j = D_FF // FFN_TF
    tiles_per_batch = pl.num_programs(1)
    n_tiles = pl.num_programs(0) * tiles_per_batch
    tile = pl.program_id(0) * tiles_per_batch + pl.program_id(1)
    step = tile * n_j + j
    n_steps = n_tiles * n_j

    def x_copy(t):
        rows = pl.ds(pl.multiple_of((t % tiles_per_batch) * FFN_TM, FFN_TM), FFN_TM)
        return pltpu.make_async_copy(x_hbm.at[t // tiles_per_batch, rows, :], xbuf_ref, x_sem)

    def w_copies(s):
        slot = s % FFN_NBUF
        cols = pl.ds(pl.multiple_of((s % n_j) * FFN_TF, FFN_TF), FFN_TF)
        return (pltpu.make_async_copy(wg_hbm.at[layer, idx, :, cols], wg_buf.at[slot], w_sems.at[0, slot]),
                pltpu.make_async_copy(wu_hbm.at[layer, idx, :, cols], wu_buf.at[slot], w_sems.at[1, slot]),
                pltpu.make_async_copy(wd_hbm.at[layer, idx, cols, :], wd_buf.at[slot], w_sems.at[2, slot]))

    @pl.when(step == 0)
    def _():
        for s in range(FFN_NBUF - 1):
            for copy in w_copies(s):
                copy.start()

    @pl.when(step + FFN_NBUF - 1 < n_steps)
    def _():
        for copy in w_copies(step + FFN_NBUF - 1):
            copy.start()

    @pl.when(j == 0)
    def _():
        @pl.when(tile == 0)
        def _():
            x_copy(tile).start()

        x_copy(tile).wait()
        x = xbuf_ref[...]
        h_ref[...] = _rms(x, g_ref[...]).astype(BF16)
        o_ref[...] = x

    @pl.when((j == 1) & (tile + 1 < n_tiles))
    def _():
        x_copy(tile + 1).start()

    for copy in w_copies(step):
        copy.wait()
    slot = step % FFN_NBUF
    h = h_ref[...]
    gate = jnp.dot(h, wg_buf[slot].astype(BF16), preferred_element_type=F32)
    up = jnp.dot(h, wu_buf[slot].astype(BF16), preferred_element_type=F32)
    act = (gate * jax.nn.sigmoid(gate)) * up * 0.5
    o_ref[...] += jnp.dot(act.astype(BF16), wd_buf[slot].astype(BF16), preferred_element_type=F32)

    if final_norm:
        @pl.when(j == n_j - 1)
        def _():
            o_ref[...] = _rms(o_ref[...], gout_ref[...])


def _ffn(x, gain, w_gate, w_up, w_down, layer, idx, gain_out=None):
    b, s, _ = x.shape
    assert D_FF // FFN_TF >= 2
    final_norm = gain_out is not None
    if gain_out is None:
        gain_out = gain
    tile = lambda bi, i, j: (bi, i, 0)
    const = lambda bi, i, j: (0, 0)
    return pl.pallas_call(
        functools.partial(_ffn_kernel, final_norm=final_norm, layer=layer, idx=idx),
        grid=(b, s // FFN_TM, D_FF // FFN_TF),
        in_specs=[
            pl.BlockSpec(memory_space=pl.ANY),
            pl.BlockSpec((1, D_MODEL), const),
            pl.BlockSpec(memory_space=pl.ANY),
            pl.BlockSpec(memory_space=pl.ANY),
            pl.BlockSpec(memory_space=pl.ANY),
            pl.BlockSpec((1, D_MODEL), const),
        ],
        out_specs=pl.BlockSpec((None, FFN_TM, D_MODEL), tile),
        out_shape=jax.ShapeDtypeStruct((b, s, D_MODEL), F32),
        scratch_shapes=[pltpu.VMEM((FFN_TM, D_MODEL), BF16), pltpu.VMEM((FFN_TM, D_MODEL), F32),
                        pltpu.SemaphoreType.DMA(()),
                        pltpu.VMEM((FFN_NBUF, D_MODEL, FFN_TF), F32),
                        pltpu.VMEM((FFN_NBUF, D_MODEL, FFN_TF), F32),
                        pltpu.VMEM((FFN_NBUF, FFN_TF, D_MODEL), F32),
                        pltpu.SemaphoreType.DMA((3, FFN_NBUF))],
        compiler_params=_params("arbitrary", "arbitrary", "arbitrary"),
        name="ffn_final" if final_norm else "ffn",
    )(x, gain.reshape(1, D_MODEL), w_gate, w_up, w_down, gain_out.reshape(1, D_MODEL))


def _head_store(z, out_ref, head0, gain=None):
    for hd in range(PROJ_TN // HEAD_DIM):
        zh = z[:, hd * HEAD_DIM:(hd + 1) * HEAD_DIM]
        if gain is not None:
            zh = _rms(zh, gain)
        out_ref[head0 + hd] = zh.astype(out_ref.dtype)


def _store_by_residue(z, out_ref, stage_ref, dil, gain=None):
    tm = z.shape[0]
    for hd in range(z.shape[1] // HEAD_DIM):
        cols = slice(hd * HEAD_DIM, (hd + 1) * HEAD_DIM)
        zh = z[:, cols]
        if gain is not None:
            zh = _rms(zh, gain)
        if dil == 1:
            out_ref[hd, 0] = zh.astype(BF16)
        else:
            stage_ref[hd] = zh
            for r in range(dil):
                out_ref[hd, r] = stage_ref[hd, pl.ds(r, tm // dil, stride=dil), :].astype(BF16)


def _even_in_kernel(x_ref, g_ref, w_hbm, qg_ref, kg_ref, a_ref, *rest):
    n_groups = len(DIL_PAIRS)
    q_refs, k_refs, v_refs = rest[:n_groups], rest[n_groups:2 * n_groups], rest[2 * n_groups:3 * n_groups]
    h_ref, stage_ref, w_ref, wstage_ref, w_sems = rest[3 * n_groups:]
    pl.when(_first_grid_step())(functools.partial(_load_weight_as_bf16, w_hbm, w_ref, wstage_ref, w_sems))
    h_ref[...] = _rms(x_ref[...], g_ref[...]).astype(BF16)

    def chunk(c):
        return jnp.dot(h_ref[...], w_ref[:, c * PROJ_TN:(c + 1) * PROJ_TN], preferred_element_type=F32)

    a_ref[...] = chunk(0)
    for g, (_, dil) in enumerate(DIL_PAIRS):
        _store_by_residue(chunk(1 + g), q_refs[g], stage_ref, dil, qg_ref[...])
        _store_by_residue(chunk(1 + n_groups + g), k_refs[g], stage_ref, dil, kg_ref[...])
        _store_by_residue(chunk(1 + 2 * n_groups + g), v_refs[g], stage_ref, dil)


def _even_in(x, gain, w_in, q_gain, k_gain):
    b, s, _ = x.shape
    n_in = w_in.shape[1]
    assert PROJ_TN == DIL_OUT
    row = lambda bi, i: (bi, i, 0)
    const = lambda bi, i: (0, 0)
    hpg = DIL_HEADS_PER_GROUP
    group_specs = [pl.BlockSpec((None, hpg, dil, PROJ_TM // dil, HEAD_DIM), lambda bi, i: (bi, 0, 0, i, 0))
                   for _, dil in DIL_PAIRS]
    group_shapes = [jax.ShapeDtypeStruct((b, hpg, dil, s // dil, HEAD_DIM), BF16) for _, dil in DIL_PAIRS]
    outs = pl.pallas_call(
        _even_in_kernel,
        grid=(b, s // PROJ_TM),
        in_specs=[
            pl.BlockSpec((None, PROJ_TM, D_MODEL), row),
            pl.BlockSpec((1, D_MODEL), const),
            pl.BlockSpec(memory_space=pl.ANY),
            pl.BlockSpec((1, HEAD_DIM), const),
            pl.BlockSpec((1, HEAD_DIM), const),
        ],
        out_specs=[pl.BlockSpec((None, PROJ_TM, POOL_WIDTH), row)] + group_specs * 3,
        out_shape=[jax.ShapeDtypeStruct((b, s, POOL_WIDTH), F32)] + group_shapes * 3,
        scratch_shapes=[pltpu.VMEM((PROJ_TM, D_MODEL), BF16),
                        pltpu.VMEM((DIL_HEADS_PER_GROUP, PROJ_TM, HEAD_DIM), F32)]
        + _weight_scratch(D_MODEL, n_in),
        compiler_params=_params("arbitrary", "arbitrary"),
        name="even_in",
    )(x, gain.reshape(1, D_MODEL), w_in,
      (q_gain * HEAD_DIM ** -0.5).reshape(1, HEAD_DIM), k_gain.reshape(1, HEAD_DIM))
    n_groups = len(DIL_PAIRS)
    return outs[0], outs[1:1 + n_groups], outs[1 + n_groups:1 + 2 * n_groups], outs[1 + 2 * n_groups:]


def _odd_in_kernel(x_ref, g_ref, w_hbm, qg_ref, kg_ref, vg_ref, q_ref, k_ref, v_ref, u_ref, vn_ref,
                   h_ref, gv_ref, w_ref, wstage_ref, w_sems):
    pl.when(_first_grid_step())(functools.partial(_load_weight_as_bf16, w_hbm, w_ref, wstage_ref, w_sems))
    h_ref[...] = _rms(x_ref[...], g_ref[...]).astype(BF16)

    def chunk(c):
        return jnp.dot(h_ref[...], w_ref[:, c * PROJ_TN:(c + 1) * PROJ_TN], preferred_element_type=F32)

    per = NA_WIDTH // PROJ_TN
    heads_per_chunk = PROJ_TN // HEAD_DIM
    for c in range(per):
        _head_store(chunk(c), q_ref, c * heads_per_chunk, qg_ref[...])
        _head_store(chunk(per + c), k_ref, c * heads_per_chunk, kg_ref[...])
        _head_store(chunk(2 * per + c), v_ref, c * heads_per_chunk)
    per_sg = SG_WIDTH // PROJ_TN
    for c in range(per_sg):
        u_ref[:, c * PROJ_TN:(c + 1) * PROJ_TN] = jax.nn.gelu(chunk(3 * per + c))
        gv_ref[:, c * PROJ_TN:(c + 1) * PROJ_TN] = jax.nn.gelu(chunk(3 * per + per_sg + c))
    vn_ref[...] = _rms(gv_ref[...], vg_ref[...]).astype(BF16)


def _odd_in(x, gain, w_in, q_gain, k_gain, v_gain):
    b, s, _ = x.shape
    n_in = w_in.shape[1]
    row = lambda bi, i: (bi, i, 0)
    by_head = lambda bi, i: (bi, 0, i, 0)
    const = lambda bi, i: (0, 0)
    return pl.pallas_call(
        _odd_in_kernel,
        grid=(b, s // PROJ_TM),
        in_specs=[
            pl.BlockSpec((None, PROJ_TM, D_MODEL), row),
            pl.BlockSpec((1, D_MODEL), const),
            pl.BlockSpec(memory_space=pl.ANY),
            pl.BlockSpec((1, HEAD_DIM), const),
            pl.BlockSpec((1, HEAD_DIM), const),
            pl.BlockSpec((1, SG_WIDTH), const),
        ],
        out_specs=[
            pl.BlockSpec((None, NA_HEADS, PROJ_TM, HEAD_DIM), by_head),
            pl.BlockSpec((None, NA_HEADS, PROJ_TM, HEAD_DIM), by_head),
            pl.BlockSpec((None, NA_HEADS, PROJ_TM, HEAD_DIM), by_head),
            pl.BlockSpec((None, PROJ_TM, SG_WIDTH), row),
            pl.BlockSpec((None, PROJ_TM, SG_WIDTH), row),
        ],
        out_shape=[
            jax.ShapeDtypeStruct((b, NA_HEADS, s, HEAD_DIM), BF16),
            jax.ShapeDtypeStruct((b, NA_HEADS, s, HEAD_DIM), BF16),
            jax.ShapeDtypeStruct((b, NA_HEADS, s, HEAD_DIM), BF16),
            jax.ShapeDtypeStruct((b, s, SG_WIDTH), F32),
            jax.ShapeDtypeStruct((b, s, SG_WIDTH), BF16),
        ],
        scratch_shapes=[pltpu.VMEM((PROJ_TM, D_MODEL), BF16), pltpu.VMEM((PROJ_TM, SG_WIDTH), F32)]
        + _weight_scratch(D_MODEL, n_in),
        compiler_params=_params("arbitrary", "arbitrary"),
        name="odd_in",
    )(x, gain.reshape(1, D_MODEL), w_in,
      (q_gain * HEAD_DIM ** -0.5).reshape(1, HEAD_DIM), k_gain.reshape(1, HEAD_DIM),
      v_gain.reshape(1, SG_WIDTH))


def _dilated_kernel(slopes_ref, q0_ref, q1_ref, q2_ref, k0_ref, k1_ref, k2_ref, v0_ref, v1_ref, v2_ref,
                    o_ref, on_ref, ls_ref, s_ref, bias_ref, *, seq):
    hh = pl.program_id(1)
    step = pl.program_id(2)
    q_refs = (q0_ref, q1_ref, q2_ref)
    k_refs = (k0_ref, k1_ref, k2_ref)
    v_refs = (v0_ref, v1_ref, v2_ref)

    @pl.when(step == 0)
    def _():
        nk = DIL_TQ + 2 * DIL_RADIUS
        base0 = (lax.broadcasted_iota(jnp.int32, (DIL_TQ, nk), 1)
                 - lax.broadcasted_iota(jnp.int32, (DIL_TQ, nk), 0))
        for g, (_, dil) in enumerate(DIL_PAIRS):
            slope_g = slopes_ref[g * DIL_HEADS_PER_GROUP + hh] * dil
            for kind in range(3):
                dist = jnp.abs(base0 - kind * DIL_RADIUS)
                bias_ref[g, kind] = jnp.where(dist <= DIL_RADIUS, -slope_g * dist.astype(F32), NEG_INF)

    for g, (window, dil) in enumerate(DIL_PAIRS):
        assert window // (2 * dil) == DIL_RADIUS
        q_ref, k_ref, v_ref = q_refs[g], k_refs[g], v_refs[g]
        sub_len = seq // dil
        per_res = DIL_TA // dil
        nq = min(per_res, DIL_TQ)
        n_sub = per_res // nq
        n_keys = min(nq + 2 * DIL_RADIUS, sub_len)
        assert nq == DIL_TQ and n_keys == DIL_TQ + 2 * DIL_RADIUS

        def place(t, nq=nq, n_sub=n_sub, per_res=per_res, sub_len=sub_len, n_keys=n_keys):
            r = t // n_sub
            q_off = pl.multiple_of((t % n_sub) * nq, nq)
            q_pos = step * per_res + q_off
            k_pos = pl.multiple_of(jnp.clip(q_pos - DIL_RADIUS, 0, sub_len - n_keys), DIL_RADIUS)
            return r, q_off, q_pos, k_pos

        def scores(t, slot, q_ref=q_ref, k_ref=k_ref, nq=nq, n_keys=n_keys, g=g, place=place):
            r, q_off, q_pos, k_pos = place(t)
            q = q_ref[r, pl.ds(q_off, nq), :]
            k = k_ref[r, pl.ds(k_pos, n_keys), :]
            s = lax.dot_general(q, k, (((1,), (1,)), ((), ())), preferred_element_type=F32)
            s_ref[slot, 0:nq, 0:n_keys] = s + bias_ref[g, (q_pos - k_pos) // DIL_RADIUS]

        def finish(t, slot, v_ref=v_ref, g=g, dil=dil, nq=nq, n_keys=n_keys, place=place):
            r, q_off, _, k_pos = place(t)
            v = v_ref[r, pl.ds(k_pos, n_keys), :]
            s = s_ref[slot, 0:nq, 0:n_keys]
            m = jnp.max(s, axis=-1, keepdims=True)
            p = jnp.exp(s - m)
            den = jnp.sum(p, axis=-1, keepdims=True)
            out = jnp.dot(p.astype(BF16), v, preferred_element_type=F32) / den
            lse = jnp.broadcast_to(m + jnp.log(den), (nq, HEAD_DIM))
            if dil == 1:
                rows = pl.ds(q_off, nq)
            else:
                rows = pl.ds(r + dil * q_off, nq, stride=dil)
            on_ref[g, rows, :] = out
            ls_ref[g, rows, :] = lse

        n_tiles = dil * n_sub
        assert n_tiles % DIL_UNROLL == 0

        def body(i, carry, scores=scores, finish=finish):
            t0 = i * DIL_UNROLL
            scores(t0, 0)
            for u in range(DIL_UNROLL):
                if u + 1 < DIL_UNROLL:
                    scores(t0 + u + 1, (u + 1) % 2)
                finish(t0 + u, u % 2)
            return carry

        lax.fori_loop(0, n_tiles // DIL_UNROLL, body, 0)

    def merge(c, carry):
        rows = pl.ds(pl.multiple_of(c * DIL_TQ, DIL_TQ), DIL_TQ)
        lse = [ls_ref[g, rows, :] for g in range(len(DIL_PAIRS))]
        top = jnp.maximum(jnp.maximum(lse[0], lse[1]), lse[2])
        num = jnp.zeros((DIL_TQ, HEAD_DIM), F32)
        den = jnp.zeros((DIL_TQ, HEAD_DIM), F32)
        for g in range(len(DIL_PAIRS)):
            w = jnp.exp(lse[g] - top)
            num = num + w * on_ref[g, rows, :]
            den = den + w
        o_ref[rows, :] = (num / den).astype(o_ref.dtype)
        return carry

    lax.fori_loop(0, DIL_TA // DIL_TQ, merge, 0)


def _dilated(q, k, v, slopes):
    b, _, dil0, s, _ = q[0].shape
    assert dil0 == 1
    n_groups = len(DIL_PAIRS)

    def q_spec(dil):
        return pl.BlockSpec((None, None, dil, DIL_TA // dil, HEAD_DIM), lambda bi, hh, i: (bi, hh, 0, i, 0))

    def kv_spec(dil):
        return pl.BlockSpec((None, None, dil, s // dil, HEAD_DIM), lambda bi, hh, i: (bi, hh, 0, 0, 0))

    dils = [dil for _, dil in DIL_PAIRS]
    return pl.pallas_call(
        functools.partial(_dilated_kernel, seq=s),
        grid=(b, DIL_HEADS_PER_GROUP, s // DIL_TA),
        in_specs=[pl.BlockSpec(memory_space=pltpu.SMEM)]
        + [q_spec(d) for d in dils] + [kv_spec(d) for d in dils] + [kv_spec(d) for d in dils],
        out_specs=pl.BlockSpec((None, None, DIL_TA, HEAD_DIM), lambda bi, hh, i: (bi, hh, i, 0)),
        out_shape=jax.ShapeDtypeStruct((b, DIL_HEADS_PER_GROUP, s, HEAD_DIM), BF16),
        scratch_shapes=[pltpu.VMEM((n_groups, DIL_TA, HEAD_DIM), F32),
                        pltpu.VMEM((n_groups, DIL_TA, HEAD_DIM), F32),
                        pltpu.VMEM((2, DIL_TQ, DIL_TQ + 2 * DIL_RADIUS), F32),
                        pltpu.VMEM((n_groups, 3, DIL_TQ, DIL_TQ + 2 * DIL_RADIUS), F32)],
        compiler_params=_params("parallel", "parallel", "arbitrary"),
        name="dilated_attn",
    )(slopes, *q, *k, *v)


def _even_out_kernel(a_ref, prev_ref, next_ref, yb_ref, pw_ref, ps_ref, wo_hbm, x_ref, o_ref,
                     ext_ref, y_ref, wo_ref, wstage_ref, w_sems, *, seq):
    pl.when(_first_grid_step())(functools.partial(_load_weight_as_bf16, wo_hbm, wo_ref, wstage_ref, w_sems))
    i = pl.program_id(1)
    tm = OUT_TM
    ext_ref[0:POOL_HALO, :] = jnp.where(i == 0, 0.0, prev_ref[...])
    ext_ref[POOL_HALO:POOL_HALO + tm, :] = a_ref[...]
    ext_ref[POOL_HALO + tm:, :] = jnp.where(i == pl.num_programs(1) - 1, 0.0, next_ref[...])
    pos = i * tm + lax.broadcasted_iota(jnp.int32, (tm, 1), 0)
    for g, window in enumerate(POOL_WINDOWS):
        half = window // 2
        cols = slice(g * HEAD_DIM, (g + 1) * HEAD_DIM)
        total = ext_ref[POOL_HALO - half:POOL_HALO - half + tm, cols]
        for shift in range(-half + 1, half):
            total = total + ext_ref[POOL_HALO + shift:POOL_HALO + shift + tm, cols]
        count = (jnp.minimum(pos + half, seq) - jnp.maximum(pos - half, 0)).astype(F32)
        pooled = total / count - a_ref[:, cols]
        ya = jnp.dot(pooled.astype(BF16), pw_ref[g], preferred_element_type=F32) * ps_ref[:, cols]
        y_ref[:, cols] = ya.astype(BF16)
    for hd in range(DIL_HEADS_PER_GROUP):
        y_ref[:, POOL_WIDTH + hd * HEAD_DIM:POOL_WIDTH + (hd + 1) * HEAD_DIM] = yb_ref[hd]
    o_ref[...] = x_ref[...] + jnp.dot(y_ref[...], wo_ref[...], preferred_element_type=F32)


def _even_out(a, yb, pool_w, pool_scale, w_out, x):
    b, s, _ = x.shape
    tm = OUT_TM
    halo_blocks = tm // POOL_HALO
    tile = lambda bi, i: (bi, i, 0)
    return pl.pallas_call(
        functools.partial(_even_out_kernel, seq=s),
        grid=(b, s // tm),
        in_specs=[
            pl.BlockSpec((None, tm, POOL_WIDTH), tile),
            pl.BlockSpec((None, POOL_HALO, POOL_WIDTH),
                         lambda bi, i: (bi, jnp.maximum(i * halo_blocks - 1, 0), 0)),
            pl.BlockSpec((None, POOL_HALO, POOL_WIDTH),
                         lambda bi, i: (bi, jnp.minimum((i + 1) * halo_blocks, s // POOL_HALO - 1), 0)),
            pl.BlockSpec((None, DIL_HEADS_PER_GROUP, tm, HEAD_DIM), lambda bi, i: (bi, 0, i, 0)),
            _resident((len(POOL_WINDOWS), HEAD_DIM, HEAD_DIM), lambda bi, i: (0, 0, 0)),
            pl.BlockSpec((1, POOL_WIDTH), lambda bi, i: (0, 0)),
            pl.BlockSpec(memory_space=pl.ANY),
            pl.BlockSpec((None, tm, D_MODEL), tile),
        ],
        out_specs=pl.BlockSpec((None, tm, D_MODEL), tile),
        out_shape=jax.ShapeDtypeStruct((b, s, D_MODEL), F32),
        scratch_shapes=[pltpu.VMEM((tm + 2 * POOL_HALO, POOL_WIDTH), F32),
                        pltpu.VMEM((tm, POOL_WIDTH + DIL_OUT), BF16)]
        + _weight_scratch(POOL_WIDTH + DIL_OUT, D_MODEL),
        compiler_params=_params("arbitrary", "arbitrary"),
        name="even_out",
    )(a, a, a, yb, pool_w, pool_scale.reshape(1, POOL_WIDTH), w_out, x)


def _na_fill_bias(pair_ref, bias_ref, rb, rows):
    k_row0 = min(max(rb * NA_QROWS - NA_ROWS // 2, 0), rows - NA_KROWS)
    left_half = lax.broadcasted_iota(jnp.int32, (GRID_W, 2 * GRID_W), 1) < GRID_W
    for qr in range(NA_QROWS):
        r = rb * NA_QROWS + qr
        row_start = min(max(r - NA_ROWS // 2, 0), rows - NA_ROWS)
        for p in range(NA_KROWS // 2):
            kr = k_row0 + 2 * p
            ok_left = row_start <= kr < row_start + NA_ROWS
            ok_right = row_start <= kr + 1 < row_start + NA_ROWS
            if ok_left or ok_right:
                block = pair_ref[kr + 1 - r + NA_ROWS - 1]
                if not ok_right:
                    block = jnp.where(left_half, block, NEG_INF)
                if not ok_left:
                    block = jnp.where(left_half, NEG_INF, block)
            else:
                block = jnp.full((GRID_W, 2 * GRID_W), NEG_INF, F32)
            bias_ref[qr * GRID_W:(qr + 1) * GRID_W, p * 2 * GRID_W:(p + 1) * 2 * GRID_W] = block


def _na_kernel(q_ref, k_ref, v_ref, pair_ref, o_ref, bias_ref, s_ref, *, rows):
    n_blocks = rows // NA_QROWS
    tq = NA_QROWS * GRID_W
    n_keys = NA_KROWS * GRID_W
    kinds = (0, 1, n_blocks - 1)
    for slot, kind in enumerate(kinds):
        _na_fill_bias(pair_ref, bias_ref.at[slot], kind, rows)

    def key_start(rb):
        return min(max(rb * NA_QROWS - NA_ROWS // 2, 0), rows - NA_KROWS) * GRID_W

    def scores(rb):
        kind_slot = 0 if rb == 0 else (2 if rb == n_blocks - 1 else 1)
        q = q_ref[rb * tq:(rb + 1) * tq, :]
        k = k_ref[key_start(rb):key_start(rb) + n_keys, :]
        s_ref[rb % 2] = (lax.dot_general(q, k, (((1,), (1,)), ((), ())), preferred_element_type=F32)
                         + bias_ref[kind_slot])

    def finish(rb):
        v = v_ref[key_start(rb):key_start(rb) + n_keys, :]
        s = s_ref[rb % 2]
        m = jnp.max(s, axis=-1, keepdims=True)
        p = jnp.exp(s - m)
        den = jnp.sum(p, axis=-1, keepdims=True)
        out = jnp.dot(p.astype(BF16), v, preferred_element_type=F32) / den
        o_ref[rb * tq:(rb + 1) * tq, :] = out.astype(o_ref.dtype)

    scores(0)
    for rb in range(n_blocks):
        if rb + 1 < n_blocks:
            scores(rb + 1)
        finish(rb)


def _na_pair_tables(rpb):
    n_heads, n_rel_rows, n_rel_cols = rpb.shape
    c = np.arange(GRID_W)
    col_start = np.clip(c - NA_COLS // 2, 0, GRID_W - NA_COLS)
    col_ok = (c[None, :] >= col_start[:, None]) & (c[None, :] < col_start[:, None] + NA_COLS)
    rel_c = np.clip(c[None, :] - c[:, None], -(NA_COLS - 1), NA_COLS - 1) + (NA_COLS - 1)
    pick = (rel_c.reshape(1, -1) == np.arange(n_rel_cols)[:, None]).astype(np.float32)
    by_col = jnp.dot(rpb.astype(F32).reshape(n_heads * n_rel_rows, n_rel_cols), pick,
                     precision=lax.Precision.HIGHEST).reshape(n_heads, n_rel_rows, GRID_W, GRID_W)
    by_col = jnp.where(col_ok[None, None], by_col, NEG_INF)
    masked = jnp.full((n_heads, 1, GRID_W, GRID_W), NEG_INF, F32)
    padded = jnp.concatenate([masked, by_col, masked], axis=1)
    return jnp.concatenate([padded[:, :-1], padded[:, 1:]], axis=-1)


def _neighbourhood(q, k, v, pair_tables):
    b, _, s, _ = q.shape
    rows = s // GRID_W
    n_blocks = rows // NA_QROWS
    assert n_blocks >= 3 and rows >= NA_KROWS
    tq = NA_QROWS * GRID_W
    n_keys = NA_KROWS * GRID_W
    head = lambda bi, h: (bi, h, 0, 0)
    return pl.pallas_call(
        functools.partial(_na_kernel, rows=rows),
        grid=(b, NA_HEADS),
        in_specs=[
            pl.BlockSpec((None, None, s, HEAD_DIM), head),
            pl.BlockSpec((None, None, s, HEAD_DIM), head),
            pl.BlockSpec((None, None, s, HEAD_DIM), head),
            pl.BlockSpec((None, 2 * NA_ROWS, GRID_W, 2 * GRID_W), lambda bi, h: (h, 0, 0, 0)),
        ],
        out_specs=pl.BlockSpec((None, None, s, HEAD_DIM), head),
        out_shape=jax.ShapeDtypeStruct((b, NA_HEADS, s, HEAD_DIM), BF16),
        scratch_shapes=[pltpu.VMEM((3, tq, n_keys), F32), pltpu.VMEM((2, tq, n_keys), F32)],
        compiler_params=_params("parallel", "parallel"),
        name="neighbourhood_attn",
    )(q, k, v, pair_tables)


def _odd_out_kernel(yc_ref, u_ref, vn_ref, ws_ref, bs_ref, wo_hbm, x_ref, o_ref, y_ref,
                    wo_ref, wstage_ref, w_sems):
    pl.when(_first_grid_step())(functools.partial(_load_weight_as_bf16, wo_hbm, wo_ref, wstage_ref, w_sems))
    width = SG_WIDTH // SG_GROUPS
    for c in range(OUT_TM // SG_CHUNK):
        rows = slice(c * SG_CHUNK, (c + 1) * SG_CHUNK)
        for g in range(SG_GROUPS):
            cols = slice(g * width, (g + 1) * width)
            sv = jnp.dot(ws_ref[g], vn_ref[rows, cols], preferred_element_type=F32) + bs_ref[g]
            y_ref[rows, NA_WIDTH + g * width:NA_WIDTH + (g + 1) * width] = (u_ref[rows, cols] * sv).astype(BF16)
    for hd in range(NA_HEADS):
        y_ref[:, hd * HEAD_DIM:(hd + 1) * HEAD_DIM] = yc_ref[hd]
    o_ref[...] = x_ref[...] + jnp.dot(y_ref[...], wo_ref[...], preferred_element_type=F32)


def _odd_out(yc, u, vn, w_s, b_s, w_out, x):
    b, s, _ = x.shape
    tm = OUT_TM
    width = SG_WIDTH // SG_GROUPS
    tile = lambda bi, i: (bi, i, 0)
    return pl.pallas_call(
        _odd_out_kernel,
        grid=(b, s // tm),
        in_specs=[
            pl.BlockSpec((None, NA_HEADS, tm, HEAD_DIM), lambda bi, i: (bi, 0, i, 0)),
            pl.BlockSpec((None, tm, SG_WIDTH), tile),
            pl.BlockSpec((None, tm, SG_WIDTH), tile),
            _resident((SG_GROUPS, SG_CHUNK, SG_CHUNK), lambda bi, i: (0, 0, 0)),
            _resident((SG_GROUPS, SG_CHUNK, width), lambda bi, i: (0, 0, 0)),
            pl.BlockSpec(memory_space=pl.ANY),
            pl.BlockSpec((None, tm, D_MODEL), tile),
        ],
        out_specs=pl.BlockSpec((None, tm, D_MODEL), tile),
        out_shape=jax.ShapeDtypeStruct((b, s, D_MODEL), F32),
        scratch_shapes=[pltpu.VMEM((tm, NA_WIDTH + SG_WIDTH), BF16)]
        + _weight_scratch(NA_WIDTH + SG_WIDTH, D_MODEL),
        compiler_params=_params("arbitrary", "arbitrary"),
        name="odd_out",
    )(yc, u, vn, w_s, b_s, w_out, x)


def kernel(x, norm_ffn1, norm_mix, norm_ffn2, norm_out, ffn_w_gate, ffn_w_up, ffn_w_down, even_w_in, pool_w, pool_scale, dil_q_gain, dil_k_gain, even_w_out, odd_w_in, na_q_gain, na_k_gain, na_rpb, sg_v_gain, sg_w, sg_b, odd_w_out):
    depth = norm_ffn1.shape[0]
    slopes = jnp.asarray(2.0 ** (-8.0 * np.arange(1, DIL_HEADS + 1) / DIL_HEADS), dtype=F32)
    width = SG_WIDTH // SG_GROUPS

    for layer in range(depth):
        x = _ffn(x, norm_ffn1[layer], ffn_w_gate, ffn_w_up, ffn_w_down, layer, 0)
        if layer % 2 == 0:
            e = layer // 2
            a, q, k, v = _even_in(x, norm_mix[layer], even_w_in[e], dil_q_gain[e], dil_k_gain[e])
            yb = _dilated(q, k, v, slopes)
            x = _even_out(a, yb, pool_w[e].astype(BF16), pool_scale[e], even_w_out[e], x)
        else:
            o = layer // 2
            q, k, v, u, vn = _odd_in(x, norm_mix[layer], odd_w_in[o],
                                     na_q_gain[o], na_k_gain[o], sg_v_gain[o])
            yc = _neighbourhood(q, k, v, _na_pair_tables(na_rpb[o]))
            b_s = jnp.broadcast_to(sg_b[o][:, :, None], (SG_GROUPS, SG_CHUNK, width))
            x = _odd_out(yc, u, vn, sg_w[o].astype(BF16), b_s, odd_w_out[o], x)
        x = _ffn(x, norm_ffn2[layer], ffn_w_gate, ffn_w_up, ffn_w_down, layer, 1, gain_out=norm_out[layer])
    return x
```

```python
import functools

import numpy as np
import jax
import jax.numpy as jnp
from jax import lax
from jax.experimental import pallas as pl
from jax.experimental.pallas import tpu as pltpu

F32 = jnp.float32
BF16 = jnp.bfloat16

D_MODEL = 2048
D_FF = 5632
HEAD_DIM = 128
POOL_WINDOWS = (2, 4, 8, 16)
POOL_WIDTH = 512
POOL_HALO = 8
DIL_PAIRS = ((128, 1), (512, 4), (2048, 16))
DIL_HEADS_PER_GROUP = 4
DIL_HEADS = 12
DIL_OUT = DIL_HEADS_PER_GROUP * HEAD_DIM
GRID_W = 64
NA_ROWS = 8
NA_COLS = 16
NA_HEADS = 8
NA_WIDTH = NA_HEADS * HEAD_DIM
NA_QROWS = 4
NA_KROWS = NA_QROWS + NA_ROWS
SG_CHUNK = 128
SG_GROUPS = 8
SG_WIDTH = 1024
RMS_EPS = 1e-6
NEG_INF = -1e30

VMEM_LIMIT_BYTES = 60 * 1024 * 1024

FFN_TM = 1024
FFN_TF = 512
FFN_TC = 256
PROJ_TM = 512
PROJ_TN = 512
DIL_TQ = 128
DIL_TA = 2048
DIL_RADIUS = 64
DIL_UNROLL = 16
OUT_TM = 512
W_CHUNK = 512


def _params(*semantics):
    return pltpu.CompilerParams(dimension_semantics=semantics, vmem_limit_bytes=VMEM_LIMIT_BYTES)


def _resident(block_shape, index_map):
    return pl.BlockSpec(block_shape, index_map, pipeline_mode=pl.Buffered(1))


def _rms(x, gain):
    return x * lax.rsqrt(jnp.mean(x * x, axis=-1, keepdims=True) + RMS_EPS) * gain


def _first_grid_step():
    return (pl.program_id(0) == 0) & (pl.program_id(1) == 0)


def _load_weight_as_bf16(w_hbm, wbf_ref, stage_ref, sems):
    n_chunks = wbf_ref.shape[1] // W_CHUNK

    def copy(c):
        cols = pl.ds(c * W_CHUNK, W_CHUNK)
        return pltpu.make_async_copy(w_hbm.at[:, cols], stage_ref.at[c % 2], sems.at[c % 2])

    copy(0).start()
    for c in range(n_chunks):
        if c + 1 < n_chunks:
            copy(c + 1).start()
        copy(c).wait()
        wbf_ref[:, c * W_CHUNK:(c + 1) * W_CHUNK] = stage_ref[c % 2].astype(BF16)


def _weight_scratch(rows, cols):
    assert cols % W_CHUNK == 0
    return [pltpu.VMEM((rows, cols), BF16), pltpu.VMEM((2, rows, W_CHUNK), F32), pltpu.SemaphoreType.DMA((2,))]


def _ffn_kernel(x_hbm, g_ref, wg_ref, wu_ref, wd_ref, gout_ref, g2_ref, o_ref, h_ref, xbuf_ref, x_sem, *,
                final_norm, chained):
    j = pl.program_id(2)
    n_j = D_FF // FFN_TF
    tiles_per_batch = pl.num_programs(1)
    n_tiles = pl.num_programs(0) * tiles_per_batch
    tile = pl.program_id(0) * tiles_per_batch + pl.program_id(1)

    def x_copy(t):
        rows = pl.ds(pl.multiple_of((t % tiles_per_batch) * FFN_TM, FFN_TM), FFN_TM)
        return pltpu.make_async_copy(x_hbm.at[t // tiles_per_batch, rows, :], xbuf_ref, x_sem)

    @pl.when(j == 0)
    def _():
        @pl.when(tile == 0)
        def _():
            x_copy(tile).start()

        x_copy(tile).wait()
        x = xbuf_ref[...]
        h_ref[...] = _rms(x, g_ref[...]).astype(BF16)
        o_ref[...] = x

    @pl.when((j == 1) & (tile + 1 < n_tiles))
    def _():
        x_copy(tile + 1).start()

    if chained:
        @pl.when(j == n_j)
        def _():
            h_ref[...] = _rms(o_ref[...], g2_ref[...]).astype(BF16)

    h = h_ref[...]
    for c in range(FFN_TF // FFN_TC):
        cols = slice(c * FFN_TC, (c + 1) * FFN_TC)
        gate = jnp.dot(h, wg_ref[:, cols].astype(BF16), preferred_element_type=F32)
        up = jnp.dot(h, wu_ref[:, cols].astype(BF16), preferred_element_type=F32)
        act = (gate * jax.nn.sigmoid(gate)) * up * 0.5
        o_ref[...] += jnp.dot(act.astype(BF16), wd_ref[cols, :].astype(BF16), preferred_element_type=F32)

    if final_norm:
        @pl.when(j == n_j - 1)
        def _():
            o_ref[...] = _rms(o_ref[...], gout_ref[...])


def _ffn(x, gain, w_gate, w_up, w_down, layer, idx, gain_out=None, then=None):
    b, s, _ = x.shape
    n_j = D_FF // FFN_TF
    assert n_j >= 2
    final_norm = gain_out is not None
    chained = then is not None
    assert final_norm or not chained
    if gain_out is None:
        gain_out = gain
    layer2, idx2, gain2 = then if chained else (layer, idx, gain)
    tile = lambda bi, i, j: (bi, i, 0)
    const = lambda bi, i, j: (0, 0)

    def w_index(j):
        second = j // n_j
        return layer + second * (layer2 - layer), idx + second * (idx2 - idx), j % n_j

    def cols_map(bi, i, j):
        la, ix, t = w_index(j)
        return (la, ix, 0, t)

    def rows_map(bi, i, j):
        la, ix, t = w_index(j)
        return (la, ix, t, 0)

    return pl.pallas_call(
        functools.partial(_ffn_kernel, final_norm=final_norm, chained=chained),
        grid=(b, s // FFN_TM, n_j * (2 if chained else 1)),
        in_specs=[
            pl.BlockSpec(memory_space=pl.ANY),
            pl.BlockSpec((1, D_MODEL), const),
            pl.BlockSpec((None, None, D_MODEL, FFN_TF), cols_map),
            pl.BlockSpec((None, None, D_MODEL, FFN_TF), cols_map),
            pl.BlockSpec((None, None, FFN_TF, D_MODEL), rows_map),
            pl.BlockSpec((1, D_MODEL), const),
            pl.BlockSpec((1, D_MODEL), const),
        ],
        out_specs=pl.BlockSpec((None, FFN_TM, D_MODEL), tile),
        out_shape=jax.ShapeDtypeStruct((b, s, D_MODEL), F32),
        scratch_shapes=[pltpu.VMEM((FFN_TM, D_MODEL), BF16), pltpu.VMEM((FFN_TM, D_MODEL), F32),
                        pltpu.SemaphoreType.DMA(())],
        compiler_params=_params("arbitrary", "arbitrary", "arbitrary"),
        name="ffn_pair" if chained else ("ffn_final" if final_norm else "ffn"),
    )(x, gain.reshape(1, D_MODEL), w_gate, w_up, w_down, gain_out.reshape(1, D_MODEL),
      gain2.reshape(1, D_MODEL))


def _head_store(z, out_ref, head0, gain=None):
    for hd in range(PROJ_TN // HEAD_DIM):
        zh = z[:, hd * HEAD_DIM:(hd + 1) * HEAD_DIM]
        if gain is not None:
            zh = _rms(zh, gain)
        out_ref[head0 + hd] = zh.astype(out_ref.dtype)


def _store_by_residue(z, out_ref, stage_ref, dil, gain=None):
    tm = z.shape[0]
    for hd in range(z.shape[1] // HEAD_DIM):
        cols = slice(hd * HEAD_DIM, (hd + 1) * HEAD_DIM)
        zh = z[:, cols]
        if gain is not None:
            zh = _rms(zh, gain)
        if dil == 1:
            out_ref[hd, 0] = zh.astype(BF16)
        else:
            stage_ref[hd] = zh
            for r in range(dil):
                out_ref[hd, r] = stage_ref[hd, pl.ds(r, tm // dil, stride=dil), :].astype(BF16)


def _even_in_kernel(x_ref, g_ref, w_hbm, qg_ref, kg_ref, a_ref, *rest):
    n_groups = len(DIL_PAIRS)
    q_refs, k_refs, v_refs = rest[:n_groups], rest[n_groups:2 * n_groups], rest[2 * n_groups:3 * n_groups]
    h_ref, stage_ref, w_ref, wstage_ref, w_sems = rest[3 * n_groups:]
    pl.when(_first_grid_step())(functools.partial(_load_weight_as_bf16, w_hbm, w_ref, wstage_ref, w_sems))
    h_ref[...] = _rms(x_ref[...], g_ref[...]).astype(BF16)

    def chunk(c):
        return jnp.dot(h_ref[...], w_ref[:, c * PROJ_TN:(c + 1) * PROJ_TN], preferred_element_type=F32)

    a_ref[...] = chunk(0)
    for g, (_, dil) in enumerate(DIL_PAIRS):
        _store_by_residue(chunk(1 + g), q_refs[g], stage_ref, dil, qg_ref[...])
        _store_by_residue(chunk(1 + n_groups + g), k_refs[g], stage_ref, dil, kg_ref[...])
        _store_by_residue(chunk(1 + 2 * n_groups + g), v_refs[g], stage_ref, dil)


def _even_in(x, gain, w_in, q_gain, k_gain):
    b, s, _ = x.shape
    n_in = w_in.shape[1]
    assert PROJ_TN == DIL_OUT
    row = lambda bi, i: (bi, i, 0)
    const = lambda bi, i: (0, 0)
    hpg = DIL_HEADS_PER_GROUP
    group_specs = [pl.BlockSpec((None, hpg, dil, PROJ_TM // dil, HEAD_DIM), lambda bi, i: (bi, 0, 0, i, 0))
                   for _, dil in DIL_PAIRS]
    group_shapes = [jax.ShapeDtypeStruct((b, hpg, dil, s // dil, HEAD_DIM), BF16) for _, dil in DIL_PAIRS]
    outs = pl.pallas_call(
        _even_in_kernel,
        grid=(b, s // PROJ_TM),
        in_specs=[
            pl.BlockSpec((None, PROJ_TM, D_MODEL), row),
            pl.BlockSpec((1, D_MODEL), const),
            pl.BlockSpec(memory_space=pl.ANY),
            pl.BlockSpec((1, HEAD_DIM), const),
            pl.BlockSpec((1, HEAD_DIM), const),
        ],
        out_specs=[pl.BlockSpec((None, PROJ_TM, POOL_WIDTH), row)] + group_specs * 3,
        out_shape=[jax.ShapeDtypeStruct((b, s, POOL_WIDTH), F32)] + group_shapes * 3,
        scratch_shapes=[pltpu.VMEM((PROJ_TM, D_MODEL), BF16),
                        pltpu.VMEM((DIL_HEADS_PER_GROUP, PROJ_TM, HEAD_DIM), F32)]
        + _weight_scratch(D_MODEL, n_in),
        compiler_params=_params("arbitrary", "arbitrary"),
        name="even_in",
    )(x, gain.reshape(1, D_MODEL), w_in,
      (q_gain * HEAD_DIM ** -0.5).reshape(1, HEAD_DIM), k_gain.reshape(1, HEAD_DIM))
    n_groups = len(DIL_PAIRS)
    return outs[0], outs[1:1 + n_groups], outs[1 + n_groups:1 + 2 * n_groups], outs[1 + 2 * n_groups:]


def _odd_in_kernel(x_ref, g_ref, w_hbm, qg_ref, kg_ref, vg_ref, q_ref, k_ref, v_ref, u_ref, vn_ref,
                   h_ref, gv_ref, w_ref, wstage_ref, w_sems):
    pl.when(_first_grid_step())(functools.partial(_load_weight_as_bf16, w_hbm, w_ref, wstage_ref, w_sems))
    h_ref[...] = _rms(x_ref[...], g_ref[...]).astype(BF16)

    def chunk(c):
        return jnp.dot(h_ref[...], w_ref[:, c * PROJ_TN:(c + 1) * PROJ_TN], preferred_element_type=F32)

    per = NA_WIDTH // PROJ_TN
    heads_per_chunk = PROJ_TN // HEAD_DIM
    for c in range(per):
        _head_store(chunk(c), q_ref, c * heads_per_chunk, qg_ref[...])
        _head_store(chunk(per + c), k_ref, c * heads_per_chunk, kg_ref[...])
        _head_store(chunk(2 * per + c), v_ref, c * heads_per_chunk)
    per_sg = SG_WIDTH // PROJ_TN
    for c in range(per_sg):
        u_ref[:, c * PROJ_TN:(c + 1) * PROJ_TN] = jax.nn.gelu(chunk(3 * per + c))
        gv_ref[:, c * PROJ_TN:(c + 1) * PROJ_TN] = jax.nn.gelu(chunk(3 * per + per_sg + c))
    vn_ref[...] = _rms(gv_ref[...], vg_ref[...]).astype(BF16)


def _odd_in(x, gain, w_in, q_gain, k_gain, v_gain):
    b, s, _ = x.shape
    n_in = w_in.shape[1]
    row = lambda bi, i: (bi, i, 0)
    by_head = lambda bi, i: (bi, 0, i, 0)
    const = lambda bi, i: (0, 0)
    return pl.pallas_call(
        _odd_in_kernel,
        grid=(b, s // PROJ_TM),
        in_specs=[
            pl.BlockSpec((None, PROJ_TM, D_MODEL), row),
            pl.BlockSpec((1, D_MODEL), const),
            pl.BlockSpec(memory_space=pl.ANY),
            pl.BlockSpec((1, HEAD_DIM), const),
            pl.BlockSpec((1, HEAD_DIM), const),
            pl.BlockSpec((1, SG_WIDTH), const),
        ],
        out_specs=[
            pl.BlockSpec((None, NA_HEADS, PROJ_TM, HEAD_DIM), by_head),
            pl.BlockSpec((None, NA_HEADS, PROJ_TM, HEAD_DIM), by_head),
            pl.BlockSpec((None, NA_HEADS, PROJ_TM, HEAD_DIM), by_head),
            pl.BlockSpec((None, PROJ_TM, SG_WIDTH), row),
            pl.BlockSpec((None, PROJ_TM, SG_WIDTH), row),
        ],
        out_shape=[
            jax.ShapeDtypeStruct((b, NA_HEADS, s, HEAD_DIM), BF16),
            jax.ShapeDtypeStruct((b, NA_HEADS, s, HEAD_DIM), BF16),
            jax.ShapeDtypeStruct((b, NA_HEADS, s, HEAD_DIM), BF16),
            jax.ShapeDtypeStruct((b, s, SG_WIDTH), F32),
            jax.ShapeDtypeStruct((b, s, SG_WIDTH), BF16),
        ],
        scratch_shapes=[pltpu.VMEM((PROJ_TM, D_MODEL), BF16), pltpu.VMEM((PROJ_TM, SG_WIDTH), F32)]
        + _weight_scratch(D_MODEL, n_in),
        compiler_params=_params("arbitrary", "arbitrary"),
        name="odd_in",
    )(x, gain.reshape(1, D_MODEL), w_in,
      (q_gain * HEAD_DIM ** -0.5).reshape(1, HEAD_DIM), k_gain.reshape(1, HEAD_DIM),
      v_gain.reshape(1, SG_WIDTH))


def _dilated_kernel(slopes_ref, q0_ref, q1_ref, q2_ref, k0_ref, k1_ref, k2_ref, v0_ref, v1_ref, v2_ref,
                    o_ref, on_ref, ls_ref, s_ref, bias_ref, *, seq):
    hh = pl.program_id(1)
    step = pl.program_id(2)
    q_refs = (q0_ref, q1_ref, q2_ref)
    k_refs = (k0_ref, k1_ref, k2_ref)
    v_refs = (v0_ref, v1_ref, v2_ref)

    @pl.when(step == 0)
    def _():
        nk = DIL_TQ + 2 * DIL_RADIUS
        base0 = (lax.broadcasted_iota(jnp.int32, (DIL_TQ, nk), 1)
                 - lax.broadcasted_iota(jnp.int32, (DIL_TQ, nk), 0))
        for g, (_, dil) in enumerate(DIL_PAIRS):
            slope_g = slopes_ref[g * DIL_HEADS_PER_GROUP + hh] * dil
            for kind in range(3):
                dist = jnp.abs(base0 - kind * DIL_RADIUS)
                bias_ref[g, kind] = jnp.where(dist <= DIL_RADIUS, -slope_g * dist.astype(F32), NEG_INF)

    for g, (window, dil) in enumerate(DIL_PAIRS):
        assert window // (2 * dil) == DIL_RADIUS
        q_ref, k_ref, v_ref = q_refs[g], k_refs[g], v_refs[g]
        sub_len = seq // dil
        per_res = DIL_TA // dil
        nq = min(per_res, DIL_TQ)
        n_sub = per_res // nq
        n_keys = min(nq + 2 * DIL_RADIUS, sub_len)
        assert nq == DIL_TQ and n_keys == DIL_TQ + 2 * DIL_RADIUS

        def place(t, nq=nq, n_sub=n_sub, per_res=per_res, sub_len=sub_len, n_keys=n_keys):
            r = t // n_sub
            q_off = pl.multiple_of((t % n_sub) * nq, nq)
            q_pos = step * per_res + q_off
            k_pos = pl.multiple_of(jnp.clip(q_pos - DIL_RADIUS, 0, sub_len - n_keys), DIL_RADIUS)
            return r, q_off, q_pos, k_pos

        def scores(t, slot, q_ref=q_ref, k_ref=k_ref, nq=nq, n_keys=n_keys, g=g, place=place):
            r, q_off, q_pos, k_pos = place(t)
            q = q_ref[r, pl.ds(q_off, nq), :]
            k = k_ref[r, pl.ds(k_pos, n_keys), :]
            s = lax.dot_general(q, k, (((1,), (1,)), ((), ())), preferred_element_type=F32)
            s_ref[slot, 0:nq, 0:n_keys] = s + bias_ref[g, (q_pos - k_pos) // DIL_RADIUS]

        def finish(t, slot, v_ref=v_ref, g=g, dil=dil, nq=nq, n_keys=n_keys, place=place):
            r, q_off, _, k_pos = place(t)
            v = v_ref[r, pl.ds(k_pos, n_keys), :]
            s = s_ref[slot, 0:nq, 0:n_keys]
            m = jnp.max(s, axis=-1, keepdims=True)
            p = jnp.exp(s - m)
            den = jnp.sum(p, axis=-1, keepdims=True)
            out = jnp.dot(p.astype(BF16), v, preferred_element_type=F32) / den
            lse = jnp.broadcast_to(m + jnp.log(den), (nq, HEAD_DIM))
            if dil == 1:
                rows = pl.ds(q_off, nq)
            else:
                rows = pl.ds(r + dil * q_off, nq, stride=dil)
            on_ref[g, rows, :] = out
            ls_ref[g, rows, :] = lse

        n_tiles = dil * n_sub
        assert n_tiles % DIL_UNROLL == 0

        def body(i, carry, scores=scores, finish=finish):
            t0 = i * DIL_UNROLL
            scores(t0, 0)
            for u in range(DIL_UNROLL):
                if u + 1 < DIL_UNROLL:
                    scores(t0 + u + 1, (u + 1) % 2)
                finish(t0 + u, u % 2)
            return carry

        lax.fori_loop(0, n_tiles // DIL_UNROLL, body, 0)

    def merge(c, carry):
        rows = pl.ds(pl.multiple_of(c * DIL_TQ, DIL_TQ), DIL_TQ)
        lse = [ls_ref[g, rows, :] for g in range(len(DIL_PAIRS))]
        top = jnp.maximum(jnp.maximum(lse[0], lse[1]), lse[2])
        num = jnp.zeros((DIL_TQ, HEAD_DIM), F32)
        den = jnp.zeros((DIL_TQ, HEAD_DIM), F32)
        for g in range(len(DIL_PAIRS)):
            w = jnp.exp(lse[g] - top)
            num = num + w * on_ref[g, rows, :]
            den = den + w
        o_ref[rows, :] = (num / den).astype(o_ref.dtype)
        return carry

    lax.fori_loop(0, DIL_TA // DIL_TQ, merge, 0)


def _dilated(q, k, v, slopes):
    b, _, dil0, s, _ = q[0].shape
    assert dil0 == 1
    n_groups = len(DIL_PAIRS)

    def q_spec(dil):
        return pl.BlockSpec((None, None, dil, DIL_TA // dil, HEAD_DIM), lambda bi, hh, i: (bi, hh, 0, i, 0))

    def kv_spec(dil):
        return pl.BlockSpec((None, None, dil, s // dil, HEAD_DIM), lambda bi, hh, i: (bi, hh, 0, 0, 0))

    dils = [dil for _, dil in DIL_PAIRS]
    return pl.pallas_call(
        functools.partial(_dilated_kernel, seq=s),
        grid=(b, DIL_HEADS_PER_GROUP, s // DIL_TA),
        in_specs=[pl.BlockSpec(memory_space=pltpu.SMEM)]
        + [q_spec(d) for d in dils] + [kv_spec(d) for d in dils] + [kv_spec(d) for d in dils],
        out_specs=pl.BlockSpec((None, None, DIL_TA, HEAD_DIM), lambda bi, hh, i: (bi, hh, i, 0)),
        out_shape=jax.ShapeDtypeStruct((b, DIL_HEADS_PER_GROUP, s, HEAD_DIM), BF16),
        scratch_shapes=[pltpu.VMEM((n_groups, DIL_TA, HEAD_DIM), F32),
                        pltpu.VMEM((n_groups, DIL_TA, HEAD_DIM), F32),
                        pltpu.VMEM((2, DIL_TQ, DIL_TQ + 2 * DIL_RADIUS), F32),
                        pltpu.VMEM((n_groups, 3, DIL_TQ, DIL_TQ + 2 * DIL_RADIUS), F32)],
        compiler_params=_params("parallel", "parallel", "arbitrary"),
        name="dilated_attn",
    )(slopes, *q, *k, *v)


def _even_out_kernel(a_ref, prev_ref, next_ref, yb_ref, pw_ref, ps_ref, wo_hbm, x_ref, o_ref,
                     ext_ref, y_ref, wo_ref, wstage_ref, w_sems, *, seq):
    pl.when(_first_grid_step())(functools.partial(_load_weight_as_bf16, wo_hbm, wo_ref, wstage_ref, w_sems))
    i = pl.program_id(1)
    tm = OUT_TM
    ext_ref[0:POOL_HALO, :] = jnp.where(i == 0, 0.0, prev_ref[...])
    ext_ref[POOL_HALO:POOL_HALO + tm, :] = a_ref[...]
    ext_ref[POOL_HALO + tm:, :] = jnp.where(i == pl.num_programs(1) - 1, 0.0, next_ref[...])
    pos = i * tm + lax.broadcasted_iota(jnp.int32, (tm, 1), 0)
    for g, window in enumerate(POOL_WINDOWS):
        half = window // 2
        cols = slice(g * HEAD_DIM, (g + 1) * HEAD_DIM)
        total = ext_ref[POOL_HALO - half:POOL_HALO - half + tm, cols]
        for shift in range(-half + 1, half):
            total = total + ext_ref[POOL_HALO + shift:POOL_HALO + shift + tm, cols]
        count = (jnp.minimum(pos + half, seq) - jnp.maximum(pos - half, 0)).astype(F32)
        pooled = total / count - a_ref[:, cols]
        ya = jnp.dot(pooled.astype(BF16), pw_ref[g], preferred_element_type=F32) * ps_ref[:, cols]
        y_ref[:, cols] = ya.astype(BF16)
    for hd in range(DIL_HEADS_PER_GROUP):
        y_ref[:, POOL_WIDTH + hd * HEAD_DIM:POOL_WIDTH + (hd + 1) * HEAD_DIM] = yb_ref[hd]
    o_ref[...] = x_ref[...] + jnp.dot(y_ref[...], wo_ref[...], preferred_element_type=F32)


def _even_out(a, yb, pool_w, pool_scale, w_out, x):
    b, s, _ = x.shape
    tm = OUT_TM
    halo_blocks = tm // POOL_HALO
    tile = lambda bi, i: (bi, i, 0)
    return pl.pallas_call(
        functools.partial(_even_out_kernel, seq=s),
        grid=(b, s // tm),
        in_specs=[
            pl.BlockSpec((None, tm, POOL_WIDTH), tile),
            pl.BlockSpec((None, POOL_HALO, POOL_WIDTH),
                         lambda bi, i: (bi, jnp.maximum(i * halo_blocks - 1, 0), 0)),
            pl.BlockSpec((None, POOL_HALO, POOL_WIDTH),
                         lambda bi, i: (bi, jnp.minimum((i + 1) * halo_blocks, s // POOL_HALO - 1), 0)),
            pl.BlockSpec((None, DIL_HEADS_PER_GROUP, tm, HEAD_DIM), lambda bi, i: (bi, 0, i, 0)),
            _resident((len(POOL_WINDOWS), HEAD_DIM, HEAD_DIM), lambda bi, i: (0, 0, 0)),
            pl.BlockSpec((1, POOL_WIDTH), lambda bi, i: (0, 0)),
            pl.BlockSpec(memory_space=pl.ANY),
            pl.BlockSpec((None, tm, D_MODEL), tile),
        ],
        out_specs=pl.BlockSpec((None, tm, D_MODEL), tile),
        out_shape=jax.ShapeDtypeStruct((b, s, D_MODEL), F32),
        scratch_shapes=[pltpu.VMEM((tm + 2 * POOL_HALO, POOL_WIDTH), F32),
                        pltpu.VMEM((tm, POOL_WIDTH + DIL_OUT), BF16)]
        + _weight_scratch(POOL_WIDTH + DIL_OUT, D_MODEL),
        compiler_params=_params("arbitrary", "arbitrary"),
        name="even_out",
    )(a, a, a, yb, pool_w, pool_scale.reshape(1, POOL_WIDTH), w_out, x)


def _na_fill_bias(pair_ref, bias_ref, rb, rows):
    k_row0 = min(max(rb * NA_QROWS - NA_ROWS // 2, 0), rows - NA_KROWS)
    left_half = lax.broadcasted_iota(jnp.int32, (GRID_W, 2 * GRID_W), 1) < GRID_W
    for qr in range(NA_QROWS):
        r = rb * NA_QROWS + qr
        row_start = min(max(r - NA_ROWS // 2, 0), rows - NA_ROWS)
        for p in range(NA_KROWS // 2):
            kr = k_row0 + 2 * p
            ok_left = row_start <= kr < row_start + NA_ROWS
            ok_right = row_start <= kr + 1 < row_start + NA_ROWS
            if ok_left or ok_right:
                block = pair_ref[kr + 1 - r + NA_ROWS - 1]
                if not ok_right:
                    block = jnp.where(left_half, block, NEG_INF)
                if not ok_left:
                    block = jnp.where(left_half, NEG_INF, block)
            else:
                block = jnp.full((GRID_W, 2 * GRID_W), NEG_INF, F32)
            bias_ref[qr * GRID_W:(qr + 1) * GRID_W, p * 2 * GRID_W:(p + 1) * 2 * GRID_W] = block


def _na_kernel(q_ref, k_ref, v_ref, pair_ref, o_ref, bias_ref, s_ref, *, rows):
    n_blocks = rows // NA_QROWS
    tq = NA_QROWS * GRID_W
    n_keys = NA_KROWS * GRID_W
    kinds = (0, 1, n_blocks - 1)
    for slot, kind in enumerate(kinds):
        _na_fill_bias(pair_ref, bias_ref.at[slot], kind, rows)

    def key_start(rb):
        return min(max(rb * NA_QROWS - NA_ROWS // 2, 0), rows - NA_KROWS) * GRID_W

    def scores(rb):
        kind_slot = 0 if rb == 0 else (2 if rb == n_blocks - 1 else 1)
        q = q_ref[rb * tq:(rb + 1) * tq, :]
        k = k_ref[key_start(rb):key_start(rb) + n_keys, :]
        s_ref[rb % 2] = (lax.dot_general(q, k, (((1,), (1,)), ((), ())), preferred_element_type=F32)
                         + bias_ref[kind_slot])

    def finish(rb):
        v = v_ref[key_start(rb):key_start(rb) + n_keys, :]
        s = s_ref[rb % 2]
        m = jnp.max(s, axis=-1, keepdims=True)
        p = jnp.exp(s - m)
        den = jnp.sum(p, axis=-1, keepdims=True)
        out = jnp.dot(p.astype(BF16), v, preferred_element_type=F32) / den
        o_ref[rb * tq:(rb + 1) * tq, :] = out.astype(o_ref.dtype)

    scores(0)
    for rb in range(n_blocks):
        if rb + 1 < n_blocks:
            scores(rb + 1)
        finish(rb)


def _na_pair_tables(rpb):
    n_heads, n_rel_rows, n_rel_cols = rpb.shape
    c = np.arange(GRID_W)
    col_start = np.clip(c - NA_COLS // 2, 0, GRID_W - NA_COLS)
    col_ok = (c[None, :] >= col_start[:, None]) & (c[None, :] < col_start[:, None] + NA_COLS)
    rel_c = np.clip(c[None, :] - c[:, None], -(NA_COLS - 1), NA_COLS - 1) + (NA_COLS - 1)
    pick = (rel_c.reshape(1, -1) == np.arange(n_rel_cols)[:, None]).astype(np.float32)
    by_col = jnp.dot(rpb.astype(F32).reshape(n_heads * n_rel_rows, n_rel_cols), pick,
                     precision=lax.Precision.HIGHEST).reshape(n_heads, n_rel_rows, GRID_W, GRID_W)
    by_col = jnp.where(col_ok[None, None], by_col, NEG_INF)
    masked = jnp.full((n_heads, 1, GRID_W, GRID_W), NEG_INF, F32)
    padded = jnp.concatenate([masked, by_col, masked], axis=1)
    return jnp.concatenate([padded[:, :-1], padded[:, 1:]], axis=-1)


def _neighbourhood(q, k, v, pair_tables):
    b, _, s, _ = q.shape
    rows = s // GRID_W
    n_blocks = rows // NA_QROWS
    assert n_blocks >= 3 and rows >= NA_KROWS
    tq = NA_QROWS * GRID_W
    n_keys = NA_KROWS * GRID_W
    head = lambda bi, h: (bi, h, 0, 0)
    return pl.pallas_call(
        functools.partial(_na_kernel, rows=rows),
        grid=(b, NA_HEADS),
        in_specs=[
            pl.BlockSpec((None, None, s, HEAD_DIM), head),
            pl.BlockSpec((None, None, s, HEAD_DIM), head),
            pl.BlockSpec((None, None, s, HEAD_DIM), head),
            pl.BlockSpec((None, 2 * NA_ROWS, GRID_W, 2 * GRID_W), lambda bi, h: (h, 0, 0, 0)),
        ],
        out_specs=pl.BlockSpec((None, None, s, HEAD_DIM), head),
        out_shape=jax.ShapeDtypeStruct((b, NA_HEADS, s, HEAD_DIM), BF16),
        scratch_shapes=[pltpu.VMEM((3, tq, n_keys), F32), pltpu.VMEM((2, tq, n_keys), F32)],
        compiler_params=_params("parallel", "parallel"),
        name="neighbourhood_attn",
    )(q, k, v, pair_tables)


def _odd_out_kernel(yc_ref, u_ref, vn_ref, ws_ref, bs_ref, wo_hbm, x_ref, o_ref, y_ref,
                    wo_ref, wstage_ref, w_sems):
    pl.when(_first_grid_step())(functools.partial(_load_weight_as_bf16, wo_hbm, wo_ref, wstage_ref, w_sems))
    width = SG_WIDTH // SG_GROUPS
    for c in range(OUT_TM // SG_CHUNK):
        rows = slice(c * SG_CHUNK, (c + 1) * SG_CHUNK)
        for g in range(SG_GROUPS):
            cols = slice(g * width, (g + 1) * width)
            sv = jnp.dot(ws_ref[g], vn_ref[rows, cols], preferred_element_type=F32) + bs_ref[g]
            y_ref[rows, NA_WIDTH + g * width:NA_WIDTH + (g + 1) * width] = (u_ref[rows, cols] * sv).astype(BF16)
    for hd in range(NA_HEADS):
        y_ref[:, hd * HEAD_DIM:(hd + 1) * HEAD_DIM] = yc_ref[hd]
    o_ref[...] = x_ref[...] + jnp.dot(y_ref[...], wo_ref[...], preferred_element_type=F32)


def _odd_out(yc, u, vn, w_s, b_s, w_out, x):
    b, s, _ = x.shape
    tm = OUT_TM
    width = SG_WIDTH // SG_GROUPS
    tile = lambda bi, i: (bi, i, 0)
    return pl.pallas_call(
        _odd_out_kernel,
        grid=(b, s // tm),
        in_specs=[
            pl.BlockSpec((None, NA_HEADS, tm, HEAD_DIM), lambda bi, i: (bi, 0, i, 0)),
            pl.BlockSpec((None, tm, SG_WIDTH), tile),
            pl.BlockSpec((None, tm, SG_WIDTH), tile),
            _resident((SG_GROUPS, SG_CHUNK, SG_CHUNK), lambda bi, i: (0, 0, 0)),
            _resident((SG_GROUPS, SG_CHUNK, width), lambda bi, i: (0, 0, 0)),
            pl.BlockSpec(memory_space=pl.ANY),
            pl.BlockSpec((None, tm, D_MODEL), tile),
        ],
        out_specs=pl.BlockSpec((None, tm, D_MODEL), tile),
        out_shape=jax.ShapeDtypeStruct((b, s, D_MODEL), F32),
        scratch_shapes=[pltpu.VMEM((tm, NA_WIDTH + SG_WIDTH), BF16)]
        + _weight_scratch(NA_WIDTH + SG_WIDTH, D_MODEL),
        compiler_params=_params("arbitrary", "arbitrary"),
        name="odd_out",
    )(yc, u, vn, w_s, b_s, w_out, x)


def kernel(x, norm_ffn1, norm_mix, norm_ffn2, norm_out, ffn_w_gate, ffn_w_up, ffn_w_down, even_w_in, pool_w, pool_scale, dil_q_gain, dil_k_gain, even_w_out, odd_w_in, na_q_gain, na_k_gain, na_rpb, sg_v_gain, sg_w, sg_b, odd_w_out):
    depth = norm_ffn1.shape[0]
    slopes = jnp.asarray(2.0 ** (-8.0 * np.arange(1, DIL_HEADS + 1) / DIL_HEADS), dtype=F32)
    width = SG_WIDTH // SG_GROUPS

    for layer in range(depth):
        if layer == 0:
            x = _ffn(x, norm_ffn1[layer], ffn_w_gate, ffn_w_up, ffn_w_down, layer, 0)
        if layer % 2 == 0:
            e = layer // 2
            a, q, k, v = _even_in(x, norm_mix[layer], even_w_in[e], dil_q_gain[e], dil_k_gain[e])
            yb = _dilated(q, k, v, slopes)
            x = _even_out(a, yb, pool_w[e].astype(BF16), pool_scale[e], even_w_out[e], x)
        else:
            o = layer // 2
            q, k, v, u, vn = _odd_in(x, norm_mix[layer], odd_w_in[o],
                                     na_q_gain[o], na_k_gain[o], sg_v_gain[o])
            yc = _neighbourhood(q, k, v, _na_pair_tables(na_rpb[o]))
            b_s = jnp.broadcast_to(sg_b[o][:, :, None], (SG_GROUPS, SG_CHUNK, width))
            x = _odd_out(yc, u, vn, sg_w[o].astype(BF16), b_s, odd_w_out[o], x)
        then = (layer + 1, 0, norm_ffn1[layer + 1]) if layer + 1 < depth else None
        x = _ffn(x, norm_ffn2[layer], ffn_w_gate, ffn_w_up, ffn_w_down, layer, 1, gain_out=norm_out[layer],
                 then=then)
    return x
```

```python
import functools

import numpy as np
import jax
import jax.numpy as jnp
from jax import lax
from jax.experimental import pallas as pl
from jax.experimental.pallas import tpu as pltpu

F32 = jnp.float32
BF16 = jnp.bfloat16

D_MODEL = 2048
D_FF = 5632
HEAD_DIM = 128
POOL_WINDOWS = (2, 4, 8, 16)
POOL_WIDTH = 512
POOL_HALO = 8
DIL_PAIRS = ((128, 1), (512, 4), (2048, 16))
DIL_HEADS_PER_GROUP = 4
DIL_HEADS = 12
DIL_OUT = DIL_HEADS_PER_GROUP * HEAD_DIM
GRID_W = 64
NA_ROWS = 8
NA_COLS = 16
NA_HEADS = 8
NA_WIDTH = NA_HEADS * HEAD_DIM
NA_QROWS = 4
NA_KROWS = NA_QROWS + NA_ROWS
SG_CHUNK = 128
SG_GROUPS = 8
SG_WIDTH = 1024
RMS_EPS = 1e-6
NEG_INF = -1e30

VMEM_LIMIT_BYTES = 60 * 1024 * 1024

FFN_TM = 1024
FFN_TF = 512
FFN_TC = 256
PROJ_TM = 512
PROJ_TN = 512
DIL_TQ = 128
DIL_TA = 2048
DIL_RADIUS = 64
DIL_UNROLL = 16
OUT_TM = 512
W_CHUNK = 512


def _params(*semantics):
    return pltpu.CompilerParams(dimension_semantics=semantics, vmem_limit_bytes=VMEM_LIMIT_BYTES)


def _resident(block_shape, index_map):
    return pl.BlockSpec(block_shape, index_map, pipeline_mode=pl.Buffered(1))


def _rms(x, gain):
    return x * lax.rsqrt(jnp.mean(x * x, axis=-1, keepdims=True) + RMS_EPS) * gain


def _first_grid_step():
    return (pl.program_id(0) == 0) & (pl.program_id(1) == 0)


def _load_weight_as_bf16(w_hbm, wbf_ref, stage_ref, sems):
    n_chunks = wbf_ref.shape[1] // W_CHUNK

    def copy(c):
        cols = pl.ds(c * W_CHUNK, W_CHUNK)
        return pltpu.make_async_copy(w_hbm.at[:, cols], stage_ref.at[c % 2], sems.at[c % 2])

    copy(0).start()
    for c in range(n_chunks):
        if c + 1 < n_chunks:
            copy(c + 1).start()
        copy(c).wait()
        wbf_ref[:, c * W_CHUNK:(c + 1) * W_CHUNK] = stage_ref[c % 2].astype(BF16)


def _weight_scratch(rows, cols):
    assert cols % W_CHUNK == 0
    return [pltpu.VMEM((rows, cols), BF16), pltpu.VMEM((2, rows, W_CHUNK), F32), pltpu.SemaphoreType.DMA((2,))]


def _ffn_kernel(x_hbm, g_ref, wg_ref, wu_ref, wd_ref, gout_ref, g2_ref, o_ref, h_ref, xbuf_ref, x_sem, *,
                final_norm, chained):
    j = pl.program_id(2)
    n_j = D_FF // FFN_TF
    tiles_per_batch = pl.num_programs(1)
    n_tiles = pl.num_programs(0) * tiles_per_batch
    tile = pl.program_id(0) * tiles_per_batch + pl.program_id(1)

    def x_copy(t):
        rows = pl.ds(pl.multiple_of((t % tiles_per_batch) * FFN_TM, FFN_TM), FFN_TM)
        return pltpu.make_async_copy(x_hbm.at[t // tiles_per_batch, rows, :], xbuf_ref, x_sem)

    @pl.when(j == 0)
    def _():
        @pl.when(tile == 0)
        def _():
            x_copy(tile).start()

        x_copy(tile).wait()
        x = xbuf_ref[...]
        h_ref[...] = _rms(x, g_ref[...]).astype(BF16)
        o_ref[...] = x

    @pl.when((j == 1) & (tile + 1 < n_tiles))
    def _():
        x_copy(tile + 1).start()

    if chained:
        @pl.when(j == n_j)
        def _():
            h_ref[...] = _rms(o_ref[...], g2_ref[...]).astype(BF16)

    h = h_ref[...]
    for c in range(FFN_TF // FFN_TC):
        cols = slice(c * FFN_TC, (c + 1) * FFN_TC)
        gate = jnp.dot(h, wg_ref[:, cols].astype(BF16), preferred_element_type=F32)
        up = jnp.dot(h, wu_ref[:, cols].astype(BF16), preferred_element_type=F32)
        act = (gate * jax.nn.sigmoid(gate)) * up * 0.5
        o_ref[...] += jnp.dot(act.astype(BF16), wd_ref[cols, :].astype(BF16), preferred_element_type=F32)

    if final_norm:
        @pl.when(j == n_j - 1)
        def _():
            o_ref[...] = _rms(o_ref[...], gout_ref[...])


def _ffn(x, gain, w_gate, w_up, w_down, layer, idx, gain_out=None, then=None):
    b, s, _ = x.shape
    n_j = D_FF // FFN_TF
    assert n_j >= 2
    final_norm = gain_out is not None
    chained = then is not None
    assert final_norm or not chained
    if gain_out is None:
        gain_out = gain
    layer2, idx2, gain2 = then if chained else (layer, idx, gain)
    tile = lambda bi, i, j: (bi, i, 0)
    const = lambda bi, i, j: (0, 0)

    def w_index(j):
        second = j // n_j
        return layer + second * (layer2 - layer), idx + second * (idx2 - idx), j % n_j

    def cols_map(bi, i, j):
        la, ix, t = w_index(j)
        return (la, ix, 0, t)

    def rows_map(bi, i, j):
        la, ix, t = w_index(j)
        return (la, ix, t, 0)

    return pl.pallas_call(
        functools.partial(_ffn_kernel, final_norm=final_norm, chained=chained),
        grid=(b, s // FFN_TM, n_j * (2 if chained else 1)),
        in_specs=[
            pl.BlockSpec(memory_space=pl.ANY),
            pl.BlockSpec((1, D_MODEL), const),
            pl.BlockSpec((None, None, D_MODEL, FFN_TF), cols_map),
            pl.BlockSpec((None, None, D_MODEL, FFN_TF), cols_map),
            pl.BlockSpec((None, None, FFN_TF, D_MODEL), rows_map),
            pl.BlockSpec((1, D_MODEL), const),
            pl.BlockSpec((1, D_MODEL), const),
        ],
        out_specs=pl.BlockSpec((None, FFN_TM, D_MODEL), tile),
        out_shape=jax.ShapeDtypeStruct((b, s, D_MODEL), F32),
        scratch_shapes=[pltpu.VMEM((FFN_TM, D_MODEL), BF16), pltpu.VMEM((FFN_TM, D_MODEL), F32),
                        pltpu.SemaphoreType.DMA(())],
        compiler_params=_params("arbitrary", "arbitrary", "arbitrary"),
        name="ffn_pair" if chained else ("ffn_final" if final_norm else "ffn"),
    )(x, gain.reshape(1, D_MODEL), w_gate, w_up, w_down, gain_out.reshape(1, D_MODEL),
      gain2.reshape(1, D_MODEL))


def _head_store(z, out_ref, head0, gain=None):
    for hd in range(PROJ_TN // HEAD_DIM):
        zh = z[:, hd * HEAD_DIM:(hd + 1) * HEAD_DIM]
        if gain is not None:
            zh = _rms(zh, gain)
        out_ref[head0 + hd] = zh.astype(out_ref.dtype)


def _store_by_residue(z, out_ref, stage_ref, dil, gain=None):
    tm = z.shape[0]
    for hd in range(z.shape[1] // HEAD_DIM):
        cols = slice(hd * HEAD_DIM, (hd + 1) * HEAD_DIM)
        zh = z[:, cols]
        if gain is not None:
            zh = _rms(zh, gain)
        if dil == 1:
            out_ref[hd, 0] = zh.astype(BF16)
        else:
            stage_ref[hd] = zh
            for r in range(dil):
                out_ref[hd, r] = stage_ref[hd, pl.ds(r, tm // dil, stride=dil), :].astype(BF16)


def _even_in_kernel(x_ref, g_ref, w_hbm, qg_ref, kg_ref, a_ref, *rest):
    n_groups = len(DIL_PAIRS)
    q_refs, k_refs, v_refs = rest[:n_groups], rest[n_groups:2 * n_groups], rest[2 * n_groups:3 * n_groups]
    h_ref, stage_ref, w_ref, wstage_ref, w_sems = rest[3 * n_groups:]
    pl.when(_first_grid_step())(functools.partial(_load_weight_as_bf16, w_hbm, w_ref, wstage_ref, w_sems))
    h_ref[...] = _rms(x_ref[...], g_ref[...]).astype(BF16)

    def chunk(c):
        return jnp.dot(h_ref[...], w_ref[:, c * PROJ_TN:(c + 1) * PROJ_TN], preferred_element_type=F32)

    a_ref[...] = chunk(0)
    for g, (_, dil) in enumerate(DIL_PAIRS):
        _store_by_residue(chunk(1 + g), q_refs[g], stage_ref, dil, qg_ref[...])
        _store_by_residue(chunk(1 + n_groups + g), k_refs[g], stage_ref, dil, kg_ref[...])
        _store_by_residue(chunk(1 + 2 * n_groups + g), v_refs[g], stage_ref, dil)


def _even_in(x, gain, w_in, q_gain, k_gain):
    b, s, _ = x.shape
    n_in = w_in.shape[1]
    assert PROJ_TN == DIL_OUT
    row = lambda bi, i: (bi, i, 0)
    const = lambda bi, i: (0, 0)
    hpg = DIL_HEADS_PER_GROUP
    group_specs = [pl.BlockSpec((None, hpg, dil, PROJ_TM // dil, HEAD_DIM), lambda bi, i: (bi, 0, 0, i, 0))
                   for _, dil in DIL_PAIRS]
    group_shapes = [jax.ShapeDtypeStruct((b, hpg, dil, s // dil, HEAD_DIM), BF16) for _, dil in DIL_PAIRS]
    outs = pl.pallas_call(
        _even_in_kernel,
        grid=(b, s // PROJ_TM),
        in_specs=[
            pl.BlockSpec((None, PROJ_TM, D_MODEL), row),
            pl.BlockSpec((1, D_MODEL), const),
            pl.BlockSpec(memory_space=pl.ANY),
            pl.BlockSpec((1, HEAD_DIM), const),
            pl.BlockSpec((1, HEAD_DIM), const),
        ],
        out_specs=[pl.BlockSpec((None, PROJ_TM, POOL_WIDTH), row)] + group_specs * 3,
        out_shape=[jax.ShapeDtypeStruct((b, s, POOL_WIDTH), F32)] + group_shapes * 3,
        scratch_shapes=[pltpu.VMEM((PROJ_TM, D_MODEL), BF16),
                        pltpu.VMEM((DIL_HEADS_PER_GROUP, PROJ_TM, HEAD_DIM), F32)]
        + _weight_scratch(D_MODEL, n_in),
        compiler_params=_params("arbitrary", "arbitrary"),
        name="even_in",
    )(x, gain.reshape(1, D_MODEL), w_in,
      (q_gain * HEAD_DIM ** -0.5).reshape(1, HEAD_DIM), k_gain.reshape(1, HEAD_DIM))
    n_groups = len(DIL_PAIRS)
    return outs[0], outs[1:1 + n_groups], outs[1 + n_groups:1 + 2 * n_groups], outs[1 + 2 * n_groups:]


def _odd_in_kernel(x_ref, g_ref, w_hbm, qg_ref, kg_ref, vg_ref, q_ref, k_ref, v_ref, u_ref, vn_ref,
                   h_ref, gv_ref, w_ref, wstage_ref, w_sems):
    pl.when(_first_grid_step())(functools.partial(_load_weight_as_bf16, w_hbm, w_ref, wstage_ref, w_sems))
    h_ref[...] = _rms(x_ref[...], g_ref[...]).astype(BF16)

    def chunk(c):
        return jnp.dot(h_ref[...], w_ref[:, c * PROJ_TN:(c + 1) * PROJ_TN], preferred_element_type=F32)

    per = NA_WIDTH // PROJ_TN
    per_sg = SG_WIDTH // PROJ_TN
    heads_per_chunk = PROJ_TN // HEAD_DIM
    for c in range(per_sg):
        gv_ref[:, c * PROJ_TN:(c + 1) * PROJ_TN] = jax.nn.gelu(chunk(3 * per + per_sg + c))
    vn_ref[...] = _rms(gv_ref[...], vg_ref[...]).astype(BF16)
    for c in range(per_sg):
        u_ref[:, c * PROJ_TN:(c + 1) * PROJ_TN] = jax.nn.gelu(chunk(3 * per + c))
    for c in range(per):
        _head_store(chunk(c), q_ref, c * heads_per_chunk, qg_ref[...])
        _head_store(chunk(per + c), k_ref, c * heads_per_chunk, kg_ref[...])
        _head_store(chunk(2 * per + c), v_ref, c * heads_per_chunk)


def _odd_in(x, gain, w_in, q_gain, k_gain, v_gain):
    b, s, _ = x.shape
    n_in = w_in.shape[1]
    row = lambda bi, i: (bi, i, 0)
    by_head = lambda bi, i: (bi, 0, i, 0)
    const = lambda bi, i: (0, 0)
    return pl.pallas_call(
        _odd_in_kernel,
        grid=(b, s // PROJ_TM),
        in_specs=[
            pl.BlockSpec((None, PROJ_TM, D_MODEL), row),
            pl.BlockSpec((1, D_MODEL), const),
            pl.BlockSpec(memory_space=pl.ANY),
            pl.BlockSpec((1, HEAD_DIM), const),
            pl.BlockSpec((1, HEAD_DIM), const),
            pl.BlockSpec((1, SG_WIDTH), const),
        ],
        out_specs=[
            pl.BlockSpec((None, NA_HEADS, PROJ_TM, HEAD_DIM), by_head),
            pl.BlockSpec((None, NA_HEADS, PROJ_TM, HEAD_DIM), by_head),
            pl.BlockSpec((None, NA_HEADS, PROJ_TM, HEAD_DIM), by_head),
            pl.BlockSpec((None, PROJ_TM, SG_WIDTH), row),
            pl.BlockSpec((None, PROJ_TM, SG_WIDTH), row),
        ],
        out_shape=[
            jax.ShapeDtypeStruct((b, NA_HEADS, s, HEAD_DIM), BF16),
            jax.ShapeDtypeStruct((b, NA_HEADS, s, HEAD_DIM), BF16),
            jax.ShapeDtypeStruct((b, NA_HEADS, s, HEAD_DIM), BF16),
            jax.ShapeDtypeStruct((b, s, SG_WIDTH), F32),
            jax.ShapeDtypeStruct((b, s, SG_WIDTH), BF16),
        ],
        scratch_shapes=[pltpu.VMEM((PROJ_TM, D_MODEL), BF16), pltpu.VMEM((PROJ_TM, SG_WIDTH), F32)]
        + _weight_scratch(D_MODEL, n_in),
        compiler_params=_params("arbitrary", "arbitrary"),
        name="odd_in",
    )(x, gain.reshape(1, D_MODEL), w_in,
      (q_gain * HEAD_DIM ** -0.5).reshape(1, HEAD_DIM), k_gain.reshape(1, HEAD_DIM),
      v_gain.reshape(1, SG_WIDTH))


def _dilated_kernel(slopes_ref, q0_ref, q1_ref, q2_ref, k0_ref, k1_ref, k2_ref, v0_ref, v1_ref, v2_ref,
                    o_ref, on_ref, ls_ref, s_ref, bias_ref, *, seq):
    hh = pl.program_id(1)
    step = pl.program_id(2)
    q_refs = (q0_ref, q1_ref, q2_ref)
    k_refs = (k0_ref, k1_ref, k2_ref)
    v_refs = (v0_ref, v1_ref, v2_ref)

    @pl.when(step == 0)
    def _():
        nk = DIL_TQ + 2 * DIL_RADIUS
        base0 = (lax.broadcasted_iota(jnp.int32, (DIL_TQ, nk), 1)
                 - lax.broadcasted_iota(jnp.int32, (DIL_TQ, nk), 0))
        for g, (_, dil) in enumerate(DIL_PAIRS):
            slope_g = slopes_ref[g * DIL_HEADS_PER_GROUP + hh] * dil
            for kind in range(3):
                dist = jnp.abs(base0 - kind * DIL_RADIUS)
                bias_ref[g, kind] = jnp.where(dist <= DIL_RADIUS, -slope_g * dist.astype(F32), NEG_INF)

    for g, (window, dil) in enumerate(DIL_PAIRS):
        assert window // (2 * dil) == DIL_RADIUS
        q_ref, k_ref, v_ref = q_refs[g], k_refs[g], v_refs[g]
        sub_len = seq // dil
        per_res = DIL_TA // dil
        nq = min(per_res, DIL_TQ)
        n_sub = per_res // nq
        n_keys = min(nq + 2 * DIL_RADIUS, sub_len)
        assert nq == DIL_TQ and n_keys == DIL_TQ + 2 * DIL_RADIUS

        def place(t, nq=nq, n_sub=n_sub, per_res=per_res, sub_len=sub_len, n_keys=n_keys):
            r = t // n_sub
            q_off = pl.multiple_of((t % n_sub) * nq, nq)
            q_pos = step * per_res + q_off
            k_pos = pl.multiple_of(jnp.clip(q_pos - DIL_RADIUS, 0, sub_len - n_keys), DIL_RADIUS)
            return r, q_off, q_pos, k_pos

        def scores(t, slot, q_ref=q_ref, k_ref=k_ref, nq=nq, n_keys=n_keys, g=g, place=place):
            r, q_off, q_pos, k_pos = place(t)
            q = q_ref[r, pl.ds(q_off, nq), :]
            k = k_ref[r, pl.ds(k_pos, n_keys), :]
            s = lax.dot_general(q, k, (((1,), (1,)), ((), ())), preferred_element_type=F32)
            s_ref[slot, 0:nq, 0:n_keys] = s + bias_ref[g, (q_pos - k_pos) // DIL_RADIUS]

        def finish(t, slot, v_ref=v_ref, g=g, dil=dil, nq=nq, n_keys=n_keys, place=place):
            r, q_off, _, k_pos = place(t)
            v = v_ref[r, pl.ds(k_pos, n_keys), :]
            s = s_ref[slot, 0:nq, 0:n_keys]
            m = jnp.max(s, axis=-1, keepdims=True)
            p = jnp.exp(s - m)
            den = jnp.sum(p, axis=-1, keepdims=True)
            out = jnp.dot(p.astype(BF16), v, preferred_element_type=F32) / den
            lse = jnp.broadcast_to(m + jnp.log(den), (nq, HEAD_DIM))
            if dil == 1:
                rows = pl.ds(q_off, nq)
            else:
                rows = pl.ds(r + dil * q_off, nq, stride=dil)
            on_ref[g, rows, :] = out
            ls_ref[g, rows, :] = lse

        n_tiles = dil * n_sub
        assert n_tiles % DIL_UNROLL == 0

        def body(i, carry, scores=scores, finish=finish):
            t0 = i * DIL_UNROLL
            scores(t0, 0)
            for u in range(DIL_UNROLL):
                if u + 1 < DIL_UNROLL:
                    scores(t0 + u + 1, (u + 1) % 2)
                finish(t0 + u, u % 2)
            return carry

        lax.fori_loop(0, n_tiles // DIL_UNROLL, body, 0)

    def merge(c, carry):
        rows = pl.ds(pl.multiple_of(c * DIL_TQ, DIL_TQ), DIL_TQ)
        lse = [ls_ref[g, rows, :] for g in range(len(DIL_PAIRS))]
        top = jnp.maximum(jnp.maximum(lse[0], lse[1]), lse[2])
        num = jnp.zeros((DIL_TQ, HEAD_DIM), F32)
        den = jnp.zeros((DIL_TQ, HEAD_DIM), F32)
        for g in range(len(DIL_PAIRS)):
            w = jnp.exp(lse[g] - top)
            num = num + w * on_ref[g, rows, :]
            den = den + w
        o_ref[rows, :] = (num / den).astype(o_ref.dtype)
        return carry

    lax.fori_loop(0, DIL_TA // DIL_TQ, merge, 0)


def _dilated(q, k, v, slopes):
    b, _, dil0, s, _ = q[0].shape
    assert dil0 == 1
    n_groups = len(DIL_PAIRS)

    def q_spec(dil):
        return pl.BlockSpec((None, None, dil, DIL_TA // dil, HEAD_DIM), lambda bi, hh, i: (bi, hh, 0, i, 0))

    def kv_spec(dil):
        return pl.BlockSpec((None, None, dil, s // dil, HEAD_DIM), lambda bi, hh, i: (bi, hh, 0, 0, 0))

    dils = [dil for _, dil in DIL_PAIRS]
    return pl.pallas_call(
        functools.partial(_dilated_kernel, seq=s),
        grid=(b, DIL_HEADS_PER_GROUP, s // DIL_TA),
        in_specs=[pl.BlockSpec(memory_space=pltpu.SMEM)]
        + [q_spec(d) for d in dils] + [kv_spec(d) for d in dils] + [kv_spec(d) for d in dils],
        out_specs=pl.BlockSpec((None, None, DIL_TA, HEAD_DIM), lambda bi, hh, i: (bi, hh, i, 0)),
        out_shape=jax.ShapeDtypeStruct((b, DIL_HEADS_PER_GROUP, s, HEAD_DIM), BF16),
        scratch_shapes=[pltpu.VMEM((n_groups, DIL_TA, HEAD_DIM), F32),
                        pltpu.VMEM((n_groups, DIL_TA, HEAD_DIM), F32),
                        pltpu.VMEM((2, DIL_TQ, DIL_TQ + 2 * DIL_RADIUS), F32),
                        pltpu.VMEM((n_groups, 3, DIL_TQ, DIL_TQ + 2 * DIL_RADIUS), F32)],
        compiler_params=_params("parallel", "parallel", "arbitrary"),
        name="dilated_attn",
    )(slopes, *q, *k, *v)


def _even_out_kernel(a_ref, prev_ref, next_ref, yb_ref, pw_ref, ps_ref, wo_hbm, x_ref, o_ref,
                     ext_ref, y_ref, wo_ref, wstage_ref, w_sems, *, seq):
    pl.when(_first_grid_step())(functools.partial(_load_weight_as_bf16, wo_hbm, wo_ref, wstage_ref, w_sems))
    i = pl.program_id(1)
    tm = OUT_TM
    ext_ref[0:POOL_HALO, :] = jnp.where(i == 0, 0.0, prev_ref[...])
    ext_ref[POOL_HALO:POOL_HALO + tm, :] = a_ref[...]
    ext_ref[POOL_HALO + tm:, :] = jnp.where(i == pl.num_programs(1) - 1, 0.0, next_ref[...])
    pos = i * tm + lax.broadcasted_iota(jnp.int32, (tm, 1), 0)
    for g, window in enumerate(POOL_WINDOWS):
        half = window // 2
        cols = slice(g * HEAD_DIM, (g + 1) * HEAD_DIM)
        total = ext_ref[POOL_HALO - half:POOL_HALO - half + tm, cols]
        for shift in range(-half + 1, half):
            total = total + ext_ref[POOL_HALO + shift:POOL_HALO + shift + tm, cols]
        count = (jnp.minimum(pos + half, seq) - jnp.maximum(pos - half, 0)).astype(F32)
        pooled = total / count - a_ref[:, cols]
        ya = jnp.dot(pooled.astype(BF16), pw_ref[g], preferred_element_type=F32) * ps_ref[:, cols]
        y_ref[:, cols] = ya.astype(BF16)
    for hd in range(DIL_HEADS_PER_GROUP):
        y_ref[:, POOL_WIDTH + hd * HEAD_DIM:POOL_WIDTH + (hd + 1) * HEAD_DIM] = yb_ref[hd]
    o_ref[...] = x_ref[...] + jnp.dot(y_ref[...], wo_ref[...], preferred_element_type=F32)


def _even_out(a, yb, pool_w, pool_scale, w_out, x):
    b, s, _ = x.shape
    tm = OUT_TM
    halo_blocks = tm // POOL_HALO
    tile = lambda bi, i: (bi, i, 0)
    return pl.pallas_call(
        functools.partial(_even_out_kernel, seq=s),
        grid=(b, s // tm),
        in_specs=[
            pl.BlockSpec((None, tm, POOL_WIDTH), tile),
            pl.BlockSpec((None, POOL_HALO, POOL_WIDTH),
                         lambda bi, i: (bi, jnp.maximum(i * halo_blocks - 1, 0), 0)),
            pl.BlockSpec((None, POOL_HALO, POOL_WIDTH),
                         lambda bi, i: (bi, jnp.minimum((i + 1) * halo_blocks, s // POOL_HALO - 1), 0)),
            pl.BlockSpec((None, DIL_HEADS_PER_GROUP, tm, HEAD_DIM), lambda bi, i: (bi, 0, i, 0)),
            _resident((len(POOL_WINDOWS), HEAD_DIM, HEAD_DIM), lambda bi, i: (0, 0, 0)),
            pl.BlockSpec((1, POOL_WIDTH), lambda bi, i: (0, 0)),
            pl.BlockSpec(memory_space=pl.ANY),
            pl.BlockSpec((None, tm, D_MODEL), tile),
        ],
        out_specs=pl.BlockSpec((None, tm, D_MODEL), tile),
        out_shape=jax.ShapeDtypeStruct((b, s, D_MODEL), F32),
        scratch_shapes=[pltpu.VMEM((tm + 2 * POOL_HALO, POOL_WIDTH), F32),
                        pltpu.VMEM((tm, POOL_WIDTH + DIL_OUT), BF16)]
        + _weight_scratch(POOL_WIDTH + DIL_OUT, D_MODEL),
        compiler_params=_params("arbitrary", "arbitrary"),
        name="even_out",
    )(a, a, a, yb, pool_w, pool_scale.reshape(1, POOL_WIDTH), w_out, x)


def _na_fill_bias(pair_ref, bias_ref, rb, rows):
    k_row0 = min(max(rb * NA_QROWS - NA_ROWS // 2, 0), rows - NA_KROWS)
    left_half = lax.broadcasted_iota(jnp.int32, (GRID_W, 2 * GRID_W), 1) < GRID_W
    for qr in range(NA_QROWS):
        r = rb * NA_QROWS + qr
        row_start = min(max(r - NA_ROWS // 2, 0), rows - NA_ROWS)
        for p in range(NA_KROWS // 2):
            kr = k_row0 + 2 * p
            ok_left = row_start <= kr < row_start + NA_ROWS
            ok_right = row_start <= kr + 1 < row_start + NA_ROWS
            if ok_left or ok_right:
                block = pair_ref[kr + 1 - r + NA_ROWS - 1]
                if not ok_right:
                    block = jnp.where(left_half, block, NEG_INF)
                if not ok_left:
                    block = jnp.where(left_half, NEG_INF, block)
            else:
                block = jnp.full((GRID_W, 2 * GRID_W), NEG_INF, F32)
            bias_ref[qr * GRID_W:(qr + 1) * GRID_W, p * 2 * GRID_W:(p + 1) * 2 * GRID_W] = block


def _na_kernel(q_ref, k_ref, v_ref, pair_ref, o_ref, bias_ref, s_ref, *, rows):
    n_blocks = rows // NA_QROWS
    tq = NA_QROWS * GRID_W
    n_keys = NA_KROWS * GRID_W
    kinds = (0, 1, n_blocks - 1)
    for slot, kind in enumerate(kinds):
        _na_fill_bias(pair_ref, bias_ref.at[slot], kind, rows)

    def key_start(rb):
        return min(max(rb * NA_QROWS - NA_ROWS // 2, 0), rows - NA_KROWS) * GRID_W

    def scores(rb):
        kind_slot = 0 if rb == 0 else (2 if rb == n_blocks - 1 else 1)
        q = q_ref[rb * tq:(rb + 1) * tq, :]
        k = k_ref[key_start(rb):key_start(rb) + n_keys, :]
        s_ref[rb % 2] = (lax.dot_general(q, k, (((1,), (1,)), ((), ())), preferred_element_type=F32)
                         + bias_ref[kind_slot])

    def finish(rb):
        v = v_ref[key_start(rb):key_start(rb) + n_keys, :]
        s = s_ref[rb % 2]
        m = jnp.max(s, axis=-1, keepdims=True)
        p = jnp.exp(s - m)
        den = jnp.sum(p, axis=-1, keepdims=True)
        out = jnp.dot(p.astype(BF16), v, preferred_element_type=F32) / den
        o_ref[rb * tq:(rb + 1) * tq, :] = out.astype(o_ref.dtype)

    scores(0)
    for rb in range(n_blocks):
        if rb + 1 < n_blocks:
            scores(rb + 1)
        finish(rb)


def _na_pair_tables(rpb):
    n_heads, n_rel_rows, n_rel_cols = rpb.shape
    c = np.arange(GRID_W)
    col_start = np.clip(c - NA_COLS // 2, 0, GRID_W - NA_COLS)
    col_ok = (c[None, :] >= col_start[:, None]) & (c[None, :] < col_start[:, None] + NA_COLS)
    rel_c = np.clip(c[None, :] - c[:, None], -(NA_COLS - 1), NA_COLS - 1) + (NA_COLS - 1)
    pick = (rel_c.reshape(1, -1) == np.arange(n_rel_cols)[:, None]).astype(np.float32)
    by_col = jnp.dot(rpb.astype(F32).reshape(n_heads * n_rel_rows, n_rel_cols), pick,
                     precision=lax.Precision.HIGHEST).reshape(n_heads, n_rel_rows, GRID_W, GRID_W)
    by_col = jnp.where(col_ok[None, None], by_col, NEG_INF)
    masked = jnp.full((n_heads, 1, GRID_W, GRID_W), NEG_INF, F32)
    padded = jnp.concatenate([masked, by_col, masked], axis=1)
    return jnp.concatenate([padded[:, :-1], padded[:, 1:]], axis=-1)


def _neighbourhood(q, k, v, pair_tables):
    b, _, s, _ = q.shape
    rows = s // GRID_W
    n_blocks = rows // NA_QROWS
    assert n_blocks >= 3 and rows >= NA_KROWS
    tq = NA_QROWS * GRID_W
    n_keys = NA_KROWS * GRID_W
    head = lambda bi, h: (bi, h, 0, 0)
    return pl.pallas_call(
        functools.partial(_na_kernel, rows=rows),
        grid=(b, NA_HEADS),
        in_specs=[
            pl.BlockSpec((None, None, s, HEAD_DIM), head),
            pl.BlockSpec((None, None, s, HEAD_DIM), head),
            pl.BlockSpec((None, None, s, HEAD_DIM), head),
            pl.BlockSpec((None, 2 * NA_ROWS, GRID_W, 2 * GRID_W), lambda bi, h: (h, 0, 0, 0)),
        ],
        out_specs=pl.BlockSpec((None, None, s, HEAD_DIM), head),
        out_shape=jax.ShapeDtypeStruct((b, NA_HEADS, s, HEAD_DIM), BF16),
        scratch_shapes=[pltpu.VMEM((3, tq, n_keys), F32), pltpu.VMEM((2, tq, n_keys), F32)],
        compiler_params=_params("parallel", "parallel"),
        name="neighbourhood_attn",
    )(q, k, v, pair_tables)


def _odd_out_kernel(yc_ref, u_ref, vn_ref, ws_ref, bs_ref, wo_hbm, x_ref, o_ref, y_ref,
                    wo_ref, wstage_ref, w_sems):
    pl.when(_first_grid_step())(functools.partial(_load_weight_as_bf16, wo_hbm, wo_ref, wstage_ref, w_sems))
    width = SG_WIDTH // SG_GROUPS
    for c in range(OUT_TM // SG_CHUNK):
        rows = slice(c * SG_CHUNK, (c + 1) * SG_CHUNK)
        for g in range(SG_GROUPS):
            cols = slice(g * width, (g + 1) * width)
            sv = jnp.dot(ws_ref[g], vn_ref[rows, cols], preferred_element_type=F32) + bs_ref[g]
            y_ref[rows, NA_WIDTH + g * width:NA_WIDTH + (g + 1) * width] = (u_ref[rows, cols] * sv).astype(BF16)
    for hd in range(NA_HEADS):
        y_ref[:, hd * HEAD_DIM:(hd + 1) * HEAD_DIM] = yc_ref[hd]
    o_ref[...] = x_ref[...] + jnp.dot(y_ref[...], wo_ref[...], preferred_element_type=F32)


def _odd_out(yc, u, vn, w_s, b_s, w_out, x):
    b, s, _ = x.shape
    tm = OUT_TM
    width = SG_WIDTH // SG_GROUPS
    tile = lambda bi, i: (bi, i, 0)
    return pl.pallas_call(
        _odd_out_kernel,
        grid=(b, s // tm),
        in_specs=[
            pl.BlockSpec((None, NA_HEADS, tm, HEAD_DIM), lambda bi, i: (bi, 0, i, 0)),
            pl.BlockSpec((None, tm, SG_WIDTH), tile),
            pl.BlockSpec((None, tm, SG_WIDTH), tile),
            _resident((SG_GROUPS, SG_CHUNK, SG_CHUNK), lambda bi, i: (0, 0, 0)),
            _resident((SG_GROUPS, SG_CHUNK, width), lambda bi, i: (0, 0, 0)),
            pl.BlockSpec(memory_space=pl.ANY),
            pl.BlockSpec((None, tm, D_MODEL), tile),
        ],
        out_specs=pl.BlockSpec((None, tm, D_MODEL), tile),
        out_shape=jax.ShapeDtypeStruct((b, s, D_MODEL), F32),
        scratch_shapes=[pltpu.VMEM((tm, NA_WIDTH + SG_WIDTH), BF16)]
        + _weight_scratch(NA_WIDTH + SG_WIDTH, D_MODEL),
        compiler_params=_params("arbitrary", "arbitrary"),
        name="odd_out",
    )(yc, u, vn, w_s, b_s, w_out, x)


def kernel(x, norm_ffn1, norm_mix, norm_ffn2, norm_out, ffn_w_gate, ffn_w_up, ffn_w_down, even_w_in, pool_w, pool_scale, dil_q_gain, dil_k_gain, even_w_out, odd_w_in, na_q_gain, na_k_gain, na_rpb, sg_v_gain, sg_w, sg_b, odd_w_out):
    depth = norm_ffn1.shape[0]
    slopes = jnp.asarray(2.0 ** (-8.0 * np.arange(1, DIL_HEADS + 1) / DIL_HEADS), dtype=F32)
    width = SG_WIDTH // SG_GROUPS

    for layer in range(depth):
        if layer == 0:
            x = _ffn(x, norm_ffn1[layer], ffn_w_gate, ffn_w_up, ffn_w_down, layer, 0)
        if layer % 2 == 0:
            e = layer // 2
            a, q, k, v = _even_in(x, norm_mix[layer], even_w_in[e], dil_q_gain[e], dil_k_gain[e])
            yb = _dilated(q, k, v, slopes)
            x = _even_out(a, yb, pool_w[e].astype(BF16), pool_scale[e], even_w_out[e], x)
        else:
            o = layer // 2
            q, k, v, u, vn = _odd_in(x, norm_mix[layer], odd_w_in[o],
                                     na_q_gain[o], na_k_gain[o], sg_v_gain[o])
            yc = _neighbourhood(q, k, v, _na_pair_tables(na_rpb[o]))
            b_s = jnp.broadcast_to(sg_b[o][:, :, None], (SG_GROUPS, SG_CHUNK, width))
            x = _odd_out(yc, u, vn, sg_w[o].astype(BF16), b_s, odd_w_out[o], x)
        then = (layer + 1, 0, norm_ffn1[layer + 1]) if layer + 1 < depth else None
        x = _ffn(x, norm_ffn2[layer], ffn_w_gate, ffn_w_up, ffn_w_down, layer, 1, gain_out=norm_out[layer],
                 then=then)
    return x
```

```python
import functools

import numpy as np
import jax
import jax.numpy as jnp
from jax import lax
from jax.experimental import pallas as pl
from jax.experimental.pallas import tpu as pltpu

F32 = jnp.float32
BF16 = jnp.bfloat16

D_MODEL = 2048
D_FF = 5632
HEAD_DIM = 128
POOL_WINDOWS = (2, 4, 8, 16)
POOL_WIDTH = 512
POOL_HALO = 8
DIL_PAIRS = ((128, 1), (512, 4), (2048, 16))
DIL_HEADS_PER_GROUP = 4
DIL_HEADS = 12
DIL_OUT = DIL_HEADS_PER_GROUP * HEAD_DIM
GRID_W = 64
NA_ROWS = 8
NA_COLS = 16
NA_HEADS = 8
NA_WIDTH = NA_HEADS * HEAD_DIM
NA_QROWS = 4
NA_KROWS = NA_QROWS + NA_ROWS
SG_CHUNK = 128
SG_GROUPS = 8
SG_WIDTH = 1024
RMS_EPS = 1e-6
NEG_INF = -1e30

VMEM_LIMIT_BYTES = 60 * 1024 * 1024

FFN_TM = 1024
FFN_TF = 512
FFN_TC = 256
PROJ_TM = 512
PROJ_TN = 512
DIL_TQ = 128
DIL_TA = 2048
DIL_RADIUS = 64
DIL_UNROLL = 16
OUT_TM = 512
W_CHUNK = 512


def _params(*semantics):
    return pltpu.CompilerParams(dimension_semantics=semantics, vmem_limit_bytes=VMEM_LIMIT_BYTES)


def _resident(block_shape, index_map):
    return pl.BlockSpec(block_shape, index_map, pipeline_mode=pl.Buffered(1))


def _rms(x, gain):
    return x * lax.rsqrt(jnp.mean(x * x, axis=-1, keepdims=True) + RMS_EPS) * gain


def _first_grid_step():
    return (pl.program_id(0) == 0) & (pl.program_id(1) == 0)


def _load_weight_as_bf16(w_hbm, wbf_ref, stage_ref, sems):
    n_chunks = wbf_ref.shape[1] // W_CHUNK

    def copy(c):
        cols = pl.ds(c * W_CHUNK, W_CHUNK)
        return pltpu.make_async_copy(w_hbm.at[:, cols], stage_ref.at[c % 2], sems.at[c % 2])

    copy(0).start()
    for c in range(n_chunks):
        if c + 1 < n_chunks:
            copy(c + 1).start()
        copy(c).wait()
        wbf_ref[:, c * W_CHUNK:(c + 1) * W_CHUNK] = stage_ref[c % 2].astype(BF16)


def _weight_scratch(rows, cols):
    assert cols % W_CHUNK == 0
    return [pltpu.VMEM((rows, cols), BF16), pltpu.VMEM((2, rows, W_CHUNK), F32), pltpu.SemaphoreType.DMA((2,))]


def _ffn_kernel(x_hbm, g_ref, wg_ref, wu_ref, wd_ref, gout_ref, g2_ref, o_ref, h_ref, xbuf_ref, x_sem, *,
                final_norm, chained):
    j = pl.program_id(2)
    n_j = D_FF // FFN_TF
    tiles_per_batch = pl.num_programs(1)
    n_tiles = pl.num_programs(0) * tiles_per_batch
    tile = pl.program_id(0) * tiles_per_batch + pl.program_id(1)

    def x_copy(t):
        rows = pl.ds(pl.multiple_of((t % tiles_per_batch) * FFN_TM, FFN_TM), FFN_TM)
        return pltpu.make_async_copy(x_hbm.at[t // tiles_per_batch, rows, :], xbuf_ref, x_sem)

    @pl.when(j == 0)
    def _():
        @pl.when(tile == 0)
        def _():
            x_copy(tile).start()

        x_copy(tile).wait()
        x = xbuf_ref[...]
        h_ref[...] = _rms(x, g_ref[...]).astype(BF16)
        o_ref[...] = x

    @pl.when((j == 1) & (tile + 1 < n_tiles))
    def _():
        x_copy(tile + 1).start()

    if chained:
        @pl.when(j == n_j)
        def _():
            h_ref[...] = _rms(o_ref[...], g2_ref[...]).astype(BF16)

    h = h_ref[...]
    for c in range(FFN_TF // FFN_TC):
        cols = slice(c * FFN_TC, (c + 1) * FFN_TC)
        gate = jnp.dot(h, wg_ref[:, cols].astype(BF16), preferred_element_type=F32)
        up = jnp.dot(h, wu_ref[:, cols].astype(BF16), preferred_element_type=F32)
        act = (gate * jax.nn.sigmoid(gate)) * up * 0.5
        o_ref[...] += jnp.dot(act.astype(BF16), wd_ref[cols, :].astype(BF16), preferred_element_type=F32)

    if final_norm:
        @pl.when(j == n_j - 1)
        def _():
            o_ref[...] = _rms(o_ref[...], gout_ref[...])


def _ffn(x, gain, w_gate, w_up, w_down, layer, idx, gain_out=None, then=None):
    b, s, _ = x.shape
    n_j = D_FF // FFN_TF
    assert n_j >= 2
    final_norm = gain_out is not None
    chained = then is not None
    assert final_norm or not chained
    if gain_out is None:
        gain_out = gain
    layer2, idx2, gain2 = then if chained else (layer, idx, gain)
    tile = lambda bi, i, j: (bi, i, 0)
    const = lambda bi, i, j: (0, 0)

    def w_index(j):
        second = j // n_j
        return layer + second * (layer2 - layer), idx + second * (idx2 - idx), j % n_j

    def cols_map(bi, i, j):
        la, ix, t = w_index(j)
        return (la, ix, 0, t)

    def rows_map(bi, i, j):
        la, ix, t = w_index(j)
        return (la, ix, t, 0)

    return pl.pallas_call(
        functools.partial(_ffn_kernel, final_norm=final_norm, chained=chained),
        grid=(b, s // FFN_TM, n_j * (2 if chained else 1)),
        in_specs=[
            pl.BlockSpec(memory_space=pl.ANY),
            pl.BlockSpec((1, D_MODEL), const),
            pl.BlockSpec((None, None, D_MODEL, FFN_TF), cols_map),
            pl.BlockSpec((None, None, D_MODEL, FFN_TF), cols_map),
            pl.BlockSpec((None, None, FFN_TF, D_MODEL), rows_map),
            pl.BlockSpec((1, D_MODEL), const),
            pl.BlockSpec((1, D_MODEL), const),
        ],
        out_specs=pl.BlockSpec((None, FFN_TM, D_MODEL), tile),
        out_shape=jax.ShapeDtypeStruct((b, s, D_MODEL), F32),
        scratch_shapes=[pltpu.VMEM((FFN_TM, D_MODEL), BF16), pltpu.VMEM((FFN_TM, D_MODEL), F32),
                        pltpu.SemaphoreType.DMA(())],
        compiler_params=_params("arbitrary", "arbitrary", "arbitrary"),
        name="ffn_pair" if chained else ("ffn_final" if final_norm else "ffn"),
    )(x, gain.reshape(1, D_MODEL), w_gate, w_up, w_down, gain_out.reshape(1, D_MODEL),
      gain2.reshape(1, D_MODEL))


def _head_store(z, out_ref, head0, gain=None):
    for hd in range(PROJ_TN // HEAD_DIM):
        zh = z[:, hd * HEAD_DIM:(hd + 1) * HEAD_DIM]
        if gain is not None:
            zh = _rms(zh, gain)
        out_ref[head0 + hd] = zh.astype(out_ref.dtype)


def _store_by_residue(z, out_ref, stage_ref, dil, gain=None):
    tm = z.shape[0]
    for hd in range(z.shape[1] // HEAD_DIM):
        cols = slice(hd * HEAD_DIM, (hd + 1) * HEAD_DIM)
        zh = z[:, cols]
        if gain is not None:
            zh = _rms(zh, gain)
        if dil == 1:
            out_ref[hd, 0] = zh.astype(BF16)
        else:
            stage_ref[hd] = zh
            for r in range(dil):
                out_ref[hd, r] = stage_ref[hd, pl.ds(r, tm // dil, stride=dil), :].astype(BF16)


def _even_in_kernel(x_ref, g_ref, w_hbm, qg_ref, kg_ref, a_ref, *rest):
    n_groups = len(DIL_PAIRS)
    q_refs, k_refs, v_refs = rest[:n_groups], rest[n_groups:2 * n_groups], rest[2 * n_groups:3 * n_groups]
    h_ref, stage_ref, w_ref, wstage_ref, w_sems = rest[3 * n_groups:]
    pl.when(_first_grid_step())(functools.partial(_load_weight_as_bf16, w_hbm, w_ref, wstage_ref, w_sems))
    h_ref[...] = _rms(x_ref[...], g_ref[...]).astype(BF16)

    def chunk(c):
        return jnp.dot(h_ref[...], w_ref[:, c * PROJ_TN:(c + 1) * PROJ_TN], preferred_element_type=F32)

    a_ref[...] = chunk(0)
    for g, (_, dil) in enumerate(DIL_PAIRS):
        _store_by_residue(chunk(1 + g), q_refs[g], stage_ref, dil, qg_ref[...])
        _store_by_residue(chunk(1 + n_groups + g), k_refs[g], stage_ref, dil, kg_ref[...])
        _store_by_residue(chunk(1 + 2 * n_groups + g), v_refs[g], stage_ref, dil)


def _even_in(x, gain, w_in, q_gain, k_gain):
    b, s, _ = x.shape
    n_in = w_in.shape[1]
    assert PROJ_TN == DIL_OUT
    row = lambda bi, i: (bi, i, 0)
    const = lambda bi, i: (0, 0)
    hpg = DIL_HEADS_PER_GROUP
    group_specs = [pl.BlockSpec((None, hpg, dil, PROJ_TM // dil, HEAD_DIM), lambda bi, i: (bi, 0, 0, i, 0))
                   for _, dil in DIL_PAIRS]
    group_shapes = [jax.ShapeDtypeStruct((b, hpg, dil, s // dil, HEAD_DIM), BF16) for _, dil in DIL_PAIRS]
    outs = pl.pallas_call(
        _even_in_kernel,
        grid=(b, s // PROJ_TM),
        in_specs=[
            pl.BlockSpec((None, PROJ_TM, D_MODEL), row),
            pl.BlockSpec((1, D_MODEL), const),
            pl.BlockSpec(memory_space=pl.ANY),
            pl.BlockSpec((1, HEAD_DIM), const),
            pl.BlockSpec((1, HEAD_DIM), const),
        ],
        out_specs=[pl.BlockSpec((None, PROJ_TM, POOL_WIDTH), row)] + group_specs * 3,
        out_shape=[jax.ShapeDtypeStruct((b, s, POOL_WIDTH), F32)] + group_shapes * 3,
        scratch_shapes=[pltpu.VMEM((PROJ_TM, D_MODEL), BF16),
                        pltpu.VMEM((DIL_HEADS_PER_GROUP, PROJ_TM, HEAD_DIM), F32)]
        + _weight_scratch(D_MODEL, n_in),
        compiler_params=_params("arbitrary", "arbitrary"),
        name="even_in",
    )(x, gain.reshape(1, D_MODEL), w_in,
      (q_gain * HEAD_DIM ** -0.5).reshape(1, HEAD_DIM), k_gain.reshape(1, HEAD_DIM))
    n_groups = len(DIL_PAIRS)
    return outs[0], outs[1:1 + n_groups], outs[1 + n_groups:1 + 2 * n_groups], outs[1 + 2 * n_groups:]


def _odd_in_kernel(x_ref, g_ref, w_hbm, qg_ref, kg_ref, vg_ref, q_ref, k_ref, v_ref, u_ref, vn_ref,
                   h_ref, gv_ref, w_ref, wstage_ref, w_sems):
    pl.when(_first_grid_step())(functools.partial(_load_weight_as_bf16, w_hbm, w_ref, wstage_ref, w_sems))
    h_ref[...] = _rms(x_ref[...], g_ref[...]).astype(BF16)

    def chunk(c):
        return jnp.dot(h_ref[...], w_ref[:, c * PROJ_TN:(c + 1) * PROJ_TN], preferred_element_type=F32)

    per = NA_WIDTH // PROJ_TN
    heads_per_chunk = PROJ_TN // HEAD_DIM
    for c in range(per):
        _head_store(chunk(c), q_ref, c * heads_per_chunk, qg_ref[...])
        _head_store(chunk(per + c), k_ref, c * heads_per_chunk, kg_ref[...])
        _head_store(chunk(2 * per + c), v_ref, c * heads_per_chunk)
    per_sg = SG_WIDTH // PROJ_TN
    for c in range(per_sg):
        u_ref[:, c * PROJ_TN:(c + 1) * PROJ_TN] = jax.nn.gelu(chunk(3 * per + c))
        gv_ref[:, c * PROJ_TN:(c + 1) * PROJ_TN] = jax.nn.gelu(chunk(3 * per + per_sg + c))
    vn_ref[...] = _rms(gv_ref[...], vg_ref[...]).astype(BF16)


def _odd_in(x, gain, w_in, q_gain, k_gain, v_gain):
    b, s, _ = x.shape
    n_in = w_in.shape[1]
    row = lambda bi, i: (bi, i, 0)
    by_head = lambda bi, i: (bi, 0, i, 0)
    const = lambda bi, i: (0, 0)
    return pl.pallas_call(
        _odd_in_kernel,
        grid=(b, s // PROJ_TM),
        in_specs=[
            pl.BlockSpec((None, PROJ_TM, D_MODEL), row),
            pl.BlockSpec((1, D_MODEL), const),
            pl.BlockSpec(memory_space=pl.ANY),
            pl.BlockSpec((1, HEAD_DIM), const),
            pl.BlockSpec((1, HEAD_DIM), const),
            pl.BlockSpec((1, SG_WIDTH), const),
        ],
        out_specs=[
            pl.BlockSpec((None, NA_HEADS, PROJ_TM, HEAD_DIM), by_head),
            pl.BlockSpec((None, NA_HEADS, PROJ_TM, HEAD_DIM), by_head),
            pl.BlockSpec((None, NA_HEADS, PROJ_TM, HEAD_DIM), by_head),
            pl.BlockSpec((None, PROJ_TM, SG_WIDTH), row),
            pl.BlockSpec((None, PROJ_TM, SG_WIDTH), row),
        ],
        out_shape=[
            jax.ShapeDtypeStruct((b, NA_HEADS, s, HEAD_DIM), BF16),
            jax.ShapeDtypeStruct((b, NA_HEADS, s, HEAD_DIM), BF16),
            jax.ShapeDtypeStruct((b, NA_HEADS, s, HEAD_DIM), BF16),
            jax.ShapeDtypeStruct((b, s, SG_WIDTH), F32),
            jax.ShapeDtypeStruct((b, s, SG_WIDTH), BF16),
        ],
        scratch_shapes=[pltpu.VMEM((PROJ_TM, D_MODEL), BF16), pltpu.VMEM((PROJ_TM, SG_WIDTH), F32)]
        + _weight_scratch(D_MODEL, n_in),
        compiler_params=_params("arbitrary", "arbitrary"),
        name="odd_in",
    )(x, gain.reshape(1, D_MODEL), w_in,
      (q_gain * HEAD_DIM ** -0.5).reshape(1, HEAD_DIM), k_gain.reshape(1, HEAD_DIM),
      v_gain.reshape(1, SG_WIDTH))


def _dilated_kernel(slopes_ref, q0_ref, q1_ref, q2_ref, k0_ref, k1_ref, k2_ref, v0_ref, v1_ref, v2_ref,
                    o_ref, on_ref, ls_ref, s_ref, bias_ref, *, seq):
    hh = pl.program_id(1)
    step = pl.program_id(2)
    q_refs = (q0_ref, q1_ref, q2_ref)
    k_refs = (k0_ref, k1_ref, k2_ref)
    v_refs = (v0_ref, v1_ref, v2_ref)

    @pl.when(step == 0)
    def _():
        nk = DIL_TQ + 2 * DIL_RADIUS
        base0 = (lax.broadcasted_iota(jnp.int32, (DIL_TQ, nk), 1)
                 - lax.broadcasted_iota(jnp.int32, (DIL_TQ, nk), 0))
        for g, (_, dil) in enumerate(DIL_PAIRS):
            slope_g = slopes_ref[g * DIL_HEADS_PER_GROUP + hh] * dil
            for kind in range(3):
                dist = jnp.abs(base0 - kind * DIL_RADIUS)
                bias_ref[g, kind] = jnp.where(dist <= DIL_RADIUS, -slope_g * dist.astype(F32), NEG_INF)

    for g, (window, dil) in enumerate(DIL_PAIRS):
        assert window // (2 * dil) == DIL_RADIUS
        q_ref, k_ref, v_ref = q_refs[g], k_refs[g], v_refs[g]
        sub_len = seq // dil
        per_res = DIL_TA // dil
        nq = min(per_res, DIL_TQ)
        n_sub = per_res // nq
        n_keys = min(nq + 2 * DIL_RADIUS, sub_len)
        assert nq == DIL_TQ and n_keys == DIL_TQ + 2 * DIL_RADIUS

        def place(t, nq=nq, n_sub=n_sub, per_res=per_res, sub_len=sub_len, n_keys=n_keys):
            r = t // n_sub
            q_off = pl.multiple_of((t % n_sub) * nq, nq)
            q_pos = step * per_res + q_off
            k_pos = pl.multiple_of(jnp.clip(q_pos - DIL_RADIUS, 0, sub_len - n_keys), DIL_RADIUS)
            return r, q_off, q_pos, k_pos

        def scores(t, slot, q_ref=q_ref, k_ref=k_ref, nq=nq, n_keys=n_keys, g=g, place=place):
            r, q_off, q_pos, k_pos = place(t)
            q = q_ref[r, pl.ds(q_off, nq), :]
            k = k_ref[r, pl.ds(k_pos, n_keys), :]
            s = lax.dot_general(q, k, (((1,), (1,)), ((), ())), preferred_element_type=F32)
            s_ref[slot, 0:nq, 0:n_keys] = s + bias_ref[g, (q_pos - k_pos) // DIL_RADIUS]

        def finish(t, slot, v_ref=v_ref, g=g, dil=dil, nq=nq, n_keys=n_keys, place=place):
            r, q_off, _, k_pos = place(t)
            v = v_ref[r, pl.ds(k_pos, n_keys), :]
            s = s_ref[slot, 0:nq, 0:n_keys]
            m = jnp.max(s, axis=-1, keepdims=True)
            p = jnp.exp(s - m)
            den = jnp.sum(p, axis=-1, keepdims=True)
            out = jnp.dot(p.astype(BF16), v, preferred_element_type=F32) / den
            lse = jnp.broadcast_to(m + jnp.log(den), (nq, HEAD_DIM))
            if dil == 1:
                rows = pl.ds(q_off, nq)
            else:
                rows = pl.ds(r + dil * q_off, nq, stride=dil)
            on_ref[g, rows, :] = out
            ls_ref[g, rows, :] = lse

        n_tiles = dil * n_sub
        assert n_tiles % DIL_UNROLL == 0

        def body(i, carry, scores=scores, finish=finish):
            t0 = i * DIL_UNROLL
            scores(t0, 0)
            for u in range(DIL_UNROLL):
                if u + 1 < DIL_UNROLL:
                    scores(t0 + u + 1, (u + 1) % 2)
                finish(t0 + u, u % 2)
            return carry

        lax.fori_loop(0, n_tiles // DIL_UNROLL, body, 0)

    def merge(c, carry):
        rows = pl.ds(pl.multiple_of(c * DIL_TQ, DIL_TQ), DIL_TQ)
        lse = [ls_ref[g, rows, :] for g in range(len(DIL_PAIRS))]
        top = jnp.maximum(jnp.maximum(lse[0], lse[1]), lse[2])
        num = jnp.zeros((DIL_TQ, HEAD_DIM), F32)
        den = jnp.zeros((DIL_TQ, HEAD_DIM), F32)
        for g in range(len(DIL_PAIRS)):
            w = jnp.exp(lse[g] - top)
            num = num + w * on_ref[g, rows, :]
            den = den + w
        o_ref[rows, :] = (num / den).astype(o_ref.dtype)
        return carry

    lax.fori_loop(0, DIL_TA // DIL_TQ, merge, 0, unroll=True)


def _dilated(q, k, v, slopes):
    b, _, dil0, s, _ = q[0].shape
    assert dil0 == 1
    n_groups = len(DIL_PAIRS)

    def q_spec(dil):
        return pl.BlockSpec((None, None, dil, DIL_TA // dil, HEAD_DIM), lambda bi, hh, i: (bi, hh, 0, i, 0))

    def kv_spec(dil):
        return pl.BlockSpec((None, None, dil, s // dil, HEAD_DIM), lambda bi, hh, i: (bi, hh, 0, 0, 0))

    dils = [dil for _, dil in DIL_PAIRS]
    return pl.pallas_call(
        functools.partial(_dilated_kernel, seq=s),
        grid=(b, DIL_HEADS_PER_GROUP, s // DIL_TA),
        in_specs=[pl.BlockSpec(memory_space=pltpu.SMEM)]
        + [q_spec(d) for d in dils] + [kv_spec(d) for d in dils] + [kv_spec(d) for d in dils],
        out_specs=pl.BlockSpec((None, None, DIL_TA, HEAD_DIM), lambda bi, hh, i: (bi, hh, i, 0)),
        out_shape=jax.ShapeDtypeStruct((b, DIL_HEADS_PER_GROUP, s, HEAD_DIM), BF16),
        scratch_shapes=[pltpu.VMEM((n_groups, DIL_TA, HEAD_DIM), F32),
                        pltpu.VMEM((n_groups, DIL_TA, HEAD_DIM), F32),
                        pltpu.VMEM((2, DIL_TQ, DIL_TQ + 2 * DIL_RADIUS), F32),
                        pltpu.VMEM((n_groups, 3, DIL_TQ, DIL_TQ + 2 * DIL_RADIUS), F32)],
        compiler_params=_params("parallel", "parallel", "arbitrary"),
        name="dilated_attn",
    )(slopes, *q, *k, *v)


def _even_out_kernel(a_ref, prev_ref, next_ref, yb_ref, pw_ref, ps_ref, wo_hbm, x_ref, o_ref,
                     ext_ref, y_ref, wo_ref, wstage_ref, w_sems, *, seq):
    pl.when(_first_grid_step())(functools.partial(_load_weight_as_bf16, wo_hbm, wo_ref, wstage_ref, w_sems))
    i = pl.program_id(1)
    tm = OUT_TM
    ext_ref[0:POOL_HALO, :] = jnp.where(i == 0, 0.0, prev_ref[...])
    ext_ref[POOL_HALO:POOL_HALO + tm, :] = a_ref[...]
    ext_ref[POOL_HALO + tm:, :] = jnp.where(i == pl.num_programs(1) - 1, 0.0, next_ref[...])
    pos = i * tm + lax.broadcasted_iota(jnp.int32, (tm, 1), 0)
    for g, window in enumerate(POOL_WINDOWS):
        half = window // 2
        cols = slice(g * HEAD_DIM, (g + 1) * HEAD_DIM)
        total = ext_ref[POOL_HALO - half:POOL_HALO - half + tm, cols]
        for shift in range(-half + 1, half):
            total = total + ext_ref[POOL_HALO + shift:POOL_HALO + shift + tm, cols]
        count = (jnp.minimum(pos + half, seq) - jnp.maximum(pos - half, 0)).astype(F32)
        pooled = total / count - a_ref[:, cols]
        ya = jnp.dot(pooled.astype(BF16), pw_ref[g], preferred_element_type=F32) * ps_ref[:, cols]
        y_ref[:, cols] = ya.astype(BF16)
    for hd in range(DIL_HEADS_PER_GROUP):
        y_ref[:, POOL_WIDTH + hd * HEAD_DIM:POOL_WIDTH + (hd + 1) * HEAD_DIM] = yb_ref[hd]
    o_ref[...] = x_ref[...] + jnp.dot(y_ref[...], wo_ref[...], preferred_element_type=F32)


def _even_out(a, yb, pool_w, pool_scale, w_out, x):
    b, s, _ = x.shape
    tm = OUT_TM
    halo_blocks = tm // POOL_HALO
    tile = lambda bi, i: (bi, i, 0)
    return pl.pallas_call(
        functools.partial(_even_out_kernel, seq=s),
        grid=(b, s // tm),
        in_specs=[
            pl.BlockSpec((None, tm, POOL_WIDTH), tile),
            pl.BlockSpec((None, POOL_HALO, POOL_WIDTH),
                         lambda bi, i: (bi, jnp.maximum(i * halo_blocks - 1, 0), 0)),
            pl.BlockSpec((None, POOL_HALO, POOL_WIDTH),
                         lambda bi, i: (bi, jnp.minimum((i + 1) * halo_blocks, s // POOL_HALO - 1), 0)),
            pl.BlockSpec((None, DIL_HEADS_PER_GROUP, tm, HEAD_DIM), lambda bi, i: (bi, 0, i, 0)),
            _resident((len(POOL_WINDOWS), HEAD_DIM, HEAD_DIM), lambda bi, i: (0, 0, 0)),
            pl.BlockSpec((1, POOL_WIDTH), lambda bi, i: (0, 0)),
            pl.BlockSpec(memory_space=pl.ANY),
            pl.BlockSpec((None, tm, D_MODEL), tile),
        ],
        out_specs=pl.BlockSpec((None, tm, D_MODEL), tile),
        out_shape=jax.ShapeDtypeStruct((b, s, D_MODEL), F32),
        scratch_shapes=[pltpu.VMEM((tm + 2 * POOL_HALO, POOL_WIDTH), F32),
                        pltpu.VMEM((tm, POOL_WIDTH + DIL_OUT), BF16)]
        + _weight_scratch(POOL_WIDTH + DIL_OUT, D_MODEL),
        compiler_params=_params("arbitrary", "arbitrary"),
        name="even_out",
    )(a, a, a, yb, pool_w, pool_scale.reshape(1, POOL_WIDTH), w_out, x)


def _na_fill_bias(pair_ref, bias_ref, rb, rows):
    k_row0 = min(max(rb * NA_QROWS - NA_ROWS // 2, 0), rows - NA_KROWS)
    left_half = lax.broadcasted_iota(jnp.int32, (GRID_W, 2 * GRID_W), 1) < GRID_W
    for qr in range(NA_QROWS):
        r = rb * NA_QROWS + qr
        row_start = min(max(r - NA_ROWS // 2, 0), rows - NA_ROWS)
        for p in range(NA_KROWS // 2):
            kr = k_row0 + 2 * p
            ok_left = row_start <= kr < row_start + NA_ROWS
            ok_right = row_start <= kr + 1 < row_start + NA_ROWS
            if ok_left or ok_right:
                block = pair_ref[kr + 1 - r + NA_ROWS - 1]
                if not ok_right:
                    block = jnp.where(left_half, block, NEG_INF)
                if not ok_left:
                    block = jnp.where(left_half, NEG_INF, block)
            else:
                block = jnp.full((GRID_W, 2 * GRID_W), NEG_INF, F32)
            bias_ref[qr * GRID_W:(qr + 1) * GRID_W, p * 2 * GRID_W:(p + 1) * 2 * GRID_W] = block


def _na_kernel(q_ref, k_ref, v_ref, pair_ref, o_ref, bias_ref, s_ref, *, rows):
    n_blocks = rows // NA_QROWS
    tq = NA_QROWS * GRID_W
    n_keys = NA_KROWS * GRID_W
    kinds = (0, 1, n_blocks - 1)
    for slot, kind in enumerate(kinds):
        _na_fill_bias(pair_ref, bias_ref.at[slot], kind, rows)

    def key_start(rb):
        return min(max(rb * NA_QROWS - NA_ROWS // 2, 0), rows - NA_KROWS) * GRID_W

    def scores(rb):
        kind_slot = 0 if rb == 0 else (2 if rb == n_blocks - 1 else 1)
        q = q_ref[rb * tq:(rb + 1) * tq, :]
        k = k_ref[key_start(rb):key_start(rb) + n_keys, :]
        s_ref[rb % 2] = (lax.dot_general(q, k, (((1,), (1,)), ((), ())), preferred_element_type=F32)
                         + bias_ref[kind_slot])

    def finish(rb):
        v = v_ref[key_start(rb):key_start(rb) + n_keys, :]
        s = s_ref[rb % 2]
        m = jnp.max(s, axis=-1, keepdims=True)
        p = jnp.exp(s - m)
        den = jnp.sum(p, axis=-1, keepdims=True)
        out = jnp.dot(p.astype(BF16), v, preferred_element_type=F32) / den
        o_ref[rb * tq:(rb + 1) * tq, :] = out.astype(o_ref.dtype)

    scores(0)
    for rb in range(n_blocks):
        if rb + 1 < n_blocks:
            scores(rb + 1)
        finish(rb)


def _na_pair_tables(rpb):
    n_heads, n_rel_rows, n_rel_cols = rpb.shape
    c = np.arange(GRID_W)
    col_start = np.clip(c - NA_COLS // 2, 0, GRID_W - NA_COLS)
    col_ok = (c[None, :] >= col_start[:, None]) & (c[None, :] < col_start[:, None] + NA_COLS)
    rel_c = np.clip(c[None, :] - c[:, None], -(NA_COLS - 1), NA_COLS - 1) + (NA_COLS - 1)
    pick = (rel_c.reshape(1, -1) == np.arange(n_rel_cols)[:, None]).astype(np.float32)
    by_col = jnp.dot(rpb.astype(F32).reshape(n_heads * n_rel_rows, n_rel_cols), pick,
                     precision=lax.Precision.HIGHEST).reshape(n_heads, n_rel_rows, GRID_W, GRID_W)
    by_col = jnp.where(col_ok[None, None], by_col, NEG_INF)
    masked = jnp.full((n_heads, 1, GRID_W, GRID_W), NEG_INF, F32)
    padded = jnp.concatenate([masked, by_col, masked], axis=1)
    return jnp.concatenate([padded[:, :-1], padded[:, 1:]], axis=-1)


def _neighbourhood(q, k, v, pair_tables):
    b, _, s, _ = q.shape
    rows = s // GRID_W
    n_blocks = rows // NA_QROWS
    assert n_blocks >= 3 and rows >= NA_KROWS
    tq = NA_QROWS * GRID_W
    n_keys = NA_KROWS * GRID_W
    head = lambda bi, h: (bi, h, 0, 0)
    return pl.pallas_call(
        functools.partial(_na_kernel, rows=rows),
        grid=(b, NA_HEADS),
        in_specs=[
            pl.BlockSpec((None, None, s, HEAD_DIM), head),
            pl.BlockSpec((None, None, s, HEAD_DIM), head),
            pl.BlockSpec((None, None, s, HEAD_DIM), head),
            pl.BlockSpec((None, 2 * NA_ROWS, GRID_W, 2 * GRID_W), lambda bi, h: (h, 0, 0, 0)),
        ],
        out_specs=pl.BlockSpec((None, None, s, HEAD_DIM), head),
        out_shape=jax.ShapeDtypeStruct((b, NA_HEADS, s, HEAD_DIM), BF16),
        scratch_shapes=[pltpu.VMEM((3, tq, n_keys), F32), pltpu.VMEM((2, tq, n_keys), F32)],
        compiler_params=_params("parallel", "parallel"),
        name="neighbourhood_attn",
    )(q, k, v, pair_tables)


def _odd_out_kernel(yc_ref, u_ref, vn_ref, ws_ref, bs_ref, wo_hbm, x_ref, o_ref, y_ref,
                    wo_ref, wstage_ref, w_sems):
    pl.when(_first_grid_step())(functools.partial(_load_weight_as_bf16, wo_hbm, wo_ref, wstage_ref, w_sems))
    width = SG_WIDTH // SG_GROUPS
    for c in range(OUT_TM // SG_CHUNK):
        rows = slice(c * SG_CHUNK, (c + 1) * SG_CHUNK)
        for g in range(SG_GROUPS):
            cols = slice(g * width, (g + 1) * width)
            sv = jnp.dot(ws_ref[g], vn_ref[rows, cols], preferred_element_type=F32) + bs_ref[g]
            y_ref[rows, NA_WIDTH + g * width:NA_WIDTH + (g + 1) * width] = (u_ref[rows, cols] * sv).astype(BF16)
    for hd in range(NA_HEADS):
        y_ref[:, hd * HEAD_DIM:(hd + 1) * HEAD_DIM] = yc_ref[hd]
    o_ref[...] = x_ref[...] + jnp.dot(y_ref[...], wo_ref[...], preferred_element_type=F32)


def _odd_out(yc, u, vn, w_s, b_s, w_out, x):
    b, s, _ = x.shape
    tm = OUT_TM
    width = SG_WIDTH // SG_GROUPS
    tile = lambda bi, i: (bi, i, 0)
    return pl.pallas_call(
        _odd_out_kernel,
        grid=(b, s // tm),
        in_specs=[
            pl.BlockSpec((None, NA_HEADS, tm, HEAD_DIM), lambda bi, i: (bi, 0, i, 0)),
            pl.BlockSpec((None, tm, SG_WIDTH), tile),
            pl.BlockSpec((None, tm, SG_WIDTH), tile),
            _resident((SG_GROUPS, SG_CHUNK, SG_CHUNK), lambda bi, i: (0, 0, 0)),
            _resident((SG_GROUPS, SG_CHUNK, width), lambda bi, i: (0, 0, 0)),
            pl.BlockSpec(memory_space=pl.ANY),
            pl.BlockSpec((None, tm, D_MODEL), tile),
        ],
        out_specs=pl.BlockSpec((None, tm, D_MODEL), tile),
        out_shape=jax.ShapeDtypeStruct((b, s, D_MODEL), F32),
        scratch_shapes=[pltpu.VMEM((tm, NA_WIDTH + SG_WIDTH), BF16)]
        + _weight_scratch(NA_WIDTH + SG_WIDTH, D_MODEL),
        compiler_params=_params("arbitrary", "arbitrary"),
        name="odd_out",
    )(yc, u, vn, w_s, b_s, w_out, x)


def kernel(x, norm_ffn1, norm_mix, norm_ffn2, norm_out, ffn_w_gate, ffn_w_up, ffn_w_down, even_w_in, pool_w, pool_scale, dil_q_gain, dil_k_gain, even_w_out, odd_w_in, na_q_gain, na_k_gain, na_rpb, sg_v_gain, sg_w, sg_b, odd_w_out):
    depth = norm_ffn1.shape[0]
    slopes = jnp.asarray(2.0 ** (-8.0 * np.arange(1, DIL_HEADS + 1) / DIL_HEADS), dtype=F32)
    width = SG_WIDTH // SG_GROUPS

    for layer in range(depth):
        if layer == 0:
            x = _ffn(x, norm_ffn1[layer], ffn_w_gate, ffn_w_up, ffn_w_down, layer, 0)
        if layer % 2 == 0:
            e = layer // 2
            a, q, k, v = _even_in(x, norm_mix[layer], even_w_in[e], dil_q_gain[e], dil_k_gain[e])
            yb = _dilated(q, k, v, slopes)
            x = _even_out(a, yb, pool_w[e].astype(BF16), pool_scale[e], even_w_out[e], x)
        else:
            o = layer // 2
            q, k, v, u, vn = _odd_in(x, norm_mix[layer], odd_w_in[o],
                                     na_q_gain[o], na_k_gain[o], sg_v_gain[o])
            yc = _neighbourhood(q, k, v, _na_pair_tables(na_rpb[o]))
            b_s = jnp.broadcast_to(sg_b[o][:, :, None], (SG_GROUPS, SG_CHUNK, width))
            x = _odd_out(yc, u, vn, sg_w[o].astype(BF16), b_s, odd_w_out[o], x)
        then = (layer + 1, 0, norm_ffn1[layer + 1]) if layer + 1 < depth else None
        x = _ffn(x, norm_ffn2[layer], ffn_w_gate, ffn_w_up, ffn_w_down, layer, 1, gain_out=norm_out[layer],
                 then=then)
    return x
```
